```python
import math
import jax
import jax.numpy as jnp
from jax import lax
import numpy as np

D_MODEL = 1024
BATCH = 2
SEQ = 16384
DEPTH = 2

GRID_W = 64
CTX_LEN = 256
N_BRANCH = 4
N_HEADS = 4
HEAD_DIM = 64
BR_WIDTH = N_HEADS * HEAD_DIM
D_FF = 2816
MACARON_W = 0.5
N_SUB = 3
N_MOD = 3
EPS = 1e-6
ROPE_BASE = 10000.0
QBLOCK = 128

RW_DECAY_LORA = 64
RW_AAA_LORA = 64
RW_GATE_LORA = 128
RW_LN_EPS = 64e-5
RW_SIZES = (BR_WIDTH, BR_WIDTH, BR_WIDTH, RW_GATE_LORA, RW_DECAY_LORA, RW_DECAY_LORA, RW_AAA_LORA, RW_AAA_LORA)
RW_COLS = 3 * BR_WIDTH + RW_GATE_LORA + 2 * RW_DECAY_LORA + 2 * RW_AAA_LORA

HY_COLS = 3 * BR_WIDTH
HY_EMB = 33
HY_FFN = 64
HY_FAST_DECAY = 0.3
HY_SLOW_DECAY = 1.5
HY_TARGET = 1e-2

MLA_Q_RANK = 192
MLA_KV_RANK = 128
MLA_NOPE = 64
MLA_ROPE = 32
MLA_V = HEAD_DIM
MLA_SIZES = (MLA_Q_RANK, MLA_KV_RANK, MLA_ROPE)
MLA_COLS = MLA_Q_RANK + MLA_KV_RANK + MLA_ROPE

DF_QK = 32
DF_V = 2 * DF_QK
DF_SUBLN_EPS = 1e-5
DF_SIZES = (2 * N_HEADS * DF_QK, 2 * N_HEADS * DF_QK, N_HEADS * DF_V)
DF_COLS = 4 * N_HEADS * DF_QK + N_HEADS * DF_V

SECTION_SIZES = (RW_COLS, HY_COLS, MLA_COLS, DF_COLS)
P_TOTAL = RW_COLS + HY_COLS + MLA_COLS + DF_COLS

kernel_name = 'hybrid_rwkv_hyena_mla_diff_prefix_dit'


def _split(x, sizes):
    cuts = [int(i) for i in np.cumsum(sizes)[:-1]]
    return jnp.split(x, cuts, axis=-1)


def _flat(x):
    return x.reshape(x.shape[:2] + (-1,))


def _rmsnorm(x, g, eps=EPS):
    xf = x.astype(jnp.float32)
    y = xf * lax.rsqrt(jnp.mean(xf * xf, axis=-1, keepdims=True) + eps)
    return (y * g.astype(jnp.float32)).astype(x.dtype)


def _modulated_norm(x, g, mod_s):
    return _rmsnorm(x, g) * (1.0 + mod_s[:, 1][:, None]) + mod_s[:, 0][:, None]


def _gated_post_add(x, y, g_post, mod_s, weight):
    return x + weight * mod_s[:, 2][:, None] * _rmsnorm(y, g_post)


def _ffn_sublayer(x, mod_s, g_pre, g_post, w_in, w_out):
    h = _modulated_norm(x, g_pre, mod_s)
    a, b = jnp.split(h @ w_in, 2, axis=-1)
    return _gated_post_add(x, (jax.nn.silu(a) * b) @ w_out, g_post, mod_s, MACARON_W)


def _conv3(x, w):
    xp = jnp.pad(x, ((0, 0), (1, 1), (0, 0)))
    return xp[:, :-2] * w[0] + xp[:, 1:-1] * w[1] + xp[:, 2:] * w[2]


def _axial_tables(n_lat, rot_dim, dtype):
    rows = n_lat // GRID_W
    row = jnp.repeat(jnp.arange(rows, dtype=jnp.float32), GRID_W)
    col = jnp.tile(jnp.arange(GRID_W, dtype=jnp.float32), rows)
    half = rot_dim // 2
    inv = ROPE_BASE ** (-jnp.arange(0, half, 2, dtype=jnp.float32) / half)

    def tab(pos):
        ang = pos[:, None] * inv[None]
        return jnp.cos(ang)[:, None].astype(dtype), jnp.sin(ang)[:, None].astype(dtype)

    return tab(row), tab(col)


def _rope_half(x, cos, sin):
    x1, x2 = jnp.split(x, 2, axis=-1)
    return jnp.concatenate([x1 * cos - x2 * sin, x2 * cos + x1 * sin], axis=-1)


def _rope_2d(x, tabs):
    (cr, sr), (cc, sc) = tabs
    xr, xc = jnp.split(x, 2, axis=-1)
    return jnp.concatenate([_rope_half(xr, cr, sr), _rope_half(xc, cc, sc)], axis=-1)


def _softmax_f32(s, scale):
    return jax.nn.softmax(s.astype(jnp.float32) * scale, axis=-1)


def _attend(q, k, v, scale):
    p = _softmax_f32(jnp.einsum('bqhd,bkhd->bhqk', q, k), scale).astype(v.dtype)
    return jnp.einsum('bhqk,bkhe->bqhe', p, v)


def _diff_attend(q1, q2, k1, k2, v, lam, scale):
    p1 = _softmax_f32(jnp.einsum('bqhd,bkhd->bhqk', q1, k1), scale)
    p2 = _softmax_f32(jnp.einsum('bqhd,bkhd->bhqk', q2, k2), scale)
    a = (p1 - lam * p2).astype(v.dtype)
    return jnp.einsum('bhqk,bkhe->bqhe', a, v)


def _by_query_blocks(fn, *qs):
    b, t = qs[0].shape[:2]
    nb = t // QBLOCK
    blocks = tuple(jnp.swapaxes(q.reshape((b, nb, QBLOCK) + q.shape[2:]), 0, 1) for q in qs)
    out = lax.map(lambda blk: fn(*blk), blocks)
    return jnp.swapaxes(out, 0, 1).reshape((b, t) + out.shape[3:])


def _rwkv_prepare(p, w0, w2, a0, a2, g2, k_k, k_a):
    r, k, v, g_a, wa_f, wa_b, aa_f, aa_b = _split(p, RW_SIZES)
    bsz, t = p.shape[:2]

    def heads(z):
        return z.reshape(bsz, t, N_HEADS, HEAD_DIM)

    kk = heads((k * k_k).astype(jnp.float32))
    kk = kk / jnp.maximum(jnp.linalg.norm(kk, axis=-1, keepdims=True), 1e-12)
    decays, ks, bs = [], [], []
    for d, (wa, aa) in enumerate(((wa_f, aa_f), (wa_b, aa_b))):
        w_log = -jax.nn.softplus(-(w0[d] + jnp.tanh(wa) @ w2[d])) - 0.5
        decays.append(heads(jnp.exp(-jnp.exp(w_log.astype(jnp.float32)))))
        a = jax.nn.sigmoid(a0[d] + aa @ a2[d])
        ks.append(heads(k * (1.0 + (a - 1.0) * k_a)))
        bs.append(kk * heads(a.astype(jnp.float32)))
    g = jax.nn.sigmoid(g_a) @ g2
    return heads(r), heads(v), kk, g, decays, ks, bs


def _wkv7_scan(state0, r, w, k, v, kk, b, reverse, with_out):
    def step(s, inp):
        r_t, w_t, k_t, v_t, kk_t, b_t = inp
        sa = jnp.einsum('bhij,bhj->bhi', s, -kk_t)
        s = s * w_t[:, :, None, :] + sa[..., None] * b_t[:, :, None, :] + v_t[..., None] * k_t[:, :, None, :]
        y = jnp.einsum('bhij,bhj->bhi', s, r_t) if with_out else None
        return s, y

    xs = tuple(jnp.moveaxis(z.astype(jnp.float32), 1, 0) for z in (r, w, k, v, kk, b))
    s_fin, ys = lax.scan(step, state0, xs, reverse=reverse)
    return s_fin, (jnp.moveaxis(ys, 0, 1) if with_out else None)


def _rwkv_dir(state0, prep, d, reverse, with_out):
    r, v, kk, g, decays, ks, bs = prep
    return _wkv7_scan(state0, r, decays[d], ks[d], v, kk, bs[d], reverse, with_out)


def _rwkv_output(y_f, y_b, prep, r_k, ln_w, ln_b):
    r, v, kk, g, decays, ks, bs = prep
    y = y_f + y_b
    mean = jnp.mean(y, axis=-1, keepdims=True)
    var = jnp.mean(jnp.square(y - mean), axis=-1, keepdims=True)
    y = _flat((y - mean) * lax.rsqrt(var + RW_LN_EPS)) * ln_w + ln_b
    bonus = sum(jnp.sum(r * k_d * r_k, axis=-1, keepdims=True) * v for k_d in ks)
    return ((y + _flat(bonus)) * g).astype(v.dtype)


def _hyena_filter(n, w1, b1, f1, w2, b2, f2, w3):
    t = jnp.linspace(0.0, 1.0, n, dtype=jnp.float32)[:, None]
    bands = (HY_EMB - 1) // 2
    wpos = 2.0 * math.pi * jnp.arange(n, dtype=jnp.float32) / n
    fr = jnp.linspace(1e-4, bands - 1, bands, dtype=jnp.float32)
    ang = wpos[:, None] * fr[None]
    z = jnp.concatenate([t, jnp.cos(ang), -jnp.sin(ang)], axis=-1)
    h = jnp.sin(f1 * (z @ w1 + b1))
    h = jnp.sin(f2 * (h @ w2 + b2))
    h = (h @ w3).astype(jnp.float32)
    max_decay = math.log(HY_TARGET) / HY_FAST_DECAY
    min_decay = math.log(HY_TARGET) / HY_SLOW_DECAY
    deltas = jnp.abs(jnp.linspace(min_decay, max_decay, BR_WIDTH, dtype=jnp.float32))
    h = h * jnp.exp(-t * jnp.tile(deltas, 2)[None])
    h_f, h_b = jnp.split(h, 2, axis=-1)
    kern = jnp.concatenate([h_f, jnp.zeros((1, BR_WIDTH), jnp.float32), h_b[:n - 1][::-1]], axis=0)
    return kern * lax.rsqrt(jnp.sum(kern * kern, axis=0, keepdims=True))


def _fftconv(u, kern):
    n = u.shape[1]
    uf = jnp.fft.rfft(u.astype(jnp.float32), n=2 * n, axis=1)
    kf = jnp.fft.rfft(kern, n=2 * n, axis=0)
    return jnp.fft.irfft(uf * kf[None], n=2 * n, axis=1)[:, :n].astype(u.dtype)


def _hyena(p, conv_w, conv_b, w1, b1, f1, w2, b2, f2, w3, bias):
    u = _conv3(p, conv_w) + conv_b
    x0, x1, v = jnp.split(u, 3, axis=-1)
    kern = _hyena_filter(u.shape[1], w1, b1, f1, w2, b2, f2, w3)
    z = v * x1
    z = _fftconv(z, kern) + z * bias
    return x0 * z


def _mla_qkv(p, q_norm, wq_b, kv_norm, wkv_b, tabs):
    cq, ckv, kpe = _split(p, MLA_SIZES)
    bsz, t = p.shape[:2]
    q = (_rmsnorm(cq, q_norm) @ wq_b).reshape(bsz, t, N_HEADS, MLA_NOPE + MLA_ROPE)
    q_nope, q_pe = q[..., :MLA_NOPE], q[..., MLA_NOPE:]
    kv = (_rmsnorm(ckv, kv_norm) @ wkv_b).reshape(bsz, t, N_HEADS, MLA_NOPE + MLA_V)
    k_nope, v = kv[..., :MLA_NOPE], kv[..., MLA_NOPE:]
    k_pe = kpe[:, :, None, :]
    if tabs is not None:
        q_pe = _rope_2d(q_pe, tabs)
        k_pe = _rope_2d(k_pe, tabs)
    q = jnp.concatenate([q_nope, q_pe], axis=-1)
    k = jnp.concatenate([k_nope, jnp.broadcast_to(k_pe, (bsz, t, N_HEADS, MLA_ROPE))], axis=-1)
    return q, k, v


def _diff_qkv(p, tabs):
    pq, pk, pv = _split(p, DF_SIZES)
    bsz, t = p.shape[:2]
    q = pq.reshape(bsz, t, 2 * N_HEADS, DF_QK)
    k = pk.reshape(bsz, t, 2 * N_HEADS, DF_QK)
    if tabs is not None:
        q = _rope_2d(q, tabs)
        k = _rope_2d(k, tabs)
    q = q.reshape(bsz, t, N_HEADS, 2, DF_QK)
    k = k.reshape(bsz, t, N_HEADS, 2, DF_QK)
    return q[..., 0, :], q[..., 1, :], k[..., 0, :], k[..., 1, :], pv.reshape(bsz, t, N_HEADS, DF_V)


def _diff_out(o, subln, lam_init):
    return _flat(_rmsnorm(o, subln, DF_SUBLN_EPS) * (1.0 - lam_init))


def _merge(h, outs, w_up, w_gate, b_gate, w_o):
    acc = sum(jax.nn.sigmoid(h @ w_gate[n] + b_gate[n]) * (o @ w_up[n]) for n, o in enumerate(outs))
    return acc @ w_o


def _token_mixing(h, hc, tabs_mla, tabs_df, lam_init, with_ctx_out, lp):
    p_rw, p_hy, p_mla, p_df = _split(h @ lp['w_in'], SECTION_SIZES)
    c_rw, c_hy, c_mla, c_df = _split(hc @ lp['w_in'], SECTION_SIZES)

    mu = lp['rw_mu']
    shift_w = jnp.stack([0.5 * mu, 1.0 - mu, 0.5 * mu])
    rw_args = (lp['rw_w0'], lp['rw_w2'], lp['rw_a0'], lp['rw_a2'], lp['rw_g2'], lp['rw_kk'], lp['rw_ka'])
    prep = _rwkv_prepare(_conv3(p_rw, shift_w), *rw_args)
    prep_c = _rwkv_prepare(_conv3(c_rw, shift_w), *rw_args)
    s0 = jnp.zeros((hc.shape[0], N_HEADS, HEAD_DIM, HEAD_DIM), jnp.float32)
    sf_c, yf_c = _rwkv_dir(s0, prep_c, 0, False, with_ctx_out)
    sb_c, yb_c = _rwkv_dir(s0, prep_c, 1, True, with_ctx_out)
    _, yf = _rwkv_dir(sf_c, prep, 0, False, True)
    _, yb = _rwkv_dir(sb_c, prep, 1, True, True)
    rw_out_args = (lp['rw_rk'], lp['rw_ln_w'], lp['rw_ln_b'])
    o_rw = _rwkv_output(yf, yb, prep, *rw_out_args)

    hy_args = (lp['hy_conv_w'], lp['hy_conv_b'], lp['hy_w1'], lp['hy_b1'], lp['hy_f1'],
               lp['hy_w2'], lp['hy_b2'], lp['hy_f2'], lp['hy_w3'], lp['hy_bias'])
    o_hy = _hyena(p_hy, *hy_args)

    mla_args = (lp['mla_q_norm'], lp['mla_wq_b'], lp['mla_kv_norm'], lp['mla_wkv_b'])
    q, k, v = _mla_qkv(p_mla, *mla_args, tabs_mla)
    qc, kc, vc = _mla_qkv(c_mla, *mla_args, None)
    k_all = jnp.concatenate([k, kc], axis=1)
    v_all = jnp.concatenate([v, vc], axis=1)
    s_mla = 1.0 / math.sqrt(MLA_NOPE + MLA_ROPE)
    o_mla = _flat(_by_query_blocks(lambda qb: _attend(qb, k_all, v_all, s_mla), q))

    lam = (jnp.exp(jnp.sum(lp['df_lq1'] * lp['df_lk1']).astype(jnp.float32))
           - jnp.exp(jnp.sum(lp['df_lq2'] * lp['df_lk2']).astype(jnp.float32)) + lam_init)
    q1, q2, k1, k2, vd = _diff_qkv(p_df, tabs_df)
    qc1, qc2, kc1, kc2, vdc = _diff_qkv(c_df, None)
    k1_all = jnp.concatenate([k1, kc1], axis=1)
    k2_all = jnp.concatenate([k2, kc2], axis=1)
    vd_all = jnp.concatenate([vd, vdc], axis=1)
    s_df = 1.0 / math.sqrt(DF_QK)
    o_df = _by_query_blocks(lambda a, b: _diff_attend(a, b, k1_all, k2_all, vd_all, lam, s_df), q1, q2)
    o_df = _diff_out(o_df, lp['df_subln'], lam_init)

    merge_args = (lp['w_up'], lp['w_gate'], lp['b_gate'], lp['w_o'])
    y = _merge(h, (o_rw, o_hy, o_mla, o_df), *merge_args)
    if not with_ctx_out:
        return y, None

    oc_rw = _rwkv_output(yf_c, yb_c, prep_c, *rw_out_args)
    oc_hy = _hyena(c_hy, *hy_args)
    oc_mla = _flat(_attend(qc, kc, vc, s_mla))
    oc_df = _diff_out(_diff_attend(qc1, qc2, kc1, kc2, vdc, lam, s_df), lp['df_subln'], lam_init)
    yc = _merge(hc, (oc_rw, oc_hy, oc_mla, oc_df), *merge_args)
    return y, yc


def setup_inputs(seed: int = 0) -> dict:
    key = jax.random.key(seed)
    keys = jax.random.split(key, 64)
    counter = [0]
    f32 = jnp.float32
    L = DEPTH
    D = D_MODEL

    def nxt():
        k = keys[counter[0]]
        counter[0] += 1
        return k

    def nrm(shape, scale):
        return scale * jax.random.normal(nxt(), shape, f32)

    def gain(shape):
        return 1.0 + nrm(shape, 0.02)

    def unif(shape, lo, hi):
        return jax.random.uniform(nxt(), shape, f32, lo, hi)

    return {
        'x': nrm((BATCH, SEQ, D), 1.0),
        'c': nrm((BATCH, D), 1.0),
        'ctx': nrm((BATCH, CTX_LEN, D), 1.0),
        'c_ctx': nrm((D,), 1.0),
        'w_mod': nrm((L, D, N_SUB * N_MOD * D), 0.5 * D ** -0.5),
        'b_mod': nrm((L, N_SUB * N_MOD * D), 0.02),
        'norm_pre': gain((L, N_SUB, D)),
        'norm_post': gain((L, N_SUB, D)),
        'ffn_w_in': nrm((L, 2, D, 2 * D_FF), D ** -0.5),
        'ffn_w_out': nrm((L, 2, D_FF, D), D_FF ** -0.5),
        'w_in': nrm((L, D, P_TOTAL), D ** -0.5),
        'rw_mu': unif((L, RW_COLS), 0.2, 0.8),
        'rw_w0': unif((L, 2, BR_WIDTH), -6.0, -1.0),
        'rw_w2': nrm((L, 2, RW_DECAY_LORA, BR_WIDTH), 0.1 * RW_DECAY_LORA ** -0.5),
        'rw_a0': nrm((L, 2, BR_WIDTH), 0.1),
        'rw_a2': nrm((L, 2, RW_AAA_LORA, BR_WIDTH), 0.1 * RW_AAA_LORA ** -0.5),
        'rw_g2': nrm((L, RW_GATE_LORA, BR_WIDTH), RW_GATE_LORA ** -0.5),
        'rw_kk': 0.85 + nrm((L, BR_WIDTH), 0.02),
        'rw_ka': gain((L, BR_WIDTH)),
        'rw_rk': nrm((L, N_HEADS, HEAD_DIM), 0.1),
        'rw_ln_w': gain((L, BR_WIDTH)),
        'rw_ln_b': nrm((L, BR_WIDTH), 0.02),
        'hy_conv_w': nrm((L, 3, HY_COLS), 3 ** -0.5),
        'hy_conv_b': nrm((L, HY_COLS), 0.02),
        'hy_w1': nrm((L, HY_EMB, HY_FFN), HY_EMB ** -0.5),
        'hy_b1': nrm((L, HY_FFN), 0.02),
        'hy_f1': gain((L, HY_FFN)),
        'hy_w2': nrm((L, HY_FFN, HY_FFN), HY_FFN ** -0.5),
        'hy_b2': nrm((L, HY_FFN), 0.02),
        'hy_f2': gain((L, HY_FFN)),
        'hy_w3': nrm((L, HY_FFN, 2 * BR_WIDTH), HY_FFN ** -0.5),
        'hy_bias': nrm((L, BR_WIDTH), 0.5),
        'mla_q_norm': gain((L, MLA_Q_RANK)),
        'mla_wq_b': nrm((L, MLA_Q_RANK, N_HEADS * (MLA_NOPE + MLA_ROPE)), MLA_Q_RANK ** -0.5),
        'mla_kv_norm': gain((L, MLA_KV_RANK)),
        'mla_wkv_b': nrm((L, MLA_KV_RANK, N_HEADS * (MLA_NOPE + MLA_V)), MLA_KV_RANK ** -0.5),
        'df_lq1': nrm((L, DF_QK), 0.1),
        'df_lk1': nrm((L, DF_QK), 0.1),
        'df_lq2': nrm((L, DF_QK), 0.1),
        'df_lk2': nrm((L, DF_QK), 0.1),
        'df_subln': gain((L, DF_V)),
        'w_up': nrm((L, N_BRANCH, BR_WIDTH, D), BR_WIDTH ** -0.5),
        'w_gate': nrm((L, N_BRANCH, D, D), D ** -0.5),
        'b_gate': nrm((L, N_BRANCH, D), 0.02),
        'w_o': nrm((L, D, D), D ** -0.5),
    }


def reference(x, c, ctx, c_ctx, w_mod, b_mod, norm_pre, norm_post, ffn_w_in, ffn_w_out, w_in,
              rw_mu, rw_w0, rw_w2, rw_a0, rw_a2, rw_g2, rw_kk, rw_ka, rw_rk, rw_ln_w, rw_ln_b,
              hy_conv_w, hy_conv_b, hy_w1, hy_b1, hy_f1, hy_w2, hy_b2, hy_f2, hy_w3, hy_bias,
              mla_q_norm, mla_wq_b, mla_kv_norm, mla_wkv_b,
              df_lq1, df_lk1, df_lq2, df_lk2, df_subln, w_up, w_gate, b_gate, w_o):
    n_lat = x.shape[1]
    tabs_mla = _axial_tables(n_lat, MLA_ROPE, x.dtype)
    tabs_df = _axial_tables(n_lat, DF_QK, x.dtype)
    s_c = jax.nn.silu(c)
    s_cc = jax.nn.silu(c_ctx)[None]
    xc = ctx
    for l in range(DEPTH):
        last = l == DEPTH - 1
        mod = (s_c @ w_mod[l] + b_mod[l]).reshape(-1, N_SUB, N_MOD, D_MODEL)
        modc = (s_cc @ w_mod[l] + b_mod[l]).reshape(1, N_SUB, N_MOD, D_MODEL)
        ffn_a = (norm_pre[l, 0], norm_post[l, 0], ffn_w_in[l, 0], ffn_w_out[l, 0])
        ffn_b = (norm_pre[l, 2], norm_post[l, 2], ffn_w_in[l, 1], ffn_w_out[l, 1])

        x = _ffn_sublayer(x, mod[:, 0], *ffn_a)
        xc = _ffn_sublayer(xc, modc[:, 0], *ffn_a)

        h = _modulated_norm(x, norm_pre[l, 1], mod[:, 1])
        hc = _modulated_norm(xc, norm_pre[l, 1], modc[:, 1])
        lp = dict(w_in=w_in[l], rw_mu=rw_mu[l], rw_w0=rw_w0[l], rw_w2=rw_w2[l], rw_a0=rw_a0[l],
                  rw_a2=rw_a2[l], rw_g2=rw_g2[l], rw_kk=rw_kk[l], rw_ka=rw_ka[l], rw_rk=rw_rk[l],
                  rw_ln_w=rw_ln_w[l], rw_ln_b=rw_ln_b[l], hy_conv_w=hy_conv_w[l], hy_conv_b=hy_conv_b[l],
                  hy_w1=hy_w1[l], hy_b1=hy_b1[l], hy_f1=hy_f1[l], hy_w2=hy_w2[l], hy_b2=hy_b2[l],
                  hy_f2=hy_f2[l], hy_w3=hy_w3[l], hy_bias=hy_bias[l], mla_q_norm=mla_q_norm[l],
                  mla_wq_b=mla_wq_b[l], mla_kv_norm=mla_kv_norm[l], mla_wkv_b=mla_wkv_b[l],
                  df_lq1=df_lq1[l], df_lk1=df_lk1[l], df_lq2=df_lq2[l], df_lk2=df_lk2[l],
                  df_subln=df_subln[l], w_up=w_up[l], w_gate=w_gate[l], b_gate=b_gate[l], w_o=w_o[l])
        lam_init = 0.8 - 0.6 * math.exp(-0.3 * l)
        y, yc = _token_mixing(h, hc, tabs_mla, tabs_df, lam_init, not last, lp)

        x = _gated_post_add(x, y, norm_post[l, 1], mod[:, 1], 1.0)
        x = _ffn_sublayer(x, mod[:, 2], *ffn_b)
        if not last:
            xc = _gated_post_add(xc, yc, norm_post[l, 1], modc[:, 1], 1.0)
            xc = _ffn_sublayer(xc, modc[:, 2], *ffn_b)
    return x
```

```python
import functools
import math

import numpy as np
import jax
import jax.numpy as jnp
from jax import lax
from jax.experimental import pallas as pl
from jax.experimental.pallas import tpu as pltpu

F32 = jnp.float32
BF16 = jnp.bfloat16

D_MODEL = 1024
GRID_W = 64
N_HEADS = 4
HEAD_DIM = 64
BR = N_HEADS * HEAD_DIM
D_FF = 2816
MACARON_W = 0.5
EPS = 1e-6
ROPE_BASE = 10000.0
RW_LN_EPS = 64e-5
RW_COLS = 1152
HY_COLS = 768
HY_EMB = 33
HY_FAST_DECAY = 0.3
HY_SLOW_DECAY = 1.5
HY_TARGET = 1e-2
MLA_Q_RANK = 192
MLA_KV_RANK = 128
MLA_NOPE = 64
MLA_ROPE = 32
DF_QK = 32
DF_V = 64
DF_SUBLN_EPS = 1e-5
OFF_HY = RW_COLS
OFF_MLA = RW_COLS + HY_COLS
OFF_DF = OFF_MLA + MLA_Q_RANK + MLA_KV_RANK + MLA_ROPE

LANES = 128
HP = 128
CHUNK = 64
VMEM_LIMIT = 56 * 1024 * 1024


def _cp(*sem):
    return pltpu.CompilerParams(dimension_semantics=sem, vmem_limit_bytes=VMEM_LIMIT)


def _tile(n, pref, mult=8):
    if n <= pref:
        return n
    t = (pref // mult) * mult
    while t > mult and n % t:
        t -= mult
    assert n % t == 0, (n, pref)
    return t


def _dot(a, b, dims=None):
    if dims is None:
        dims = (((a.ndim - 1,), (0,)), ((), ()))
    return lax.dot_general(a, b, dims, preferred_element_type=F32)


NT = (((1,), (1,)), ((), ()))
TN = (((0,), (0,)), ((), ()))


def _split(x):
    hi = x.astype(BF16)
    lo = (x - hi.astype(F32)).astype(BF16)
    return hi, lo


def _dot3(a, b, dims=None):
    ah, al = _split(a)
    bh, bl = _split(b)
    return _dot(ah, bh, dims) + (_dot(ah, bl, dims) + _dot(al, bh, dims))


def _dot1(a, b, dims=None):
    return _dot(a.astype(BF16), b.astype(BF16), dims)


def _dotc(ch, cl, x):
    xh, xl = _split(x)
    return _dot(ch, xh) + (_dot(ch, xl) + _dot(cl, xh))


def _dot_exact_lhs(m, x):
    x1 = x.astype(BF16)
    r1 = x - x1.astype(F32)
    x2 = r1.astype(BF16)
    x3 = (r1 - x2.astype(F32)).astype(BF16)
    return _dot(m, x1) + (_dot(m, x2) + _dot(m, x3))


def _rms(x, g, eps=EPS):
    return x * lax.rsqrt(jnp.mean(x * x, axis=-1, keepdims=True) + eps) * g


def _sigmoid(x):
    return 1.0 / (1.0 + jnp.exp(-x))


def _silu(x):
    return x * _sigmoid(x)


def _softplus(z):
    return jnp.maximum(z, 0.0) + jnp.log(1.0 + jnp.exp(-jnp.abs(z)))


def _np_split(a):
    a = jnp.asarray(a, F32)
    hi = a.astype(BF16)
    lo = (a - hi.astype(F32)).astype(BF16)
    return hi, lo


def _mod_kernel(s_ref, w_ref, b_ref, o_ref):
    s = _silu(s_ref[...])
    o_ref[...] = _dot3(s, w_ref[...]) + b_ref[...]


def _modulation(s_rows, w_mod, b_mod):
    L, D, NM = w_mod.shape
    tn = _tile(NM, 1152, LANES)
    return pl.pallas_call(
        _mod_kernel,
        grid=(L, NM // tn),
        in_specs=[pl.BlockSpec((8, D), lambda l, j: (0, 0)),
                  pl.BlockSpec((None, D, tn), lambda l, j: (l, 0, j)),
                  pl.BlockSpec((None, 1, tn), lambda l, j: (l, 0, j))],
        out_specs=pl.BlockSpec((None, 8, tn), lambda l, j: (l, 0, j)),
        out_shape=jax.ShapeDtypeStruct((L, 8, NM), F32),
        compiler_params=_cp("parallel", "parallel"),
        name="modulation",
    )(s_rows, w_mod, b_mod.reshape(L, 1, NM))


def _mod_spec(m, nargs):
    if m.shape[0] == 1:
        return pl.BlockSpec((None, 1, D_MODEL), lambda *a: (0, 0, 0))
    return pl.BlockSpec((None, 1, D_MODEL), lambda *a: (a[0], 0, 0))


def _row_spec(n):
    return pl.BlockSpec((1, n), lambda *a: (0, 0))


def _ffn_kernel(x_ref, sh_ref, sc_ref, gt_ref, gpre_ref, gpost_ref, wa_ref, wb_ref, wo_ref, o_ref, h_sc, acc_sc):
    f = pl.program_id(2)

    @pl.when(f == 0)
    def _():
        h = _rms(x_ref[...], gpre_ref[...]) * (1.0 + sc_ref[...]) + sh_ref[...]
        h_sc[...] = h.astype(BF16)
        acc_sc[...] = jnp.zeros_like(acc_sc)

    h = h_sc[...]
    a = _dot(h, wa_ref[...])
    b = _dot(h, wb_ref[...])
    acc_sc[...] += _dot((_silu(a) * b).astype(BF16), wo_ref[...])

    @pl.when(f == pl.num_programs(2) - 1)
    def _():
        o_ref[...] = x_ref[...] + MACARON_W * gt_ref[...] * _rms(acc_sc[...], gpost_ref[...])


def _ffn(x, mod3, g_pre, g_post, w_in, w_out):
    B, T, D = x.shape
    F = w_out.shape[0]
    tm = _tile(T, 1024)
    tf = _tile(F, 256, LANES)
    nf = F // tf
    xspec = pl.BlockSpec((None, tm, D), lambda b, i, f: (b, i, 0))
    return pl.pallas_call(
        _ffn_kernel,
        grid=(B, T // tm, nf),
        in_specs=[xspec, _mod_spec(mod3[0], 3), _mod_spec(mod3[1], 3), _mod_spec(mod3[2], 3),
                  _row_spec(D), _row_spec(D),
                  pl.BlockSpec((D, tf), lambda b, i, f: (0, f)),
                  pl.BlockSpec((D, tf), lambda b, i, f: (0, nf + f)),
                  pl.BlockSpec((tf, D), lambda b, i, f: (f, 0))],
        out_specs=xspec,
        out_shape=jax.ShapeDtypeStruct((B, T, D), F32),
        scratch_shapes=[pltpu.VMEM((tm, D), BF16), pltpu.VMEM((tm, D), F32)],
        compiler_params=_cp("parallel", "parallel", "arbitrary"),
        name="ffn",
    )(x, mod3[0], mod3[1], mod3[2], g_pre.reshape(1, D), g_post.reshape(1, D), w_in, w_in, w_out)


def _proj_kernel(x_ref, sh_ref, sc_ref, gpre_ref, wrw, why, wmla, wdf, orw, ohy, omla, odf):
    h = (_rms(x_ref[...], gpre_ref[...]) * (1.0 + sc_ref[...]) + sh_ref[...]).astype(BF16)
    orw[...] = _dot(h, wrw[...])
    ohy[...] = _dot(h, why[...])
    omla[...] = _dot(h, wmla[...])
    odf[...] = _dot(h, wdf[...])


def _project(x, shift, scale, g_pre, ws):
    B, T, D = x.shape
    tm = _tile(T, 256)
    xspec = pl.BlockSpec((None, tm, D), lambda b, i: (b, i, 0))
    wspecs = [pl.BlockSpec(w.shape, lambda b, i: (0, 0)) for w in ws]
    ospecs = [pl.BlockSpec((None, tm, w.shape[1]), lambda b, i: (b, i, 0)) for w in ws]
    oshapes = [jax.ShapeDtypeStruct((B, T, w.shape[1]), F32) for w in ws]
    return pl.pallas_call(
        _proj_kernel,
        grid=(B, T // tm),
        in_specs=[xspec, _mod_spec(shift, 2), _mod_spec(scale, 2), _row_spec(D)] + wspecs,
        out_specs=ospecs,
        out_shape=oshapes,
        compiler_params=_cp("parallel", "parallel"),
        name="mix_project",
    )(x, shift, scale, g_pre.reshape(1, D), *ws)


def _halo_specs(tm, T, C):
    nb8 = T // 8
    r = tm // 8
    main = pl.BlockSpec((None, tm, C), lambda b, i: (b, i, 0))
    prev = pl.BlockSpec((None, 8, C), lambda b, i: (b, jnp.maximum(i * r - 1, 0), 0))
    nxt = pl.BlockSpec((None, 8, C), lambda b, i: (b, jnp.minimum((i + 1) * r, nb8 - 1), 0))
    return main, prev, nxt


def _conv3_tile(p, prev8, next8, w0, w1, w2):
    i = pl.program_id(1)
    n_i = pl.num_programs(1)
    tm = p.shape[0]
    row = lax.broadcasted_iota(jnp.int32, p.shape, 0)
    prev_row = jnp.where(i > 0, prev8[7:8, :], 0.0)
    next_row = jnp.where(i < n_i - 1, next8[0:1, :], 0.0)
    pm = jnp.where(row == 0, prev_row, pltpu.roll(p, 1, 0))
    pp = jnp.where(row == tm - 1, next_row, pltpu.roll(p, tm - 1, 0))
    return pm * w0 + p * w1 + pp * w2


def _rwkv_prep_kernel(p_ref, pv_ref, nx_ref, mu_ref, w0_ref, w2_ref, a0_ref, a2_ref, g2_ref, kk_ref, ka_ref,
                      rk_ref, bd_ref, r_o, v_o, kk_o, lw0_o, lw1_o, k0_o, k1_o, b0_o, b1_o, g_o, bonus_o):
    mu = mu_ref[...]
    p = _conv3_tile(p_ref[...], pv_ref[...], nx_ref[...], 0.5 * mu, 1.0 - mu, 0.5 * mu)
    r = p[:, 0:BR]
    k = p[:, BR:2 * BR]
    v = p[:, 2 * BR:3 * BR]
    g_a = p[:, 3 * BR:3 * BR + 128]
    wa = p[:, 896:1024]
    aa = p[:, 1024:1152]
    bd = bd_ref[...]
    kkraw = k * kk_ref[...]
    ss = _dot_exact_lhs_rhs(kkraw * kkraw, bd)
    kk = kkraw / jnp.maximum(jnp.sqrt(ss), 1e-12)
    wl = w0_ref[...] + _dot3(jnp.tanh(wa), w2_ref[...])
    lw = -jnp.exp(-_softplus(-wl) - 0.5)
    a = _sigmoid(a0_ref[...] + _dot3(aa, a2_ref[...]))
    a_f = a[:, :BR]
    a_b = a[:, BR:]
    ka = ka_ref[...]
    k_f = k * (1.0 + (a_f - 1.0) * ka)
    k_b = k * (1.0 + (a_b - 1.0) * ka)
    r_o[...] = r
    v_o[...] = v
    kk_o[...] = kk
    lw0_o[...] = lw[:, :BR]
    lw1_o[...] = lw[:, BR:]
    k0_o[...] = k_f
    k1_o[...] = k_b
    b0_o[...] = kk * a_f
    b1_o[...] = kk * a_b
    g_o[...] = _dot1(_sigmoid(g_a), g2_ref[...])
    bonus_o[...] = _dot_exact_lhs_rhs(r * (k_f + k_b) * rk_ref[...], bd) * v


def _dot_exact_lhs_rhs(x, m):
    x1 = x.astype(BF16)
    r1 = x - x1.astype(F32)
    x2 = r1.astype(BF16)
    x3 = (r1 - x2.astype(F32)).astype(BF16)
    return _dot(x1, m) + (_dot(x2, m) + _dot(x3, m))


def _block_diag_ones():
    h = np.arange(BR) // HEAD_DIM
    return jnp.asarray((h[:, None] == h[None, :]).astype(np.float32), BF16)


def _rwkv_prepare(p_rw, lp):
    B, T, C = p_rw.shape
    tm = _tile(T, 256)
    main, prev, nxt = _halo_specs(tm, T, C)
    z = jnp.zeros((64, BR), F32)
    w2 = jnp.concatenate([jnp.concatenate([lp['rw_w2'][0], z], 1), jnp.concatenate([z, lp['rw_w2'][1]], 1)], 0)
    a2 = jnp.concatenate([jnp.concatenate([lp['rw_a2'][0], z], 1), jnp.concatenate([z, lp['rw_a2'][1]], 1)], 0)
    params = [lp['rw_mu'].reshape(1, C), lp['rw_w0'].reshape(1, 2 * BR), w2, lp['rw_a0'].reshape(1, 2 * BR), a2,
              lp['rw_g2'].astype(BF16), lp['rw_kk'].reshape(1, BR), lp['rw_ka'].reshape(1, BR),
              lp['rw_rk'].reshape(1, BR), _block_diag_ones()]
    pspecs = [pl.BlockSpec(a.shape, lambda b, i: (0, 0)) for a in params]
    ospec = pl.BlockSpec((None, tm, BR), lambda b, i: (b, i, 0))
    names = ('r', 'v', 'kk', 'lw0', 'lw1', 'k0', 'k1', 'b0', 'b1', 'g', 'bonus')
    outs = pl.pallas_call(
        _rwkv_prep_kernel,
        grid=(B, T // tm),
        in_specs=[main, prev, nxt] + pspecs,
        out_specs=[ospec] * len(names),
        out_shape=[jax.ShapeDtypeStruct((B, T, BR), F32)] * len(names),
        compiler_params=_cp("parallel", "parallel"),
        name="rwkv_prepare",
    )(p_rw, p_rw, p_rw, *params)
    return dict(zip(names, outs))


def _rwkv_chunk(r, lw, k, v, kk, b, ht, tri, mask_s, mask_i, eye, tot_row):
    g = _dot_exact_lhs(tri, lw)
    gtot = g[tot_row:tot_row + 1, :]
    eng = jnp.exp(-g)
    rt = r * jnp.exp(g)
    at = -kk * jnp.exp(g - lw)
    bt = b * eng
    kt = k * eng
    et = jnp.exp(gtot - g)
    bh = b * et
    kh = k * et
    lab = jnp.where(mask_s, _dot3(at, bt, NT), 0.0)
    lak = jnp.where(mask_s, _dot3(at, kt, NT), 0.0)
    mrb = jnp.where(mask_i, _dot3(rt, bt, NT), 0.0)
    mrk = jnp.where(mask_i, _dot3(rt, kt, NT), 0.0)
    tinv = eye + lab
    pw = lab
    steps = int(math.log2(r.shape[0])) - 1
    for _ in range(steps):
        pw = _dot3(pw, pw)
        tinv = tinv + _dot3(tinv, pw)
    w = _dot3(at, ht, NT) + _dot3(lak, v)
    u = _dot3(tinv, w)
    y = _dot3(rt, ht, NT) + _dot3(mrb, u) + _dot3(mrk, v)
    hn = ht * jnp.exp(gtot) + _dot3(u, bh, TN) + _dot3(v, kh, TN)
    return y, hn


def _rwkv_scan_kernel(rf, vf, kkf, lwf, kf, bf, rb, vb, kkb, lwb, kb, bb, s0_ref, yf_o, yb_o, s_o, st):
    c = pl.program_id(1)
    C = rf.shape[0]

    @pl.when(c == 0)
    def _():
        st[...] = s0_ref[...]

    row = lax.broadcasted_iota(jnp.int32, (C, C), 0)
    col = lax.broadcasted_iota(jnp.int32, (C, C), 1)
    eye = (row == col).astype(F32)
    for d, (r_, v_, kk_, lw_, k_, b_, y_o) in enumerate(((rf, vf, kkf, lwf, kf, bf, yf_o),
                                                         (rb, vb, kkb, lwb, kb, bb, yb_o))):
        if d == 0:
            mask_i = col <= row
            mask_s = col < row
            tot_row = C - 1
        else:
            mask_i = col >= row
            mask_s = col > row
            tot_row = 0
        tri = mask_i.astype(BF16)
        for h in range(N_HEADS):
            sl = pl.ds(h * HEAD_DIM, HEAD_DIM)
            y, hn = _rwkv_chunk(r_[:, sl], lw_[:, sl], k_[:, sl], v_[:, sl], kk_[:, sl], b_[:, sl],
                                st[d, h], tri, mask_s, mask_i, eye, tot_row)
            y_o[:, sl] = y
            st[d, h] = hn

    @pl.when(c == pl.num_programs(1) - 1)
    def _():
        s_o[...] = st[...]


def _rwkv_scan(prep, s0):
    B, T, _ = prep['r'].shape
    nc = T // CHUNK
    fwd = pl.BlockSpec((None, CHUNK, BR), lambda b, c: (b, c, 0))
    bwd = pl.BlockSpec((None, CHUNK, BR), lambda b, c: (b, nc - 1 - c, 0))
    sspec = pl.BlockSpec((None, 2, N_HEADS, HEAD_DIM, HEAD_DIM), lambda b, c: (b, 0, 0, 0, 0))
    return pl.pallas_call(
        _rwkv_scan_kernel,
        grid=(B, nc),
        in_specs=[fwd] * 6 + [bwd] * 6 + [sspec],
        out_specs=[fwd, bwd, sspec],
        out_shape=[jax.ShapeDtypeStruct((B, T, BR), F32), jax.ShapeDtypeStruct((B, T, BR), F32),
                   jax.ShapeDtypeStruct(s0.shape, F32)],
        scratch_shapes=[pltpu.VMEM((2, N_HEADS, HEAD_DIM, HEAD_DIM), F32)],
        compiler_params=_cp("parallel", "arbitrary"),
        name="rwkv_scan",
    )(prep['r'], prep['v'], prep['kk'], prep['lw0'], prep['k0'], prep['b0'],
      prep['r'], prep['v'], prep['kk'], prep['lw1'], prep['k1'], prep['b1'], s0)


def _rwkv_out_kernel(yf, yb, g, bonus, lnw, lnb, bd, o):
    y = yf[...] + yb[...]
    m = bd[...]
    inv_n = 1.0 / HEAD_DIM
    mean = _dot_exact_lhs_rhs(y, m) * inv_n
    yc = y - mean
    var = _dot_exact_lhs_rhs(yc * yc, m) * inv_n
    yn = yc * lax.rsqrt(var + RW_LN_EPS) * lnw[...] + lnb[...]
    o[...] = (yn + bonus[...]) * g[...]


def _rwkv_output(yf, yb, prep, lp):
    B, T, _ = yf.shape
    tm = _tile(T, 512)
    spec = pl.BlockSpec((None, tm, BR), lambda b, i: (b, i, 0))
    return pl.pallas_call(
        _rwkv_out_kernel,
        grid=(B, T // tm),
        in_specs=[spec] * 4 + [_row_spec(BR), _row_spec(BR), pl.BlockSpec((BR, BR), lambda b, i: (0, 0))],
        out_specs=spec,
        out_shape=jax.ShapeDtypeStruct((B, T, BR), F32),
        compiler_params=_cp("parallel", "parallel"),
        name="rwkv_output",
    )(yf, yb, prep['g'], prep['bonus'], lp['rw_ln_w'].reshape(1, BR), lp['rw_ln_b'].reshape(1, BR),
      _block_diag_ones())


def _hy_pre_kernel(p_ref, pv_ref, nx_ref, w_ref, b_ref, x0_o, z_o):
    w = w_ref[...]
    u = _conv3_tile(p_ref[...], pv_ref[...], nx_ref[...], w[0:1, :], w[1:2, :], w[2:3, :]) + b_ref[...]
    x0_o[...] = u[:, 0:BR]
    z_o[...] = u[:, 2 * BR:3 * BR] * u[:, BR:2 * BR]


def _hyena_pre(p_hy, lp):
    B, T, C = p_hy.shape
    tm = _tile(T, 512)
    main, prev, nxt = _halo_specs(tm, T, C)
    ospec = pl.BlockSpec((None, tm, BR), lambda b, i: (b, i, 0))
    return pl.pallas_call(
        _hy_pre_kernel,
        grid=(B, T // tm),
        in_specs=[main, prev, nxt, pl.BlockSpec((3, C), lambda b, i: (0, 0)), _row_spec(C)],
        out_specs=[ospec, ospec],
        out_shape=[jax.ShapeDtypeStruct((B, T, BR), F32)] * 2,
        compiler_params=_cp("parallel", "parallel"),
        name="hyena_pre",
    )(p_hy, p_hy, p_hy, lp['hy_conv_w'], lp['hy_conv_b'].reshape(1, C))


def _hy_filter_kernel(z_ref, t_ref, mf_ref, mb_ref, w1, b1, f1, w2, b2, f2, w3, dl, k_o, ss_o):
    h = jnp.sin(f1[...] * (_dot3(z_ref[...], w1[...]) + b1[...]))
    h = jnp.sin(f2[...] * (_dot3(h, w2[...]) + b2[...]))
    h = _dot3(h, w3[...]) * jnp.exp(-t_ref[...] * dl[...])
    kern = mf_ref[...] * h[:, :BR] + mb_ref[...] * h[:, BR:]
    k_o[...] = kern

    @pl.when(pl.program_id(0) == 0)
    def _():
        ss_o[...] = jnp.zeros_like(ss_o)

    ss_o[...] += jnp.sum(kern * kern, axis=0, keepdims=True)


def _hyena_filter(n, lp):
    t = jnp.linspace(0.0, 1.0, n, dtype=F32)[:, None]
    bands = (HY_EMB - 1) // 2
    wpos = 2.0 * math.pi * jnp.arange(n, dtype=F32) / n
    fr = jnp.linspace(1e-4, bands - 1, bands, dtype=F32)
    ang = wpos[:, None] * fr[None]
    z = jnp.concatenate([t, jnp.cos(ang), -jnp.sin(ang)], axis=-1)
    rho = np.arange(2 * n)
    pos = np.where(rho < n, rho, np.clip(2 * n - 1 - rho, 0, n - 1))
    z_ext = jnp.pad(z[pos], ((0, 0), (0, LANES - HY_EMB)))
    t_ext = t[pos]
    mf = jnp.asarray((rho < n).astype(np.float32))[:, None]
    mb = jnp.asarray((rho > n).astype(np.float32))[:, None]
    max_decay = math.log(HY_TARGET) / HY_FAST_DECAY
    min_decay = math.log(HY_TARGET) / HY_SLOW_DECAY
    deltas = jnp.abs(jnp.linspace(min_decay, max_decay, BR, dtype=F32))
    dl = jnp.tile(deltas, 2)[None]
    w1 = jnp.pad(lp['hy_w1'], ((0, LANES - HY_EMB), (0, 0)))
    params = [w1, lp['hy_b1'][None], lp['hy_f1'][None], lp['hy_w2'], lp['hy_b2'][None], lp['hy_f2'][None],
              lp['hy_w3'], dl]
    tm = _tile(2 * n, 512)
    return pl.pallas_call(
        _hy_filter_kernel,
        grid=(2 * n // tm,),
        in_specs=[pl.BlockSpec((tm, LANES), lambda i: (i, 0))] + [pl.BlockSpec((tm, 1), lambda i: (i, 0))] * 3
        + [pl.BlockSpec(a.shape, lambda i: (0, 0)) for a in params],
        out_specs=[pl.BlockSpec((tm, BR), lambda i: (i, 0)), pl.BlockSpec((1, BR), lambda i: (0, 0))],
        out_shape=[jax.ShapeDtypeStruct((2 * n, BR), F32), jax.ShapeDtypeStruct((1, BR), F32)],
        compiler_params=_cp("arbitrary"),
        name="hyena_filter",
    )(z_ext, t_ext, mf, mb, *params)


def _dft_tables(N1, N2):
    N = N1 * N2

    def mat(n):
        a = -2.0 * np.pi * np.outer(np.arange(n), np.arange(n)) / n
        return np.cos(a), np.sin(a)

    f1r, f1i = mat(N1)
    f2r, f2i = mat(N2)
    a = -2.0 * np.pi * np.outer(np.arange(N1), np.arange(N2)) / N
    return dict(f1r=f1r, f1i=f1i, f2r=f2r, f2i=f2i,
                twr=jnp.asarray(np.cos(a)[:, :, None], F32), twi=jnp.asarray(np.sin(a)[:, :, None], F32))


def _fft_in_kernel(x_ref, frh, frl, fih, fil, ar_o, ai_o):
    x = x_ref[...]
    ar_o[...] = _dotc(frh[...], frl[...], x)
    ai_o[...] = _dotc(fih[...], fil[...], x)


def _fft_in(x2d, tabs, N1):
    B, n1u, cols = x2d.shape
    consts = list(_np_split(tabs['f1r'][:, :n1u])) + list(_np_split(tabs['f1i'][:, :n1u]))
    tc = _tile(cols, 8192, LANES)
    ospec = pl.BlockSpec((None, N1, tc), lambda b, j: (b, 0, j))
    return pl.pallas_call(
        _fft_in_kernel,
        grid=(B, cols // tc),
        in_specs=[pl.BlockSpec((None, n1u, tc), lambda b, j: (b, 0, j))]
        + [pl.BlockSpec((N1, n1u), lambda b, j: (0, 0))] * 4,
        out_specs=[ospec, ospec],
        out_shape=[jax.ShapeDtypeStruct((B, N1, cols), F32)] * 2,
        compiler_params=_cp("parallel", "parallel"),
        name="fft_first_axis",
    )(x2d, *consts)


def _cplx_dft(frh, frl, fih, fil, xr, xi, conj):
    rr = _dotc(frh, frl, xr)
    ii = _dotc(fih, fil, xi)
    ri = _dotc(frh, frl, xi)
    ir = _dotc(fih, fil, xr)
    if conj:
        return rr + ii, ri - ir
    return rr - ii, ri + ir


def _fft_spec_kernel(ar, ai, twr, twi, frh, frl, fih, fil, sc, kr_o, ki_o):
    tr, ti = twr[...], twi[...]
    xr = ar[...] * tr - ai[...] * ti
    xi = ar[...] * ti + ai[...] * tr
    br, bi = _cplx_dft(frh[...], frl[...], fih[...], fil[...], xr, xi, False)
    kr_o[...] = br * sc[...]
    ki_o[...] = bi * sc[...]


def _fft_conv_kernel(ar, ai, twr, twi, frh, frl, fih, fil, kr, ki, dr_o, di_o):
    tr, ti = twr[...], twi[...]
    f = (frh[...], frl[...], fih[...], fil[...])
    xr = ar[...] * tr - ai[...] * ti
    xi = ar[...] * ti + ai[...] * tr
    br, bi = _cplx_dft(*f, xr, xi, False)
    cr = br * kr[...] - bi * ki[...]
    ci = br * ki[...] + bi * kr[...]
    dr, di = _cplx_dft(*f, cr, ci, True)
    dr_o[...] = dr * tr + di * ti
    di_o[...] = di * tr - dr * ti


def _fft_mid(a_re, a_im, tabs, kf=None, scale=None):
    B, N1, N2, C = a_re.shape
    consts = list(_np_split(tabs['f2r'])) + list(_np_split(tabs['f2i']))
    aspec = pl.BlockSpec((None, None, N2, C), lambda b, k: (b, k, 0, 0))
    tspec = pl.BlockSpec((None, N2, 1), lambda b, k: (k, 0, 0))
    fspec = pl.BlockSpec((N2, N2), lambda b, k: (0, 0))
    if kf is None:
        extra, especs, kern, nm = [scale], [_row_spec(C)], _fft_spec_kernel, "fft_filter_spectrum"
    else:
        kspec = pl.BlockSpec((None, None, N2, C), lambda b, k: (0, k, 0, 0))
        extra, especs, kern, nm = list(kf), [kspec, kspec], _fft_conv_kernel, "fft_second_axis_conv"
    return pl.pallas_call(
        kern,
        grid=(B, N1),
        in_specs=[aspec, aspec, tspec, tspec] + [fspec] * 4 + especs,
        out_specs=[aspec, aspec],
        out_shape=[jax.ShapeDtypeStruct((B, N1, N2, C), F32)] * 2,
        compiler_params=_cp("parallel", "parallel"),
        name=nm,
    )(a_re, a_im, tabs['twr'], tabs['twi'], *consts, *extra)


def _fft_out_kernel(dr, di, frh, frl, fih, fil, x0, z, bias, o):
    zc = _dotc(frh[...], frl[...], dr[...]) + _dotc(fih[...], fil[...], di[...])
    zz = z[...]
    o[...] = x0[...] * (zc + zz * bias[...])


def _fft_out(d_re, d_im, tabs, x0_2d, z_2d, bias_row):
    B, N1, cols = d_re.shape
    n1u = x0_2d.shape[1]
    consts = list(_np_split(tabs['f1r'][:n1u, :])) + list(_np_split(tabs['f1i'][:n1u, :]))
    tc = _tile(cols, 8192, LANES)
    dspec = pl.BlockSpec((None, N1, tc), lambda b, j: (b, 0, j))
    xspec = pl.BlockSpec((None, n1u, tc), lambda b, j: (b, 0, j))
    return pl.pallas_call(
        _fft_out_kernel,
        grid=(B, cols // tc),
        in_specs=[dspec, dspec] + [pl.BlockSpec((n1u, N1), lambda b, j: (0, 0))] * 4
        + [xspec, xspec, pl.BlockSpec((1, tc), lambda b, j: (0, j))],
        out_specs=xspec,
        out_shape=jax.ShapeDtypeStruct((B, n1u, cols), F32),
        compiler_params=_cp("parallel", "parallel"),
        name="fft_last_axis_gate",
    )(d_re, d_im, *consts, x0_2d, z_2d, bias_row)


def _hy_gate_kernel(zc, x0, z, bias, o):
    o[...] = x0[...] * (zc[...] + z[...] * bias[...])


def _hyena(p_hy, lp):
    B, n, _ = p_hy.shape
    x0, z = _hyena_pre(p_hy, lp)
    kern, ss = _hyena_filter(n, lp)
    N = 2 * n
    N2 = min(256, n)
    N1 = N // N2
    tabs = _dft_tables(N1, N2)
    scale = lax.rsqrt(ss) * (1.0 / N)
    bias = lp['hy_bias'].reshape(1, BR)
    if N1 >= 4:
        n1u = N1 // 2
        k_re, k_im = _fft_in(kern.reshape(1, N1, N2 * BR), tabs, N1)
        kf = _fft_mid(k_re.reshape(1, N1, N2, BR), k_im.reshape(1, N1, N2, BR), tabs, scale=scale)
        a_re, a_im = _fft_in(z.reshape(B, n1u, N2 * BR), tabs, N1)
        d_re, d_im = _fft_mid(a_re.reshape(B, N1, N2, BR), a_im.reshape(B, N1, N2, BR), tabs, kf=kf)
        o = _fft_out(d_re.reshape(B, N1, N2 * BR), d_im.reshape(B, N1, N2 * BR), tabs,
                     x0.reshape(B, n1u, N2 * BR), z.reshape(B, n1u, N2 * BR), jnp.tile(bias, (1, N2)))
        return o.reshape(B, n, BR)
    tabs = _dft_tables(1, N)
    kf = _fft_mid(kern.reshape(1, 1, N, BR), jnp.zeros((1, 1, N, BR), F32), tabs, scale=scale)
    zp = jnp.pad(z, ((0, 0), (0, n), (0, 0))).reshape(B, 1, N, BR)
    d_re, _ = _fft_mid(zp, jnp.zeros_like(zp), tabs, kf=kf)
    zc = d_re.reshape(B, N, BR)[:, :n]
    tm = _tile(n, 512)
    spec = pl.BlockSpec((None, tm, BR), lambda b, i: (b, i, 0))
    return pl.pallas_call(
        _hy_gate_kernel,
        grid=(B, n // tm),
        in_specs=[spec, spec, spec, _row_spec(BR)],
        out_specs=spec,
        out_shape=jax.ShapeDtypeStruct((B, n, BR), F32),
        compiler_params=_cp("parallel", "parallel"),
        name="hyena_gate",
    )(zc, x0, z, bias)


def _rope_tables(n_lat):
    rows = n_lat // GRID_W
    row = jnp.repeat(jnp.arange(rows, dtype=F32), GRID_W)
    col = jnp.tile(jnp.arange(GRID_W, dtype=F32), rows)
    half = MLA_ROPE // 2
    inv = ROPE_BASE ** (-jnp.arange(0, half, 2, dtype=F32) / half)
    ar = row[:, None] * inv[None]
    ac = col[:, None] * inv[None]
    cos = jnp.concatenate([jnp.cos(ar), jnp.cos(ar), jnp.cos(ac), jnp.cos(ac)], axis=-1)
    sin = jnp.concatenate([jnp.sin(ar), jnp.sin(ar), jnp.sin(ac), jnp.sin(ac)], axis=-1)
    return cos, sin


def _rot_index():
    j = np.arange(32)
    first = (j % 16) < 8
    src = np.where(first, j + 8, j - 8)
    sign = np.where(first, -1.0, 1.0)
    return src, sign


def _take_cols(w, idx, sign=None):
    idx = np.asarray(idx)
    s = np.where(idx >= 0, 1.0, 0.0) if sign is None else np.where(idx >= 0, np.asarray(sign), 0.0)
    return jnp.take(w, jnp.asarray(np.maximum(idx, 0)), axis=1) * jnp.asarray(s, w.dtype)[None, :]


def _mla_prep_kernel(p_ref, cq_t, sq_t, ck_t, qn, kvn, wqc, wqs, wk, wv, epe, q_o, k_o, v_o):
    p = p_ref[...]
    cq = p[:, 0:256]
    cqn = (cq * lax.rsqrt(jnp.sum(cq * cq, axis=-1, keepdims=True) * (1.0 / MLA_Q_RANK) + EPS) * qn[...]).astype(BF16)
    ckv = p[:, 256:384]
    ckvn = _rms(ckv, kvn[...]).astype(BF16)
    q = _dot(cqn, wqc[...]) * cq_t[...] + _dot(cqn, wqs[...]) * sq_t[...]
    pe = (p[:, 384:512] * ck_t[...]).astype(BF16)
    k = _dot(ckvn, wk[...]) + _dot(pe, epe[...])
    v = _dot(ckvn, wv[...])
    for h in range(N_HEADS):
        q_o[h] = q[:, h * HP:(h + 1) * HP].astype(BF16)
        k_o[h] = k[:, h * HP:(h + 1) * HP].astype(BF16)
        v_o[h] = v[:, h * HP:(h + 1) * HP].astype(BF16)


def _mla_weights(lp):
    s = 1.0 / math.sqrt(MLA_NOPE + MLA_ROPE)
    src, sign = _rot_index()
    hd = MLA_NOPE + MLA_ROPE
    qc_idx, qs_idx, qs_sign, k_idx, v_idx = [], [], [], [], []
    for h in range(N_HEADS):
        qc_idx += list(h * hd + np.arange(hd)) + [-1] * (HP - hd)
        qs_idx += [-1] * MLA_NOPE + list(h * hd + MLA_NOPE + src) + [-1] * (HP - hd)
        qs_sign += [0.0] * MLA_NOPE + list(sign) + [0.0] * (HP - hd)
        k_idx += list(h * 128 + np.arange(MLA_NOPE)) + [-1] * (HP - MLA_NOPE)
        v_idx += list(h * 128 + MLA_NOPE + np.arange(64)) + [-1] * (HP - 64)
    wq = jnp.pad(lp['mla_wq_b'] * s, ((0, 256 - MLA_Q_RANK), (0, 0)))
    wqc = _take_cols(wq, qc_idx).astype(BF16)
    wqs = _take_cols(wq, qs_idx, qs_sign).astype(BF16)
    wk = _take_cols(lp['mla_wkv_b'], k_idx).astype(BF16)
    wv = _take_cols(lp['mla_wkv_b'], v_idx).astype(BF16)
    epe = np.zeros((128, N_HEADS * HP), np.float32)
    for h in range(N_HEADS):
        for j in range(MLA_ROPE):
            epe[j, h * HP + MLA_NOPE + j] = 1.0
            epe[MLA_ROPE + j, h * HP + MLA_NOPE + j] = 1.0
    qn = jnp.pad(lp['mla_q_norm'], (0, 256 - MLA_Q_RANK)).reshape(1, 256)
    return qn, lp['mla_kv_norm'].reshape(1, MLA_KV_RANK), wqc, wqs, wk, wv, jnp.asarray(epe, BF16)


def _mla_tables(cos, sin):
    T = cos.shape[0]
    one = jnp.ones((T, MLA_NOPE), F32)
    zero = jnp.zeros((T, HP - MLA_NOPE - MLA_ROPE), F32)
    zn = jnp.zeros((T, MLA_NOPE), F32)
    cq = jnp.tile(jnp.concatenate([one, cos, zero], -1), (1, N_HEADS))
    sq = jnp.tile(jnp.concatenate([zn, sin, zero], -1), (1, N_HEADS))
    ck = jnp.concatenate([cos, sin, jnp.zeros((T, 64), F32)], -1)
    return cq, sq, ck


def _mla_prep(p_mla, tables, weights):
    B, T, C = p_mla.shape
    tm = _tile(T, 256)
    W = N_HEADS * HP
    tspecs = [pl.BlockSpec((tm, W), lambda b, i: (i, 0)), pl.BlockSpec((tm, W), lambda b, i: (i, 0)),
              pl.BlockSpec((tm, 128), lambda b, i: (i, 0))]
    wspecs = [pl.BlockSpec(w.shape, lambda b, i: (0, 0)) for w in weights]
    ospec = pl.BlockSpec((None, N_HEADS, tm, HP), lambda b, i: (b, 0, i, 0))
    return pl.pallas_call(
        _mla_prep_kernel,
        grid=(B, T // tm),
        in_specs=[pl.BlockSpec((None, tm, C), lambda b, i: (b, i, 0))] + tspecs + wspecs,
        out_specs=[ospec] * 3,
        out_shape=[jax.ShapeDtypeStruct((B, N_HEADS, T, HP), BF16)] * 3,
        compiler_params=_cp("parallel", "parallel"),
        name="mla_prepare",
    )(p_mla, *tables, *weights)


def _df_prep_kernel(p_ref, c_t, s_t, q_o, k1_o, k2_o, v_o):
    W = N_HEADS * HP
    p = p_ref[...]
    c = c_t[...]
    s = s_t[...]
    q = p[:, 0:W] * c + p[:, W:2 * W] * s
    k = p[:, 2 * W:3 * W] * c + p[:, 3 * W:4 * W] * s
    v = p[:, 4 * W:5 * W]
    lane = lax.broadcasted_iota(jnp.int32, (p.shape[0], HP), 1)
    for h in range(N_HEADS):
        sl = slice(h * HP, (h + 1) * HP)
        q_o[h] = (q[:, sl] * (1.0 / math.sqrt(DF_QK))).astype(BF16)
        kh = k[:, sl]
        k1_o[h] = jnp.where(lane < DF_QK, kh, 0.0).astype(BF16)
        k2_o[h] = jnp.where(lane >= DF_QK, kh, 0.0).astype(BF16)
        v_o[h] = v[:, sl].astype(BF16)


def _df_tables(cos, sin):
    T = cos.shape[0]
    zero = jnp.zeros((T, HP - 2 * DF_QK), F32)
    c = jnp.tile(jnp.concatenate([cos, cos, zero], -1), (1, N_HEADS))
    s = jnp.tile(jnp.concatenate([sin, sin, zero], -1), (1, N_HEADS))
    return c, s


def _df_prep(p_df, tables):
    B, T, C = p_df.shape
    tm = _tile(T, 256)
    W = N_HEADS * HP
    tspec = pl.BlockSpec((tm, W), lambda b, i: (i, 0))
    ospec = pl.BlockSpec((None, N_HEADS, tm, HP), lambda b, i: (b, 0, i, 0))
    return pl.pallas_call(
        _df_prep_kernel,
        grid=(B, T // tm),
        in_specs=[pl.BlockSpec((None, tm, C), lambda b, i: (b, i, 0)), tspec, tspec],
        out_specs=[ospec] * 4,
        out_shape=[jax.ShapeDtypeStruct((B, N_HEADS, T, HP), BF16)] * 4,
        compiler_params=_cp("parallel", "parallel"),
        name="diff_prepare",
    )(p_df, *tables)


def _online_step(q, k, v, m_sc, l_sc, acc_sc):
    s = _dot(q, k, NT)
    m_prev = m_sc[...]
    m_new = jnp.maximum(m_prev, jnp.max(s, axis=-1, keepdims=True))
    alpha = jnp.exp(m_prev - m_new)
    p = jnp.exp(s - m_new)
    l_sc[...] = alpha * l_sc[...] + jnp.sum(p, axis=-1, keepdims=True)
    acc_sc[...] = alpha * acc_sc[...] + _dot(p.astype(BF16), v)
    m_sc[...] = m_new


def _mla_attn_kernel(q_ref, k_ref, v_ref, o_ref, m_sc, l_sc, acc_sc):
    j = pl.program_id(3)

    @pl.when(j == 0)
    def _():
        m_sc[...] = jnp.full_like(m_sc, -jnp.inf)
        l_sc[...] = jnp.zeros_like(l_sc)
        acc_sc[...] = jnp.zeros_like(acc_sc)

    _online_step(q_ref[...], k_ref[...], v_ref[...], m_sc, l_sc, acc_sc)

    @pl.when(j == pl.num_programs(3) - 1)
    def _():
        o_ref[...] = acc_sc[...] / l_sc[...]


def _attn_specs(Tq, Tk):
    tq = _tile(Tq, 512)
    tk = _tile(Tk, 1280, LANES)
    qspec = pl.BlockSpec((None, None, tq, HP), lambda b, h, i, j: (b, h, i, 0))
    kspec = pl.BlockSpec((None, None, tk, HP), lambda b, h, i, j: (b, h, j, 0))
    return tq, tk, qspec, kspec


def _mla_attention(q, k, v):
    B, H, Tq, _ = q.shape
    Tk = k.shape[2]
    tq, tk, qspec, kspec = _attn_specs(Tq, Tk)
    return pl.pallas_call(
        _mla_attn_kernel,
        grid=(B, H, Tq // tq, Tk // tk),
        in_specs=[qspec, kspec, kspec],
        out_specs=qspec,
        out_shape=jax.ShapeDtypeStruct((B, H, Tq, HP), F32),
        scratch_shapes=[pltpu.VMEM((tq, 1), F32), pltpu.VMEM((tq, 1), F32), pltpu.VMEM((tq, HP), F32)],
        compiler_params=_cp("parallel", "parallel", "parallel", "arbitrary"),
        name="mla_attention",
    )(q, k, v)


def _df_attn_kernel(lam_init, q_ref, k1_ref, k2_ref, v_ref, lq1, lk1, lq2, lk2, sub, o_ref,
                    m1, l1, a1, m2, l2, a2):
    j = pl.program_id(3)

    @pl.when(j == 0)
    def _():
        for m_, l_, a_ in ((m1, l1, a1), (m2, l2, a2)):
            m_[...] = jnp.full_like(m_, -jnp.inf)
            l_[...] = jnp.zeros_like(l_)
            a_[...] = jnp.zeros_like(a_)

    q = q_ref[...]
    v = v_ref[...]
    _online_step(q, k1_ref[...], v, m1, l1, a1)
    _online_step(q, k2_ref[...], v, m2, l2, a2)

    @pl.when(j == pl.num_programs(3) - 1)
    def _():
        lam = (jnp.exp(jnp.sum(lq1[...] * lk1[...], axis=-1, keepdims=True))
               - jnp.exp(jnp.sum(lq2[...] * lk2[...], axis=-1, keepdims=True)) + lam_init)
        o = a1[...] / l1[...] - lam * (a2[...] / l2[...])
        ms = jnp.sum(o * o, axis=-1, keepdims=True) * (1.0 / DF_V)
        o_ref[...] = o * lax.rsqrt(ms + DF_SUBLN_EPS) * sub[...] * (1.0 - lam_init)


def _df_attention(q, k1, k2, v, lp, lam_init):
    B, H, Tq, _ = q.shape
    Tk = k1.shape[2]
    tq, tk, qspec, kspec = _attn_specs(Tq, Tk)
    lspec = pl.BlockSpec((1, DF_QK), lambda b, h, i, j: (0, 0))
    sub = jnp.pad(lp['df_subln'], (0, HP - DF_V)).reshape(1, HP)
    stat = [pltpu.VMEM((tq, 1), F32), pltpu.VMEM((tq, 1), F32), pltpu.VMEM((tq, HP), F32)]
    return pl.pallas_call(
        functools.partial(_df_attn_kernel, lam_init),
        grid=(B, H, Tq // tq, Tk // tk),
        in_specs=[qspec, kspec, kspec, kspec, lspec, lspec, lspec, lspec,
                  pl.BlockSpec((1, HP), lambda b, h, i, j: (0, 0))],
        out_specs=qspec,
        out_shape=jax.ShapeDtypeStruct((B, H, Tq, HP), F32),
        scratch_shapes=stat + stat,
        compiler_params=_cp("parallel", "parallel", "parallel", "arbitrary"),
        name="diff_attention",
    )(q, k1, k2, v, lp['df_lq1'].reshape(1, DF_QK), lp['df_lk1'].reshape(1, DF_QK),
      lp['df_lq2'].reshape(1, DF_QK), lp['df_lk2'].reshape(1, DF_QK), sub)


def _merge_kernel(x_ref, sh_ref, sc_ref, gt_ref, gpre_ref, gpost_ref, orw, ohy, omla, odf,
                  wg, bg, urw, uhy, umla, udf, wo, o_ref):
    x = x_ref[...]
    h = (_rms(x, gpre_ref[...]) * (1.0 + sc_ref[...]) + sh_ref[...]).astype(BF16)
    ups = (_dot1(orw[...], urw[...]),
           _dot1(ohy[...], uhy[...]),
           sum(_dot1(omla[hh], umla[hh]) for hh in range(N_HEADS)),
           sum(_dot1(odf[hh], udf[hh]) for hh in range(N_HEADS)))
    acc = None
    for n in range(4):
        t = _sigmoid(_dot(h, wg[n]) + bg[n]) * ups[n]
        acc = t if acc is None else acc + t
    y = _dot(acc.astype(BF16), wo[...])
    o_ref[...] = x + gt_ref[...] * _rms(y, gpost_ref[...])


def _pad_head_rows(w):
    return jnp.pad(w.reshape(N_HEADS, HEAD_DIM, -1), ((0, 0), (0, HP - HEAD_DIM), (0, 0)))


def _merge(x, mod3, g_pre, g_post, outs, mw):
    B, T, D = x.shape
    o_rw, o_hy, o_mla, o_df = outs
    tm = _tile(T, 512)
    xspec = pl.BlockSpec((None, tm, D), lambda b, i: (b, i, 0))
    bspec = pl.BlockSpec((None, tm, BR), lambda b, i: (b, i, 0))
    hspec = pl.BlockSpec((None, N_HEADS, tm, HP), lambda b, i: (b, 0, i, 0))

    def full(a):
        nd = a.ndim
        return pl.BlockSpec(a.shape, lambda b, i: (0,) * nd)

    ws = [mw['wg'], mw['bg'], mw['urw'], mw['uhy'], mw['umla'], mw['udf'], mw['wo']]
    return pl.pallas_call(
        _merge_kernel,
        grid=(B, T // tm),
        in_specs=[xspec, _mod_spec(mod3[0], 2), _mod_spec(mod3[1], 2), _mod_spec(mod3[2], 2),
                  _row_spec(D), _row_spec(D), bspec, bspec, hspec, hspec] + [full(a) for a in ws],
        out_specs=xspec,
        out_shape=jax.ShapeDtypeStruct((B, T, D), F32),
        compiler_params=_cp("parallel", "parallel"),
        name="gated_merge",
    )(x, mod3[0], mod3[1], mod3[2], g_pre.reshape(1, D), g_post.reshape(1, D), o_rw, o_hy, o_mla, o_df, *ws)


def _mix_weights(w_in):
    src, sign = _rot_index()
    w_rw = w_in[:, :RW_COLS]
    w_hy = w_in[:, OFF_HY:OFF_HY + HY_COLS]
    kpe0 = OFF_MLA + MLA_Q_RANK + MLA_KV_RANK
    mla_idx = (list(OFF_MLA + np.arange(MLA_Q_RANK)) + [-1] * (256 - MLA_Q_RANK)
               + list(OFF_MLA + MLA_Q_RANK + np.arange(MLA_KV_RANK))
               + list(kpe0 + np.arange(MLA_ROPE)) + list(kpe0 + src) + [-1] * 64)
    mla_sign = [1.0] * 256 + [1.0] * MLA_KV_RANK + [1.0] * MLA_ROPE + list(sign) + [0.0] * 64
    w_mla = _take_cols(w_in, mla_idx, mla_sign)
    pad = [-1] * (HP - 2 * DF_QK)
    src2 = np.concatenate([src, DF_QK + src])
    sign2 = np.concatenate([sign, sign])
    idx, sg = [], []
    for base, rot in ((OFF_DF, False), (OFF_DF, True), (OFF_DF + 256, False), (OFF_DF + 256, True),
                      (OFF_DF + 512, False)):
        for h in range(N_HEADS):
            cols = base + h * 64 + (src2 if rot else np.arange(64))
            idx += list(cols) + pad
            sg += (list(sign2) if rot else [1.0] * 64) + [0.0] * len(pad)
    w_df = _take_cols(w_in, idx, sg)
    return [w.astype(BF16) for w in (w_rw, w_hy, w_mla, w_df)]


def _merge_weights(lp):
    return dict(wg=lp['w_gate'].astype(BF16), bg=lp['b_gate'].reshape(4, 1, D_MODEL),
                urw=lp['w_up'][0].astype(BF16), uhy=lp['w_up'][1].astype(BF16),
                umla=_pad_head_rows(lp['w_up'][2]).astype(BF16), udf=_pad_head_rows(lp['w_up'][3]).astype(BF16),
                wo=lp['w_o'].astype(BF16))


def _token_mixing(x, xc, mod, modc, g_pre, lam_init, with_ctx_out, lp, rope):
    B, T, _ = x.shape
    ws = _mix_weights(lp['w_in'])
    p_rw, p_hy, p_mla, p_df = _project(x, mod[0], mod[1], g_pre, ws)
    c_rw, c_hy, c_mla, c_df = _project(xc, modc[0], modc[1], g_pre, ws)

    prep = _rwkv_prepare(p_rw, lp)
    prep_c = _rwkv_prepare(c_rw, lp)
    s0 = jnp.zeros((B, 2, N_HEADS, HEAD_DIM, HEAD_DIM), F32)
    yf_c, yb_c, s_c = _rwkv_scan(prep_c, s0)
    yf, yb, _ = _rwkv_scan(prep, s_c)
    o_rw = _rwkv_output(yf, yb, prep, lp)

    o_hy = _hyena(p_hy, lp)

    cos, sin = rope
    Tc = xc.shape[1]
    ones, zeros = jnp.ones((Tc, MLA_ROPE), F32), jnp.zeros((Tc, MLA_ROPE), F32)
    mla_w = _mla_weights(lp)
    q, k, v = _mla_prep(p_mla, _mla_tables(cos, sin), mla_w)
    qc, kc, vc = _mla_prep(c_mla, _mla_tables(ones, zeros), mla_w)
    o_mla = _mla_attention(q, jnp.concatenate([k, kc], 2), jnp.concatenate([v, vc], 2))

    dq, dk1, dk2, dv = _df_prep(p_df, _df_tables(cos, sin))
    dqc, dk1c, dk2c, dvc = _df_prep(c_df, _df_tables(ones, zeros))
    o_df = _df_attention(dq, jnp.concatenate([dk1, dk1c], 2), jnp.concatenate([dk2, dk2c], 2),
                         jnp.concatenate([dv, dvc], 2), lp, lam_init)

    outs = (o_rw, o_hy, o_mla, o_df)
    if not with_ctx_out:
        return outs, None
    oc_rw = _rwkv_output(yf_c, yb_c, prep_c, lp)
    oc_hy = _hyena(c_hy, lp)
    oc_mla = _mla_attention(qc, kc, vc)
    oc_df = _df_attention(dqc, dk1c, dk2c, dvc, lp, lam_init)
    return outs, (oc_rw, oc_hy, oc_mla, oc_df)


def kernel(x, c, ctx, c_ctx, w_mod, b_mod, norm_pre, norm_post, ffn_w_in, ffn_w_out, w_in, rw_mu, rw_w0, rw_w2, rw_a0, rw_a2, rw_g2, rw_kk, rw_ka, rw_rk, rw_ln_w, rw_ln_b, hy_conv_w, hy_conv_b, hy_w1, hy_b1, hy_f1, hy_w2, hy_b2, hy_f2, hy_w3, hy_bias, mla_q_norm, mla_wq_b, mla_kv_norm, mla_wkv_b, df_lq1, df_lk1, df_lq2, df_lk2, df_subln, w_up, w_gate, b_gate, w_o):
    params = dict(w_in=w_in, rw_mu=rw_mu, rw_w0=rw_w0, rw_w2=rw_w2, rw_a0=rw_a0, rw_a2=rw_a2, rw_g2=rw_g2,
                  rw_kk=rw_kk, rw_ka=rw_ka, rw_rk=rw_rk, rw_ln_w=rw_ln_w, rw_ln_b=rw_ln_b, hy_conv_w=hy_conv_w,
                  hy_conv_b=hy_conv_b, hy_w1=hy_w1, hy_b1=hy_b1, hy_f1=hy_f1, hy_w2=hy_w2, hy_b2=hy_b2,
                  hy_f2=hy_f2, hy_w3=hy_w3, hy_bias=hy_bias, mla_q_norm=mla_q_norm, mla_wq_b=mla_wq_b,
                  mla_kv_norm=mla_kv_norm, mla_wkv_b=mla_wkv_b, df_lq1=df_lq1, df_lk1=df_lk1, df_lq2=df_lq2,
                  df_lk2=df_lk2, df_subln=df_subln, w_up=w_up, w_gate=w_gate, b_gate=b_gate, w_o=w_o)
    B, T, D = x.shape
    depth = w_mod.shape[0]
    assert B <= 7 and D == D_MODEL and T % 128 == 0 and ctx.shape[1] % CHUNK == 0
    rope = _rope_tables(T)

    s_rows = jnp.concatenate([c, c_ctx[None], jnp.zeros((8 - B - 1, D), F32)], 0)
    mod_all = _modulation(s_rows, w_mod, b_mod).reshape(depth, 8, 3, 3, D)
    ffn_in = ffn_w_in.astype(BF16)
    ffn_out = ffn_w_out.astype(BF16)

    xc = ctx
    for l in range(depth):
        last = l == depth - 1
        lp = {k_: v_[l] for k_, v_ in params.items()}
        mod = [[mod_all[l, :B, s, m][:, None, :] for m in range(3)] for s in range(3)]
        modc = [[mod_all[l, B:B + 1, s, m][:, None, :] for m in range(3)] for s in range(3)]
        ffn_a = (norm_pre[l, 0], norm_post[l, 0], ffn_in[l, 0], ffn_out[l, 0])
        ffn_b = (norm_pre[l, 2], norm_post[l, 2], ffn_in[l, 1], ffn_out[l, 1])

        x = _ffn(x, mod[0], *ffn_a)
        xc = _ffn(xc, modc[0], *ffn_a)

        lam_init = 0.8 - 0.6 * math.exp(-0.3 * l)
        outs, outs_c = _token_mixing(x, xc, mod[1], modc[1], norm_pre[l, 1], lam_init, not last, lp, rope)
        mw = _merge_weights(lp)
        x = _merge(x, mod[1], norm_pre[l, 1], norm_post[l, 1], outs, mw)
        x = _ffn(x, mod[2], *ffn_b)
        if not last:
            xc = _merge(xc, modc[1], norm_pre[l, 1], norm_post[l, 1], outs_c, mw)
            xc = _ffn(xc, modc[2], *ffn_b)
    return x
```

```python
import functools
import math

import numpy as np
import jax
import jax.numpy as jnp
from jax import lax
from jax.experimental import pallas as pl
from jax.experimental.pallas import tpu as pltpu

F32 = jnp.float32
BF16 = jnp.bfloat16

D_MODEL = 1024
GRID_W = 64
N_HEADS = 4
HEAD_DIM = 64
BR = N_HEADS * HEAD_DIM
D_FF = 2816
MACARON_W = 0.5
EPS = 1e-6
ROPE_BASE = 10000.0
RW_LN_EPS = 64e-5
RW_COLS = 1152
HY_COLS = 768
HY_EMB = 33
HY_FAST_DECAY = 0.3
HY_SLOW_DECAY = 1.5
HY_TARGET = 1e-2
MLA_Q_RANK = 192
MLA_KV_RANK = 128
MLA_NOPE = 64
MLA_ROPE = 32
DF_QK = 32
DF_V = 64
DF_SUBLN_EPS = 1e-5
OFF_HY = RW_COLS
OFF_MLA = RW_COLS + HY_COLS
OFF_DF = OFF_MLA + MLA_Q_RANK + MLA_KV_RANK + MLA_ROPE

LANES = 128
HP = 128
ONES_LANE = 64
LOG2E = 1.4426950408889634
ATTN_TQ = 256
CHUNK = 64
RW_P_GRAM = 1
RW_P_INV = (3, 3)
RW_P_STATE = 3
RW_P_APPLY = 1
VMEM_LIMIT = 56 * 1024 * 1024


def _cp(*sem):
    return pltpu.CompilerParams(dimension_semantics=sem, vmem_limit_bytes=VMEM_LIMIT)


def _tile(n, pref, mult=8):
    if n <= pref:
        return n
    t = (pref // mult) * mult
    while t > mult and n % t:
        t -= mult
    assert n % t == 0, (n, pref)
    return t


def _dot(a, b, dims=None):
    if dims is None:
        dims = (((a.ndim - 1,), (0,)), ((), ()))
    return lax.dot_general(a, b, dims, preferred_element_type=F32)


NT = (((1,), (1,)), ((), ()))
TN = (((0,), (0,)), ((), ()))


def _split(x):
    hi = x.astype(BF16)
    lo = (x - hi.astype(F32)).astype(BF16)
    return hi, lo


def _dot3(a, b, dims=None):
    ah, al = _split(a)
    bh, bl = _split(b)
    return _dot(ah, bh, dims) + (_dot(ah, bl, dims) + _dot(al, bh, dims))


def _dot1(a, b, dims=None):
    return _dot(a.astype(BF16), b.astype(BF16), dims)


def _dotc(ch, cl, x):
    xh, xl = _split(x)
    return _dot(ch, xh) + (_dot(ch, xl) + _dot(cl, xh))


def _dot_exact_lhs(m, x):
    x1 = x.astype(BF16)
    r1 = x - x1.astype(F32)
    x2 = r1.astype(BF16)
    x3 = (r1 - x2.astype(F32)).astype(BF16)
    return _dot(m, x1) + (_dot(m, x2) + _dot(m, x3))


def _rms(x, g, eps=EPS):
    return x * lax.rsqrt(jnp.mean(x * x, axis=-1, keepdims=True) + eps) * g


def _sigmoid(x):
    return 1.0 / (1.0 + jnp.exp(-x))


def _silu(x):
    return x * _sigmoid(x)


def _softplus(z):
    return jnp.maximum(z, 0.0) + jnp.log(1.0 + jnp.exp(-jnp.abs(z)))


def _np_split(a):
    a = jnp.asarray(a, F32)
    hi = a.astype(BF16)
    lo = (a - hi.astype(F32)).astype(BF16)
    return hi, lo


def _mod_kernel(s_ref, w_ref, b_ref, o_ref):
    s = _silu(s_ref[...])
    o_ref[...] = _dot3(s, w_ref[...]) + b_ref[...]


def _modulation(s_rows, w_mod, b_mod):
    L, D, NM = w_mod.shape
    tn = _tile(NM, 1152, LANES)
    return pl.pallas_call(
        _mod_kernel,
        grid=(L, NM // tn),
        in_specs=[pl.BlockSpec((8, D), lambda l, j: (0, 0)),
                  pl.BlockSpec((None, D, tn), lambda l, j: (l, 0, j)),
                  pl.BlockSpec((None, 1, tn), lambda l, j: (l, 0, j))],
        out_specs=pl.BlockSpec((None, 8, tn), lambda l, j: (l, 0, j)),
        out_shape=jax.ShapeDtypeStruct((L, 8, NM), F32),
        compiler_params=_cp("parallel", "parallel"),
        name="modulation",
    )(s_rows, w_mod, b_mod.reshape(L, 1, NM))


def _mod_spec(m, nargs):
    if m.shape[0] == 1:
        return pl.BlockSpec((None, 1, D_MODEL), lambda *a: (0, 0, 0))
    return pl.BlockSpec((None, 1, D_MODEL), lambda *a: (a[0], 0, 0))


def _row_spec(n):
    return pl.BlockSpec((1, n), lambda *a: (0, 0))


def _ffn_kernel(x_ref, sh_ref, sc_ref, gt_ref, gpre_ref, gpost_ref, wa_ref, wb_ref, wo_ref, o_ref, h_sc, acc_sc):
    f = pl.program_id(2)

    @pl.when(f == 0)
    def _():
        h = _rms(x_ref[...], gpre_ref[...]) * (1.0 + sc_ref[...]) + sh_ref[...]
        h_sc[...] = h.astype(BF16)
        acc_sc[...] = jnp.zeros_like(acc_sc)

    h = h_sc[...]
    a = _dot(h, wa_ref[...])
    b = _dot(h, wb_ref[...])
    acc_sc[...] += _dot((_silu(a) * b).astype(BF16), wo_ref[...])

    @pl.when(f == pl.num_programs(2) - 1)
    def _():
        o_ref[...] = x_ref[...] + MACARON_W * gt_ref[...] * _rms(acc_sc[...], gpost_ref[...])


def _ffn(x, mod3, g_pre, g_post, w_in, w_out):
    B, T, D = x.shape
    F = w_out.shape[0]
    tm = _tile(T, 1024)
    tf = _tile(F, 256, LANES)
    nf = F // tf
    xspec = pl.BlockSpec((None, tm, D), lambda b, i, f: (b, i, 0))
    return pl.pallas_call(
        _ffn_kernel,
        grid=(B, T // tm, nf),
        in_specs=[xspec, _mod_spec(mod3[0], 3), _mod_spec(mod3[1], 3), _mod_spec(mod3[2], 3),
                  _row_spec(D), _row_spec(D),
                  pl.BlockSpec((D, tf), lambda b, i, f: (0, f)),
                  pl.BlockSpec((D, tf), lambda b, i, f: (0, nf + f)),
                  pl.BlockSpec((tf, D), lambda b, i, f: (f, 0))],
        out_specs=xspec,
        out_shape=jax.ShapeDtypeStruct((B, T, D), F32),
        scratch_shapes=[pltpu.VMEM((tm, D), BF16), pltpu.VMEM((tm, D), F32)],
        compiler_params=_cp("parallel", "parallel", "arbitrary"),
        name="ffn",
    )(x, mod3[0], mod3[1], mod3[2], g_pre.reshape(1, D), g_post.reshape(1, D), w_in, w_in, w_out)


def _proj_kernel(x_ref, sh_ref, sc_ref, gpre_ref, wrw, why, wmla, wdf, orw, ohy, omla, odf):
    h = (_rms(x_ref[...], gpre_ref[...]) * (1.0 + sc_ref[...]) + sh_ref[...]).astype(BF16)
    orw[...] = _dot(h, wrw[...])
    ohy[...] = _dot(h, why[...])
    omla[...] = _dot(h, wmla[...])
    odf[...] = _dot(h, wdf[...])


def _project(x, shift, scale, g_pre, ws):
    B, T, D = x.shape
    tm = _tile(T, 256)
    xspec = pl.BlockSpec((None, tm, D), lambda b, i: (b, i, 0))
    wspecs = [pl.BlockSpec(w.shape, lambda b, i: (0, 0)) for w in ws]
    ospecs = [pl.BlockSpec((None, tm, w.shape[1]), lambda b, i: (b, i, 0)) for w in ws]
    oshapes = [jax.ShapeDtypeStruct((B, T, w.shape[1]), F32) for w in ws]
    return pl.pallas_call(
        _proj_kernel,
        grid=(B, T // tm),
        in_specs=[xspec, _mod_spec(shift, 2), _mod_spec(scale, 2), _row_spec(D)] + wspecs,
        out_specs=ospecs,
        out_shape=oshapes,
        compiler_params=_cp("parallel", "parallel"),
        name="mix_project",
    )(x, shift, scale, g_pre.reshape(1, D), *ws)


def _halo_specs(tm, T, C):
    nb8 = T // 8
    r = tm // 8
    main = pl.BlockSpec((None, tm, C), lambda b, i: (b, i, 0))
    prev = pl.BlockSpec((None, 8, C), lambda b, i: (b, jnp.maximum(i * r - 1, 0), 0))
    nxt = pl.BlockSpec((None, 8, C), lambda b, i: (b, jnp.minimum((i + 1) * r, nb8 - 1), 0))
    return main, prev, nxt


def _conv3_tile(p, prev8, next8, w0, w1, w2):
    i = pl.program_id(1)
    n_i = pl.num_programs(1)
    tm = p.shape[0]
    row = lax.broadcasted_iota(jnp.int32, p.shape, 0)
    prev_row = jnp.where(i > 0, prev8[7:8, :], 0.0)
    next_row = jnp.where(i < n_i - 1, next8[0:1, :], 0.0)
    pm = jnp.where(row == 0, prev_row, pltpu.roll(p, 1, 0))
    pp = jnp.where(row == tm - 1, next_row, pltpu.roll(p, tm - 1, 0))
    return pm * w0 + p * w1 + pp * w2


def _rwkv_prep_kernel(p_ref, pv_ref, nx_ref, mu_ref, w0_ref, w2_ref, a0_ref, a2_ref, g2_ref, kk_ref, ka_ref,
                      rk_ref, bd_ref, trif_ref, trib_ref, onec_ref, v_o, g_o, bonus_o, *dir_outs):
    mu = mu_ref[...]
    p = _conv3_tile(p_ref[...], pv_ref[...], nx_ref[...], 0.5 * mu, 1.0 - mu, 0.5 * mu)
    r = p[:, 0:BR]
    k = p[:, BR:2 * BR]
    v = p[:, 2 * BR:3 * BR]
    g_a = p[:, 3 * BR:3 * BR + 128]
    wa = p[:, 896:1024]
    aa = p[:, 1024:1152]
    bd = bd_ref[...]
    kkraw = k * kk_ref[...]
    ss = _dot_exact_lhs_rhs(kkraw * kkraw, bd)
    kk = kkraw / jnp.maximum(jnp.sqrt(ss), 1e-12)
    wl = w0_ref[...] + _dot3(jnp.tanh(wa), w2_ref[...])
    lw = -jnp.exp(-_softplus(-wl) - 0.5)
    a = _sigmoid(a0_ref[...] + _dot3(aa, a2_ref[...]))
    a_f = a[:, :BR]
    a_b = a[:, BR:]
    ka = ka_ref[...]
    k_f = k * (1.0 + (a_f - 1.0) * ka)
    k_b = k * (1.0 + (a_b - 1.0) * ka)
    v_o[...] = v
    g_o[...] = _dot1(_sigmoid(g_a), g2_ref[...])
    bonus_o[...] = _dot_exact_lhs_rhs(r * (k_f + k_b) * rk_ref[...], bd) * v
    onec = onec_ref[...]
    for d, (tri_ref, k_d, a_d) in enumerate(((trif_ref, k_f, a_f), (trib_ref, k_b, a_b))):
        rt_o, at_o, bt_o, kt_o, bh_o, kh_o, eg_o = dir_outs[7 * d:7 * d + 7]
        lw_d = lw[:, d * BR:(d + 1) * BR]
        b_d = kk * a_d
        g = _dot_exact_lhs(tri_ref[...], lw_d)
        gtot = _dot_exact_lhs(onec, lw_d)
        eng = jnp.exp(-g)
        et = jnp.exp(gtot - g)
        rt_o[...] = r * jnp.exp(g)
        at_o[...] = -kk * jnp.exp(g - lw_d)
        bt_o[...] = b_d * eng
        kt_o[...] = k_d * eng
        bh_o[...] = b_d * et
        kh_o[...] = k_d * et
        eg_o[...] = jnp.exp(gtot)


def _dot_exact_lhs_rhs(x, m):
    x1 = x.astype(BF16)
    r1 = x - x1.astype(F32)
    x2 = r1.astype(BF16)
    x3 = (r1 - x2.astype(F32)).astype(BF16)
    return _dot(x1, m) + (_dot(x2, m) + _dot(x3, m))


def _block_diag_ones():
    h = np.arange(BR) // HEAD_DIM
    return jnp.asarray((h[:, None] == h[None, :]).astype(np.float32), BF16)


def _rwkv_prepare(p_rw, lp):
    B, T, C = p_rw.shape
    tm = _tile(T, 256)
    main, prev, nxt = _halo_specs(tm, T, C)
    z = jnp.zeros((64, BR), F32)
    w2 = jnp.concatenate([jnp.concatenate([lp['rw_w2'][0], z], 1), jnp.concatenate([z, lp['rw_w2'][1]], 1)], 0)
    a2 = jnp.concatenate([jnp.concatenate([lp['rw_a2'][0], z], 1), jnp.concatenate([z, lp['rw_a2'][1]], 1)], 0)
    params = [lp['rw_mu'].reshape(1, C), lp['rw_w0'].reshape(1, 2 * BR), w2, lp['rw_a0'].reshape(1, 2 * BR), a2,
              lp['rw_g2'].astype(BF16), lp['rw_kk'].reshape(1, BR), lp['rw_ka'].reshape(1, BR),
              lp['rw_rk'].reshape(1, BR), _block_diag_ones()]
    pos = np.arange(tm)
    same = (pos[:, None] // CHUNK) == (pos[None, :] // CHUNK)
    params += [jnp.asarray((same & (pos[None, :] <= pos[:, None])).astype(np.float32), BF16),
               jnp.asarray((same & (pos[None, :] >= pos[:, None])).astype(np.float32), BF16),
               jnp.asarray(same.astype(np.float32), BF16)]
    pspecs = [pl.BlockSpec(a.shape, lambda b, i: (0, 0)) for a in params]
    ospec = pl.BlockSpec((None, tm, BR), lambda b, i: (b, i, 0))
    names = ('v', 'g', 'bonus') + tuple(n + str(d) for d in (0, 1) for n in ('rt', 'at', 'bt', 'kt', 'bh', 'kh', 'eg'))
    outs = pl.pallas_call(
        _rwkv_prep_kernel,
        grid=(B, T // tm),
        in_specs=[main, prev, nxt] + pspecs,
        out_specs=[ospec] * len(names),
        out_shape=[jax.ShapeDtypeStruct((B, T, BR), F32)] * len(names),
        compiler_params=_cp("parallel", "parallel"),
        name="rwkv_prepare",
    )(p_rw, p_rw, p_rw, *params)
    return dict(zip(names, outs))


def _dot3s(a, b, dims=None):
    if dims is None:
        dims = (((1,), (0,)), ((), ()))
    ca, cb = dims[0][0][0], dims[0][1][0]
    a_lo = a - a.astype(BF16).astype(F32)
    b_lo = b - b.astype(BF16).astype(F32)
    sa = jnp.concatenate([a, a_lo, a], axis=ca).astype(BF16)
    sb = jnp.concatenate([b, b, b_lo], axis=cb).astype(BF16)
    return _dot(sa, sb, dims)


def _mmp(a, b, dims, passes):
    return _dot3s(a, b, dims) if passes == 3 else _dot1(a, b, dims)


def _rwkv_chunks(chains, eye):
    n = range(len(chains))
    rt, at, bt, kt, bh, kh, eg, v, ht, mask_s, mask_i = zip(*chains)
    C = rt[0].shape[0]
    gm = [_mmp(jnp.concatenate([at[i], rt[i]], 0), jnp.concatenate([bt[i], kt[i]], 0), NT, RW_P_GRAM) for i in n]
    top = [jnp.where(mask_s[i], gm[i][:C], 0.0) for i in n]
    bot = [jnp.where(mask_i[i], gm[i][C:], 0.0) for i in n]
    pw = [top[i][:, :C] for i in n]
    tinv = [eye + pw[i] for i in n]
    for _ in range(int(math.log2(C)) - 1):
        pw = [_mmp(pw[i], pw[i], None, RW_P_INV[0]) for i in n]
        upd = [_mmp(tinv[i], pw[i], None, RW_P_INV[1]) for i in n]
        tinv = [tinv[i] + upd[i] for i in n]
    w0 = [_mmp(at[i], ht[i], NT, RW_P_STATE) for i in n]
    w1 = [_mmp(top[i][:, C:], v[i], None, RW_P_APPLY) for i in n]
    u = [_mmp(tinv[i], w0[i] + w1[i], None, RW_P_STATE) for i in n]
    uv = [jnp.concatenate([u[i], v[i]], 0) for i in n]
    y0 = [_mmp(rt[i], ht[i], NT, RW_P_STATE) for i in n]
    y1 = [_mmp(bot[i], uv[i], None, RW_P_APPLY) for i in n]
    hu = [_mmp(uv[i], jnp.concatenate([bh[i], kh[i]], 0), TN, RW_P_STATE) for i in n]
    return [(y0[i] + y1[i], ht[i] * eg[i] + hu[i]) for i in n]


def _rwkv_scan_kernel(*refs):
    fwd_in, bwd_in = refs[0:8], refs[8:16]
    s0_ref, yf_o, yb_o, s_o, st = refs[16:]
    c = pl.program_id(1)
    C = yf_o.shape[0]

    @pl.when(c == 0)
    def _():
        st[...] = s0_ref[...]

    row = lax.broadcasted_iota(jnp.int32, (C, 2 * C), 0)
    col = lax.broadcasted_iota(jnp.int32, (C, 2 * C), 1)
    col = jnp.where(col >= C, col - C, col)
    eye = (lax.broadcasted_iota(jnp.int32, (C, C), 0) == lax.broadcasted_iota(jnp.int32, (C, C), 1)).astype(F32)
    chains = []
    for d, ins in enumerate((fwd_in, bwd_in)):
        if d == 0:
            mask_i = col <= row
            mask_s = col < row
        else:
            mask_i = col >= row
            mask_s = col > row
        for h in range(N_HEADS):
            sl = pl.ds(h * HEAD_DIM, HEAD_DIM)
            rt, at, bt, kt, bh, kh, eg, v = (ref[:, sl] for ref in ins)
            chains.append((rt, at, bt, kt, bh, kh, eg[0:1, :], v, st[d, h], mask_s, mask_i))
    results = _rwkv_chunks(chains, eye)
    for d, y_o in enumerate((yf_o, yb_o)):
        for h in range(N_HEADS):
            y, hn = results[d * N_HEADS + h]
            y_o[:, pl.ds(h * HEAD_DIM, HEAD_DIM)] = y
            st[d, h] = hn

    @pl.when(c == pl.num_programs(1) - 1)
    def _():
        s_o[...] = st[...]


def _rwkv_scan(prep, s0):
    B, T, _ = prep['v'].shape
    nc = T // CHUNK
    fwd = pl.BlockSpec((None, CHUNK, BR), lambda b, c: (b, c, 0))
    bwd = pl.BlockSpec((None, CHUNK, BR), lambda b, c: (b, nc - 1 - c, 0))
    sspec = pl.BlockSpec((None, 2, N_HEADS, HEAD_DIM, HEAD_DIM), lambda b, c: (b, 0, 0, 0, 0))
    ins = [prep[n + str(d)] for d in (0, 1) for n in ('rt', 'at', 'bt', 'kt', 'bh', 'kh', 'eg')]
    return pl.pallas_call(
        _rwkv_scan_kernel,
        grid=(B, nc),
        in_specs=[fwd] * 8 + [bwd] * 8 + [sspec],
        out_specs=[fwd, bwd, sspec],
        out_shape=[jax.ShapeDtypeStruct((B, T, BR), F32), jax.ShapeDtypeStruct((B, T, BR), F32),
                   jax.ShapeDtypeStruct(s0.shape, F32)],
        scratch_shapes=[pltpu.VMEM((2, N_HEADS, HEAD_DIM, HEAD_DIM), F32)],
        compiler_params=_cp("parallel", "arbitrary"),
        name="rwkv_scan",
    )(*ins[:7], prep['v'], *ins[7:], prep['v'], s0)


def _rwkv_out_kernel(yf, yb, g, bonus, lnw, lnb, bd, o):
    y = yf[...] + yb[...]
    m = bd[...]
    inv_n = 1.0 / HEAD_DIM
    mean = _dot_exact_lhs_rhs(y, m) * inv_n
    yc = y - mean
    var = _dot_exact_lhs_rhs(yc * yc, m) * inv_n
    yn = yc * lax.rsqrt(var + RW_LN_EPS) * lnw[...] + lnb[...]
    o[...] = (yn + bonus[...]) * g[...]


def _rwkv_output(yf, yb, prep, lp):
    B, T, _ = yf.shape
    tm = _tile(T, 512)
    spec = pl.BlockSpec((None, tm, BR), lambda b, i: (b, i, 0))
    return pl.pallas_call(
        _rwkv_out_kernel,
        grid=(B, T // tm),
        in_specs=[spec] * 4 + [_row_spec(BR), _row_spec(BR), pl.BlockSpec((BR, BR), lambda b, i: (0, 0))],
        out_specs=spec,
        out_shape=jax.ShapeDtypeStruct((B, T, BR), F32),
        compiler_params=_cp("parallel", "parallel"),
        name="rwkv_output",
    )(yf, yb, prep['g'], prep['bonus'], lp['rw_ln_w'].reshape(1, BR), lp['rw_ln_b'].reshape(1, BR),
      _block_diag_ones())


def _hy_pre_kernel(p_ref, pv_ref, nx_ref, w_ref, b_ref, x0_o, z_o):
    w = w_ref[...]
    u = _conv3_tile(p_ref[...], pv_ref[...], nx_ref[...], w[0:1, :], w[1:2, :], w[2:3, :]) + b_ref[...]
    x0_o[...] = u[:, 0:BR]
    z_o[...] = u[:, 2 * BR:3 * BR] * u[:, BR:2 * BR]


def _hyena_pre(p_hy, lp):
    B, T, C = p_hy.shape
    tm = _tile(T, 512)
    main, prev, nxt = _halo_specs(tm, T, C)
    ospec = pl.BlockSpec((None, tm, BR), lambda b, i: (b, i, 0))
    return pl.pallas_call(
        _hy_pre_kernel,
        grid=(B, T // tm),
        in_specs=[main, prev, nxt, pl.BlockSpec((3, C), lambda b, i: (0, 0)), _row_spec(C)],
        out_specs=[ospec, ospec],
        out_shape=[jax.ShapeDtypeStruct((B, T, BR), F32)] * 2,
        compiler_params=_cp("parallel", "parallel"),
        name="hyena_pre",
    )(p_hy, p_hy, p_hy, lp['hy_conv_w'], lp['hy_conv_b'].reshape(1, C))


def _hy_filter_kernel(z_ref, t_ref, mf_ref, mb_ref, w1, b1, f1, w2, b2, f2, w3, dl, k_o, ss_o):
    h = jnp.sin(f1[...] * (_dot3(z_ref[...], w1[...]) + b1[...]))
    h = jnp.sin(f2[...] * (_dot3(h, w2[...]) + b2[...]))
    h = _dot3(h, w3[...]) * jnp.exp(-t_ref[...] * dl[...])
    kern = mf_ref[...] * h[:, :BR] + mb_ref[...] * h[:, BR:]
    k_o[...] = kern

    @pl.when(pl.program_id(0) == 0)
    def _():
        ss_o[...] = jnp.zeros_like(ss_o)

    ss_o[...] += jnp.sum(kern * kern, axis=0, keepdims=True)


def _hyena_filter(n, lp):
    t = jnp.linspace(0.0, 1.0, n, dtype=F32)[:, None]
    bands = (HY_EMB - 1) // 2
    wpos = 2.0 * math.pi * jnp.arange(n, dtype=F32) / n
    fr = jnp.linspace(1e-4, bands - 1, bands, dtype=F32)
    ang = wpos[:, None] * fr[None]
    z = jnp.concatenate([t, jnp.cos(ang), -jnp.sin(ang)], axis=-1)
    rho = np.arange(2 * n)
    pos = np.where(rho < n, rho, np.clip(2 * n - 1 - rho, 0, n - 1))
    z_ext = jnp.pad(z[pos], ((0, 0), (0, LANES - HY_EMB)))
    t_ext = t[pos]
    mf = jnp.asarray((rho < n).astype(np.float32))[:, None]
    mb = jnp.asarray((rho > n).astype(np.float32))[:, None]
    max_decay = math.log(HY_TARGET) / HY_FAST_DECAY
    min_decay = math.log(HY_TARGET) / HY_SLOW_DECAY
    deltas = jnp.abs(jnp.linspace(min_decay, max_decay, BR, dtype=F32))
    dl = jnp.tile(deltas, 2)[None]
    w1 = jnp.pad(lp['hy_w1'], ((0, LANES - HY_EMB), (0, 0)))
    params = [w1, lp['hy_b1'][None], lp['hy_f1'][None], lp['hy_w2'], lp['hy_b2'][None], lp['hy_f2'][None],
              lp['hy_w3'], dl]
    tm = _tile(2 * n, 512)
    return pl.pallas_call(
        _hy_filter_kernel,
        grid=(2 * n // tm,),
        in_specs=[pl.BlockSpec((tm, LANES), lambda i: (i, 0))] + [pl.BlockSpec((tm, 1), lambda i: (i, 0))] * 3
        + [pl.BlockSpec(a.shape, lambda i: (0, 0)) for a in params],
        out_specs=[pl.BlockSpec((tm, BR), lambda i: (i, 0)), pl.BlockSpec((1, BR), lambda i: (0, 0))],
        out_shape=[jax.ShapeDtypeStruct((2 * n, BR), F32), jax.ShapeDtypeStruct((1, BR), F32)],
        compiler_params=_cp("arbitrary"),
        name="hyena_filter",
    )(z_ext, t_ext, mf, mb, *params)


def _dft_tables(N1, N2):
    N = N1 * N2

    def mat(n):
        a = -2.0 * np.pi * np.outer(np.arange(n), np.arange(n)) / n
        return np.cos(a), np.sin(a)

    f1r, f1i = mat(N1)
    f2r, f2i = mat(N2)
    a = -2.0 * np.pi * np.outer(np.arange(N1), np.arange(N2)) / N
    return dict(f1r=f1r, f1i=f1i, f2r=f2r, f2i=f2i,
                twr=jnp.asarray(np.cos(a)[:, :, None], F32), twi=jnp.asarray(np.sin(a)[:, :, None], F32))


def _fft_in_kernel(x_ref, frh, frl, fih, fil, ar_o, ai_o):
    x = x_ref[...]
    ar_o[...] = _dotc(frh[...], frl[...], x)
    ai_o[...] = _dotc(fih[...], fil[...], x)


def _fft_in(x2d, tabs, N1):
    B, n1u, cols = x2d.shape
    consts = list(_np_split(tabs['f1r'][:, :n1u])) + list(_np_split(tabs['f1i'][:, :n1u]))
    tc = _tile(cols, 8192, LANES)
    ospec = pl.BlockSpec((None, N1, tc), lambda b, j: (b, 0, j))
    return pl.pallas_call(
        _fft_in_kernel,
        grid=(B, cols // tc),
        in_specs=[pl.BlockSpec((None, n1u, tc), lambda b, j: (b, 0, j))]
        + [pl.BlockSpec((N1, n1u), lambda b, j: (0, 0))] * 4,
        out_specs=[ospec, ospec],
        out_shape=[jax.ShapeDtypeStruct((B, N1, cols), F32)] * 2,
        compiler_params=_cp("parallel", "parallel"),
        name="fft_first_axis",
    )(x2d, *consts)


def _cplx_dft(frh, frl, fih, fil, xr, xi, conj):
    rr = _dotc(frh, frl, xr)
    ii = _dotc(fih, fil, xi)
    ri = _dotc(frh, frl, xi)
    ir = _dotc(fih, fil, xr)
    if conj:
        return rr + ii, ri - ir
    return rr - ii, ri + ir


def _fft_spec_kernel(ar, ai, twr, twi, frh, frl, fih, fil, sc, kr_o, ki_o):
    tr, ti = twr[...], twi[...]
    xr = ar[...] * tr - ai[...] * ti
    xi = ar[...] * ti + ai[...] * tr
    br, bi = _cplx_dft(frh[...], frl[...], fih[...], fil[...], xr, xi, False)
    kr_o[...] = br * sc[...]
    ki_o[...] = bi * sc[...]


def _fft_conv_kernel(ar, ai, twr, twi, frh, frl, fih, fil, kr, ki, dr_o, di_o):
    tr, ti = twr[...], twi[...]
    f = (frh[...], frl[...], fih[...], fil[...])
    xr = ar[...] * tr - ai[...] * ti
    xi = ar[...] * ti + ai[...] * tr
    br, bi = _cplx_dft(*f, xr, xi, False)
    cr = br * kr[...] - bi * ki[...]
    ci = br * ki[...] + bi * kr[...]
    dr, di = _cplx_dft(*f, cr, ci, True)
    dr_o[...] = dr * tr + di * ti
    di_o[...] = di * tr - dr * ti


def _fft_mid(a_re, a_im, tabs, kf=None, scale=None):
    B, N1, N2, C = a_re.shape
    consts = list(_np_split(tabs['f2r'])) + list(_np_split(tabs['f2i']))
    aspec = pl.BlockSpec((None, None, N2, C), lambda b, k: (b, k, 0, 0))
    tspec = pl.BlockSpec((None, N2, 1), lambda b, k: (k, 0, 0))
    fspec = pl.BlockSpec((N2, N2), lambda b, k: (0, 0))
    if kf is None:
        extra, especs, kern, nm = [scale], [_row_spec(C)], _fft_spec_kernel, "fft_filter_spectrum"
    else:
        kspec = pl.BlockSpec((None, None, N2, C), lambda b, k: (0, k, 0, 0))
        extra, especs, kern, nm = list(kf), [kspec, kspec], _fft_conv_kernel, "fft_second_axis_conv"
    return pl.pallas_call(
        kern,
        grid=(B, N1),
        in_specs=[aspec, aspec, tspec, tspec] + [fspec] * 4 + especs,
        out_specs=[aspec, aspec],
        out_shape=[jax.ShapeDtypeStruct((B, N1, N2, C), F32)] * 2,
        compiler_params=_cp("parallel", "parallel"),
        name=nm,
    )(a_re, a_im, tabs['twr'], tabs['twi'], *consts, *extra)


def _fft_out_kernel(dr, di, frh, frl, fih, fil, x0, z, bias, o):
    zc = _dotc(frh[...], frl[...], dr[...]) + _dotc(fih[...], fil[...], di[...])
    zz = z[...]
    o[...] = x0[...] * (zc + zz * bias[...])


def _fft_out(d_re, d_im, tabs, x0_2d, z_2d, bias_row):
    B, N1, cols = d_re.shape
    n1u = x0_2d.shape[1]
    consts = list(_np_split(tabs['f1r'][:n1u, :])) + list(_np_split(tabs['f1i'][:n1u, :]))
    tc = _tile(cols, 8192, LANES)
    dspec = pl.BlockSpec((None, N1, tc), lambda b, j: (b, 0, j))
    xspec = pl.BlockSpec((None, n1u, tc), lambda b, j: (b, 0, j))
    return pl.pallas_call(
        _fft_out_kernel,
        grid=(B, cols // tc),
        in_specs=[dspec, dspec] + [pl.BlockSpec((n1u, N1), lambda b, j: (0, 0))] * 4
        + [xspec, xspec, pl.BlockSpec((1, tc), lambda b, j: (0, j))],
        out_specs=xspec,
        out_shape=jax.ShapeDtypeStruct((B, n1u, cols), F32),
        compiler_params=_cp("parallel", "parallel"),
        name="fft_last_axis_gate",
    )(d_re, d_im, *consts, x0_2d, z_2d, bias_row)


def _hy_gate_kernel(zc, x0, z, bias, o):
    o[...] = x0[...] * (zc[...] + z[...] * bias[...])


def _hyena(p_hy, lp):
    B, n, _ = p_hy.shape
    x0, z = _hyena_pre(p_hy, lp)
    kern, ss = _hyena_filter(n, lp)
    N = 2 * n
    N2 = min(256, n)
    N1 = N // N2
    tabs = _dft_tables(N1, N2)
    scale = lax.rsqrt(ss) * (1.0 / N)
    bias = lp['hy_bias'].reshape(1, BR)
    if N1 >= 4:
        n1u = N1 // 2
        k_re, k_im = _fft_in(kern.reshape(1, N1, N2 * BR), tabs, N1)
        kf = _fft_mid(k_re.reshape(1, N1, N2, BR), k_im.reshape(1, N1, N2, BR), tabs, scale=scale)
        a_re, a_im = _fft_in(z.reshape(B, n1u, N2 * BR), tabs, N1)
        d_re, d_im = _fft_mid(a_re.reshape(B, N1, N2, BR), a_im.reshape(B, N1, N2, BR), tabs, kf=kf)
        o = _fft_out(d_re.reshape(B, N1, N2 * BR), d_im.reshape(B, N1, N2 * BR), tabs,
                     x0.reshape(B, n1u, N2 * BR), z.reshape(B, n1u, N2 * BR), jnp.tile(bias, (1, N2)))
        return o.reshape(B, n, BR)
    tabs = _dft_tables(1, N)
    kf = _fft_mid(kern.reshape(1, 1, N, BR), jnp.zeros((1, 1, N, BR), F32), tabs, scale=scale)
    zp = jnp.pad(z, ((0, 0), (0, n), (0, 0))).reshape(B, 1, N, BR)
    d_re, _ = _fft_mid(zp, jnp.zeros_like(zp), tabs, kf=kf)
    zc = d_re.reshape(B, N, BR)[:, :n]
    tm = _tile(n, 512)
    spec = pl.BlockSpec((None, tm, BR), lambda b, i: (b, i, 0))
    return pl.pallas_call(
        _hy_gate_kernel,
        grid=(B, n // tm),
        in_specs=[spec, spec, spec, _row_spec(BR)],
        out_specs=spec,
        out_shape=jax.ShapeDtypeStruct((B, n, BR), F32),
        compiler_params=_cp("parallel", "parallel"),
        name="hyena_gate",
    )(zc, x0, z, bias)


def _rope_tables(n_lat):
    rows = n_lat // GRID_W
    row = jnp.repeat(jnp.arange(rows, dtype=F32), GRID_W)
    col = jnp.tile(jnp.arange(GRID_W, dtype=F32), rows)
    half = MLA_ROPE // 2
    inv = ROPE_BASE ** (-jnp.arange(0, half, 2, dtype=F32) / half)
    ar = row[:, None] * inv[None]
    ac = col[:, None] * inv[None]
    cos = jnp.concatenate([jnp.cos(ar), jnp.cos(ar), jnp.cos(ac), jnp.cos(ac)], axis=-1)
    sin = jnp.concatenate([jnp.sin(ar), jnp.sin(ar), jnp.sin(ac), jnp.sin(ac)], axis=-1)
    return cos, sin


def _rot_index():
    j = np.arange(32)
    first = (j % 16) < 8
    src = np.where(first, j + 8, j - 8)
    sign = np.where(first, -1.0, 1.0)
    return src, sign


def _take_cols(w, idx, sign=None):
    idx = np.asarray(idx)
    s = np.where(idx >= 0, 1.0, 0.0) if sign is None else np.where(idx >= 0, np.asarray(sign), 0.0)
    return jnp.take(w, jnp.asarray(np.maximum(idx, 0)), axis=1) * jnp.asarray(s, w.dtype)[None, :]


def _mla_prep_kernel(p_ref, cq_t, sq_t, ck_t, qn, kvn, wqc, wqs, wk, wv, epe, q_o, k_o, v_o):
    p = p_ref[...]
    cq = p[:, 0:256]
    cqn = (cq * lax.rsqrt(jnp.sum(cq * cq, axis=-1, keepdims=True) * (1.0 / MLA_Q_RANK) + EPS) * qn[...]).astype(BF16)
    ckv = p[:, 256:384]
    ckvn = _rms(ckv, kvn[...]).astype(BF16)
    q = _dot(cqn, wqc[...]) * cq_t[...] + _dot(cqn, wqs[...]) * sq_t[...]
    pe = (p[:, 384:512] * ck_t[...]).astype(BF16)
    k = _dot(ckvn, wk[...]) + _dot(pe, epe[...])
    v = _dot(ckvn, wv[...])
    lane = lax.broadcasted_iota(jnp.int32, (p.shape[0], HP), 1)
    for h in range(N_HEADS):
        q_o[h] = q[:, h * HP:(h + 1) * HP].astype(BF16)
        k_o[h] = k[:, h * HP:(h + 1) * HP].astype(BF16)
        v_o[h] = jnp.where(lane == ONES_LANE, 1.0, v[:, h * HP:(h + 1) * HP]).astype(BF16)


def _mla_weights(lp):
    s = LOG2E / math.sqrt(MLA_NOPE + MLA_ROPE)
    src, sign = _rot_index()
    hd = MLA_NOPE + MLA_ROPE
    qc_idx, qs_idx, qs_sign, k_idx, v_idx = [], [], [], [], []
    for h in range(N_HEADS):
        qc_idx += list(h * hd + np.arange(hd)) + [-1] * (HP - hd)
        qs_idx += [-1] * MLA_NOPE + list(h * hd + MLA_NOPE + src) + [-1] * (HP - hd)
        qs_sign += [0.0] * MLA_NOPE + list(sign) + [0.0] * (HP - hd)
        k_idx += list(h * 128 + np.arange(MLA_NOPE)) + [-1] * (HP - MLA_NOPE)
        v_idx += list(h * 128 + MLA_NOPE + np.arange(64)) + [-1] * (HP - 64)
    wq = jnp.pad(lp['mla_wq_b'] * s, ((0, 256 - MLA_Q_RANK), (0, 0)))
    wqc = _take_cols(wq, qc_idx).astype(BF16)
    wqs = _take_cols(wq, qs_idx, qs_sign).astype(BF16)
    wk = _take_cols(lp['mla_wkv_b'], k_idx).astype(BF16)
    wv = _take_cols(lp['mla_wkv_b'], v_idx).astype(BF16)
    epe = np.zeros((128, N_HEADS * HP), np.float32)
    for h in range(N_HEADS):
        for j in range(MLA_ROPE):
            epe[j, h * HP + MLA_NOPE + j] = 1.0
            epe[MLA_ROPE + j, h * HP + MLA_NOPE + j] = 1.0
    qn = jnp.pad(lp['mla_q_norm'], (0, 256 - MLA_Q_RANK)).reshape(1, 256)
    return qn, lp['mla_kv_norm'].reshape(1, MLA_KV_RANK), wqc, wqs, wk, wv, jnp.asarray(epe, BF16)


def _mla_tables(cos, sin):
    T = cos.shape[0]
    one = jnp.ones((T, MLA_NOPE), F32)
    zero = jnp.zeros((T, HP - MLA_NOPE - MLA_ROPE), F32)
    zn = jnp.zeros((T, MLA_NOPE), F32)
    cq = jnp.tile(jnp.concatenate([one, cos, zero], -1), (1, N_HEADS))
    sq = jnp.tile(jnp.concatenate([zn, sin, zero], -1), (1, N_HEADS))
    ck = jnp.concatenate([cos, sin, jnp.zeros((T, 64), F32)], -1)
    return cq, sq, ck


def _mla_prep(p_mla, tables, weights):
    B, T, C = p_mla.shape
    tm = _tile(T, 256)
    W = N_HEADS * HP
    tspecs = [pl.BlockSpec((tm, W), lambda b, i: (i, 0)), pl.BlockSpec((tm, W), lambda b, i: (i, 0)),
              pl.BlockSpec((tm, 128), lambda b, i: (i, 0))]
    wspecs = [pl.BlockSpec(w.shape, lambda b, i: (0, 0)) for w in weights]
    ospec = pl.BlockSpec((None, N_HEADS, tm, HP), lambda b, i: (b, 0, i, 0))
    return pl.pallas_call(
        _mla_prep_kernel,
        grid=(B, T // tm),
        in_specs=[pl.BlockSpec((None, tm, C), lambda b, i: (b, i, 0))] + tspecs + wspecs,
        out_specs=[ospec] * 3,
        out_shape=[jax.ShapeDtypeStruct((B, N_HEADS, T, HP), BF16)] * 3,
        compiler_params=_cp("parallel", "parallel"),
        name="mla_prepare",
    )(p_mla, *tables, *weights)


def _df_prep_kernel(p_ref, c_t, s_t, q_o, k1_o, k2_o, v_o):
    W = N_HEADS * HP
    p = p_ref[...]
    c = c_t[...]
    s = s_t[...]
    q = p[:, 0:W] * c + p[:, W:2 * W] * s
    k = p[:, 2 * W:3 * W] * c + p[:, 3 * W:4 * W] * s
    v = p[:, 4 * W:5 * W]
    lane = lax.broadcasted_iota(jnp.int32, (p.shape[0], HP), 1)
    for h in range(N_HEADS):
        sl = slice(h * HP, (h + 1) * HP)
        q_o[h] = (q[:, sl] * (LOG2E / math.sqrt(DF_QK))).astype(BF16)
        kh = k[:, sl]
        k1_o[h] = jnp.where(lane < DF_QK, kh, 0.0).astype(BF16)
        k2_o[h] = jnp.where(lane >= DF_QK, kh, 0.0).astype(BF16)
        v_o[h] = jnp.where(lane == ONES_LANE, 1.0, v[:, sl]).astype(BF16)


def _df_tables(cos, sin):
    T = cos.shape[0]
    zero = jnp.zeros((T, HP - 2 * DF_QK), F32)
    c = jnp.tile(jnp.concatenate([cos, cos, zero], -1), (1, N_HEADS))
    s = jnp.tile(jnp.concatenate([sin, sin, zero], -1), (1, N_HEADS))
    return c, s


def _df_prep(p_df, tables):
    B, T, C = p_df.shape
    tm = _tile(T, 256)
    W = N_HEADS * HP
    tspec = pl.BlockSpec((tm, W), lambda b, i: (i, 0))
    ospec = pl.BlockSpec((None, N_HEADS, tm, HP), lambda b, i: (b, 0, i, 0))
    return pl.pallas_call(
        _df_prep_kernel,
        grid=(B, T // tm),
        in_specs=[pl.BlockSpec((None, tm, C), lambda b, i: (b, i, 0)), tspec, tspec],
        out_specs=[ospec] * 4,
        out_shape=[jax.ShapeDtypeStruct((B, N_HEADS, T, HP), BF16)] * 4,
        compiler_params=_cp("parallel", "parallel"),
        name="diff_prepare",
    )(p_df, *tables)


def _softmax_pv(q, k_ref, v_ref):
    s = _dot(q, k_ref[...], NT)
    p = jnp.exp2((s - jnp.max(s, axis=-1, keepdims=True)).astype(BF16))
    acc = _dot(p, v_ref[...])
    return acc / acc[:, ONES_LANE:ONES_LANE + 1]


def _mla_attn_kernel(q_ref, k_ref, v_ref, o_ref):
    o_ref[...] = _softmax_pv(q_ref[...], k_ref, v_ref)


def _attn_specs(Tq, Tk):
    tq = _tile(Tq, ATTN_TQ)
    qspec = pl.BlockSpec((None, None, tq, HP), lambda b, h, i: (b, h, i, 0))
    kspec = pl.BlockSpec((None, None, Tk, HP), lambda b, h, i: (b, h, 0, 0))
    return tq, qspec, kspec


def _mla_attention(q, k, v):
    B, H, Tq, _ = q.shape
    tq, qspec, kspec = _attn_specs(Tq, k.shape[2])
    return pl.pallas_call(
        _mla_attn_kernel,
        grid=(B, H, Tq // tq),
        in_specs=[qspec, kspec, kspec],
        out_specs=qspec,
        out_shape=jax.ShapeDtypeStruct((B, H, Tq, HP), F32),
        compiler_params=_cp("parallel", "parallel", "parallel"),
        name="mla_attention",
    )(q, k, v)


def _df_attn_kernel(lam_init, q_ref, k1_ref, k2_ref, v_ref, lq1, lk1, lq2, lk2, sub, o_ref):
    q = q_ref[...]
    lam = (jnp.exp(jnp.sum(lq1[...] * lk1[...], axis=-1, keepdims=True))
           - jnp.exp(jnp.sum(lq2[...] * lk2[...], axis=-1, keepdims=True)) + lam_init)
    o = _softmax_pv(q, k1_ref, v_ref) - lam * _softmax_pv(q, k2_ref, v_ref)
    lane = lax.broadcasted_iota(jnp.int32, o.shape, 1)
    o = jnp.where(lane < DF_V, o, 0.0)
    ms = jnp.sum(o * o, axis=-1, keepdims=True) * (1.0 / DF_V)
    o_ref[...] = o * lax.rsqrt(ms + DF_SUBLN_EPS) * sub[...] * (1.0 - lam_init)


def _df_attention(q, k1, k2, v, lp, lam_init):
    B, H, Tq, _ = q.shape
    tq, qspec, kspec = _attn_specs(Tq, k1.shape[2])
    lspec = pl.BlockSpec((1, DF_QK), lambda b, h, i: (0, 0))
    sub = jnp.pad(lp['df_subln'], (0, HP - DF_V)).reshape(1, HP)
    return pl.pallas_call(
        functools.partial(_df_attn_kernel, lam_init),
        grid=(B, H, Tq // tq),
        in_specs=[qspec, kspec, kspec, kspec, lspec, lspec, lspec, lspec,
                  pl.BlockSpec((1, HP), lambda b, h, i: (0, 0))],
        out_specs=qspec,
        out_shape=jax.ShapeDtypeStruct((B, H, Tq, HP), F32),
        compiler_params=_cp("parallel", "parallel", "parallel"),
        name="diff_attention",
    )(q, k1, k2, v, lp['df_lq1'].reshape(1, DF_QK), lp['df_lk1'].reshape(1, DF_QK),
      lp['df_lq2'].reshape(1, DF_QK), lp['df_lk2'].reshape(1, DF_QK), sub)


def _merge_kernel(x_ref, sh_ref, sc_ref, gt_ref, gpre_ref, gpost_ref, orw, ohy, omla, odf,
                  wg, bg, urw, uhy, umla, udf, wo, o_ref):
    x = x_ref[...]
    h = (_rms(x, gpre_ref[...]) * (1.0 + sc_ref[...]) + sh_ref[...]).astype(BF16)
    ups = (_dot1(orw[...], urw[...]),
           _dot1(ohy[...], uhy[...]),
           sum(_dot1(omla[hh], umla[hh]) for hh in range(N_HEADS)),
           sum(_dot1(odf[hh], udf[hh]) for hh in range(N_HEADS)))
    acc = None
    for n in range(4):
        t = _sigmoid(_dot(h, wg[n]) + bg[n]) * ups[n]
        acc = t if acc is None else acc + t
    y = _dot(acc.astype(BF16), wo[...])
    o_ref[...] = x + gt_ref[...] * _rms(y, gpost_ref[...])


def _pad_head_rows(w):
    return jnp.pad(w.reshape(N_HEADS, HEAD_DIM, -1), ((0, 0), (0, HP - HEAD_DIM), (0, 0)))


def _merge(x, mod3, g_pre, g_post, outs, mw):
    B, T, D = x.shape
    o_rw, o_hy, o_mla, o_df = outs
    tm = _tile(T, 512)
    xspec = pl.BlockSpec((None, tm, D), lambda b, i: (b, i, 0))
    bspec = pl.BlockSpec((None, tm, BR), lambda b, i: (b, i, 0))
    hspec = pl.BlockSpec((None, N_HEADS, tm, HP), lambda b, i: (b, 0, i, 0))

    def full(a):
        nd = a.ndim
        return pl.BlockSpec(a.shape, lambda b, i: (0,) * nd)

    ws = [mw['wg'], mw['bg'], mw['urw'], mw['uhy'], mw['umla'], mw['udf'], mw['wo']]
    return pl.pallas_call(
        _merge_kernel,
        grid=(B, T // tm),
        in_specs=[xspec, _mod_spec(mod3[0], 2), _mod_spec(mod3[1], 2), _mod_spec(mod3[2], 2),
                  _row_spec(D), _row_spec(D), bspec, bspec, hspec, hspec] + [full(a) for a in ws],
        out_specs=xspec,
        out_shape=jax.ShapeDtypeStruct((B, T, D), F32),
        compiler_params=_cp("parallel", "parallel"),
        name="gated_merge",
    )(x, mod3[0], mod3[1], mod3[2], g_pre.reshape(1, D), g_post.reshape(1, D), o_rw, o_hy, o_mla, o_df, *ws)


def _mix_weights(w_in):
    src, sign = _rot_index()
    w_rw = w_in[:, :RW_COLS]
    w_hy = w_in[:, OFF_HY:OFF_HY + HY_COLS]
    kpe0 = OFF_MLA + MLA_Q_RANK + MLA_KV_RANK
    mla_idx = (list(OFF_MLA + np.arange(MLA_Q_RANK)) + [-1] * (256 - MLA_Q_RANK)
               + list(OFF_MLA + MLA_Q_RANK + np.arange(MLA_KV_RANK))
               + list(kpe0 + np.arange(MLA_ROPE)) + list(kpe0 + src) + [-1] * 64)
    mla_sign = [1.0] * 256 + [1.0] * MLA_KV_RANK + [1.0] * MLA_ROPE + list(sign) + [0.0] * 64
    w_mla = _take_cols(w_in, mla_idx, mla_sign)
    pad = [-1] * (HP - 2 * DF_QK)
    src2 = np.concatenate([src, DF_QK + src])
    sign2 = np.concatenate([sign, sign])
    idx, sg = [], []
    for base, rot in ((OFF_DF, False), (OFF_DF, True), (OFF_DF + 256, False), (OFF_DF + 256, True),
                      (OFF_DF + 512, False)):
        for h in range(N_HEADS):
            cols = base + h * 64 + (src2 if rot else np.arange(64))
            idx += list(cols) + pad
            sg += (list(sign2) if rot else [1.0] * 64) + [0.0] * len(pad)
    w_df = _take_cols(w_in, idx, sg)
    return [w.astype(BF16) for w in (w_rw, w_hy, w_mla, w_df)]


def _merge_weights(lp):
    return dict(wg=lp['w_gate'].astype(BF16), bg=lp['b_gate'].reshape(4, 1, D_MODEL),
                urw=lp['w_up'][0].astype(BF16), uhy=lp['w_up'][1].astype(BF16),
                umla=_pad_head_rows(lp['w_up'][2]).astype(BF16), udf=_pad_head_rows(lp['w_up'][3]).astype(BF16),
                wo=lp['w_o'].astype(BF16))


def _token_mixing(x, xc, mod, modc, g_pre, lam_init, with_ctx_out, lp, rope):
    B, T, _ = x.shape
    ws = _mix_weights(lp['w_in'])
    p_rw, p_hy, p_mla, p_df = _project(x, mod[0], mod[1], g_pre, ws)
    c_rw, c_hy, c_mla, c_df = _project(xc, modc[0], modc[1], g_pre, ws)

    prep = _rwkv_prepare(p_rw, lp)
    prep_c = _rwkv_prepare(c_rw, lp)
    s0 = jnp.zeros((B, 2, N_HEADS, HEAD_DIM, HEAD_DIM), F32)
    yf_c, yb_c, s_c = _rwkv_scan(prep_c, s0)
    yf, yb, _ = _rwkv_scan(prep, s_c)
    o_rw = _rwkv_output(yf, yb, prep, lp)

    o_hy = _hyena(p_hy, lp)

    cos, sin = rope
    Tc = xc.shape[1]
    ones, zeros = jnp.ones((Tc, MLA_ROPE), F32), jnp.zeros((Tc, MLA_ROPE), F32)
    mla_w = _mla_weights(lp)
    q, k, v = _mla_prep(p_mla, _mla_tables(cos, sin), mla_w)
    qc, kc, vc = _mla_prep(c_mla, _mla_tables(ones, zeros), mla_w)
    o_mla = _mla_attention(q, jnp.concatenate([k, kc], 2), jnp.concatenate([v, vc], 2))

    dq, dk1, dk2, dv = _df_prep(p_df, _df_tables(cos, sin))
    dqc, dk1c, dk2c, dvc = _df_prep(c_df, _df_tables(ones, zeros))
    o_df = _df_attention(dq, jnp.concatenate([dk1, dk1c], 2), jnp.concatenate([dk2, dk2c], 2),
                         jnp.concatenate([dv, dvc], 2), lp, lam_init)

    outs = (o_rw, o_hy, o_mla, o_df)
    if not with_ctx_out:
        return outs, None
    oc_rw = _rwkv_output(yf_c, yb_c, prep_c, lp)
    oc_hy = _hyena(c_hy, lp)
    oc_mla = _mla_attention(qc, kc, vc)
    oc_df = _df_attention(dqc, dk1c, dk2c, dvc, lp, lam_init)
    return outs, (oc_rw, oc_hy, oc_mla, oc_df)


def kernel(x, c, ctx, c_ctx, w_mod, b_mod, norm_pre, norm_post, ffn_w_in, ffn_w_out, w_in, rw_mu, rw_w0, rw_w2, rw_a0, rw_a2, rw_g2, rw_kk, rw_ka, rw_rk, rw_ln_w, rw_ln_b, hy_conv_w, hy_conv_b, hy_w1, hy_b1, hy_f1, hy_w2, hy_b2, hy_f2, hy_w3, hy_bias, mla_q_norm, mla_wq_b, mla_kv_norm, mla_wkv_b, df_lq1, df_lk1, df_lq2, df_lk2, df_subln, w_up, w_gate, b_gate, w_o):
    params = dict(w_in=w_in, rw_mu=rw_mu, rw_w0=rw_w0, rw_w2=rw_w2, rw_a0=rw_a0, rw_a2=rw_a2, rw_g2=rw_g2,
                  rw_kk=rw_kk, rw_ka=rw_ka, rw_rk=rw_rk, rw_ln_w=rw_ln_w, rw_ln_b=rw_ln_b, hy_conv_w=hy_conv_w,
                  hy_conv_b=hy_conv_b, hy_w1=hy_w1, hy_b1=hy_b1, hy_f1=hy_f1, hy_w2=hy_w2, hy_b2=hy_b2,
                  hy_f2=hy_f2, hy_w3=hy_w3, hy_bias=hy_bias, mla_q_norm=mla_q_norm, mla_wq_b=mla_wq_b,
                  mla_kv_norm=mla_kv_norm, mla_wkv_b=mla_wkv_b, df_lq1=df_lq1, df_lk1=df_lk1, df_lq2=df_lq2,
                  df_lk2=df_lk2, df_subln=df_subln, w_up=w_up, w_gate=w_gate, b_gate=b_gate, w_o=w_o)
    B, T, D = x.shape
    depth = w_mod.shape[0]
    assert B <= 7 and D == D_MODEL and T % 128 == 0 and ctx.shape[1] % CHUNK == 0
    rope = _rope_tables(T)

    s_rows = jnp.concatenate([c, c_ctx[None], jnp.zeros((8 - B - 1, D), F32)], 0)
    mod_all = _modulation(s_rows, w_mod, b_mod).reshape(depth, 8, 3, 3, D)
    ffn_in = ffn_w_in.astype(BF16)
    ffn_out = ffn_w_out.astype(BF16)

    xc = ctx
    for l in range(depth):
        last = l == depth - 1
        lp = {k_: v_[l] for k_, v_ in params.items()}
        mod = [[mod_all[l, :B, s, m][:, None, :] for m in range(3)] for s in range(3)]
        modc = [[mod_all[l, B:B + 1, s, m][:, None, :] for m in range(3)] for s in range(3)]
        ffn_a = (norm_pre[l, 0], norm_post[l, 0], ffn_in[l, 0], ffn_out[l, 0])
        ffn_b = (norm_pre[l, 2], norm_post[l, 2], ffn_in[l, 1], ffn_out[l, 1])

        x = _ffn(x, mod[0], *ffn_a)
        xc = _ffn(xc, modc[0], *ffn_a)

        lam_init = 0.8 - 0.6 * math.exp(-0.3 * l)
        outs, outs_c = _token_mixing(x, xc, mod[1], modc[1], norm_pre[l, 1], lam_init, not last, lp, rope)
        mw = _merge_weights(lp)
        x = _merge(x, mod[1], norm_pre[l, 1], norm_post[l, 1], outs, mw)
        x = _ffn(x, mod[2], *ffn_b)
        if not last:
            xc = _merge(xc, modc[1], norm_pre[l, 1], norm_post[l, 1], outs_c, mw)
            xc = _ffn(xc, modc[2], *ffn_b)
    return x
```

```python
import functools
import math

import numpy as np
import jax
import jax.numpy as jnp
from jax import lax
from jax.experimental import pallas as pl
from jax.experimental.pallas import tpu as pltpu

F32 = jnp.float32
BF16 = jnp.bfloat16

D_MODEL = 1024
GRID_W = 64
N_HEADS = 4
HEAD_DIM = 64
BR = N_HEADS * HEAD_DIM
D_FF = 2816
MACARON_W = 0.5
EPS = 1e-6
ROPE_BASE = 10000.0
RW_LN_EPS = 64e-5
RW_COLS = 1152
HY_COLS = 768
HY_EMB = 33
HY_FAST_DECAY = 0.3
HY_SLOW_DECAY = 1.5
HY_TARGET = 1e-2
MLA_Q_RANK = 192
MLA_KV_RANK = 128
MLA_NOPE = 64
MLA_ROPE = 32
DF_QK = 32
DF_V = 64
DF_SUBLN_EPS = 1e-5
OFF_HY = RW_COLS
OFF_MLA = RW_COLS + HY_COLS
OFF_DF = OFF_MLA + MLA_Q_RANK + MLA_KV_RANK + MLA_ROPE

LANES = 128
HP = 128
ONES_LANE = 64
LOG2E = 1.4426950408889634
FFN_TM = 1024
FFN_TF = 1408
ATTN_TQ = 256
CHUNK = 64
RW_P_GRAM = 1
RW_P_INV = (3, 3)
RW_P_STATE = 3
RW_P_APPLY = 1
VMEM_LIMIT = 56 * 1024 * 1024


def _cp(*sem):
    return pltpu.CompilerParams(dimension_semantics=sem, vmem_limit_bytes=VMEM_LIMIT)


def _tile(n, pref, mult=8):
    if n <= pref:
        return n
    t = (pref // mult) * mult
    while t > mult and n % t:
        t -= mult
    assert n % t == 0, (n, pref)
    return t


def _dot(a, b, dims=None):
    if dims is None:
        dims = (((a.ndim - 1,), (0,)), ((), ()))
    return lax.dot_general(a, b, dims, preferred_element_type=F32)


NT = (((1,), (1,)), ((), ()))
TN = (((0,), (0,)), ((), ()))


def _split(x):
    hi = x.astype(BF16)
    lo = (x - hi.astype(F32)).astype(BF16)
    return hi, lo


def _dot3(a, b, dims=None):
    ah, al = _split(a)
    bh, bl = _split(b)
    return _dot(ah, bh, dims) + (_dot(ah, bl, dims) + _dot(al, bh, dims))


def _dot1(a, b, dims=None):
    return _dot(a.astype(BF16), b.astype(BF16), dims)


def _dotc(ch, cl, x):
    xh, xl = _split(x)
    return _dot(ch, xh) + (_dot(ch, xl) + _dot(cl, xh))


def _dot_exact_lhs(m, x):
    x1 = x.astype(BF16)
    r1 = x - x1.astype(F32)
    x2 = r1.astype(BF16)
    x3 = (r1 - x2.astype(F32)).astype(BF16)
    return _dot(m, x1) + (_dot(m, x2) + _dot(m, x3))


def _rms(x, g, eps=EPS):
    return x * lax.rsqrt(jnp.mean(x * x, axis=-1, keepdims=True) + eps) * g


def _sigmoid(x):
    return 1.0 / (1.0 + jnp.exp(-x))


def _silu(x):
    return x * _sigmoid(x)


def _softplus(z):
    return jnp.maximum(z, 0.0) + jnp.log(1.0 + jnp.exp(-jnp.abs(z)))


def _np_split(a):
    a = jnp.asarray(a, F32)
    hi = a.astype(BF16)
    lo = (a - hi.astype(F32)).astype(BF16)
    return hi, lo


def _mod_kernel(s_ref, w_ref, b_ref, o_ref):
    s = _silu(s_ref[...])
    o_ref[...] = _dot3(s, w_ref[...]) + b_ref[...]


def _modulation(s_rows, w_mod, b_mod):
    L, D, NM = w_mod.shape
    tn = _tile(NM, 1152, LANES)
    return pl.pallas_call(
        _mod_kernel,
        grid=(L, NM // tn),
        in_specs=[pl.BlockSpec((8, D), lambda l, j: (0, 0)),
                  pl.BlockSpec((None, D, tn), lambda l, j: (l, 0, j)),
                  pl.BlockSpec((None, 1, tn), lambda l, j: (l, 0, j))],
        out_specs=pl.BlockSpec((None, 8, tn), lambda l, j: (l, 0, j)),
        out_shape=jax.ShapeDtypeStruct((L, 8, NM), F32),
        compiler_params=_cp("parallel", "parallel"),
        name="modulation",
    )(s_rows, w_mod, b_mod.reshape(L, 1, NM))


def _mod_spec(m, nargs):
    if m.shape[0] == 1:
        return pl.BlockSpec((None, 1, D_MODEL), lambda *a: (0, 0, 0))
    return pl.BlockSpec((None, 1, D_MODEL), lambda *a: (a[0], 0, 0))


def _row_spec(n):
    return pl.BlockSpec((1, n), lambda *a: (0, 0))


def _ffn_kernel(x_ref, sh_ref, sc_ref, gt_ref, gpre_ref, gpost_ref, wa_ref, wb_ref, wo_ref, o_ref, h_sc, acc_sc):
    f = pl.program_id(2)

    @pl.when(f == 0)
    def _():
        h = _rms(x_ref[...], gpre_ref[...]) * (1.0 + sc_ref[...]) + sh_ref[...]
        h_sc[...] = h.astype(BF16)
        acc_sc[...] = jnp.zeros_like(acc_sc)

    h = h_sc[...]
    a = _dot(h, wa_ref[...])
    b = _dot(h, wb_ref[...])
    acc_sc[...] += _dot((_silu(a) * b).astype(BF16), wo_ref[...])

    @pl.when(f == pl.num_programs(2) - 1)
    def _():
        o_ref[...] = x_ref[...] + MACARON_W * gt_ref[...] * _rms(acc_sc[...], gpost_ref[...])


def _ffn(x, mod3, g_pre, g_post, w_in, w_out):
    B, T, D = x.shape
    F = w_out.shape[0]
    tm = _tile(T, FFN_TM)
    tf = _tile(F, FFN_TF, LANES)
    nf = F // tf
    xspec = pl.BlockSpec((None, tm, D), lambda b, i, f: (b, i, 0))
    return pl.pallas_call(
        _ffn_kernel,
        grid=(B, T // tm, nf),
        in_specs=[xspec, _mod_spec(mod3[0], 3), _mod_spec(mod3[1], 3), _mod_spec(mod3[2], 3),
                  _row_spec(D), _row_spec(D),
                  pl.BlockSpec((D, tf), lambda b, i, f: (0, f)),
                  pl.BlockSpec((D, tf), lambda b, i, f: (0, nf + f)),
                  pl.BlockSpec((tf, D), lambda b, i, f: (f, 0))],
        out_specs=xspec,
        out_shape=jax.ShapeDtypeStruct((B, T, D), F32),
        scratch_shapes=[pltpu.VMEM((tm, D), BF16), pltpu.VMEM((tm, D), F32)],
        compiler_params=_cp("parallel", "parallel", "arbitrary"),
        name="ffn",
    )(x, mod3[0], mod3[1], mod3[2], g_pre.reshape(1, D), g_post.reshape(1, D), w_in, w_in, w_out)


def _proj_kernel(x_ref, sh_ref, sc_ref, gpre_ref, wrw, why, wmla, wdf, orw, ohy, omla, odf):
    h = (_rms(x_ref[...], gpre_ref[...]) * (1.0 + sc_ref[...]) + sh_ref[...]).astype(BF16)
    orw[...] = _dot(h, wrw[...])
    ohy[...] = _dot(h, why[...])
    omla[...] = _dot(h, wmla[...])
    odf[...] = _dot(h, wdf[...])


def _project(x, shift, scale, g_pre, ws):
    B, T, D = x.shape
    tm = _tile(T, 256)
    xspec = pl.BlockSpec((None, tm, D), lambda b, i: (b, i, 0))
    wspecs = [pl.BlockSpec(w.shape, lambda b, i: (0, 0)) for w in ws]
    ospecs = [pl.BlockSpec((None, tm, w.shape[1]), lambda b, i: (b, i, 0)) for w in ws]
    oshapes = [jax.ShapeDtypeStruct((B, T, w.shape[1]), F32) for w in ws]
    return pl.pallas_call(
        _proj_kernel,
        grid=(B, T // tm),
        in_specs=[xspec, _mod_spec(shift, 2), _mod_spec(scale, 2), _row_spec(D)] + wspecs,
        out_specs=ospecs,
        out_shape=oshapes,
        compiler_params=_cp("parallel", "parallel"),
        name="mix_project",
    )(x, shift, scale, g_pre.reshape(1, D), *ws)


def _halo_specs(tm, T, C):
    nb8 = T // 8
    r = tm // 8
    main = pl.BlockSpec((None, tm, C), lambda b, i: (b, i, 0))
    prev = pl.BlockSpec((None, 8, C), lambda b, i: (b, jnp.maximum(i * r - 1, 0), 0))
    nxt = pl.BlockSpec((None, 8, C), lambda b, i: (b, jnp.minimum((i + 1) * r, nb8 - 1), 0))
    return main, prev, nxt


def _conv3_tile(p, prev8, next8, w0, w1, w2):
    i = pl.program_id(1)
    n_i = pl.num_programs(1)
    tm = p.shape[0]
    row = lax.broadcasted_iota(jnp.int32, p.shape, 0)
    prev_row = jnp.where(i > 0, prev8[7:8, :], 0.0)
    next_row = jnp.where(i < n_i - 1, next8[0:1, :], 0.0)
    pm = jnp.where(row == 0, prev_row, pltpu.roll(p, 1, 0))
    pp = jnp.where(row == tm - 1, next_row, pltpu.roll(p, tm - 1, 0))
    return pm * w0 + p * w1 + pp * w2


def _rwkv_prep_kernel(p_ref, pv_ref, nx_ref, mu_ref, w0_ref, w2_ref, a0_ref, a2_ref, g2_ref, kk_ref, ka_ref,
                      rk_ref, bd_ref, trif_ref, trib_ref, onec_ref, v_o, g_o, bonus_o, *dir_outs):
    mu = mu_ref[...]
    p = _conv3_tile(p_ref[...], pv_ref[...], nx_ref[...], 0.5 * mu, 1.0 - mu, 0.5 * mu)
    r = p[:, 0:BR]
    k = p[:, BR:2 * BR]
    v = p[:, 2 * BR:3 * BR]
    g_a = p[:, 3 * BR:3 * BR + 128]
    wa = p[:, 896:1024]
    aa = p[:, 1024:1152]
    bd = bd_ref[...]
    kkraw = k * kk_ref[...]
    ss = _dot_exact_lhs_rhs(kkraw * kkraw, bd)
    kk = kkraw / jnp.maximum(jnp.sqrt(ss), 1e-12)
    wl = w0_ref[...] + _dot3(jnp.tanh(wa), w2_ref[...])
    lw = -jnp.exp(-_softplus(-wl) - 0.5)
    a = _sigmoid(a0_ref[...] + _dot3(aa, a2_ref[...]))
    a_f = a[:, :BR]
    a_b = a[:, BR:]
    ka = ka_ref[...]
    k_f = k * (1.0 + (a_f - 1.0) * ka)
    k_b = k * (1.0 + (a_b - 1.0) * ka)
    v_o[...] = v
    g_o[...] = _dot1(_sigmoid(g_a), g2_ref[...])
    bonus_o[...] = _dot_exact_lhs_rhs(r * (k_f + k_b) * rk_ref[...], bd) * v
    onec = onec_ref[...]
    for d, (tri_ref, k_d, a_d) in enumerate(((trif_ref, k_f, a_f), (trib_ref, k_b, a_b))):
        rt_o, at_o, bt_o, kt_o, bh_o, kh_o, eg_o = dir_outs[7 * d:7 * d + 7]
        lw_d = lw[:, d * BR:(d + 1) * BR]
        b_d = kk * a_d
        g = _dot_exact_lhs(tri_ref[...], lw_d)
        gtot = _dot_exact_lhs(onec, lw_d)
        eng = jnp.exp(-g)
        et = jnp.exp(gtot - g)
        rt_o[...] = r * jnp.exp(g)
        at_o[...] = -kk * jnp.exp(g - lw_d)
        bt_o[...] = b_d * eng
        kt_o[...] = k_d * eng
        bh_o[...] = b_d * et
        kh_o[...] = k_d * et
        eg_o[...] = jnp.exp(gtot)


def _dot_exact_lhs_rhs(x, m):
    x1 = x.astype(BF16)
    r1 = x - x1.astype(F32)
    x2 = r1.astype(BF16)
    x3 = (r1 - x2.astype(F32)).astype(BF16)
    return _dot(x1, m) + (_dot(x2, m) + _dot(x3, m))


def _block_diag_ones():
    h = np.arange(BR) // HEAD_DIM
    return jnp.asarray((h[:, None] == h[None, :]).astype(np.float32), BF16)


def _rwkv_prepare(p_rw, lp):
    B, T, C = p_rw.shape
    tm = _tile(T, 256)
    main, prev, nxt = _halo_specs(tm, T, C)
    z = jnp.zeros((64, BR), F32)
    w2 = jnp.concatenate([jnp.concatenate([lp['rw_w2'][0], z], 1), jnp.concatenate([z, lp['rw_w2'][1]], 1)], 0)
    a2 = jnp.concatenate([jnp.concatenate([lp['rw_a2'][0], z], 1), jnp.concatenate([z, lp['rw_a2'][1]], 1)], 0)
    params = [lp['rw_mu'].reshape(1, C), lp['rw_w0'].reshape(1, 2 * BR), w2, lp['rw_a0'].reshape(1, 2 * BR), a2,
              lp['rw_g2'].astype(BF16), lp['rw_kk'].reshape(1, BR), lp['rw_ka'].reshape(1, BR),
              lp['rw_rk'].reshape(1, BR), _block_diag_ones()]
    pos = np.arange(tm)
    same = (pos[:, None] // CHUNK) == (pos[None, :] // CHUNK)
    params += [jnp.asarray((same & (pos[None, :] <= pos[:, None])).astype(np.float32), BF16),
               jnp.asarray((same & (pos[None, :] >= pos[:, None])).astype(np.float32), BF16),
               jnp.asarray(same.astype(np.float32), BF16)]
    pspecs = [pl.BlockSpec(a.shape, lambda b, i: (0, 0)) for a in params]
    ospec = pl.BlockSpec((None, tm, BR), lambda b, i: (b, i, 0))
    names = ('v', 'g', 'bonus') + tuple(n + str(d) for d in (0, 1) for n in ('rt', 'at', 'bt', 'kt', 'bh', 'kh', 'eg'))
    outs = pl.pallas_call(
        _rwkv_prep_kernel,
        grid=(B, T // tm),
        in_specs=[main, prev, nxt] + pspecs,
        out_specs=[ospec] * len(names),
        out_shape=[jax.ShapeDtypeStruct((B, T, BR), F32)] * len(names),
        compiler_params=_cp("parallel", "parallel"),
        name="rwkv_prepare",
    )(p_rw, p_rw, p_rw, *params)
    return dict(zip(names, outs))


def _dot3s(a, b, dims=None):
    if dims is None:
        dims = (((1,), (0,)), ((), ()))
    ca, cb = dims[0][0][0], dims[0][1][0]
    a_lo = a - a.astype(BF16).astype(F32)
    b_lo = b - b.astype(BF16).astype(F32)
    sa = jnp.concatenate([a, a_lo, a], axis=ca).astype(BF16)
    sb = jnp.concatenate([b, b, b_lo], axis=cb).astype(BF16)
    return _dot(sa, sb, dims)


def _mmp(a, b, dims, passes):
    return _dot3s(a, b, dims) if passes == 3 else _dot1(a, b, dims)


def _rwkv_chunks(chains, eye):
    n = range(len(chains))
    rt, at, bt, kt, bh, kh, eg, v, ht, mask_s, mask_i = zip(*chains)
    C = rt[0].shape[0]
    gm = [_mmp(jnp.concatenate([at[i], rt[i]], 0), jnp.concatenate([bt[i], kt[i]], 0), NT, RW_P_GRAM) for i in n]
    top = [jnp.where(mask_s[i], gm[i][:C], 0.0) for i in n]
    bot = [jnp.where(mask_i[i], gm[i][C:], 0.0) for i in n]
    pw = [top[i][:, :C] for i in n]
    tinv = [eye + pw[i] for i in n]
    for _ in range(int(math.log2(C)) - 1):
        pw = [_mmp(pw[i], pw[i], None, RW_P_INV[0]) for i in n]
        upd = [_mmp(tinv[i], pw[i], None, RW_P_INV[1]) for i in n]
        tinv = [tinv[i] + upd[i] for i in n]
    w0 = [_mmp(at[i], ht[i], NT, RW_P_STATE) for i in n]
    w1 = [_mmp(top[i][:, C:], v[i], None, RW_P_APPLY) for i in n]
    u = [_mmp(tinv[i], w0[i] + w1[i], None, RW_P_STATE) for i in n]
    uv = [jnp.concatenate([u[i], v[i]], 0) for i in n]
    y0 = [_mmp(rt[i], ht[i], NT, RW_P_STATE) for i in n]
    y1 = [_mmp(bot[i], uv[i], None, RW_P_APPLY) for i in n]
    hu = [_mmp(uv[i], jnp.concatenate([bh[i], kh[i]], 0), TN, RW_P_STATE) for i in n]
    return [(y0[i] + y1[i], ht[i] * eg[i] + hu[i]) for i in n]


def _rwkv_scan_kernel(*refs):
    fwd_in, bwd_in = refs[0:8], refs[8:16]
    s0_ref, yf_o, yb_o, s_o, st = refs[16:]
    c = pl.program_id(0)
    nb, C = yf_o.shape[0], yf_o.shape[1]

    @pl.when(c == 0)
    def _():
        st[...] = s0_ref[...]

    row = lax.broadcasted_iota(jnp.int32, (C, 2 * C), 0)
    col = lax.broadcasted_iota(jnp.int32, (C, 2 * C), 1)
    col = jnp.where(col >= C, col - C, col)
    eye = (lax.broadcasted_iota(jnp.int32, (C, C), 0) == lax.broadcasted_iota(jnp.int32, (C, C), 1)).astype(F32)
    chains = []
    for b in range(nb):
        for d, ins in enumerate((fwd_in, bwd_in)):
            if d == 0:
                mask_i = col <= row
                mask_s = col < row
            else:
                mask_i = col >= row
                mask_s = col > row
            for h in range(N_HEADS):
                sl = pl.ds(h * HEAD_DIM, HEAD_DIM)
                rt, at, bt, kt, bh, kh, eg, v = (ref[b, :, sl] for ref in ins)
                chains.append((rt, at, bt, kt, bh, kh, eg[0:1, :], v, st[b, d, h], mask_s, mask_i))
    results = iter(_rwkv_chunks(chains, eye))
    for b in range(nb):
        for d, y_o in enumerate((yf_o, yb_o)):
            for h in range(N_HEADS):
                y, hn = next(results)
                y_o[b, :, pl.ds(h * HEAD_DIM, HEAD_DIM)] = y
                st[b, d, h] = hn

    @pl.when(c == pl.num_programs(0) - 1)
    def _():
        s_o[...] = st[...]


def _rwkv_scan(prep, s0):
    B, T, _ = prep['v'].shape
    nc = T // CHUNK
    fwd = pl.BlockSpec((B, CHUNK, BR), lambda c: (0, c, 0))
    bwd = pl.BlockSpec((B, CHUNK, BR), lambda c: (0, nc - 1 - c, 0))
    sspec = pl.BlockSpec(s0.shape, lambda c: (0, 0, 0, 0, 0))
    ins = [prep[n + str(d)] for d in (0, 1) for n in ('rt', 'at', 'bt', 'kt', 'bh', 'kh', 'eg')]
    return pl.pallas_call(
        _rwkv_scan_kernel,
        grid=(nc,),
        in_specs=[fwd] * 8 + [bwd] * 8 + [sspec],
        out_specs=[fwd, bwd, sspec],
        out_shape=[jax.ShapeDtypeStruct((B, T, BR), F32), jax.ShapeDtypeStruct((B, T, BR), F32),
                   jax.ShapeDtypeStruct(s0.shape, F32)],
        scratch_shapes=[pltpu.VMEM(s0.shape, F32)],
        compiler_params=_cp("arbitrary"),
        name="rwkv_scan",
    )(*ins[:7], prep['v'], *ins[7:], prep['v'], s0)


def _rwkv_out_kernel(yf, yb, g, bonus, lnw, lnb, bd, o):
    y = yf[...] + yb[...]
    m = bd[...]
    inv_n = 1.0 / HEAD_DIM
    mean = _dot_exact_lhs_rhs(y, m) * inv_n
    yc = y - mean
    var = _dot_exact_lhs_rhs(yc * yc, m) * inv_n
    yn = yc * lax.rsqrt(var + RW_LN_EPS) * lnw[...] + lnb[...]
    o[...] = (yn + bonus[...]) * g[...]


def _rwkv_output(yf, yb, prep, lp):
    B, T, _ = yf.shape
    tm = _tile(T, 512)
    spec = pl.BlockSpec((None, tm, BR), lambda b, i: (b, i, 0))
    return pl.pallas_call(
        _rwkv_out_kernel,
        grid=(B, T // tm),
        in_specs=[spec] * 4 + [_row_spec(BR), _row_spec(BR), pl.BlockSpec((BR, BR), lambda b, i: (0, 0))],
        out_specs=spec,
        out_shape=jax.ShapeDtypeStruct((B, T, BR), F32),
        compiler_params=_cp("parallel", "parallel"),
        name="rwkv_output",
    )(yf, yb, prep['g'], prep['bonus'], lp['rw_ln_w'].reshape(1, BR), lp['rw_ln_b'].reshape(1, BR),
      _block_diag_ones())


def _hy_pre_kernel(p_ref, pv_ref, nx_ref, w_ref, b_ref, x0_o, z_o):
    w = w_ref[...]
    u = _conv3_tile(p_ref[...], pv_ref[...], nx_ref[...], w[0:1, :], w[1:2, :], w[2:3, :]) + b_ref[...]
    x0_o[...] = u[:, 0:BR]
    z_o[...] = u[:, 2 * BR:3 * BR] * u[:, BR:2 * BR]


def _hyena_pre(p_hy, lp):
    B, T, C = p_hy.shape
    tm = _tile(T, 512)
    main, prev, nxt = _halo_specs(tm, T, C)
    ospec = pl.BlockSpec((None, tm, BR), lambda b, i: (b, i, 0))
    return pl.pallas_call(
        _hy_pre_kernel,
        grid=(B, T // tm),
        in_specs=[main, prev, nxt, pl.BlockSpec((3, C), lambda b, i: (0, 0)), _row_spec(C)],
        out_specs=[ospec, ospec],
        out_shape=[jax.ShapeDtypeStruct((B, T, BR), F32)] * 2,
        compiler_params=_cp("parallel", "parallel"),
        name="hyena_pre",
    )(p_hy, p_hy, p_hy, lp['hy_conv_w'], lp['hy_conv_b'].reshape(1, C))


def _hy_filter_kernel(z_ref, t_ref, mf_ref, mb_ref, w1, b1, f1, w2, b2, f2, w3, dl, k_o, ss_o):
    h = jnp.sin(f1[...] * (_dot3(z_ref[...], w1[...]) + b1[...]))
    h = jnp.sin(f2[...] * (_dot3(h, w2[...]) + b2[...]))
    h = _dot3(h, w3[...]) * jnp.exp(-t_ref[...] * dl[...])
    kern = mf_ref[...] * h[:, :BR] + mb_ref[...] * h[:, BR:]
    k_o[...] = kern

    @pl.when(pl.program_id(0) == 0)
    def _():
        ss_o[...] = jnp.zeros_like(ss_o)

    ss_o[...] += jnp.sum(kern * kern, axis=0, keepdims=True)


def _hyena_filter(n, lp):
    t = jnp.linspace(0.0, 1.0, n, dtype=F32)[:, None]
    bands = (HY_EMB - 1) // 2
    wpos = 2.0 * math.pi * jnp.arange(n, dtype=F32) / n
    fr = jnp.linspace(1e-4, bands - 1, bands, dtype=F32)
    ang = wpos[:, None] * fr[None]
    z = jnp.concatenate([t, jnp.cos(ang), -jnp.sin(ang)], axis=-1)
    rho = np.arange(2 * n)
    z_ext = jnp.pad(jnp.concatenate([z, z[::-1]], 0), ((0, 0), (0, LANES - HY_EMB)))
    t_ext = jnp.concatenate([t, t[::-1]], 0)
    mf = jnp.asarray((rho < n).astype(np.float32))[:, None]
    mb = jnp.asarray((rho > n).astype(np.float32))[:, None]
    max_decay = math.log(HY_TARGET) / HY_FAST_DECAY
    min_decay = math.log(HY_TARGET) / HY_SLOW_DECAY
    deltas = jnp.abs(jnp.linspace(min_decay, max_decay, BR, dtype=F32))
    dl = jnp.tile(deltas, 2)[None]
    w1 = jnp.pad(lp['hy_w1'], ((0, LANES - HY_EMB), (0, 0)))
    params = [w1, lp['hy_b1'][None], lp['hy_f1'][None], lp['hy_w2'], lp['hy_b2'][None], lp['hy_f2'][None],
              lp['hy_w3'], dl]
    tm = _tile(2 * n, 512)
    return pl.pallas_call(
        _hy_filter_kernel,
        grid=(2 * n // tm,),
        in_specs=[pl.BlockSpec((tm, LANES), lambda i: (i, 0))] + [pl.BlockSpec((tm, 1), lambda i: (i, 0))] * 3
        + [pl.BlockSpec(a.shape, lambda i: (0, 0)) for a in params],
        out_specs=[pl.BlockSpec((tm, BR), lambda i: (i, 0)), pl.BlockSpec((1, BR), lambda i: (0, 0))],
        out_shape=[jax.ShapeDtypeStruct((2 * n, BR), F32), jax.ShapeDtypeStruct((1, BR), F32)],
        compiler_params=_cp("arbitrary"),
        name="hyena_filter",
    )(z_ext, t_ext, mf, mb, *params)


def _dft_tables(N1, N2):
    N = N1 * N2

    def mat(n):
        a = -2.0 * np.pi * np.outer(np.arange(n), np.arange(n)) / n
        return np.cos(a), np.sin(a)

    f1r, f1i = mat(N1)
    f2r, f2i = mat(N2)
    a = -2.0 * np.pi * np.outer(np.arange(N1), np.arange(N2)) / N
    return dict(f1r=f1r, f1i=f1i, f2r=f2r, f2i=f2i,
                twr=jnp.asarray(np.cos(a)[:, :, None], F32), twi=jnp.asarray(np.sin(a)[:, :, None], F32))


def _fft_in_kernel(x_ref, frh, frl, fih, fil, ar_o, ai_o):
    x = x_ref[...]
    ar_o[...] = _dotc(frh[...], frl[...], x)
    ai_o[...] = _dotc(fih[...], fil[...], x)


def _fft_in(x2d, tabs, N1):
    B, n1u, cols = x2d.shape
    consts = list(_np_split(tabs['f1r'][:, :n1u])) + list(_np_split(tabs['f1i'][:, :n1u]))
    tc = _tile(cols, 8192, LANES)
    ospec = pl.BlockSpec((None, N1, tc), lambda b, j: (b, 0, j))
    return pl.pallas_call(
        _fft_in_kernel,
        grid=(B, cols // tc),
        in_specs=[pl.BlockSpec((None, n1u, tc), lambda b, j: (b, 0, j))]
        + [pl.BlockSpec((N1, n1u), lambda b, j: (0, 0))] * 4,
        out_specs=[ospec, ospec],
        out_shape=[jax.ShapeDtypeStruct((B, N1, cols), F32)] * 2,
        compiler_params=_cp("parallel", "parallel"),
        name="fft_first_axis",
    )(x2d, *consts)


def _cplx_dft(frh, frl, fih, fil, xr, xi, conj):
    rr = _dotc(frh, frl, xr)
    ii = _dotc(fih, fil, xi)
    ri = _dotc(frh, frl, xi)
    ir = _dotc(fih, fil, xr)
    if conj:
        return rr + ii, ri - ir
    return rr - ii, ri + ir


def _fft_spec_kernel(ar, ai, twr, twi, frh, frl, fih, fil, sc, kr_o, ki_o):
    tr, ti = twr[...], twi[...]
    xr = ar[...] * tr - ai[...] * ti
    xi = ar[...] * ti + ai[...] * tr
    br, bi = _cplx_dft(frh[...], frl[...], fih[...], fil[...], xr, xi, False)
    kr_o[...] = br * sc[...]
    ki_o[...] = bi * sc[...]


def _fft_conv_kernel(ar, ai, twr, twi, frh, frl, fih, fil, kr, ki, dr_o, di_o):
    tr, ti = twr[...], twi[...]
    f = (frh[...], frl[...], fih[...], fil[...])
    xr = ar[...] * tr - ai[...] * ti
    xi = ar[...] * ti + ai[...] * tr
    br, bi = _cplx_dft(*f, xr, xi, False)
    cr = br * kr[...] - bi * ki[...]
    ci = br * ki[...] + bi * kr[...]
    dr, di = _cplx_dft(*f, cr, ci, True)
    dr_o[...] = dr * tr + di * ti
    di_o[...] = di * tr - dr * ti


def _fft_mid(a_re, a_im, tabs, kf=None, scale=None):
    B, N1, N2, C = a_re.shape
    consts = list(_np_split(tabs['f2r'])) + list(_np_split(tabs['f2i']))
    aspec = pl.BlockSpec((None, None, N2, C), lambda b, k: (b, k, 0, 0))
    tspec = pl.BlockSpec((None, N2, 1), lambda b, k: (k, 0, 0))
    fspec = pl.BlockSpec((N2, N2), lambda b, k: (0, 0))
    if kf is None:
        extra, especs, kern, nm = [scale], [_row_spec(C)], _fft_spec_kernel, "fft_filter_spectrum"
    else:
        kspec = pl.BlockSpec((None, None, N2, C), lambda b, k: (0, k, 0, 0))
        extra, especs, kern, nm = list(kf), [kspec, kspec], _fft_conv_kernel, "fft_second_axis_conv"
    return pl.pallas_call(
        kern,
        grid=(B, N1),
        in_specs=[aspec, aspec, tspec, tspec] + [fspec] * 4 + especs,
        out_specs=[aspec, aspec],
        out_shape=[jax.ShapeDtypeStruct((B, N1, N2, C), F32)] * 2,
        compiler_params=_cp("parallel", "parallel"),
        name=nm,
    )(a_re, a_im, tabs['twr'], tabs['twi'], *consts, *extra)


def _fft_out_kernel(dr, di, frh, frl, fih, fil, x0, z, bias, o):
    zc = _dotc(frh[...], frl[...], dr[...]) + _dotc(fih[...], fil[...], di[...])
    zz = z[...]
    o[...] = x0[...] * (zc + zz * bias[...])


def _fft_out(d_re, d_im, tabs, x0_2d, z_2d, bias_row):
    B, N1, cols = d_re.shape
    n1u = x0_2d.shape[1]
    consts = list(_np_split(tabs['f1r'][:n1u, :])) + list(_np_split(tabs['f1i'][:n1u, :]))
    tc = _tile(cols, 8192, LANES)
    dspec = pl.BlockSpec((None, N1, tc), lambda b, j: (b, 0, j))
    xspec = pl.BlockSpec((None, n1u, tc), lambda b, j: (b, 0, j))
    return pl.pallas_call(
        _fft_out_kernel,
        grid=(B, cols // tc),
        in_specs=[dspec, dspec] + [pl.BlockSpec((n1u, N1), lambda b, j: (0, 0))] * 4
        + [xspec, xspec, pl.BlockSpec((1, tc), lambda b, j: (0, j))],
        out_specs=xspec,
        out_shape=jax.ShapeDtypeStruct((B, n1u, cols), F32),
        compiler_params=_cp("parallel", "parallel"),
        name="fft_last_axis_gate",
    )(d_re, d_im, *consts, x0_2d, z_2d, bias_row)


def _hy_gate_kernel(zc, x0, z, bias, o):
    o[...] = x0[...] * (zc[...] + z[...] * bias[...])


def _hyena(p_hy, lp):
    B, n, _ = p_hy.shape
    x0, z = _hyena_pre(p_hy, lp)
    kern, ss = _hyena_filter(n, lp)
    N = 2 * n
    N2 = min(256, n)
    N1 = N // N2
    tabs = _dft_tables(N1, N2)
    scale = lax.rsqrt(ss) * (1.0 / N)
    bias = lp['hy_bias'].reshape(1, BR)
    if N1 >= 4:
        n1u = N1 // 2
        k_re, k_im = _fft_in(kern.reshape(1, N1, N2 * BR), tabs, N1)
        kf = _fft_mid(k_re.reshape(1, N1, N2, BR), k_im.reshape(1, N1, N2, BR), tabs, scale=scale)
        a_re, a_im = _fft_in(z.reshape(B, n1u, N2 * BR), tabs, N1)
        d_re, d_im = _fft_mid(a_re.reshape(B, N1, N2, BR), a_im.reshape(B, N1, N2, BR), tabs, kf=kf)
        o = _fft_out(d_re.reshape(B, N1, N2 * BR), d_im.reshape(B, N1, N2 * BR), tabs,
                     x0.reshape(B, n1u, N2 * BR), z.reshape(B, n1u, N2 * BR), jnp.tile(bias, (1, N2)))
        return o.reshape(B, n, BR)
    tabs = _dft_tables(1, N)
    kf = _fft_mid(kern.reshape(1, 1, N, BR), jnp.zeros((1, 1, N, BR), F32), tabs, scale=scale)
    zp = jnp.pad(z, ((0, 0), (0, n), (0, 0))).reshape(B, 1, N, BR)
    d_re, _ = _fft_mid(zp, jnp.zeros_like(zp), tabs, kf=kf)
    zc = d_re.reshape(B, N, BR)[:, :n]
    tm = _tile(n, 512)
    spec = pl.BlockSpec((None, tm, BR), lambda b, i: (b, i, 0))
    return pl.pallas_call(
        _hy_gate_kernel,
        grid=(B, n // tm),
        in_specs=[spec, spec, spec, _row_spec(BR)],
        out_specs=spec,
        out_shape=jax.ShapeDtypeStruct((B, n, BR), F32),
        compiler_params=_cp("parallel", "parallel"),
        name="hyena_gate",
    )(zc, x0, z, bias)


def _rope_tables(n_lat):
    rows = n_lat // GRID_W
    row = jnp.repeat(jnp.arange(rows, dtype=F32), GRID_W)
    col = jnp.tile(jnp.arange(GRID_W, dtype=F32), rows)
    half = MLA_ROPE // 2
    inv = ROPE_BASE ** (-jnp.arange(0, half, 2, dtype=F32) / half)
    ar = row[:, None] * inv[None]
    ac = col[:, None] * inv[None]
    cos = jnp.concatenate([jnp.cos(ar), jnp.cos(ar), jnp.cos(ac), jnp.cos(ac)], axis=-1)
    sin = jnp.concatenate([jnp.sin(ar), jnp.sin(ar), jnp.sin(ac), jnp.sin(ac)], axis=-1)
    return cos, sin


def _rot_index():
    j = np.arange(32)
    first = (j % 16) < 8
    src = np.where(first, j + 8, j - 8)
    sign = np.where(first, -1.0, 1.0)
    return src, sign


def _take_cols(w, idx, sign=None):
    idx = np.asarray(idx)
    sg = np.where(idx >= 0, 1.0 if sign is None else np.asarray(sign, np.float64), 0.0)
    pieces, i, n = [], 0, len(idx)
    while i < n:
        j = i + 1
        if sg[i] == 0.0:
            while j < n and sg[j] == 0.0:
                j += 1
            pieces.append(jnp.zeros((w.shape[0], j - i), w.dtype))
        else:
            while j < n and sg[j] == sg[i] and idx[j] == idx[j - 1] + 1:
                j += 1
            piece = w[:, int(idx[i]):int(idx[i]) + (j - i)]
            pieces.append(piece if sg[i] == 1.0 else piece * sg[i])
        i = j
    return jnp.concatenate(pieces, axis=1)


def _kv_buffers(B, T, tm, n_kv, bufs, total_rows):
    qshape = jax.ShapeDtypeStruct((B, N_HEADS, T, HP), BF16)
    qspec = pl.BlockSpec((None, N_HEADS, tm, HP), lambda b, i: (b, 0, i, 0))
    if bufs is None:
        kvshape = jax.ShapeDtypeStruct((B, N_HEADS, total_rows, HP), BF16)
        return [], [qspec] * (1 + n_kv), [qshape] + [kvshape] * n_kv, []
    off = (bufs[0].shape[2] - T) // tm
    kvspec = pl.BlockSpec((None, N_HEADS, tm, HP), lambda b, i: (b, 0, off + i, 0))
    kvshape = jax.ShapeDtypeStruct(bufs[0].shape, BF16)
    return ([pl.BlockSpec(memory_space=pl.ANY)] * n_kv, [qspec] + [kvspec] * n_kv, [qshape] + [kvshape] * n_kv,
            list(bufs))


def _mla_prep_kernel(p_ref, cq_t, sq_t, ck_t, qn, kvn, wqc, wqs, wk, wv, epe, *rest):
    q_o, k_o, v_o = rest[-3:]
    p = p_ref[...]
    cq = p[:, 0:256]
    cqn = (cq * lax.rsqrt(jnp.sum(cq * cq, axis=-1, keepdims=True) * (1.0 / MLA_Q_RANK) + EPS) * qn[...]).astype(BF16)
    ckv = p[:, 256:384]
    ckvn = _rms(ckv, kvn[...]).astype(BF16)
    q = _dot(cqn, wqc[...]) * cq_t[...] + _dot(cqn, wqs[...]) * sq_t[...]
    pe = (p[:, 384:512] * ck_t[...]).astype(BF16)
    k = _dot(ckvn, wk[...]) + _dot(pe, epe[...])
    v = _dot(ckvn, wv[...])
    lane = lax.broadcasted_iota(jnp.int32, (p.shape[0], HP), 1)
    for h in range(N_HEADS):
        q_o[h] = q[:, h * HP:(h + 1) * HP].astype(BF16)
        k_o[h] = k[:, h * HP:(h + 1) * HP].astype(BF16)
        v_o[h] = jnp.where(lane == ONES_LANE, 1.0, v[:, h * HP:(h + 1) * HP]).astype(BF16)


def _mla_weights(lp):
    s = LOG2E / math.sqrt(MLA_NOPE + MLA_ROPE)
    src, sign = _rot_index()
    hd = MLA_NOPE + MLA_ROPE
    qc_idx, qs_idx, qs_sign, k_idx, v_idx = [], [], [], [], []
    for h in range(N_HEADS):
        qc_idx += list(h * hd + np.arange(hd)) + [-1] * (HP - hd)
        qs_idx += [-1] * MLA_NOPE + list(h * hd + MLA_NOPE + src) + [-1] * (HP - hd)
        qs_sign += [0.0] * MLA_NOPE + list(sign) + [0.0] * (HP - hd)
        k_idx += list(h * 128 + np.arange(MLA_NOPE)) + [-1] * (HP - MLA_NOPE)
        v_idx += list(h * 128 + MLA_NOPE + np.arange(64)) + [-1] * (HP - 64)
    wq = jnp.pad(lp['mla_wq_b'] * s, ((0, 256 - MLA_Q_RANK), (0, 0)))
    wqc = _take_cols(wq, qc_idx).astype(BF16)
    wqs = _take_cols(wq, qs_idx, qs_sign).astype(BF16)
    wk = _take_cols(lp['mla_wkv_b'], k_idx).astype(BF16)
    wv = _take_cols(lp['mla_wkv_b'], v_idx).astype(BF16)
    epe = np.zeros((128, N_HEADS * HP), np.float32)
    for h in range(N_HEADS):
        for j in range(MLA_ROPE):
            epe[j, h * HP + MLA_NOPE + j] = 1.0
            epe[MLA_ROPE + j, h * HP + MLA_NOPE + j] = 1.0
    qn = jnp.pad(lp['mla_q_norm'], (0, 256 - MLA_Q_RANK)).reshape(1, 256)
    return qn, lp['mla_kv_norm'].reshape(1, MLA_KV_RANK), wqc, wqs, wk, wv, jnp.asarray(epe, BF16)


def _mla_tables(cos, sin):
    T = cos.shape[0]
    one = jnp.ones((T, MLA_NOPE), F32)
    zero = jnp.zeros((T, HP - MLA_NOPE - MLA_ROPE), F32)
    zn = jnp.zeros((T, MLA_NOPE), F32)
    cq = jnp.tile(jnp.concatenate([one, cos, zero], -1), (1, N_HEADS))
    sq = jnp.tile(jnp.concatenate([zn, sin, zero], -1), (1, N_HEADS))
    ck = jnp.concatenate([cos, sin, jnp.zeros((T, 64), F32)], -1)
    return cq, sq, ck


def _mla_prep(p_mla, tables, weights, total_rows, bufs=None):
    B, T, C = p_mla.shape
    tm = _tile(T, 256)
    W = N_HEADS * HP
    tspecs = [pl.BlockSpec((tm, W), lambda b, i: (i, 0)), pl.BlockSpec((tm, W), lambda b, i: (i, 0)),
              pl.BlockSpec((tm, 128), lambda b, i: (i, 0))]
    wspecs = [pl.BlockSpec(w.shape, lambda b, i: (0, 0)) for w in weights]
    aspecs, ospecs, oshapes, extra = _kv_buffers(B, T, tm, 2, bufs, total_rows)
    n_in = 1 + len(tables) + len(weights)
    return pl.pallas_call(
        _mla_prep_kernel,
        grid=(B, T // tm),
        in_specs=[pl.BlockSpec((None, tm, C), lambda b, i: (b, i, 0))] + tspecs + wspecs + aspecs,
        out_specs=ospecs,
        out_shape=oshapes,
        input_output_aliases={n_in + j: 1 + j for j in range(len(extra))},
        compiler_params=_cp("parallel", "parallel"),
        name="mla_prepare",
    )(p_mla, *tables, *weights, *extra)


def _df_prep_kernel(p_ref, c_t, s_t, *rest):
    q_o, k1_o, k2_o, v_o = rest[-4:]
    W = N_HEADS * HP
    p = p_ref[...]
    c = c_t[...]
    s = s_t[...]
    q = p[:, 0:W] * c + p[:, W:2 * W] * s
    k = p[:, 2 * W:3 * W] * c + p[:, 3 * W:4 * W] * s
    v = p[:, 4 * W:5 * W]
    lane = lax.broadcasted_iota(jnp.int32, (p.shape[0], HP), 1)
    for h in range(N_HEADS):
        sl = slice(h * HP, (h + 1) * HP)
        q_o[h] = (q[:, sl] * (LOG2E / math.sqrt(DF_QK))).astype(BF16)
        kh = k[:, sl]
        k1_o[h] = jnp.where(lane < DF_QK, kh, 0.0).astype(BF16)
        k2_o[h] = jnp.where(lane >= DF_QK, kh, 0.0).astype(BF16)
        v_o[h] = jnp.where(lane == ONES_LANE, 1.0, v[:, sl]).astype(BF16)


def _df_tables(cos, sin):
    T = cos.shape[0]
    zero = jnp.zeros((T, HP - 2 * DF_QK), F32)
    c = jnp.tile(jnp.concatenate([cos, cos, zero], -1), (1, N_HEADS))
    s = jnp.tile(jnp.concatenate([sin, sin, zero], -1), (1, N_HEADS))
    return c, s


def _df_prep(p_df, tables, total_rows, bufs=None):
    B, T, C = p_df.shape
    tm = _tile(T, 256)
    W = N_HEADS * HP
    tspec = pl.BlockSpec((tm, W), lambda b, i: (i, 0))
    aspecs, ospecs, oshapes, extra = _kv_buffers(B, T, tm, 3, bufs, total_rows)
    return pl.pallas_call(
        _df_prep_kernel,
        grid=(B, T // tm),
        in_specs=[pl.BlockSpec((None, tm, C), lambda b, i: (b, i, 0)), tspec, tspec] + aspecs,
        out_specs=ospecs,
        out_shape=oshapes,
        input_output_aliases={3 + j: 1 + j for j in range(len(extra))},
        compiler_params=_cp("parallel", "parallel"),
        name="diff_prepare",
    )(p_df, *tables, *extra)


def _softmax_pv(q, k_ref, v_ref):
    s = _dot(q, k_ref[...], NT)
    p = jnp.exp2((s - jnp.max(s, axis=-1, keepdims=True)).astype(BF16))
    acc = _dot(p, v_ref[...])
    return acc / acc[:, ONES_LANE:ONES_LANE + 1]


def _mla_attn_kernel(q_ref, k_ref, v_ref, o_ref):
    o_ref[...] = _softmax_pv(q_ref[...], k_ref, v_ref)


def _attn_specs(q, key_rows):
    tq = _tile(q.shape[2], ATTN_TQ)
    first, n = key_rows
    assert first % n == 0
    qspec = pl.BlockSpec((None, None, tq, HP), lambda b, h, i: (b, h, i, 0))
    kspec = pl.BlockSpec((None, None, n, HP), lambda b, h, i: (b, h, first // n, 0))
    return tq, qspec, kspec


def _mla_attention(q, k, v, key_rows):
    B, H, Tq, _ = q.shape
    tq, qspec, kspec = _attn_specs(q, key_rows)
    return pl.pallas_call(
        _mla_attn_kernel,
        grid=(B, H, Tq // tq),
        in_specs=[qspec, kspec, kspec],
        out_specs=qspec,
        out_shape=jax.ShapeDtypeStruct((B, H, Tq, HP), F32),
        compiler_params=_cp("parallel", "parallel", "parallel"),
        name="mla_attention",
    )(q, k, v)


def _df_attn_kernel(lam_init, q_ref, k1_ref, k2_ref, v_ref, lq1, lk1, lq2, lk2, sub, o_ref):
    q = q_ref[...]
    lam = (jnp.exp(jnp.sum(lq1[...] * lk1[...], axis=-1, keepdims=True))
           - jnp.exp(jnp.sum(lq2[...] * lk2[...], axis=-1, keepdims=True)) + lam_init)
    o = _softmax_pv(q, k1_ref, v_ref) - lam * _softmax_pv(q, k2_ref, v_ref)
    lane = lax.broadcasted_iota(jnp.int32, o.shape, 1)
    o = jnp.where(lane < DF_V, o, 0.0)
    ms = jnp.sum(o * o, axis=-1, keepdims=True) * (1.0 / DF_V)
    o_ref[...] = o * lax.rsqrt(ms + DF_SUBLN_EPS) * sub[...] * (1.0 - lam_init)


def _df_attention(q, k1, k2, v, key_rows, lp, lam_init):
    B, H, Tq, _ = q.shape
    tq, qspec, kspec = _attn_specs(q, key_rows)
    lspec = pl.BlockSpec((1, DF_QK), lambda b, h, i: (0, 0))
    sub = jnp.pad(lp['df_subln'], (0, HP - DF_V)).reshape(1, HP)
    return pl.pallas_call(
        functools.partial(_df_attn_kernel, lam_init),
        grid=(B, H, Tq // tq),
        in_specs=[qspec, kspec, kspec, kspec, lspec, lspec, lspec, lspec,
                  pl.BlockSpec((1, HP), lambda b, h, i: (0, 0))],
        out_specs=qspec,
        out_shape=jax.ShapeDtypeStruct((B, H, Tq, HP), F32),
        compiler_params=_cp("parallel", "parallel", "parallel"),
        name="diff_attention",
    )(q, k1, k2, v, lp['df_lq1'].reshape(1, DF_QK), lp['df_lk1'].reshape(1, DF_QK),
      lp['df_lq2'].reshape(1, DF_QK), lp['df_lk2'].reshape(1, DF_QK), sub)


def _merge_kernel(x_ref, sh_ref, sc_ref, gt_ref, gpre_ref, gpost_ref, orw, ohy, omla, odf,
                  wg, bg, urw, uhy, umla, udf, wo, o_ref):
    x = x_ref[...]
    h = (_rms(x, gpre_ref[...]) * (1.0 + sc_ref[...]) + sh_ref[...]).astype(BF16)
    ups = (_dot1(orw[...], urw[...]),
           _dot1(ohy[...], uhy[...]),
           sum(_dot1(omla[hh], umla[hh]) for hh in range(N_HEADS)),
           sum(_dot1(odf[hh], udf[hh]) for hh in range(N_HEADS)))
    acc = None
    for n in range(4):
        t = _sigmoid(_dot(h, wg[n]) + bg[n]) * ups[n]
        acc = t if acc is None else acc + t
    y = _dot(acc.astype(BF16), wo[...])
    o_ref[...] = x + gt_ref[...] * _rms(y, gpost_ref[...])


def _pad_head_rows(w):
    return jnp.pad(w.reshape(N_HEADS, HEAD_DIM, -1), ((0, 0), (0, HP - HEAD_DIM), (0, 0)))


def _merge(x, mod3, g_pre, g_post, outs, mw):
    B, T, D = x.shape
    o_rw, o_hy, o_mla, o_df = outs
    tm = _tile(T, 512)
    xspec = pl.BlockSpec((None, tm, D), lambda b, i: (b, i, 0))
    bspec = pl.BlockSpec((None, tm, BR), lambda b, i: (b, i, 0))
    hspec = pl.BlockSpec((None, N_HEADS, tm, HP), lambda b, i: (b, 0, i, 0))

    def full(a):
        nd = a.ndim
        return pl.BlockSpec(a.shape, lambda b, i: (0,) * nd)

    ws = [mw['wg'], mw['bg'], mw['urw'], mw['uhy'], mw['umla'], mw['udf'], mw['wo']]
    return pl.pallas_call(
        _merge_kernel,
        grid=(B, T // tm),
        in_specs=[xspec, _mod_spec(mod3[0], 2), _mod_spec(mod3[1], 2), _mod_spec(mod3[2], 2),
                  _row_spec(D), _row_spec(D), bspec, bspec, hspec, hspec] + [full(a) for a in ws],
        out_specs=xspec,
        out_shape=jax.ShapeDtypeStruct((B, T, D), F32),
        compiler_params=_cp("parallel", "parallel"),
        name="gated_merge",
    )(x, mod3[0], mod3[1], mod3[2], g_pre.reshape(1, D), g_post.reshape(1, D), o_rw, o_hy, o_mla, o_df, *ws)


def _mix_weights(w_in):
    src, sign = _rot_index()
    w_rw = w_in[:, :RW_COLS]
    w_hy = w_in[:, OFF_HY:OFF_HY + HY_COLS]
    kpe0 = OFF_MLA + MLA_Q_RANK + MLA_KV_RANK
    mla_idx = (list(OFF_MLA + np.arange(MLA_Q_RANK)) + [-1] * (256 - MLA_Q_RANK)
               + list(OFF_MLA + MLA_Q_RANK + np.arange(MLA_KV_RANK))
               + list(kpe0 + np.arange(MLA_ROPE)) + list(kpe0 + src) + [-1] * 64)
    mla_sign = [1.0] * 256 + [1.0] * MLA_KV_RANK + [1.0] * MLA_ROPE + list(sign) + [0.0] * 64
    w_mla = _take_cols(w_in, mla_idx, mla_sign)
    pad = [-1] * (HP - 2 * DF_QK)
    src2 = np.concatenate([src, DF_QK + src])
    sign2 = np.concatenate([sign, sign])
    idx, sg = [], []
    for base, rot in ((OFF_DF, False), (OFF_DF, True), (OFF_DF + 256, False), (OFF_DF + 256, True),
                      (OFF_DF + 512, False)):
        for h in range(N_HEADS):
            cols = base + h * 64 + (src2 if rot else np.arange(64))
            idx += list(cols) + pad
            sg += (list(sign2) if rot else [1.0] * 64) + [0.0] * len(pad)
    w_df = _take_cols(w_in, idx, sg)
    return [w.astype(BF16) for w in (w_rw, w_hy, w_mla, w_df)]


def _merge_weights(lp):
    return dict(wg=lp['w_gate'].astype(BF16), bg=lp['b_gate'].reshape(4, 1, D_MODEL),
                urw=lp['w_up'][0].astype(BF16), uhy=lp['w_up'][1].astype(BF16),
                umla=_pad_head_rows(lp['w_up'][2]).astype(BF16), udf=_pad_head_rows(lp['w_up'][3]).astype(BF16),
                wo=lp['w_o'].astype(BF16))


def _token_mixing(x, xc, mod, modc, g_pre, lam_init, with_ctx_out, lp, rope):
    B, T, _ = x.shape
    ws = _mix_weights(lp['w_in'])
    p_rw, p_hy, p_mla, p_df = _project(x, mod[0], mod[1], g_pre, ws)
    c_rw, c_hy, c_mla, c_df = _project(xc, modc[0], modc[1], g_pre, ws)

    prep = _rwkv_prepare(p_rw, lp)
    prep_c = _rwkv_prepare(c_rw, lp)
    s0 = jnp.zeros((B, 2, N_HEADS, HEAD_DIM, HEAD_DIM), F32)
    yf_c, yb_c, s_c = _rwkv_scan(prep_c, s0)
    yf, yb, _ = _rwkv_scan(prep, s_c)
    o_rw = _rwkv_output(yf, yb, prep, lp)

    o_hy = _hyena(p_hy, lp)

    cos, sin = rope
    Tc = xc.shape[1]
    ones, zeros = jnp.ones((Tc, MLA_ROPE), F32), jnp.zeros((Tc, MLA_ROPE), F32)
    mla_w = _mla_weights(lp)
    q, k, v = _mla_prep(p_mla, _mla_tables(cos, sin), mla_w, T + Tc)
    qc, k, v = _mla_prep(c_mla, _mla_tables(ones, zeros), mla_w, T + Tc, bufs=(k, v))
    o_mla = _mla_attention(q, k, v, (0, T + Tc))

    dq, dk1, dk2, dv = _df_prep(p_df, _df_tables(cos, sin), T + Tc)
    dqc, dk1, dk2, dv = _df_prep(c_df, _df_tables(ones, zeros), T + Tc, bufs=(dk1, dk2, dv))
    o_df = _df_attention(dq, dk1, dk2, dv, (0, T + Tc), lp, lam_init)

    outs = (o_rw, o_hy, o_mla, o_df)
    if not with_ctx_out:
        return outs, None
    oc_rw = _rwkv_output(yf_c, yb_c, prep_c, lp)
    oc_hy = _hyena(c_hy, lp)
    oc_mla = _mla_attention(qc, k, v, (T, Tc))
    oc_df = _df_attention(dqc, dk1, dk2, dv, (T, Tc), lp, lam_init)
    return outs, (oc_rw, oc_hy, oc_mla, oc_df)


def kernel(x, c, ctx, c_ctx, w_mod, b_mod, norm_pre, norm_post, ffn_w_in, ffn_w_out, w_in, rw_mu, rw_w0, rw_w2, rw_a0, rw_a2, rw_g2, rw_kk, rw_ka, rw_rk, rw_ln_w, rw_ln_b, hy_conv_w, hy_conv_b, hy_w1, hy_b1, hy_f1, hy_w2, hy_b2, hy_f2, hy_w3, hy_bias, mla_q_norm, mla_wq_b, mla_kv_norm, mla_wkv_b, df_lq1, df_lk1, df_lq2, df_lk2, df_subln, w_up, w_gate, b_gate, w_o):
    params = dict(w_in=w_in, rw_mu=rw_mu, rw_w0=rw_w0, rw_w2=rw_w2, rw_a0=rw_a0, rw_a2=rw_a2, rw_g2=rw_g2,
                  rw_kk=rw_kk, rw_ka=rw_ka, rw_rk=rw_rk, rw_ln_w=rw_ln_w, rw_ln_b=rw_ln_b, hy_conv_w=hy_conv_w,
                  hy_conv_b=hy_conv_b, hy_w1=hy_w1, hy_b1=hy_b1, hy_f1=hy_f1, hy_w2=hy_w2, hy_b2=hy_b2,
                  hy_f2=hy_f2, hy_w3=hy_w3, hy_bias=hy_bias, mla_q_norm=mla_q_norm, mla_wq_b=mla_wq_b,
                  mla_kv_norm=mla_kv_norm, mla_wkv_b=mla_wkv_b, df_lq1=df_lq1, df_lk1=df_lk1, df_lq2=df_lq2,
                  df_lk2=df_lk2, df_subln=df_subln, w_up=w_up, w_gate=w_gate, b_gate=b_gate, w_o=w_o)
    B, T, D = x.shape
    depth = w_mod.shape[0]
    assert B <= 7 and D == D_MODEL and T % 128 == 0 and ctx.shape[1] % CHUNK == 0 and T % ctx.shape[1] == 0
    rope = _rope_tables(T)

    s_rows = jnp.concatenate([c, c_ctx[None], jnp.zeros((8 - B - 1, D), F32)], 0)
    mod_all = _modulation(s_rows, w_mod, b_mod).reshape(depth, 8, 3, 3, D)
    ffn_in = ffn_w_in.astype(BF16)
    ffn_out = ffn_w_out.astype(BF16)

    xc = ctx
    for l in range(depth):
        last = l == depth - 1
        lp = {k_: v_[l] for k_, v_ in params.items()}
        mod = [[mod_all[l, :B, s, m][:, None, :] for m in range(3)] for s in range(3)]
        modc = [[mod_all[l, B:B + 1, s, m][:, None, :] for m in range(3)] for s in range(3)]
        ffn_a = (norm_pre[l, 0], norm_post[l, 0], ffn_in[l, 0], ffn_out[l, 0])
        ffn_b = (norm_pre[l, 2], norm_post[l, 2], ffn_in[l, 1], ffn_out[l, 1])

        x = _ffn(x, mod[0], *ffn_a)
        xc = _ffn(xc, modc[0], *ffn_a)

        lam_init = 0.8 - 0.6 * math.exp(-0.3 * l)
        outs, outs_c = _token_mixing(x, xc, mod[1], modc[1], norm_pre[l, 1], lam_init, not last, lp, rope)
        mw = _merge_weights(lp)
        x = _merge(x, mod[1], norm_pre[l, 1], norm_post[l, 1], outs, mw)
        x = _ffn(x, mod[2], *ffn_b)
        if not last:
            xc = _merge(xc, modc[1], norm_pre[l, 1], norm_post[l, 1], outs_c, mw)
            xc = _ffn(xc, modc[2], *ffn_b)
    return x
```

```python
import functools
import math

import numpy as np
import jax
import jax.numpy as jnp
from jax import lax
from jax.experimental import pallas as pl
from jax.experimental.pallas import tpu as pltpu

F32 = jnp.float32
BF16 = jnp.bfloat16
MIX_OUT = BF16

D_MODEL = 1024
GRID_W = 64
N_HEADS = 4
HEAD_DIM = 64
BR = N_HEADS * HEAD_DIM
D_FF = 2816
MACARON_W = 0.5
EPS = 1e-6
ROPE_BASE = 10000.0
RW_LN_EPS = 64e-5
RW_COLS = 1152
HY_COLS = 768
HY_EMB = 33
HY_FAST_DECAY = 0.3
HY_SLOW_DECAY = 1.5
HY_TARGET = 1e-2
MLA_Q_RANK = 192
MLA_KV_RANK = 128
MLA_NOPE = 64
MLA_ROPE = 32
DF_QK = 32
DF_V = 64
DF_SUBLN_EPS = 1e-5
OFF_HY = RW_COLS
OFF_MLA = RW_COLS + HY_COLS
OFF_DF = OFF_MLA + MLA_Q_RANK + MLA_KV_RANK + MLA_ROPE

LANES = 128
HP = 128
ONES_LANE = 64
LOG2E = 1.4426950408889634
FFN_TM = 256
FFN_TF = 2816
ATTN_TQ = 256
FFT_DATA_PASSES = 1
CHUNK = 64
RW_P_GRAM = 1
RW_P_INV = (3, 3)
RW_P_STATE = 3
RW_P_APPLY = 1
VMEM_LIMIT = 56 * 1024 * 1024


def _cp(*sem):
    return pltpu.CompilerParams(dimension_semantics=sem, vmem_limit_bytes=VMEM_LIMIT)


def _tile(n, pref, mult=8):
    if n <= pref:
        return n
    t = (pref // mult) * mult
    while t > mult and n % t:
        t -= mult
    assert n % t == 0, (n, pref)
    return t


def _dot(a, b, dims=None):
    if dims is None:
        dims = (((a.ndim - 1,), (0,)), ((), ()))
    return lax.dot_general(a, b, dims, preferred_element_type=F32)


NT = (((1,), (1,)), ((), ()))
TN = (((0,), (0,)), ((), ()))


def _split(x):
    hi = x.astype(BF16)
    lo = (x - hi.astype(F32)).astype(BF16)
    return hi, lo


def _dot3(a, b, dims=None):
    ah, al = _split(a)
    bh, bl = _split(b)
    return _dot(ah, bh, dims) + (_dot(ah, bl, dims) + _dot(al, bh, dims))


def _dot1(a, b, dims=None):
    return _dot(a.astype(BF16), b.astype(BF16), dims)


def _dotc(ch, cl, x, passes=3):
    if passes == 1:
        return _dot(ch, x.astype(BF16))
    xh, xl = _split(x)
    return _dot(ch, xh) + (_dot(ch, xl) + _dot(cl, xh))


def _dot_exact_lhs(m, x):
    x1 = x.astype(BF16)
    r1 = x - x1.astype(F32)
    x2 = r1.astype(BF16)
    x3 = (r1 - x2.astype(F32)).astype(BF16)
    return _dot(m, x1) + (_dot(m, x2) + _dot(m, x3))


def _rms(x, g, eps=EPS):
    return x * lax.rsqrt(jnp.mean(x * x, axis=-1, keepdims=True) + eps) * g


def _sigmoid(x):
    return 1.0 / (1.0 + jnp.exp(-x))


def _silu(x):
    return x * _sigmoid(x)


def _softplus(z):
    return jnp.maximum(z, 0.0) + jnp.log(1.0 + jnp.exp(-jnp.abs(z)))


def _np_split(a):
    a = jnp.asarray(a, F32)
    hi = a.astype(BF16)
    lo = (a - hi.astype(F32)).astype(BF16)
    return hi, lo


def _mod_kernel(s_ref, w_ref, b_ref, o_ref):
    s = _silu(s_ref[...])
    o_ref[...] = _dot3(s, w_ref[...]) + b_ref[...]


def _modulation(s_rows, w_mod, b_mod):
    L, D, NM = w_mod.shape
    tn = _tile(NM, 1152, LANES)
    return pl.pallas_call(
        _mod_kernel,
        grid=(L, NM // tn),
        in_specs=[pl.BlockSpec((8, D), lambda l, j: (0, 0)),
                  pl.BlockSpec((None, D, tn), lambda l, j: (l, 0, j)),
                  pl.BlockSpec((None, 1, tn), lambda l, j: (l, 0, j))],
        out_specs=pl.BlockSpec((None, 8, tn), lambda l, j: (l, 0, j)),
        out_shape=jax.ShapeDtypeStruct((L, 8, NM), F32),
        compiler_params=_cp("parallel", "parallel"),
        name="modulation",
    )(s_rows, w_mod, b_mod.reshape(L, 1, NM))


def _mod_spec(m, nargs):
    if m.shape[0] == 1:
        return pl.BlockSpec((None, 1, D_MODEL), lambda *a: (0, 0, 0))
    return pl.BlockSpec((None, 1, D_MODEL), lambda *a: (a[0], 0, 0))


def _row_spec(n):
    return pl.BlockSpec((1, n), lambda *a: (0, 0))


def _ffn_kernel(x_ref, sh_ref, sc_ref, gt_ref, gpre_ref, gpost_ref, wa_ref, wb_ref, wo_ref, o_ref, h_sc, acc_sc):
    f = pl.program_id(2)

    @pl.when(f == 0)
    def _():
        h = _rms(x_ref[...], gpre_ref[...]) * (1.0 + sc_ref[...]) + sh_ref[...]
        h_sc[...] = h.astype(BF16)
        acc_sc[...] = jnp.zeros_like(acc_sc)

    h = h_sc[...]
    a = _dot(h, wa_ref[...])
    b = _dot(h, wb_ref[...])
    acc_sc[...] += _dot((_silu(a) * b).astype(BF16), wo_ref[...])

    @pl.when(f == pl.num_programs(2) - 1)
    def _():
        o_ref[...] = x_ref[...] + MACARON_W * gt_ref[...] * _rms(acc_sc[...], gpost_ref[...])


def _ffn(x, mod3, g_pre, g_post, w_in, w_out):
    B, T, D = x.shape
    F = w_out.shape[0]
    tm = _tile(T, FFN_TM)
    tf = _tile(F, FFN_TF, LANES)
    nf = F // tf
    xspec = pl.BlockSpec((None, tm, D), lambda b, i, f: (b, i, 0))
    return pl.pallas_call(
        _ffn_kernel,
        grid=(B, T // tm, nf),
        in_specs=[xspec, _mod_spec(mod3[0], 3), _mod_spec(mod3[1], 3), _mod_spec(mod3[2], 3),
                  _row_spec(D), _row_spec(D),
                  pl.BlockSpec((D, tf), lambda b, i, f: (0, f)),
                  pl.BlockSpec((D, tf), lambda b, i, f: (0, nf + f)),
                  pl.BlockSpec((tf, D), lambda b, i, f: (f, 0))],
        out_specs=xspec,
        out_shape=jax.ShapeDtypeStruct((B, T, D), F32),
        scratch_shapes=[pltpu.VMEM((tm, D), BF16), pltpu.VMEM((tm, D), F32)],
        compiler_params=_cp("parallel", "parallel", "arbitrary"),
        name="ffn",
    )(x, mod3[0], mod3[1], mod3[2], g_pre.reshape(1, D), g_post.reshape(1, D), w_in, w_in, w_out)


def _proj_kernel(x_ref, sh_ref, sc_ref, gpre_ref, wrw, why, wmla, wdf, orw, ohy, omla, odf):
    h = (_rms(x_ref[...], gpre_ref[...]) * (1.0 + sc_ref[...]) + sh_ref[...]).astype(BF16)
    orw[...] = _dot(h, wrw[...])
    ohy[...] = _dot(h, why[...])
    omla[...] = _dot(h, wmla[...]).astype(omla.dtype)
    odf[...] = _dot(h, wdf[...]).astype(odf.dtype)


def _project(x, shift, scale, g_pre, ws):
    B, T, D = x.shape
    tm = _tile(T, 256)
    xspec = pl.BlockSpec((None, tm, D), lambda b, i: (b, i, 0))
    wspecs = [pl.BlockSpec(w.shape, lambda b, i: (0, 0)) for w in ws]
    ospecs = [pl.BlockSpec((None, tm, w.shape[1]), lambda b, i: (b, i, 0)) for w in ws]
    oshapes = [jax.ShapeDtypeStruct((B, T, w.shape[1]), dt) for w, dt in zip(ws, (F32, F32, BF16, BF16))]
    return pl.pallas_call(
        _proj_kernel,
        grid=(B, T // tm),
        in_specs=[xspec, _mod_spec(shift, 2), _mod_spec(scale, 2), _row_spec(D)] + wspecs,
        out_specs=ospecs,
        out_shape=oshapes,
        compiler_params=_cp("parallel", "parallel"),
        name="mix_project",
    )(x, shift, scale, g_pre.reshape(1, D), *ws)


def _halo_specs(tm, T, C):
    nb8 = T // 8
    r = tm // 8
    main = pl.BlockSpec((None, tm, C), lambda b, i: (b, i, 0))
    prev = pl.BlockSpec((None, 8, C), lambda b, i: (b, jnp.maximum(i * r - 1, 0), 0))
    nxt = pl.BlockSpec((None, 8, C), lambda b, i: (b, jnp.minimum((i + 1) * r, nb8 - 1), 0))
    return main, prev, nxt


def _conv3_tile(p, prev8, next8, w0, w1, w2):
    i = pl.program_id(1)
    n_i = pl.num_programs(1)
    tm = p.shape[0]
    row = lax.broadcasted_iota(jnp.int32, p.shape, 0)
    prev_row = jnp.where(i > 0, prev8[7:8, :], 0.0)
    next_row = jnp.where(i < n_i - 1, next8[0:1, :], 0.0)
    pm = jnp.where(row == 0, prev_row, pltpu.roll(p, 1, 0))
    pp = jnp.where(row == tm - 1, next_row, pltpu.roll(p, tm - 1, 0))
    return pm * w0 + p * w1 + pp * w2


def _rwkv_prep_kernel(p_ref, pv_ref, nx_ref, mu_ref, w0_ref, w2_ref, a0_ref, a2_ref, g2_ref, kk_ref, ka_ref,
                      rk_ref, bd_ref, trif_ref, trib_ref, onec_ref, v_o, g_o, bonus_o, *dir_outs):
    mu = mu_ref[...]
    p = _conv3_tile(p_ref[...], pv_ref[...], nx_ref[...], 0.5 * mu, 1.0 - mu, 0.5 * mu)
    r = p[:, 0:BR]
    k = p[:, BR:2 * BR]
    v = p[:, 2 * BR:3 * BR]
    g_a = p[:, 3 * BR:3 * BR + 128]
    wa = p[:, 896:1024]
    aa = p[:, 1024:1152]
    bd = bd_ref[...]
    kkraw = k * kk_ref[...]
    ss = _dot_exact_lhs_rhs(kkraw * kkraw, bd)
    kk = kkraw / jnp.maximum(jnp.sqrt(ss), 1e-12)
    wl = w0_ref[...] + _dot3(jnp.tanh(wa), w2_ref[...])
    lw = -jnp.exp(-_softplus(-wl) - 0.5)
    a = _sigmoid(a0_ref[...] + _dot3(aa, a2_ref[...]))
    a_f = a[:, :BR]
    a_b = a[:, BR:]
    ka = ka_ref[...]
    k_f = k * (1.0 + (a_f - 1.0) * ka)
    k_b = k * (1.0 + (a_b - 1.0) * ka)
    v_o[...] = v
    g_o[...] = _dot1(_sigmoid(g_a), g2_ref[...])
    bonus_o[...] = _dot_exact_lhs_rhs(r * (k_f + k_b) * rk_ref[...], bd) * v
    onec = onec_ref[...]
    for d, (tri_ref, k_d, a_d) in enumerate(((trif_ref, k_f, a_f), (trib_ref, k_b, a_b))):
        rt_o, at_o, bt_o, kt_o, bh_o, kh_o, eg_o = dir_outs[7 * d:7 * d + 7]
        lw_d = lw[:, d * BR:(d + 1) * BR]
        b_d = kk * a_d
        g = _dot_exact_lhs(tri_ref[...], lw_d)
        gtot = _dot_exact_lhs(onec, lw_d)
        eng = jnp.exp(-g)
        et = jnp.exp(gtot - g)
        rt_o[...] = r * jnp.exp(g)
        at_o[...] = -kk * jnp.exp(g - lw_d)
        bt_o[...] = b_d * eng
        kt_o[...] = k_d * eng
        bh_o[...] = b_d * et
        kh_o[...] = k_d * et
        eg_o[...] = jnp.exp(gtot)


def _dot_exact_lhs_rhs(x, m):
    x1 = x.astype(BF16)
    r1 = x - x1.astype(F32)
    x2 = r1.astype(BF16)
    x3 = (r1 - x2.astype(F32)).astype(BF16)
    return _dot(x1, m) + (_dot(x2, m) + _dot(x3, m))


def _block_diag_ones():
    h = np.arange(BR) // HEAD_DIM
    return jnp.asarray((h[:, None] == h[None, :]).astype(np.float32), BF16)


def _rwkv_prepare(p_rw, lp):
    B, T, C = p_rw.shape
    tm = _tile(T, 256)
    main, prev, nxt = _halo_specs(tm, T, C)
    z = jnp.zeros((64, BR), F32)
    w2 = jnp.concatenate([jnp.concatenate([lp['rw_w2'][0], z], 1), jnp.concatenate([z, lp['rw_w2'][1]], 1)], 0)
    a2 = jnp.concatenate([jnp.concatenate([lp['rw_a2'][0], z], 1), jnp.concatenate([z, lp['rw_a2'][1]], 1)], 0)
    params = [lp['rw_mu'].reshape(1, C), lp['rw_w0'].reshape(1, 2 * BR), w2, lp['rw_a0'].reshape(1, 2 * BR), a2,
              lp['rw_g2'].astype(BF16), lp['rw_kk'].reshape(1, BR), lp['rw_ka'].reshape(1, BR),
              lp['rw_rk'].reshape(1, BR), _block_diag_ones()]
    pos = np.arange(tm)
    same = (pos[:, None] // CHUNK) == (pos[None, :] // CHUNK)
    params += [jnp.asarray((same & (pos[None, :] <= pos[:, None])).astype(np.float32), BF16),
               jnp.asarray((same & (pos[None, :] >= pos[:, None])).astype(np.float32), BF16),
               jnp.asarray(same.astype(np.float32), BF16)]
    pspecs = [pl.BlockSpec(a.shape, lambda b, i: (0, 0)) for a in params]
    ospec = pl.BlockSpec((None, tm, BR), lambda b, i: (b, i, 0))
    names = ('v', 'g', 'bonus') + tuple(n + str(d) for d in (0, 1) for n in ('rt', 'at', 'bt', 'kt', 'bh', 'kh', 'eg'))
    outs = pl.pallas_call(
        _rwkv_prep_kernel,
        grid=(B, T // tm),
        in_specs=[main, prev, nxt] + pspecs,
        out_specs=[ospec] * len(names),
        out_shape=[jax.ShapeDtypeStruct((B, T, BR), F32)] * len(names),
        compiler_params=_cp("parallel", "parallel"),
        name="rwkv_prepare",
    )(p_rw, p_rw, p_rw, *params)
    return dict(zip(names, outs))


def _dot3s(a, b, dims=None):
    if dims is None:
        dims = (((1,), (0,)), ((), ()))
    ca, cb = dims[0][0][0], dims[0][1][0]
    a_lo = a - a.astype(BF16).astype(F32)
    b_lo = b - b.astype(BF16).astype(F32)
    sa = jnp.concatenate([a, a_lo, a], axis=ca).astype(BF16)
    sb = jnp.concatenate([b, b, b_lo], axis=cb).astype(BF16)
    return _dot(sa, sb, dims)


def _mmp(a, b, dims, passes):
    return _dot3s(a, b, dims) if passes == 3 else _dot1(a, b, dims)


def _rwkv_chunks(chains, eye):
    n = range(len(chains))
    rt, at, bt, kt, bh, kh, eg, v, ht, mask_s, mask_i = zip(*chains)
    C = rt[0].shape[0]
    gm = [_mmp(jnp.concatenate([at[i], rt[i]], 0), jnp.concatenate([bt[i], kt[i]], 0), NT, RW_P_GRAM) for i in n]
    top = [jnp.where(mask_s[i], gm[i][:C], 0.0) for i in n]
    bot = [jnp.where(mask_i[i], gm[i][C:], 0.0) for i in n]
    pw = [top[i][:, :C] for i in n]
    tinv = [eye + pw[i] for i in n]
    for _ in range(int(math.log2(C)) - 1):
        pw = [_mmp(pw[i], pw[i], None, RW_P_INV[0]) for i in n]
        upd = [_mmp(tinv[i], pw[i], None, RW_P_INV[1]) for i in n]
        tinv = [tinv[i] + upd[i] for i in n]
    w0 = [_mmp(at[i], ht[i], NT, RW_P_STATE) for i in n]
    w1 = [_mmp(top[i][:, C:], v[i], None, RW_P_APPLY) for i in n]
    u = [_mmp(tinv[i], w0[i] + w1[i], None, RW_P_STATE) for i in n]
    uv = [jnp.concatenate([u[i], v[i]], 0) for i in n]
    y0 = [_mmp(rt[i], ht[i], NT, RW_P_STATE) for i in n]
    y1 = [_mmp(bot[i], uv[i], None, RW_P_APPLY) for i in n]
    hu = [_mmp(uv[i], jnp.concatenate([bh[i], kh[i]], 0), TN, RW_P_STATE) for i in n]
    return [(y0[i] + y1[i], ht[i] * eg[i] + hu[i]) for i in n]


def _rwkv_scan_kernel(*refs):
    fwd_in, bwd_in = refs[0:8], refs[8:16]
    s0_ref, yf_o, yb_o, s_o, st = refs[16:]
    c = pl.program_id(0)
    nb, C = yf_o.shape[0], yf_o.shape[1]

    @pl.when(c == 0)
    def _():
        st[...] = s0_ref[...]

    row = lax.broadcasted_iota(jnp.int32, (C, 2 * C), 0)
    col = lax.broadcasted_iota(jnp.int32, (C, 2 * C), 1)
    col = jnp.where(col >= C, col - C, col)
    eye = (lax.broadcasted_iota(jnp.int32, (C, C), 0) == lax.broadcasted_iota(jnp.int32, (C, C), 1)).astype(F32)
    chains = []
    for b in range(nb):
        for d, ins in enumerate((fwd_in, bwd_in)):
            if d == 0:
                mask_i = col <= row
                mask_s = col < row
            else:
                mask_i = col >= row
                mask_s = col > row
            for h in range(N_HEADS):
                sl = pl.ds(h * HEAD_DIM, HEAD_DIM)
                rt, at, bt, kt, bh, kh, eg, v = (ref[b, :, sl] for ref in ins)
                chains.append((rt, at, bt, kt, bh, kh, eg[0:1, :], v, st[b, d, h], mask_s, mask_i))
    results = iter(_rwkv_chunks(chains, eye))
    for b in range(nb):
        for d, y_o in enumerate((yf_o, yb_o)):
            for h in range(N_HEADS):
                y, hn = next(results)
                y_o[b, :, pl.ds(h * HEAD_DIM, HEAD_DIM)] = y
                st[b, d, h] = hn

    @pl.when(c == pl.num_programs(0) - 1)
    def _():
        s_o[...] = st[...]


def _rwkv_scan(prep, s0):
    B, T, _ = prep['v'].shape
    nc = T // CHUNK
    fwd = pl.BlockSpec((B, CHUNK, BR), lambda c: (0, c, 0))
    bwd = pl.BlockSpec((B, CHUNK, BR), lambda c: (0, nc - 1 - c, 0))
    sspec = pl.BlockSpec(s0.shape, lambda c: (0, 0, 0, 0, 0))
    ins = [prep[n + str(d)] for d in (0, 1) for n in ('rt', 'at', 'bt', 'kt', 'bh', 'kh', 'eg')]
    return pl.pallas_call(
        _rwkv_scan_kernel,
        grid=(nc,),
        in_specs=[fwd] * 8 + [bwd] * 8 + [sspec],
        out_specs=[fwd, bwd, sspec],
        out_shape=[jax.ShapeDtypeStruct((B, T, BR), F32), jax.ShapeDtypeStruct((B, T, BR), F32),
                   jax.ShapeDtypeStruct(s0.shape, F32)],
        scratch_shapes=[pltpu.VMEM(s0.shape, F32)],
        compiler_params=_cp("arbitrary"),
        name="rwkv_scan",
    )(*ins[:7], prep['v'], *ins[7:], prep['v'], s0)


def _rwkv_out_kernel(yf, yb, g, bonus, lnw, lnb, bd, o):
    y = yf[...] + yb[...]
    m = bd[...]
    inv_n = 1.0 / HEAD_DIM
    mean = _dot_exact_lhs_rhs(y, m) * inv_n
    yc = y - mean
    var = _dot_exact_lhs_rhs(yc * yc, m) * inv_n
    yn = yc * lax.rsqrt(var + RW_LN_EPS) * lnw[...] + lnb[...]
    o[...] = ((yn + bonus[...]) * g[...]).astype(o.dtype)


def _rwkv_output(yf, yb, prep, lp):
    B, T, _ = yf.shape
    tm = _tile(T, 512)
    spec = pl.BlockSpec((None, tm, BR), lambda b, i: (b, i, 0))
    return pl.pallas_call(
        _rwkv_out_kernel,
        grid=(B, T // tm),
        in_specs=[spec] * 4 + [_row_spec(BR), _row_spec(BR), pl.BlockSpec((BR, BR), lambda b, i: (0, 0))],
        out_specs=spec,
        out_shape=jax.ShapeDtypeStruct((B, T, BR), MIX_OUT),
        compiler_params=_cp("parallel", "parallel"),
        name="rwkv_output",
    )(yf, yb, prep['g'], prep['bonus'], lp['rw_ln_w'].reshape(1, BR), lp['rw_ln_b'].reshape(1, BR),
      _block_diag_ones())


def _hy_pre_kernel(p_ref, pv_ref, nx_ref, w_ref, b_ref, x0_o, z_o):
    w = w_ref[...]
    u = _conv3_tile(p_ref[...], pv_ref[...], nx_ref[...], w[0:1, :], w[1:2, :], w[2:3, :]) + b_ref[...]
    x0_o[...] = u[:, 0:BR]
    z_o[...] = u[:, 2 * BR:3 * BR] * u[:, BR:2 * BR]


def _hyena_pre(p_hy, lp):
    B, T, C = p_hy.shape
    tm = _tile(T, 512)
    main, prev, nxt = _halo_specs(tm, T, C)
    ospec = pl.BlockSpec((None, tm, BR), lambda b, i: (b, i, 0))
    return pl.pallas_call(
        _hy_pre_kernel,
        grid=(B, T // tm),
        in_specs=[main, prev, nxt, pl.BlockSpec((3, C), lambda b, i: (0, 0)), _row_spec(C)],
        out_specs=[ospec, ospec],
        out_shape=[jax.ShapeDtypeStruct((B, T, BR), F32)] * 2,
        compiler_params=_cp("parallel", "parallel"),
        name="hyena_pre",
    )(p_hy, p_hy, p_hy, lp['hy_conv_w'], lp['hy_conv_b'].reshape(1, C))


def _hy_filter_kernel(z_ref, t_ref, mf_ref, mb_ref, w1, b1, f1, w2, b2, f2, w3, dl, k_o, ss_o):
    h = jnp.sin(f1[...] * (_dot3(z_ref[...], w1[...]) + b1[...]))
    h = jnp.sin(f2[...] * (_dot3(h, w2[...]) + b2[...]))
    h = _dot3(h, w3[...]) * jnp.exp(-t_ref[...] * dl[...])
    kern = mf_ref[...] * h[:, :BR] + mb_ref[...] * h[:, BR:]
    k_o[...] = kern

    @pl.when(pl.program_id(0) == 0)
    def _():
        ss_o[...] = jnp.zeros_like(ss_o)

    ss_o[...] += jnp.sum(kern * kern, axis=0, keepdims=True)


def _hyena_filter(n, lp):
    t = jnp.linspace(0.0, 1.0, n, dtype=F32)[:, None]
    bands = (HY_EMB - 1) // 2
    wpos = 2.0 * math.pi * jnp.arange(n, dtype=F32) / n
    fr = jnp.linspace(1e-4, bands - 1, bands, dtype=F32)
    ang = wpos[:, None] * fr[None]
    z = jnp.concatenate([t, jnp.cos(ang), -jnp.sin(ang)], axis=-1)
    rho = np.arange(2 * n)
    z_ext = jnp.pad(jnp.concatenate([z, z[::-1]], 0), ((0, 0), (0, LANES - HY_EMB)))
    t_ext = jnp.concatenate([t, t[::-1]], 0)
    mf = jnp.asarray((rho < n).astype(np.float32))[:, None]
    mb = jnp.asarray((rho > n).astype(np.float32))[:, None]
    max_decay = math.log(HY_TARGET) / HY_FAST_DECAY
    min_decay = math.log(HY_TARGET) / HY_SLOW_DECAY
    deltas = jnp.abs(jnp.linspace(min_decay, max_decay, BR, dtype=F32))
    dl = jnp.tile(deltas, 2)[None]
    w1 = jnp.pad(lp['hy_w1'], ((0, LANES - HY_EMB), (0, 0)))
    params = [w1, lp['hy_b1'][None], lp['hy_f1'][None], lp['hy_w2'], lp['hy_b2'][None], lp['hy_f2'][None],
              lp['hy_w3'], dl]
    tm = _tile(2 * n, 512)
    return pl.pallas_call(
        _hy_filter_kernel,
        grid=(2 * n // tm,),
        in_specs=[pl.BlockSpec((tm, LANES), lambda i: (i, 0))] + [pl.BlockSpec((tm, 1), lambda i: (i, 0))] * 3
        + [pl.BlockSpec(a.shape, lambda i: (0, 0)) for a in params],
        out_specs=[pl.BlockSpec((tm, BR), lambda i: (i, 0)), pl.BlockSpec((1, BR), lambda i: (0, 0))],
        out_shape=[jax.ShapeDtypeStruct((2 * n, BR), F32), jax.ShapeDtypeStruct((1, BR), F32)],
        compiler_params=_cp("arbitrary"),
        name="hyena_filter",
    )(z_ext, t_ext, mf, mb, *params)


def _dft_tables(N1, N2):
    N = N1 * N2

    def mat(n):
        a = -2.0 * np.pi * np.outer(np.arange(n), np.arange(n)) / n
        return np.cos(a), np.sin(a)

    f1r, f1i = mat(N1)
    f2r, f2i = mat(N2)
    a = -2.0 * np.pi * np.outer(np.arange(N1), np.arange(N2)) / N
    return dict(f1r=f1r, f1i=f1i, f2r=f2r, f2i=f2i,
                twr=jnp.asarray(np.cos(a)[:, :, None], F32), twi=jnp.asarray(np.sin(a)[:, :, None], F32))


def _fft_in_kernel(passes, x_ref, frh, frl, fih, fil, ar_o, ai_o):
    x = x_ref[...]
    ar_o[...] = _dotc(frh[...], frl[...], x, passes)
    ai_o[...] = _dotc(fih[...], fil[...], x, passes)


def _fft_in(x2d, tabs, N1, passes):
    B, n1u, cols = x2d.shape
    consts = list(_np_split(tabs['f1r'][:, :n1u])) + list(_np_split(tabs['f1i'][:, :n1u]))
    tc = _tile(cols, 8192, LANES)
    ospec = pl.BlockSpec((None, N1, tc), lambda b, j: (b, 0, j))
    return pl.pallas_call(
        functools.partial(_fft_in_kernel, passes),
        grid=(B, cols // tc),
        in_specs=[pl.BlockSpec((None, n1u, tc), lambda b, j: (b, 0, j))]
        + [pl.BlockSpec((N1, n1u), lambda b, j: (0, 0))] * 4,
        out_specs=[ospec, ospec],
        out_shape=[jax.ShapeDtypeStruct((B, N1, cols), F32)] * 2,
        compiler_params=_cp("parallel", "parallel"),
        name="fft_first_axis",
    )(x2d, *consts)


def _cplx_dft(frh, frl, fih, fil, xr, xi, conj, passes):
    rr = _dotc(frh, frl, xr, passes)
    ii = _dotc(fih, fil, xi, passes)
    ri = _dotc(frh, frl, xi, passes)
    ir = _dotc(fih, fil, xr, passes)
    if conj:
        return rr + ii, ri - ir
    return rr - ii, ri + ir


def _fft_spec_kernel(ar, ai, twr, twi, frh, frl, fih, fil, sc, kr_o, ki_o):
    tr, ti = twr[...], twi[...]
    xr = ar[...] * tr - ai[...] * ti
    xi = ar[...] * ti + ai[...] * tr
    br, bi = _cplx_dft(frh[...], frl[...], fih[...], fil[...], xr, xi, False, 3)
    kr_o[...] = br * sc[...]
    ki_o[...] = bi * sc[...]


def _fft_conv_kernel(ar, ai, twr, twi, frh, frl, fih, fil, kr, ki, dr_o, di_o):
    tr, ti = twr[...], twi[...]
    f = (frh[...], frl[...], fih[...], fil[...])
    xr = ar[...] * tr - ai[...] * ti
    xi = ar[...] * ti + ai[...] * tr
    br, bi = _cplx_dft(*f, xr, xi, False, FFT_DATA_PASSES)
    cr = br * kr[...] - bi * ki[...]
    ci = br * ki[...] + bi * kr[...]
    dr, di = _cplx_dft(*f, cr, ci, True, FFT_DATA_PASSES)
    dr_o[...] = dr * tr + di * ti
    di_o[...] = di * tr - dr * ti


def _fft_mid(a_re, a_im, tabs, kf=None, scale=None):
    B, N1, N2, C = a_re.shape
    consts = list(_np_split(tabs['f2r'])) + list(_np_split(tabs['f2i']))
    aspec = pl.BlockSpec((None, None, N2, C), lambda b, k: (b, k, 0, 0))
    tspec = pl.BlockSpec((None, N2, 1), lambda b, k: (k, 0, 0))
    fspec = pl.BlockSpec((N2, N2), lambda b, k: (0, 0))
    if kf is None:
        extra, especs, kern, nm = [scale], [_row_spec(C)], _fft_spec_kernel, "fft_filter_spectrum"
    else:
        kspec = pl.BlockSpec((None, None, N2, C), lambda b, k: (0, k, 0, 0))
        extra, especs, kern, nm = list(kf), [kspec, kspec], _fft_conv_kernel, "fft_second_axis_conv"
    return pl.pallas_call(
        kern,
        grid=(B, N1),
        in_specs=[aspec, aspec, tspec, tspec] + [fspec] * 4 + especs,
        out_specs=[aspec, aspec],
        out_shape=[jax.ShapeDtypeStruct((B, N1, N2, C), F32)] * 2,
        compiler_params=_cp("parallel", "parallel"),
        name=nm,
    )(a_re, a_im, tabs['twr'], tabs['twi'], *consts, *extra)


def _fft_out_kernel(dr, di, frh, frl, fih, fil, x0, z, bias, o):
    zc = (_dotc(frh[...], frl[...], dr[...], FFT_DATA_PASSES)
          + _dotc(fih[...], fil[...], di[...], FFT_DATA_PASSES))
    zz = z[...]
    o[...] = (x0[...] * (zc + zz * bias[...])).astype(o.dtype)


def _fft_out(d_re, d_im, tabs, x0_2d, z_2d, bias_row):
    B, N1, cols = d_re.shape
    n1u = x0_2d.shape[1]
    consts = list(_np_split(tabs['f1r'][:n1u, :])) + list(_np_split(tabs['f1i'][:n1u, :]))
    tc = _tile(cols, 8192, LANES)
    dspec = pl.BlockSpec((None, N1, tc), lambda b, j: (b, 0, j))
    xspec = pl.BlockSpec((None, n1u, tc), lambda b, j: (b, 0, j))
    return pl.pallas_call(
        _fft_out_kernel,
        grid=(B, cols // tc),
        in_specs=[dspec, dspec] + [pl.BlockSpec((n1u, N1), lambda b, j: (0, 0))] * 4
        + [xspec, xspec, pl.BlockSpec((1, tc), lambda b, j: (0, j))],
        out_specs=xspec,
        out_shape=jax.ShapeDtypeStruct((B, n1u, cols), MIX_OUT),
        compiler_params=_cp("parallel", "parallel"),
        name="fft_last_axis_gate",
    )(d_re, d_im, *consts, x0_2d, z_2d, bias_row)


def _hy_gate_kernel(zc, x0, z, bias, o):
    o[...] = (x0[...] * (zc[...] + z[...] * bias[...])).astype(o.dtype)


def _hyena(p_hy, lp):
    B, n, _ = p_hy.shape
    x0, z = _hyena_pre(p_hy, lp)
    kern, ss = _hyena_filter(n, lp)
    N = 2 * n
    N2 = min(256, n)
    N1 = N // N2
    tabs = _dft_tables(N1, N2)
    scale = lax.rsqrt(ss) * (1.0 / N)
    bias = lp['hy_bias'].reshape(1, BR)
    if N1 >= 4:
        n1u = N1 // 2
        k_re, k_im = _fft_in(kern.reshape(1, N1, N2 * BR), tabs, N1, 3)
        kf = _fft_mid(k_re.reshape(1, N1, N2, BR), k_im.reshape(1, N1, N2, BR), tabs, scale=scale)
        a_re, a_im = _fft_in(z.reshape(B, n1u, N2 * BR), tabs, N1, FFT_DATA_PASSES)
        d_re, d_im = _fft_mid(a_re.reshape(B, N1, N2, BR), a_im.reshape(B, N1, N2, BR), tabs, kf=kf)
        o = _fft_out(d_re.reshape(B, N1, N2 * BR), d_im.reshape(B, N1, N2 * BR), tabs,
                     x0.reshape(B, n1u, N2 * BR), z.reshape(B, n1u, N2 * BR), jnp.tile(bias, (1, N2)))
        return o.reshape(B, n, BR)
    tabs = _dft_tables(1, N)
    kf = _fft_mid(kern.reshape(1, 1, N, BR), jnp.zeros((1, 1, N, BR), F32), tabs, scale=scale)
    zp = jnp.pad(z, ((0, 0), (0, n), (0, 0))).reshape(B, 1, N, BR)
    d_re, _ = _fft_mid(zp, jnp.zeros_like(zp), tabs, kf=kf)
    zc = d_re.reshape(B, N, BR)[:, :n]
    tm = _tile(n, 512)
    spec = pl.BlockSpec((None, tm, BR), lambda b, i: (b, i, 0))
    return pl.pallas_call(
        _hy_gate_kernel,
        grid=(B, n // tm),
        in_specs=[spec, spec, spec, _row_spec(BR)],
        out_specs=spec,
        out_shape=jax.ShapeDtypeStruct((B, n, BR), MIX_OUT),
        compiler_params=_cp("parallel", "parallel"),
        name="hyena_gate",
    )(zc, x0, z, bias)


def _rope_tables(n_lat):
    rows = n_lat // GRID_W
    row = jnp.repeat(jnp.arange(rows, dtype=F32), GRID_W)
    col = jnp.tile(jnp.arange(GRID_W, dtype=F32), rows)
    half = MLA_ROPE // 2
    inv = ROPE_BASE ** (-jnp.arange(0, half, 2, dtype=F32) / half)
    ar = row[:, None] * inv[None]
    ac = col[:, None] * inv[None]
    cos = jnp.concatenate([jnp.cos(ar), jnp.cos(ar), jnp.cos(ac), jnp.cos(ac)], axis=-1)
    sin = jnp.concatenate([jnp.sin(ar), jnp.sin(ar), jnp.sin(ac), jnp.sin(ac)], axis=-1)
    return cos, sin


def _rot_index():
    j = np.arange(32)
    first = (j % 16) < 8
    src = np.where(first, j + 8, j - 8)
    sign = np.where(first, -1.0, 1.0)
    return src, sign


def _take_cols(w, idx, sign=None):
    idx = np.asarray(idx)
    sg = np.where(idx >= 0, 1.0 if sign is None else np.asarray(sign, np.float64), 0.0)
    pieces, i, n = [], 0, len(idx)
    while i < n:
        j = i + 1
        if sg[i] == 0.0:
            while j < n and sg[j] == 0.0:
                j += 1
            pieces.append(jnp.zeros((w.shape[0], j - i), w.dtype))
        else:
            while j < n and sg[j] == sg[i] and idx[j] == idx[j - 1] + 1:
                j += 1
            piece = w[:, int(idx[i]):int(idx[i]) + (j - i)]
            pieces.append(piece if sg[i] == 1.0 else piece * sg[i])
        i = j
    return jnp.concatenate(pieces, axis=1)


def _kv_buffers(B, T, tm, n_kv, bufs, total_rows):
    qshape = jax.ShapeDtypeStruct((B, N_HEADS, T, HP), BF16)
    qspec = pl.BlockSpec((None, N_HEADS, tm, HP), lambda b, i: (b, 0, i, 0))
    if bufs is None:
        kvshape = jax.ShapeDtypeStruct((B, N_HEADS, total_rows, HP), BF16)
        return [], [qspec] * (1 + n_kv), [qshape] + [kvshape] * n_kv, []
    off = (bufs[0].shape[2] - T) // tm
    kvspec = pl.BlockSpec((None, N_HEADS, tm, HP), lambda b, i: (b, 0, off + i, 0))
    kvshape = jax.ShapeDtypeStruct(bufs[0].shape, BF16)
    return ([pl.BlockSpec(memory_space=pl.ANY)] * n_kv, [qspec] + [kvspec] * n_kv, [qshape] + [kvshape] * n_kv,
            list(bufs))


def _mla_prep_kernel(p_ref, cq_t, sq_t, ck_t, qn, kvn, wqc, wqs, wk, wv, epe, *rest):
    q_o, k_o, v_o = rest[-3:]
    p = p_ref[...].astype(F32)
    cq = p[:, 0:256]
    cqn = (cq * lax.rsqrt(jnp.sum(cq * cq, axis=-1, keepdims=True) * (1.0 / MLA_Q_RANK) + EPS) * qn[...]).astype(BF16)
    ckv = p[:, 256:384]
    ckvn = _rms(ckv, kvn[...]).astype(BF16)
    q = _dot(cqn, wqc[...]) * cq_t[...] + _dot(cqn, wqs[...]) * sq_t[...]
    pe = (p[:, 384:512] * ck_t[...]).astype(BF16)
    k = _dot(ckvn, wk[...]) + _dot(pe, epe[...])
    v = _dot(ckvn, wv[...])
    lane = lax.broadcasted_iota(jnp.int32, (p.shape[0], HP), 1)
    for h in range(N_HEADS):
        q_o[h] = q[:, h * HP:(h + 1) * HP].astype(BF16)
        k_o[h] = k[:, h * HP:(h + 1) * HP].astype(BF16)
        v_o[h] = jnp.where(lane == ONES_LANE, 1.0, v[:, h * HP:(h + 1) * HP]).astype(BF16)


def _mla_weights(lp):
    s = LOG2E / math.sqrt(MLA_NOPE + MLA_ROPE)
    src, sign = _rot_index()
    hd = MLA_NOPE + MLA_ROPE
    qc_idx, qs_idx, qs_sign, k_idx, v_idx = [], [], [], [], []
    for h in range(N_HEADS):
        qc_idx += list(h * hd + np.arange(hd)) + [-1] * (HP - hd)
        qs_idx += [-1] * MLA_NOPE + list(h * hd + MLA_NOPE + src) + [-1] * (HP - hd)
        qs_sign += [0.0] * MLA_NOPE + list(sign) + [0.0] * (HP - hd)
        k_idx += list(h * 128 + np.arange(MLA_NOPE)) + [-1] * (HP - MLA_NOPE)
        v_idx += list(h * 128 + MLA_NOPE + np.arange(64)) + [-1] * (HP - 64)
    wq = jnp.pad(lp['mla_wq_b'] * s, ((0, 256 - MLA_Q_RANK), (0, 0)))
    wqc = _take_cols(wq, qc_idx).astype(BF16)
    wqs = _take_cols(wq, qs_idx, qs_sign).astype(BF16)
    wk = _take_cols(lp['mla_wkv_b'], k_idx).astype(BF16)
    wv = _take_cols(lp['mla_wkv_b'], v_idx).astype(BF16)
    epe = np.zeros((128, N_HEADS * HP), np.float32)
    for h in range(N_HEADS):
        for j in range(MLA_ROPE):
            epe[j, h * HP + MLA_NOPE + j] = 1.0
            epe[MLA_ROPE + j, h * HP + MLA_NOPE + j] = 1.0
    qn = jnp.pad(lp['mla_q_norm'], (0, 256 - MLA_Q_RANK)).reshape(1, 256)
    return qn, lp['mla_kv_norm'].reshape(1, MLA_KV_RANK), wqc, wqs, wk, wv, jnp.asarray(epe, BF16)


def _mla_tables(cos, sin):
    T = cos.shape[0]
    one = jnp.ones((T, MLA_NOPE), F32)
    zero = jnp.zeros((T, HP - MLA_NOPE - MLA_ROPE), F32)
    zn = jnp.zeros((T, MLA_NOPE), F32)
    cq = jnp.tile(jnp.concatenate([one, cos, zero], -1), (1, N_HEADS))
    sq = jnp.tile(jnp.concatenate([zn, sin, zero], -1), (1, N_HEADS))
    ck = jnp.concatenate([cos, sin, jnp.zeros((T, 64), F32)], -1)
    return cq, sq, ck


def _mla_prep(p_mla, tables, weights, total_rows, bufs=None):
    B, T, C = p_mla.shape
    tm = _tile(T, 256)
    W = N_HEADS * HP
    tspecs = [pl.BlockSpec((tm, W), lambda b, i: (i, 0)), pl.BlockSpec((tm, W), lambda b, i: (i, 0)),
              pl.BlockSpec((tm, 128), lambda b, i: (i, 0))]
    wspecs = [pl.BlockSpec(w.shape, lambda b, i: (0, 0)) for w in weights]
    aspecs, ospecs, oshapes, extra = _kv_buffers(B, T, tm, 2, bufs, total_rows)
    n_in = 1 + len(tables) + len(weights)
    return pl.pallas_call(
        _mla_prep_kernel,
        grid=(B, T // tm),
        in_specs=[pl.BlockSpec((None, tm, C), lambda b, i: (b, i, 0))] + tspecs + wspecs + aspecs,
        out_specs=ospecs,
        out_shape=oshapes,
        input_output_aliases={n_in + j: 1 + j for j in range(len(extra))},
        compiler_params=_cp("parallel", "parallel"),
        name="mla_prepare",
    )(p_mla, *tables, *weights, *extra)


def _df_prep_kernel(p_ref, c_t, s_t, *rest):
    q_o, k1_o, k2_o, v_o = rest[-4:]
    W = N_HEADS * HP
    p = p_ref[...].astype(F32)
    c = c_t[...]
    s = s_t[...]
    q = p[:, 0:W] * c + p[:, W:2 * W] * s
    k = p[:, 2 * W:3 * W] * c + p[:, 3 * W:4 * W] * s
    v = p[:, 4 * W:5 * W]
    lane = lax.broadcasted_iota(jnp.int32, (p.shape[0], HP), 1)
    for h in range(N_HEADS):
        sl = slice(h * HP, (h + 1) * HP)
        q_o[h] = (q[:, sl] * (LOG2E / math.sqrt(DF_QK))).astype(BF16)
        kh = k[:, sl]
        k1_o[h] = jnp.where(lane < DF_QK, kh, 0.0).astype(BF16)
        k2_o[h] = jnp.where(lane >= DF_QK, kh, 0.0).astype(BF16)
        v_o[h] = jnp.where(lane == ONES_LANE, 1.0, v[:, sl]).astype(BF16)


def _df_tables(cos, sin):
    T = cos.shape[0]
    zero = jnp.zeros((T, HP - 2 * DF_QK), F32)
    c = jnp.tile(jnp.concatenate([cos, cos, zero], -1), (1, N_HEADS))
    s = jnp.tile(jnp.concatenate([sin, sin, zero], -1), (1, N_HEADS))
    return c, s


def _df_prep(p_df, tables, total_rows, bufs=None):
    B, T, C = p_df.shape
    tm = _tile(T, 256)
    W = N_HEADS * HP
    tspec = pl.BlockSpec((tm, W), lambda b, i: (i, 0))
    aspecs, ospecs, oshapes, extra = _kv_buffers(B, T, tm, 3, bufs, total_rows)
    return pl.pallas_call(
        _df_prep_kernel,
        grid=(B, T // tm),
        in_specs=[pl.BlockSpec((None, tm, C), lambda b, i: (b, i, 0)), tspec, tspec] + aspecs,
        out_specs=ospecs,
        out_shape=oshapes,
        input_output_aliases={3 + j: 1 + j for j in range(len(extra))},
        compiler_params=_cp("parallel", "parallel"),
        name="diff_prepare",
    )(p_df, *tables, *extra)


def _softmax_pv(q, k_ref, v_ref):
    s = _dot(q, k_ref[...], NT)
    p = jnp.exp2((s - jnp.max(s, axis=-1, keepdims=True)).astype(BF16))
    acc = _dot(p, v_ref[...])
    return acc / acc[:, ONES_LANE:ONES_LANE + 1]


def _mla_attn_kernel(q_ref, k_ref, v_ref, o_ref):
    o_ref[...] = _softmax_pv(q_ref[...], k_ref, v_ref).astype(o_ref.dtype)


def _attn_specs(q, key_rows):
    tq = _tile(q.shape[2], ATTN_TQ)
    first, n = key_rows
    assert first % n == 0
    qspec = pl.BlockSpec((None, None, tq, HP), lambda b, h, i: (b, h, i, 0))
    kspec = pl.BlockSpec((None, None, n, HP), lambda b, h, i: (b, h, first // n, 0))
    return tq, qspec, kspec


def _mla_attention(q, k, v, key_rows):
    B, H, Tq, _ = q.shape
    tq, qspec, kspec = _attn_specs(q, key_rows)
    return pl.pallas_call(
        _mla_attn_kernel,
        grid=(B, H, Tq // tq),
        in_specs=[qspec, kspec, kspec],
        out_specs=qspec,
        out_shape=jax.ShapeDtypeStruct((B, H, Tq, HP), MIX_OUT),
        compiler_params=_cp("parallel", "parallel", "parallel"),
        name="mla_attention",
    )(q, k, v)


def _df_attn_kernel(lam_init, q_ref, k1_ref, k2_ref, v_ref, lq1, lk1, lq2, lk2, sub, o_ref):
    q = q_ref[...]
    lam = (jnp.exp(jnp.sum(lq1[...] * lk1[...], axis=-1, keepdims=True))
           - jnp.exp(jnp.sum(lq2[...] * lk2[...], axis=-1, keepdims=True)) + lam_init)
    o = _softmax_pv(q, k1_ref, v_ref) - lam * _softmax_pv(q, k2_ref, v_ref)
    lane = lax.broadcasted_iota(jnp.int32, o.shape, 1)
    o = jnp.where(lane < DF_V, o, 0.0)
    ms = jnp.sum(o * o, axis=-1, keepdims=True) * (1.0 / DF_V)
    o_ref[...] = (o * lax.rsqrt(ms + DF_SUBLN_EPS) * sub[...] * (1.0 - lam_init)).astype(o_ref.dtype)


def _df_attention(q, k1, k2, v, key_rows, lp, lam_init):
    B, H, Tq, _ = q.shape
    tq, qspec, kspec = _attn_specs(q, key_rows)
    lspec = pl.BlockSpec((1, DF_QK), lambda b, h, i: (0, 0))
    sub = jnp.pad(lp['df_subln'], (0, HP - DF_V)).reshape(1, HP)
    return pl.pallas_call(
        functools.partial(_df_attn_kernel, lam_init),
        grid=(B, H, Tq // tq),
        in_specs=[qspec, kspec, kspec, kspec, lspec, lspec, lspec, lspec,
                  pl.BlockSpec((1, HP), lambda b, h, i: (0, 0))],
        out_specs=qspec,
        out_shape=jax.ShapeDtypeStruct((B, H, Tq, HP), MIX_OUT),
        compiler_params=_cp("parallel", "parallel", "parallel"),
        name="diff_attention",
    )(q, k1, k2, v, lp['df_lq1'].reshape(1, DF_QK), lp['df_lk1'].reshape(1, DF_QK),
      lp['df_lq2'].reshape(1, DF_QK), lp['df_lk2'].reshape(1, DF_QK), sub)


def _merge_kernel(x_ref, sh_ref, sc_ref, gt_ref, gpre_ref, gpost_ref, orw, ohy, omla, odf,
                  wg, bg, urw, uhy, umla, udf, wo, o_ref):
    x = x_ref[...]
    h = (_rms(x, gpre_ref[...]) * (1.0 + sc_ref[...]) + sh_ref[...]).astype(BF16)
    ups = (_dot1(orw[...], urw[...]),
           _dot1(ohy[...], uhy[...]),
           sum(_dot1(omla[hh], umla[hh]) for hh in range(N_HEADS)),
           sum(_dot1(odf[hh], udf[hh]) for hh in range(N_HEADS)))
    acc = None
    for n in range(4):
        t = _sigmoid(_dot(h, wg[n]) + bg[n]) * ups[n]
        acc = t if acc is None else acc + t
    y = _dot(acc.astype(BF16), wo[...])
    o_ref[...] = x + gt_ref[...] * _rms(y, gpost_ref[...])


def _pad_head_rows(w):
    return jnp.pad(w.reshape(N_HEADS, HEAD_DIM, -1), ((0, 0), (0, HP - HEAD_DIM), (0, 0)))


def _merge(x, mod3, g_pre, g_post, outs, mw):
    B, T, D = x.shape
    o_rw, o_hy, o_mla, o_df = outs
    tm = _tile(T, 512)
    xspec = pl.BlockSpec((None, tm, D), lambda b, i: (b, i, 0))
    bspec = pl.BlockSpec((None, tm, BR), lambda b, i: (b, i, 0))
    hspec = pl.BlockSpec((None, N_HEADS, tm, HP), lambda b, i: (b, 0, i, 0))

    def full(a):
        nd = a.ndim
        return pl.BlockSpec(a.shape, lambda b, i: (0,) * nd)

    ws = [mw['wg'], mw['bg'], mw['urw'], mw['uhy'], mw['umla'], mw['udf'], mw['wo']]
    return pl.pallas_call(
        _merge_kernel,
        grid=(B, T // tm),
        in_specs=[xspec, _mod_spec(mod3[0], 2), _mod_spec(mod3[1], 2), _mod_spec(mod3[2], 2),
                  _row_spec(D), _row_spec(D), bspec, bspec, hspec, hspec] + [full(a) for a in ws],
        out_specs=xspec,
        out_shape=jax.ShapeDtypeStruct((B, T, D), F32),
        compiler_params=_cp("parallel", "parallel"),
        name="gated_merge",
    )(x, mod3[0], mod3[1], mod3[2], g_pre.reshape(1, D), g_post.reshape(1, D), o_rw, o_hy, o_mla, o_df, *ws)


def _mix_weights(w_in):
    src, sign = _rot_index()
    w_rw = w_in[:, :RW_COLS]
    w_hy = w_in[:, OFF_HY:OFF_HY + HY_COLS]
    kpe0 = OFF_MLA + MLA_Q_RANK + MLA_KV_RANK
    mla_idx = (list(OFF_MLA + np.arange(MLA_Q_RANK)) + [-1] * (256 - MLA_Q_RANK)
               + list(OFF_MLA + MLA_Q_RANK + np.arange(MLA_KV_RANK))
               + list(kpe0 + np.arange(MLA_ROPE)) + list(kpe0 + src) + [-1] * 64)
    mla_sign = [1.0] * 256 + [1.0] * MLA_KV_RANK + [1.0] * MLA_ROPE + list(sign) + [0.0] * 64
    w_mla = _take_cols(w_in, mla_idx, mla_sign)
    pad = [-1] * (HP - 2 * DF_QK)
    src2 = np.concatenate([src, DF_QK + src])
    sign2 = np.concatenate([sign, sign])
    idx, sg = [], []
    for base, rot in ((OFF_DF, False), (OFF_DF, True), (OFF_DF + 256, False), (OFF_DF + 256, True),
                      (OFF_DF + 512, False)):
        for h in range(N_HEADS):
            cols = base + h * 64 + (src2 if rot else np.arange(64))
            idx += list(cols) + pad
            sg += (list(sign2) if rot else [1.0] * 64) + [0.0] * len(pad)
    w_df = _take_cols(w_in, idx, sg)
    return [w.astype(BF16) for w in (w_rw, w_hy, w_mla, w_df)]


def _merge_weights(lp):
    return dict(wg=lp['w_gate'].astype(BF16), bg=lp['b_gate'].reshape(4, 1, D_MODEL),
                urw=lp['w_up'][0].astype(BF16), uhy=lp['w_up'][1].astype(BF16),
                umla=_pad_head_rows(lp['w_up'][2]).astype(BF16), udf=_pad_head_rows(lp['w_up'][3]).astype(BF16),
                wo=lp['w_o'].astype(BF16))


def _token_mixing(x, xc, mod, modc, g_pre, lam_init, with_ctx_out, lp, rope):
    B, T, _ = x.shape
    ws = _mix_weights(lp['w_in'])
    p_rw, p_hy, p_mla, p_df = _project(x, mod[0], mod[1], g_pre, ws)
    c_rw, c_hy, c_mla, c_df = _project(xc, modc[0], modc[1], g_pre, ws)

    prep = _rwkv_prepare(p_rw, lp)
    prep_c = _rwkv_prepare(c_rw, lp)
    s0 = jnp.zeros((B, 2, N_HEADS, HEAD_DIM, HEAD_DIM), F32)
    yf_c, yb_c, s_c = _rwkv_scan(prep_c, s0)
    yf, yb, _ = _rwkv_scan(prep, s_c)
    o_rw = _rwkv_output(yf, yb, prep, lp)

    o_hy = _hyena(p_hy, lp)

    cos, sin = rope
    Tc = xc.shape[1]
    ones, zeros = jnp.ones((Tc, MLA_ROPE), F32), jnp.zeros((Tc, MLA_ROPE), F32)
    mla_w = _mla_weights(lp)
    q, k, v = _mla_prep(p_mla, _mla_tables(cos, sin), mla_w, T + Tc)
    qc, k, v = _mla_prep(c_mla, _mla_tables(ones, zeros), mla_w, T + Tc, bufs=(k, v))
    o_mla = _mla_attention(q, k, v, (0, T + Tc))

    dq, dk1, dk2, dv = _df_prep(p_df, _df_tables(cos, sin), T + Tc)
    dqc, dk1, dk2, dv = _df_prep(c_df, _df_tables(ones, zeros), T + Tc, bufs=(dk1, dk2, dv))
    o_df = _df_attention(dq, dk1, dk2, dv, (0, T + Tc), lp, lam_init)

    outs = (o_rw, o_hy, o_mla, o_df)
    if not with_ctx_out:
        return outs, None
    oc_rw = _rwkv_output(yf_c, yb_c, prep_c, lp)
    oc_hy = _hyena(c_hy, lp)
    oc_mla = _mla_attention(qc, k, v, (T, Tc))
    oc_df = _df_attention(dqc, dk1, dk2, dv, (T, Tc), lp, lam_init)
    return outs, (oc_rw, oc_hy, oc_mla, oc_df)


def kernel(x, c, ctx, c_ctx, w_mod, b_mod, norm_pre, norm_post, ffn_w_in, ffn_w_out, w_in, rw_mu, rw_w0, rw_w2, rw_a0, rw_a2, rw_g2, rw_kk, rw_ka, rw_rk, rw_ln_w, rw_ln_b, hy_conv_w, hy_conv_b, hy_w1, hy_b1, hy_f1, hy_w2, hy_b2, hy_f2, hy_w3, hy_bias, mla_q_norm, mla_wq_b, mla_kv_norm, mla_wkv_b, df_lq1, df_lk1, df_lq2, df_lk2, df_subln, w_up, w_gate, b_gate, w_o):
    params = dict(w_in=w_in, rw_mu=rw_mu, rw_w0=rw_w0, rw_w2=rw_w2, rw_a0=rw_a0, rw_a2=rw_a2, rw_g2=rw_g2,
                  rw_kk=rw_kk, rw_ka=rw_ka, rw_rk=rw_rk, rw_ln_w=rw_ln_w, rw_ln_b=rw_ln_b, hy_conv_w=hy_conv_w,
                  hy_conv_b=hy_conv_b, hy_w1=hy_w1, hy_b1=hy_b1, hy_f1=hy_f1, hy_w2=hy_w2, hy_b2=hy_b2,
                  hy_f2=hy_f2, hy_w3=hy_w3, hy_bias=hy_bias, mla_q_norm=mla_q_norm, mla_wq_b=mla_wq_b,
                  mla_kv_norm=mla_kv_norm, mla_wkv_b=mla_wkv_b, df_lq1=df_lq1, df_lk1=df_lk1, df_lq2=df_lq2,
                  df_lk2=df_lk2, df_subln=df_subln, w_up=w_up, w_gate=w_gate, b_gate=b_gate, w_o=w_o)
    B, T, D = x.shape
    depth = w_mod.shape[0]
    assert B <= 7 and D == D_MODEL and T % 128 == 0 and ctx.shape[1] % CHUNK == 0 and T % ctx.shape[1] == 0
    rope = _rope_tables(T)

    s_rows = jnp.concatenate([c, c_ctx[None], jnp.zeros((8 - B - 1, D), F32)], 0)
    mod_all = _modulation(s_rows, w_mod, b_mod).reshape(depth, 8, 3, 3, D)
    ffn_in = ffn_w_in.astype(BF16)
    ffn_out = ffn_w_out.astype(BF16)

    xc = ctx
    for l in range(depth):
        last = l == depth - 1
        lp = {k_: v_[l] for k_, v_ in params.items()}
        mod = [[mod_all[l, :B, s, m][:, None, :] for m in range(3)] for s in range(3)]
        modc = [[mod_all[l, B:B + 1, s, m][:, None, :] for m in range(3)] for s in range(3)]
        ffn_a = (norm_pre[l, 0], norm_post[l, 0], ffn_in[l, 0], ffn_out[l, 0])
        ffn_b = (norm_pre[l, 2], norm_post[l, 2], ffn_in[l, 1], ffn_out[l, 1])

        x = _ffn(x, mod[0], *ffn_a)
        xc = _ffn(xc, modc[0], *ffn_a)

        lam_init = 0.8 - 0.6 * math.exp(-0.3 * l)
        outs, outs_c = _token_mixing(x, xc, mod[1], modc[1], norm_pre[l, 1], lam_init, not last, lp, rope)
        mw = _merge_weights(lp)
        x = _merge(x, mod[1], norm_pre[l, 1], norm_post[l, 1], outs, mw)
        x = _ffn(x, mod[2], *ffn_b)
        if not last:
            xc = _merge(xc, modc[1], norm_pre[l, 1], norm_post[l, 1], outs_c, mw)
            xc = _ffn(xc, modc[2], *ffn_b)
    return x
```

```python
import functools
import math

import numpy as np
import jax
import jax.numpy as jnp
from jax import lax
from jax.experimental import pallas as pl
from jax.experimental.pallas import tpu as pltpu

F32 = jnp.float32
BF16 = jnp.bfloat16
MIX_OUT = BF16

D_MODEL = 1024
GRID_W = 64
N_HEADS = 4
HEAD_DIM = 64
BR = N_HEADS * HEAD_DIM
D_FF = 2816
MACARON_W = 0.5
EPS = 1e-6
ROPE_BASE = 10000.0
RW_LN_EPS = 64e-5
RW_COLS = 1152
HY_COLS = 768
HY_EMB = 33
HY_FAST_DECAY = 0.3
HY_SLOW_DECAY = 1.5
HY_TARGET = 1e-2
MLA_Q_RANK = 192
MLA_KV_RANK = 128
MLA_NOPE = 64
MLA_ROPE = 32
DF_QK = 32
DF_V = 64
DF_SUBLN_EPS = 1e-5
OFF_HY = RW_COLS
OFF_MLA = RW_COLS + HY_COLS
OFF_DF = OFF_MLA + MLA_Q_RANK + MLA_KV_RANK + MLA_ROPE

LANES = 128
HP = 128
ONES_LANE = 64
LOG2E = 1.4426950408889634
FFN_TM = 256
FFN_TF = 2816
ATTN_TQ = 256
FFT_DATA_PASSES = 1
CHUNK = 64
RW_P_GRAM = 1
RW_P_INV = (3, 3)
RW_P_STATE = 3
RW_P_APPLY = 1
VMEM_LIMIT = 56 * 1024 * 1024


def _cp(*sem):
    return pltpu.CompilerParams(dimension_semantics=sem, vmem_limit_bytes=VMEM_LIMIT)


def _tile(n, pref, mult=8):
    if n <= pref:
        return n
    t = (pref // mult) * mult
    while t > mult and n % t:
        t -= mult
    assert n % t == 0, (n, pref)
    return t


def _dot(a, b, dims=None):
    if dims is None:
        dims = (((a.ndim - 1,), (0,)), ((), ()))
    return lax.dot_general(a, b, dims, preferred_element_type=F32)


NT = (((1,), (1,)), ((), ()))
TN = (((0,), (0,)), ((), ()))


def _split(x):
    hi = x.astype(BF16)
    lo = (x - hi.astype(F32)).astype(BF16)
    return hi, lo


def _dot3(a, b, dims=None):
    ah, al = _split(a)
    bh, bl = _split(b)
    return _dot(ah, bh, dims) + (_dot(ah, bl, dims) + _dot(al, bh, dims))


def _dot1(a, b, dims=None):
    return _dot(a.astype(BF16), b.astype(BF16), dims)


def _dotc(ch, cl, x, passes=3):
    if passes == 1:
        return _dot(ch, x.astype(BF16))
    xh, xl = _split(x)
    return _dot(ch, xh) + (_dot(ch, xl) + _dot(cl, xh))


def _dot_exact_lhs(m, x):
    x1 = x.astype(BF16)
    r1 = x - x1.astype(F32)
    x2 = r1.astype(BF16)
    x3 = (r1 - x2.astype(F32)).astype(BF16)
    return _dot(m, x1) + (_dot(m, x2) + _dot(m, x3))


def _rms(x, g, eps=EPS):
    return x * lax.rsqrt(jnp.mean(x * x, axis=-1, keepdims=True) + eps) * g


def _sigmoid(x):
    return 1.0 / (1.0 + jnp.exp(-x))


def _silu(x):
    return x * _sigmoid(x)


def _softplus(z):
    return jnp.maximum(z, 0.0) + jnp.log(1.0 + jnp.exp(-jnp.abs(z)))


def _np_split(a):
    a = jnp.asarray(a, F32)
    hi = a.astype(BF16)
    lo = (a - hi.astype(F32)).astype(BF16)
    return hi, lo


def _mod_kernel(s_ref, w_ref, b_ref, o_ref):
    s = _silu(s_ref[...])
    o_ref[...] = _dot3(s, w_ref[...]) + b_ref[...]


def _modulation(s_rows, w_mod, b_mod):
    L, D, NM = w_mod.shape
    tn = _tile(NM, 1152, LANES)
    return pl.pallas_call(
        _mod_kernel,
        grid=(L, NM // tn),
        in_specs=[pl.BlockSpec((8, D), lambda l, j: (0, 0)),
                  pl.BlockSpec((None, D, tn), lambda l, j: (l, 0, j)),
                  pl.BlockSpec((None, 1, tn), lambda l, j: (l, 0, j))],
        out_specs=pl.BlockSpec((None, 8, tn), lambda l, j: (l, 0, j)),
        out_shape=jax.ShapeDtypeStruct((L, 8, NM), F32),
        compiler_params=_cp("parallel", "parallel"),
        name="modulation",
    )(s_rows, w_mod, b_mod.reshape(L, 1, NM))


def _mod_spec(m, nargs):
    if m.shape[0] == 1:
        return pl.BlockSpec((None, 1, D_MODEL), lambda *a: (0, 0, 0))
    return pl.BlockSpec((None, 1, D_MODEL), lambda *a: (a[0], 0, 0))


def _row_spec(n):
    return pl.BlockSpec((1, n), lambda *a: (0, 0))


def _ffn_kernel(x_ref, sh_ref, sc_ref, gt_ref, gpre_ref, gpost_ref, wa_ref, wb_ref, wo_ref, o_ref, h_sc, acc_sc):
    f = pl.program_id(2)

    @pl.when(f == 0)
    def _():
        h = _rms(x_ref[...], gpre_ref[...]) * (1.0 + sc_ref[...]) + sh_ref[...]
        h_sc[...] = h.astype(BF16)
        acc_sc[...] = jnp.zeros_like(acc_sc)

    h = h_sc[...]
    a = _dot(h, wa_ref[...])
    b = _dot(h, wb_ref[...])
    acc_sc[...] += _dot((_silu(a) * b).astype(BF16), wo_ref[...])

    @pl.when(f == pl.num_programs(2) - 1)
    def _():
        o_ref[...] = x_ref[...] + MACARON_W * gt_ref[...] * _rms(acc_sc[...], gpost_ref[...])


def _ffn(x, mod3, g_pre, g_post, w_in, w_out):
    B, T, D = x.shape
    F = w_out.shape[0]
    tm = _tile(T, FFN_TM)
    tf = _tile(F, FFN_TF, LANES)
    nf = F // tf
    xspec = pl.BlockSpec((None, tm, D), lambda b, i, f: (b, i, 0))
    return pl.pallas_call(
        _ffn_kernel,
        grid=(B, T // tm, nf),
        in_specs=[xspec, _mod_spec(mod3[0], 3), _mod_spec(mod3[1], 3), _mod_spec(mod3[2], 3),
                  _row_spec(D), _row_spec(D),
                  pl.BlockSpec((D, tf), lambda b, i, f: (0, f)),
                  pl.BlockSpec((D, tf), lambda b, i, f: (0, nf + f)),
                  pl.BlockSpec((tf, D), lambda b, i, f: (f, 0))],
        out_specs=xspec,
        out_shape=jax.ShapeDtypeStruct((B, T, D), F32),
        scratch_shapes=[pltpu.VMEM((tm, D), BF16), pltpu.VMEM((tm, D), F32)],
        compiler_params=_cp("parallel", "parallel", "arbitrary"),
        name="ffn",
    )(x, mod3[0], mod3[1], mod3[2], g_pre.reshape(1, D), g_post.reshape(1, D), w_in, w_in, w_out)


def _proj_kernel(x_ref, sh_ref, sc_ref, gpre_ref, wrw, why, wmla, wdf, orw, ohy, omla, odf):
    h = (_rms(x_ref[...], gpre_ref[...]) * (1.0 + sc_ref[...]) + sh_ref[...]).astype(BF16)
    orw[...] = _dot(h, wrw[...])
    ohy[...] = _dot(h, why[...])
    omla[...] = _dot(h, wmla[...]).astype(omla.dtype)
    odf[...] = _dot(h, wdf[...]).astype(odf.dtype)


def _project(x, shift, scale, g_pre, ws):
    B, T, D = x.shape
    tm = _tile(T, 256)
    xspec = pl.BlockSpec((None, tm, D), lambda b, i: (b, i, 0))
    wspecs = [pl.BlockSpec(w.shape, lambda b, i: (0, 0)) for w in ws]
    ospecs = [pl.BlockSpec((None, tm, w.shape[1]), lambda b, i: (b, i, 0)) for w in ws]
    oshapes = [jax.ShapeDtypeStruct((B, T, w.shape[1]), dt) for w, dt in zip(ws, (F32, F32, BF16, BF16))]
    return pl.pallas_call(
        _proj_kernel,
        grid=(B, T // tm),
        in_specs=[xspec, _mod_spec(shift, 2), _mod_spec(scale, 2), _row_spec(D)] + wspecs,
        out_specs=ospecs,
        out_shape=oshapes,
        compiler_params=_cp("parallel", "parallel"),
        name="mix_project",
    )(x, shift, scale, g_pre.reshape(1, D), *ws)


def _halo_specs(tm, T, C):
    nb8 = T // 8
    r = tm // 8
    main = pl.BlockSpec((None, tm, C), lambda b, i: (b, i, 0))
    prev = pl.BlockSpec((None, 8, C), lambda b, i: (b, jnp.maximum(i * r - 1, 0), 0))
    nxt = pl.BlockSpec((None, 8, C), lambda b, i: (b, jnp.minimum((i + 1) * r, nb8 - 1), 0))
    return main, prev, nxt


def _conv3_tile(p, prev8, next8, w0, w1, w2):
    i = pl.program_id(1)
    n_i = pl.num_programs(1)
    tm = p.shape[0]
    row = lax.broadcasted_iota(jnp.int32, p.shape, 0)
    prev_row = jnp.where(i > 0, prev8[7:8, :], 0.0)
    next_row = jnp.where(i < n_i - 1, next8[0:1, :], 0.0)
    pm = jnp.where(row == 0, prev_row, pltpu.roll(p, 1, 0))
    pp = jnp.where(row == tm - 1, next_row, pltpu.roll(p, tm - 1, 0))
    return pm * w0 + p * w1 + pp * w2


def _rwkv_prep_kernel(p_ref, pv_ref, nx_ref, mu_ref, w0_ref, w2_ref, a0_ref, a2_ref, g2_ref, kk_ref, ka_ref,
                      rk_ref, bd_ref, trif_ref, trib_ref, onec_ref, v_o, g_o, bonus_o, *dir_outs):
    mu = mu_ref[...]
    p = _conv3_tile(p_ref[...], pv_ref[...], nx_ref[...], 0.5 * mu, 1.0 - mu, 0.5 * mu)
    r = p[:, 0:BR]
    k = p[:, BR:2 * BR]
    v = p[:, 2 * BR:3 * BR]
    g_a = p[:, 3 * BR:3 * BR + 128]
    wa = p[:, 896:1024]
    aa = p[:, 1024:1152]
    bd = bd_ref[...]
    kkraw = k * kk_ref[...]
    ss = _dot_exact_lhs_rhs(kkraw * kkraw, bd)
    kk = kkraw / jnp.maximum(jnp.sqrt(ss), 1e-12)
    wl = w0_ref[...] + _dot3(jnp.tanh(wa), w2_ref[...])
    lw = -jnp.exp(-_softplus(-wl) - 0.5)
    a = _sigmoid(a0_ref[...] + _dot3(aa, a2_ref[...]))
    a_f = a[:, :BR]
    a_b = a[:, BR:]
    ka = ka_ref[...]
    k_f = k * (1.0 + (a_f - 1.0) * ka)
    k_b = k * (1.0 + (a_b - 1.0) * ka)
    v_o[...] = v
    g_o[...] = _dot1(_sigmoid(g_a), g2_ref[...])
    bonus_o[...] = _dot_exact_lhs_rhs(r * (k_f + k_b) * rk_ref[...], bd) * v
    onec = onec_ref[...]
    for d, (tri_ref, k_d, a_d) in enumerate(((trif_ref, k_f, a_f), (trib_ref, k_b, a_b))):
        rt_o, at_o, bt_o, kt_o, bh_o, kh_o, eg_o = dir_outs[7 * d:7 * d + 7]
        lw_d = lw[:, d * BR:(d + 1) * BR]
        b_d = kk * a_d
        g = _dot_exact_lhs(tri_ref[...], lw_d)
        gtot = _dot_exact_lhs(onec, lw_d)
        eng = jnp.exp(-g)
        et = jnp.exp(gtot - g)
        rt_o[...] = r * jnp.exp(g)
        at_o[...] = -kk * jnp.exp(g - lw_d)
        bt_o[...] = b_d * eng
        kt_o[...] = k_d * eng
        bh_o[...] = b_d * et
        kh_o[...] = k_d * et
        eg_o[...] = jnp.exp(gtot)


def _dot_exact_lhs_rhs(x, m):
    x1 = x.astype(BF16)
    r1 = x - x1.astype(F32)
    x2 = r1.astype(BF16)
    x3 = (r1 - x2.astype(F32)).astype(BF16)
    return _dot(x1, m) + (_dot(x2, m) + _dot(x3, m))


def _block_diag_ones():
    h = np.arange(BR) // HEAD_DIM
    return jnp.asarray((h[:, None] == h[None, :]).astype(np.float32), BF16)


def _rwkv_prepare(p_rw, lp):
    B, T, C = p_rw.shape
    tm = _tile(T, 256)
    main, prev, nxt = _halo_specs(tm, T, C)
    z = jnp.zeros((64, BR), F32)
    w2 = jnp.concatenate([jnp.concatenate([lp['rw_w2'][0], z], 1), jnp.concatenate([z, lp['rw_w2'][1]], 1)], 0)
    a2 = jnp.concatenate([jnp.concatenate([lp['rw_a2'][0], z], 1), jnp.concatenate([z, lp['rw_a2'][1]], 1)], 0)
    params = [lp['rw_mu'].reshape(1, C), lp['rw_w0'].reshape(1, 2 * BR), w2, lp['rw_a0'].reshape(1, 2 * BR), a2,
              lp['rw_g2'].astype(BF16), lp['rw_kk'].reshape(1, BR), lp['rw_ka'].reshape(1, BR),
              lp['rw_rk'].reshape(1, BR), _block_diag_ones()]
    pos = np.arange(tm)
    same = (pos[:, None] // CHUNK) == (pos[None, :] // CHUNK)
    params += [jnp.asarray((same & (pos[None, :] <= pos[:, None])).astype(np.float32), BF16),
               jnp.asarray((same & (pos[None, :] >= pos[:, None])).astype(np.float32), BF16),
               jnp.asarray(same.astype(np.float32), BF16)]
    pspecs = [pl.BlockSpec(a.shape, lambda b, i: (0, 0)) for a in params]
    ospec = pl.BlockSpec((None, tm, BR), lambda b, i: (b, i, 0))
    names = ('v', 'g', 'bonus') + tuple(n + str(d) for d in (0, 1) for n in ('rt', 'at', 'bt', 'kt', 'bh', 'kh', 'eg'))
    outs = pl.pallas_call(
        _rwkv_prep_kernel,
        grid=(B, T // tm),
        in_specs=[main, prev, nxt] + pspecs,
        out_specs=[ospec] * len(names),
        out_shape=[jax.ShapeDtypeStruct((B, T, BR), F32)] * len(names),
        compiler_params=_cp("parallel", "parallel"),
        name="rwkv_prepare",
    )(p_rw, p_rw, p_rw, *params)
    return dict(zip(names, outs))


def _dot3s(a, b, dims=None):
    if dims is None:
        dims = (((1,), (0,)), ((), ()))
    ca, cb = dims[0][0][0], dims[0][1][0]
    a_lo = a - a.astype(BF16).astype(F32)
    b_lo = b - b.astype(BF16).astype(F32)
    sa = jnp.concatenate([a, a_lo, a], axis=ca).astype(BF16)
    sb = jnp.concatenate([b, b, b_lo], axis=cb).astype(BF16)
    return _dot(sa, sb, dims)


def _mmp(a, b, dims, passes):
    return _dot3s(a, b, dims) if passes == 3 else _dot1(a, b, dims)


def _rwkv_chunks(chains, eye):
    n = range(len(chains))
    rt, at, bt, kt, bh, kh, eg, v, ht, mask_s, mask_i = zip(*chains)
    C = rt[0].shape[0]
    gm = [_mmp(jnp.concatenate([at[i], rt[i]], 0), jnp.concatenate([bt[i], kt[i]], 0), NT, RW_P_GRAM) for i in n]
    top = [jnp.where(mask_s[i], gm[i][:C], 0.0) for i in n]
    bot = [jnp.where(mask_i[i], gm[i][C:], 0.0) for i in n]
    pw = [top[i][:, :C] for i in n]
    tinv = [eye + pw[i] for i in n]
    for _ in range(int(math.log2(C)) - 1):
        pw = [_mmp(pw[i], pw[i], None, RW_P_INV[0]) for i in n]
        upd = [_mmp(tinv[i], pw[i], None, RW_P_INV[1]) for i in n]
        tinv = [tinv[i] + upd[i] for i in n]
    w0 = [_mmp(at[i], ht[i], NT, RW_P_STATE) for i in n]
    w1 = [_mmp(top[i][:, C:], v[i], None, RW_P_APPLY) for i in n]
    u = [_mmp(tinv[i], w0[i] + w1[i], None, RW_P_STATE) for i in n]
    uv = [jnp.concatenate([u[i], v[i]], 0) for i in n]
    y0 = [_mmp(rt[i], ht[i], NT, RW_P_STATE) for i in n]
    y1 = [_mmp(bot[i], uv[i], None, RW_P_APPLY) for i in n]
    hu = [_mmp(uv[i], jnp.concatenate([bh[i], kh[i]], 0), TN, RW_P_STATE) for i in n]
    return [(y0[i] + y1[i], ht[i] * eg[i] + hu[i]) for i in n]


def _rwkv_scan_kernel(*refs):
    fwd_in, bwd_in = refs[0:8], refs[8:16]
    s0_ref, yf_o, yb_o, s_o, st = refs[16:]
    c = pl.program_id(0)
    nb, C = yf_o.shape[0], yf_o.shape[1]

    @pl.when(c == 0)
    def _():
        st[...] = s0_ref[...]

    row = lax.broadcasted_iota(jnp.int32, (C, 2 * C), 0)
    col = lax.broadcasted_iota(jnp.int32, (C, 2 * C), 1)
    col = jnp.where(col >= C, col - C, col)
    eye = (lax.broadcasted_iota(jnp.int32, (C, C), 0) == lax.broadcasted_iota(jnp.int32, (C, C), 1)).astype(F32)
    chains = []
    for b in range(nb):
        for d, ins in enumerate((fwd_in, bwd_in)):
            if d == 0:
                mask_i = col <= row
                mask_s = col < row
            else:
                mask_i = col >= row
                mask_s = col > row
            for h in range(N_HEADS):
                sl = pl.ds(h * HEAD_DIM, HEAD_DIM)
                rt, at, bt, kt, bh, kh, eg, v = (ref[b, :, sl] for ref in ins)
                chains.append((rt, at, bt, kt, bh, kh, eg[0:1, :], v, st[b, d, h], mask_s, mask_i))
    results = iter(_rwkv_chunks(chains, eye))
    for b in range(nb):
        for d, y_o in enumerate((yf_o, yb_o)):
            for h in range(N_HEADS):
                y, hn = next(results)
                y_o[b, :, pl.ds(h * HEAD_DIM, HEAD_DIM)] = y
                st[b, d, h] = hn

    @pl.when(c == pl.num_programs(0) - 1)
    def _():
        s_o[...] = st[...]


def _rwkv_scan(prep, s0):
    B, T, _ = prep['v'].shape
    nc = T // CHUNK
    fwd = pl.BlockSpec((B, CHUNK, BR), lambda c: (0, c, 0))
    bwd = pl.BlockSpec((B, CHUNK, BR), lambda c: (0, nc - 1 - c, 0))
    sspec = pl.BlockSpec(s0.shape, lambda c: (0, 0, 0, 0, 0))
    ins = [prep[n + str(d)] for d in (0, 1) for n in ('rt', 'at', 'bt', 'kt', 'bh', 'kh', 'eg')]
    return pl.pallas_call(
        _rwkv_scan_kernel,
        grid=(nc,),
        in_specs=[fwd] * 8 + [bwd] * 8 + [sspec],
        out_specs=[fwd, bwd, sspec],
        out_shape=[jax.ShapeDtypeStruct((B, T, BR), F32), jax.ShapeDtypeStruct((B, T, BR), F32),
                   jax.ShapeDtypeStruct(s0.shape, F32)],
        scratch_shapes=[pltpu.VMEM(s0.shape, F32)],
        compiler_params=_cp("arbitrary"),
        name="rwkv_scan",
    )(*ins[:7], prep['v'], *ins[7:], prep['v'], s0)


def _rwkv_out_kernel(yf, yb, g, bonus, lnw, lnb, bd, o):
    y = yf[...] + yb[...]
    m = bd[...]
    inv_n = 1.0 / HEAD_DIM
    mean = _dot_exact_lhs_rhs(y, m) * inv_n
    yc = y - mean
    var = _dot_exact_lhs_rhs(yc * yc, m) * inv_n
    yn = yc * lax.rsqrt(var + RW_LN_EPS) * lnw[...] + lnb[...]
    o[...] = ((yn + bonus[...]) * g[...]).astype(o.dtype)


def _rwkv_output(yf, yb, prep, lp):
    B, T, _ = yf.shape
    tm = _tile(T, 512)
    spec = pl.BlockSpec((None, tm, BR), lambda b, i: (b, i, 0))
    return pl.pallas_call(
        _rwkv_out_kernel,
        grid=(B, T // tm),
        in_specs=[spec] * 4 + [_row_spec(BR), _row_spec(BR), pl.BlockSpec((BR, BR), lambda b, i: (0, 0))],
        out_specs=spec,
        out_shape=jax.ShapeDtypeStruct((B, T, BR), MIX_OUT),
        compiler_params=_cp("parallel", "parallel"),
        name="rwkv_output",
    )(yf, yb, prep['g'], prep['bonus'], lp['rw_ln_w'].reshape(1, BR), lp['rw_ln_b'].reshape(1, BR),
      _block_diag_ones())


def _hy_pre_kernel(p_ref, pv_ref, nx_ref, w_ref, b_ref, x0_o, z_o):
    w = w_ref[...]
    u = _conv3_tile(p_ref[...], pv_ref[...], nx_ref[...], w[0:1, :], w[1:2, :], w[2:3, :]) + b_ref[...]
    x0_o[...] = u[:, 0:BR]
    z_o[...] = u[:, 2 * BR:3 * BR] * u[:, BR:2 * BR]


def _hyena_pre(p_hy, lp):
    B, T, C = p_hy.shape
    tm = _tile(T, 512)
    main, prev, nxt = _halo_specs(tm, T, C)
    ospec = pl.BlockSpec((None, tm, BR), lambda b, i: (b, i, 0))
    return pl.pallas_call(
        _hy_pre_kernel,
        grid=(B, T // tm),
        in_specs=[main, prev, nxt, pl.BlockSpec((3, C), lambda b, i: (0, 0)), _row_spec(C)],
        out_specs=[ospec, ospec],
        out_shape=[jax.ShapeDtypeStruct((B, T, BR), F32)] * 2,
        compiler_params=_cp("parallel", "parallel"),
        name="hyena_pre",
    )(p_hy, p_hy, p_hy, lp['hy_conv_w'], lp['hy_conv_b'].reshape(1, C))


def _hy_filter_kernel(z_ref, t_ref, mf_ref, mb_ref, w1, b1, f1, w2, b2, f2, w3, dl, k_o, ss_o):
    h = jnp.sin(f1[...] * (_dot3(z_ref[...], w1[...]) + b1[...]))
    h = jnp.sin(f2[...] * (_dot3(h, w2[...]) + b2[...]))
    h = _dot1(h, w3[...]) * jnp.exp(-t_ref[...] * dl[...])
    kern = mf_ref[...] * h[:, :BR] + mb_ref[...] * h[:, BR:]
    k_o[...] = kern

    @pl.when(pl.program_id(0) == 0)
    def _():
        ss_o[...] = jnp.zeros_like(ss_o)

    ss_o[...] += jnp.sum(kern * kern, axis=0, keepdims=True)


def _hyena_filter(n, lp):
    t = jnp.linspace(0.0, 1.0, n, dtype=F32)
    bands = (HY_EMB - 1) // 2
    wpos = 2.0 * math.pi * jnp.arange(n, dtype=F32) / n
    fr = jnp.linspace(1e-4, bands - 1, bands, dtype=F32)
    t_ext = jnp.concatenate([t, t[::-1]])[:, None]
    ang = jnp.concatenate([wpos, wpos[::-1]])[:, None] * fr[None]
    z_ext = jnp.concatenate([t_ext, jnp.cos(ang), -jnp.sin(ang), jnp.zeros((2 * n, LANES - HY_EMB), F32)], axis=-1)
    rho = np.arange(2 * n)
    mf = jnp.asarray((rho < n).astype(np.float32))[:, None]
    mb = jnp.asarray((rho > n).astype(np.float32))[:, None]
    max_decay = math.log(HY_TARGET) / HY_FAST_DECAY
    min_decay = math.log(HY_TARGET) / HY_SLOW_DECAY
    deltas = jnp.abs(jnp.linspace(min_decay, max_decay, BR, dtype=F32))
    dl = jnp.tile(deltas, 2)[None]
    w1 = jnp.pad(lp['hy_w1'], ((0, LANES - HY_EMB), (0, 0)))
    params = [w1, lp['hy_b1'][None], lp['hy_f1'][None], lp['hy_w2'], lp['hy_b2'][None], lp['hy_f2'][None],
              lp['hy_w3'], dl]
    tm = _tile(2 * n, 512)
    return pl.pallas_call(
        _hy_filter_kernel,
        grid=(2 * n // tm,),
        in_specs=[pl.BlockSpec((tm, LANES), lambda i: (i, 0))] + [pl.BlockSpec((tm, 1), lambda i: (i, 0))] * 3
        + [pl.BlockSpec(a.shape, lambda i: (0, 0)) for a in params],
        out_specs=[pl.BlockSpec((tm, BR), lambda i: (i, 0)), pl.BlockSpec((1, BR), lambda i: (0, 0))],
        out_shape=[jax.ShapeDtypeStruct((2 * n, BR), F32), jax.ShapeDtypeStruct((1, BR), F32)],
        compiler_params=_cp("arbitrary"),
        name="hyena_filter",
    )(z_ext, t_ext, mf, mb, *params)


def _dft_tables(N1, N2):
    N = N1 * N2

    def mat(n):
        a = -2.0 * np.pi * np.outer(np.arange(n), np.arange(n)) / n
        return np.cos(a), np.sin(a)

    f1r, f1i = mat(N1)
    f2r, f2i = mat(N2)
    a = -2.0 * np.pi * np.outer(np.arange(N1), np.arange(N2)) / N
    return dict(f1r=f1r, f1i=f1i, f2r=f2r, f2i=f2i,
                twr=jnp.asarray(np.cos(a)[:, :, None], F32), twi=jnp.asarray(np.sin(a)[:, :, None], F32))


def _fft_in_kernel(passes, x_ref, frh, frl, fih, fil, ar_o, ai_o):
    x = x_ref[...]
    ar_o[...] = _dotc(frh[...], frl[...], x, passes).astype(ar_o.dtype)
    ai_o[...] = _dotc(fih[...], fil[...], x, passes).astype(ai_o.dtype)


def _fft_in(x2d, tabs, N1, passes):
    B, n1u, cols = x2d.shape
    consts = list(_np_split(tabs['f1r'][:, :n1u])) + list(_np_split(tabs['f1i'][:, :n1u]))
    tc = _tile(cols, 8192, LANES)
    ospec = pl.BlockSpec((None, N1, tc), lambda b, j: (b, 0, j))
    return pl.pallas_call(
        functools.partial(_fft_in_kernel, passes),
        grid=(B, cols // tc),
        in_specs=[pl.BlockSpec((None, n1u, tc), lambda b, j: (b, 0, j))]
        + [pl.BlockSpec((N1, n1u), lambda b, j: (0, 0))] * 4,
        out_specs=[ospec, ospec],
        out_shape=[jax.ShapeDtypeStruct((B, N1, cols), F32 if passes == 3 else BF16)] * 2,
        compiler_params=_cp("parallel", "parallel"),
        name="fft_first_axis",
    )(x2d, *consts)


def _cplx_dft(frh, frl, fih, fil, xr, xi, conj, passes):
    rr = _dotc(frh, frl, xr, passes)
    ii = _dotc(fih, fil, xi, passes)
    ri = _dotc(frh, frl, xi, passes)
    ir = _dotc(fih, fil, xr, passes)
    if conj:
        return rr + ii, ri - ir
    return rr - ii, ri + ir


def _fft_spec_kernel(ar, ai, twr, twi, frh, frl, fih, fil, sc, kr_o, ki_o):
    tr, ti = twr[...], twi[...]
    xr = ar[...] * tr - ai[...] * ti
    xi = ar[...] * ti + ai[...] * tr
    br, bi = _cplx_dft(frh[...], frl[...], fih[...], fil[...], xr, xi, False, 3)
    kr_o[...] = br * sc[...]
    ki_o[...] = bi * sc[...]


def _fft_conv_kernel(ar, ai, twr, twi, frh, frl, fih, fil, kr, ki, dr_o, di_o):
    tr, ti = twr[...], twi[...]
    f = (frh[...], frl[...], fih[...], fil[...])
    a_r, a_i = ar[...].astype(F32), ai[...].astype(F32)
    xr = a_r * tr - a_i * ti
    xi = a_r * ti + a_i * tr
    br, bi = _cplx_dft(*f, xr, xi, False, FFT_DATA_PASSES)
    cr = br * kr[...] - bi * ki[...]
    ci = br * ki[...] + bi * kr[...]
    dr, di = _cplx_dft(*f, cr, ci, True, FFT_DATA_PASSES)
    dr_o[...] = (dr * tr + di * ti).astype(dr_o.dtype)
    di_o[...] = (di * tr - dr * ti).astype(di_o.dtype)


def _fft_mid(a_re, a_im, tabs, kf=None, scale=None):
    B, N1, N2, C = a_re.shape
    consts = list(_np_split(tabs['f2r'])) + list(_np_split(tabs['f2i']))
    aspec = pl.BlockSpec((None, None, N2, C), lambda b, k: (b, k, 0, 0))
    tspec = pl.BlockSpec((None, N2, 1), lambda b, k: (k, 0, 0))
    fspec = pl.BlockSpec((N2, N2), lambda b, k: (0, 0))
    if kf is None:
        extra, especs, kern, nm = [scale], [_row_spec(C)], _fft_spec_kernel, "fft_filter_spectrum"
    else:
        kspec = pl.BlockSpec((None, None, N2, C), lambda b, k: (0, k, 0, 0))
        extra, especs, kern, nm = list(kf), [kspec, kspec], _fft_conv_kernel, "fft_second_axis_conv"
    return pl.pallas_call(
        kern,
        grid=(B, N1),
        in_specs=[aspec, aspec, tspec, tspec] + [fspec] * 4 + especs,
        out_specs=[aspec, aspec],
        out_shape=[jax.ShapeDtypeStruct((B, N1, N2, C), a_re.dtype)] * 2,
        compiler_params=_cp("parallel", "parallel"),
        name=nm,
    )(a_re, a_im, tabs['twr'], tabs['twi'], *consts, *extra)


def _fft_out_kernel(dr, di, frh, frl, fih, fil, x0, z, bias, o):
    zc = (_dotc(frh[...], frl[...], dr[...].astype(F32), FFT_DATA_PASSES)
          + _dotc(fih[...], fil[...], di[...].astype(F32), FFT_DATA_PASSES))
    zz = z[...]
    o[...] = (x0[...] * (zc + zz * bias[...])).astype(o.dtype)


def _fft_out(d_re, d_im, tabs, x0_2d, z_2d, bias_row):
    B, N1, cols = d_re.shape
    n1u = x0_2d.shape[1]
    consts = list(_np_split(tabs['f1r'][:n1u, :])) + list(_np_split(tabs['f1i'][:n1u, :]))
    tc = _tile(cols, 8192, LANES)
    dspec = pl.BlockSpec((None, N1, tc), lambda b, j: (b, 0, j))
    xspec = pl.BlockSpec((None, n1u, tc), lambda b, j: (b, 0, j))
    return pl.pallas_call(
        _fft_out_kernel,
        grid=(B, cols // tc),
        in_specs=[dspec, dspec] + [pl.BlockSpec((n1u, N1), lambda b, j: (0, 0))] * 4
        + [xspec, xspec, pl.BlockSpec((1, tc), lambda b, j: (0, j))],
        out_specs=xspec,
        out_shape=jax.ShapeDtypeStruct((B, n1u, cols), MIX_OUT),
        compiler_params=_cp("parallel", "parallel"),
        name="fft_last_axis_gate",
    )(d_re, d_im, *consts, x0_2d, z_2d, bias_row)


def _hy_gate_kernel(zc, x0, z, bias, o):
    o[...] = (x0[...] * (zc[...] + z[...] * bias[...])).astype(o.dtype)


def _hyena(p_hy, lp):
    B, n, _ = p_hy.shape
    x0, z = _hyena_pre(p_hy, lp)
    kern, ss = _hyena_filter(n, lp)
    N = 2 * n
    N2 = min(256, n)
    N1 = N // N2
    tabs = _dft_tables(N1, N2)
    scale = lax.rsqrt(ss) * (1.0 / N)
    bias = lp['hy_bias'].reshape(1, BR)
    if N1 >= 4:
        n1u = N1 // 2
        k_re, k_im = _fft_in(kern.reshape(1, N1, N2 * BR), tabs, N1, 3)
        kf = _fft_mid(k_re.reshape(1, N1, N2, BR), k_im.reshape(1, N1, N2, BR), tabs, scale=scale)
        a_re, a_im = _fft_in(z.reshape(B, n1u, N2 * BR), tabs, N1, FFT_DATA_PASSES)
        d_re, d_im = _fft_mid(a_re.reshape(B, N1, N2, BR), a_im.reshape(B, N1, N2, BR), tabs, kf=kf)
        o = _fft_out(d_re.reshape(B, N1, N2 * BR), d_im.reshape(B, N1, N2 * BR), tabs,
                     x0.reshape(B, n1u, N2 * BR), z.reshape(B, n1u, N2 * BR), jnp.tile(bias, (1, N2)))
        return o.reshape(B, n, BR)
    tabs = _dft_tables(1, N)
    kf = _fft_mid(kern.reshape(1, 1, N, BR), jnp.zeros((1, 1, N, BR), F32), tabs, scale=scale)
    zp = jnp.pad(z, ((0, 0), (0, n), (0, 0))).reshape(B, 1, N, BR)
    d_re, _ = _fft_mid(zp, jnp.zeros_like(zp), tabs, kf=kf)
    zc = d_re.reshape(B, N, BR)[:, :n]
    tm = _tile(n, 512)
    spec = pl.BlockSpec((None, tm, BR), lambda b, i: (b, i, 0))
    return pl.pallas_call(
        _hy_gate_kernel,
        grid=(B, n // tm),
        in_specs=[spec, spec, spec, _row_spec(BR)],
        out_specs=spec,
        out_shape=jax.ShapeDtypeStruct((B, n, BR), MIX_OUT),
        compiler_params=_cp("parallel", "parallel"),
        name="hyena_gate",
    )(zc, x0, z, bias)


def _rope_tables(n_lat):
    rows = n_lat // GRID_W
    row = jnp.repeat(jnp.arange(rows, dtype=F32), GRID_W)
    col = jnp.tile(jnp.arange(GRID_W, dtype=F32), rows)
    half = MLA_ROPE // 2
    inv = ROPE_BASE ** (-jnp.arange(0, half, 2, dtype=F32) / half)
    ar = row[:, None] * inv[None]
    ac = col[:, None] * inv[None]
    cos = jnp.concatenate([jnp.cos(ar), jnp.cos(ar), jnp.cos(ac), jnp.cos(ac)], axis=-1)
    sin = jnp.concatenate([jnp.sin(ar), jnp.sin(ar), jnp.sin(ac), jnp.sin(ac)], axis=-1)
    return cos, sin


def _rot_index():
    j = np.arange(32)
    first = (j % 16) < 8
    src = np.where(first, j + 8, j - 8)
    sign = np.where(first, -1.0, 1.0)
    return src, sign


def _take_cols(w, idx, sign=None):
    idx = np.asarray(idx)
    sg = np.where(idx >= 0, 1.0 if sign is None else np.asarray(sign, np.float64), 0.0)
    pieces, i, n = [], 0, len(idx)
    while i < n:
        j = i + 1
        if sg[i] == 0.0:
            while j < n and sg[j] == 0.0:
                j += 1
            pieces.append(jnp.zeros((w.shape[0], j - i), w.dtype))
        else:
            while j < n and sg[j] == sg[i] and idx[j] == idx[j - 1] + 1:
                j += 1
            piece = w[:, int(idx[i]):int(idx[i]) + (j - i)]
            pieces.append(piece if sg[i] == 1.0 else piece * sg[i])
        i = j
    return jnp.concatenate(pieces, axis=1)


def _kv_buffers(B, T, tm, n_kv, bufs, total_rows):
    qshape = jax.ShapeDtypeStruct((B, N_HEADS, T, HP), BF16)
    qspec = pl.BlockSpec((None, N_HEADS, tm, HP), lambda b, i: (b, 0, i, 0))
    if bufs is None:
        kvshape = jax.ShapeDtypeStruct((B, N_HEADS, total_rows, HP), BF16)
        return [], [qspec] * (1 + n_kv), [qshape] + [kvshape] * n_kv, []
    off = (bufs[0].shape[2] - T) // tm
    kvspec = pl.BlockSpec((None, N_HEADS, tm, HP), lambda b, i: (b, 0, off + i, 0))
    kvshape = jax.ShapeDtypeStruct(bufs[0].shape, BF16)
    return ([pl.BlockSpec(memory_space=pl.ANY)] * n_kv, [qspec] + [kvspec] * n_kv, [qshape] + [kvshape] * n_kv,
            list(bufs))


def _mla_prep_kernel(p_ref, cq_t, sq_t, ck_t, qn, kvn, wqc, wqs, wk, wv, epe, *rest):
    q_o, k_o, v_o = rest[-3:]
    p = p_ref[...].astype(F32)
    cq = p[:, 0:256]
    cqn = (cq * lax.rsqrt(jnp.sum(cq * cq, axis=-1, keepdims=True) * (1.0 / MLA_Q_RANK) + EPS) * qn[...]).astype(BF16)
    ckv = p[:, 256:384]
    ckvn = _rms(ckv, kvn[...]).astype(BF16)
    cq_all = jnp.concatenate([cq_t[...]] * N_HEADS, axis=1)
    sq_all = jnp.concatenate([sq_t[...]] * N_HEADS, axis=1)
    q = _dot(cqn, wqc[...]) * cq_all + _dot(cqn, wqs[...]) * sq_all
    pe = (p[:, 384:512] * ck_t[...]).astype(BF16)
    k = _dot(ckvn, wk[...]) + _dot(pe, epe[...])
    v = _dot(ckvn, wv[...])
    lane = lax.broadcasted_iota(jnp.int32, (p.shape[0], HP), 1)
    for h in range(N_HEADS):
        q_o[h] = q[:, h * HP:(h + 1) * HP].astype(BF16)
        k_o[h] = k[:, h * HP:(h + 1) * HP].astype(BF16)
        v_o[h] = jnp.where(lane == ONES_LANE, 1.0, v[:, h * HP:(h + 1) * HP]).astype(BF16)


def _mla_weights(lp):
    s = LOG2E / math.sqrt(MLA_NOPE + MLA_ROPE)
    src, sign = _rot_index()
    hd = MLA_NOPE + MLA_ROPE
    qc_idx, qs_idx, qs_sign, k_idx, v_idx = [], [], [], [], []
    for h in range(N_HEADS):
        qc_idx += list(h * hd + np.arange(hd)) + [-1] * (HP - hd)
        qs_idx += [-1] * MLA_NOPE + list(h * hd + MLA_NOPE + src) + [-1] * (HP - hd)
        qs_sign += [0.0] * MLA_NOPE + list(sign) + [0.0] * (HP - hd)
        k_idx += list(h * 128 + np.arange(MLA_NOPE)) + [-1] * (HP - MLA_NOPE)
        v_idx += list(h * 128 + MLA_NOPE + np.arange(64)) + [-1] * (HP - 64)
    wq = jnp.pad(lp['mla_wq_b'] * s, ((0, 256 - MLA_Q_RANK), (0, 0)))
    wqc = _take_cols(wq, qc_idx).astype(BF16)
    wqs = _take_cols(wq, qs_idx, qs_sign).astype(BF16)
    wk = _take_cols(lp['mla_wkv_b'], k_idx).astype(BF16)
    wv = _take_cols(lp['mla_wkv_b'], v_idx).astype(BF16)
    epe = np.zeros((128, N_HEADS * HP), np.float32)
    for h in range(N_HEADS):
        for j in range(MLA_ROPE):
            epe[j, h * HP + MLA_NOPE + j] = 1.0
            epe[MLA_ROPE + j, h * HP + MLA_NOPE + j] = 1.0
    qn = jnp.pad(lp['mla_q_norm'], (0, 256 - MLA_Q_RANK)).reshape(1, 256)
    return qn, lp['mla_kv_norm'].reshape(1, MLA_KV_RANK), wqc, wqs, wk, wv, jnp.asarray(epe, BF16)


def _mla_tables(cos, sin):
    T = cos.shape[0]
    one = jnp.ones((T, MLA_NOPE), F32)
    zero = jnp.zeros((T, HP - MLA_NOPE - MLA_ROPE), F32)
    zn = jnp.zeros((T, MLA_NOPE), F32)
    cq = jnp.concatenate([one, cos, zero], -1)
    sq = jnp.concatenate([zn, sin, zero], -1)
    ck = jnp.concatenate([cos, sin, jnp.zeros((T, 64), F32)], -1)
    return cq, sq, ck


def _mla_prep(p_mla, tables, weights, total_rows, bufs=None):
    B, T, C = p_mla.shape
    tm = _tile(T, 256)
    W = N_HEADS * HP
    tspecs = [pl.BlockSpec((tm, HP), lambda b, i: (i, 0))] * 3
    wspecs = [pl.BlockSpec(w.shape, lambda b, i: (0, 0)) for w in weights]
    aspecs, ospecs, oshapes, extra = _kv_buffers(B, T, tm, 2, bufs, total_rows)
    n_in = 1 + len(tables) + len(weights)
    return pl.pallas_call(
        _mla_prep_kernel,
        grid=(B, T // tm),
        in_specs=[pl.BlockSpec((None, tm, C), lambda b, i: (b, i, 0))] + tspecs + wspecs + aspecs,
        out_specs=ospecs,
        out_shape=oshapes,
        input_output_aliases={n_in + j: 1 + j for j in range(len(extra))},
        compiler_params=_cp("parallel", "parallel"),
        name="mla_prepare",
    )(p_mla, *tables, *weights, *extra)


def _df_prep_kernel(p_ref, c_t, s_t, *rest):
    q_o, k1_o, k2_o, v_o = rest[-4:]
    p = p_ref[...].astype(F32)
    c = jnp.concatenate([c_t[...]] * 2, axis=1)
    s = jnp.concatenate([s_t[...]] * 2, axis=1)
    q = (p[:, 0:BR] * c + p[:, BR:2 * BR] * s) * (LOG2E / math.sqrt(DF_QK))
    k = p[:, 2 * BR:3 * BR] * c + p[:, 3 * BR:4 * BR] * s
    v = p[:, 4 * BR:5 * BR]
    lane = lax.broadcasted_iota(jnp.int32, (p.shape[0], HP), 1)
    for h in range(N_HEADS):
        pair = slice((h // 2) * HP, (h // 2 + 1) * HP)

        def head(x):
            return x[:, pair] if h % 2 == 0 else pltpu.roll(x[:, pair], HEAD_DIM, 1)

        kh = head(k)
        q_o[h] = jnp.where(lane < 2 * DF_QK, head(q), 0.0).astype(BF16)
        k1_o[h] = jnp.where(lane < DF_QK, kh, 0.0).astype(BF16)
        k2_o[h] = jnp.where((lane >= DF_QK) & (lane < 2 * DF_QK), kh, 0.0).astype(BF16)
        v_o[h] = jnp.where(lane == ONES_LANE, 1.0, jnp.where(lane < DF_V, head(v), 0.0)).astype(BF16)


def _df_tables(cos, sin):
    T = cos.shape[0]
    c = jnp.concatenate([cos] * (HP // DF_QK), -1)
    s = jnp.concatenate([sin] * (HP // DF_QK), -1)
    return c, s


def _df_prep(p_df, tables, total_rows, bufs=None):
    B, T, C = p_df.shape
    tm = _tile(T, 256)
    W = N_HEADS * HP
    tspec = pl.BlockSpec((tm, HP), lambda b, i: (i, 0))
    aspecs, ospecs, oshapes, extra = _kv_buffers(B, T, tm, 3, bufs, total_rows)
    return pl.pallas_call(
        _df_prep_kernel,
        grid=(B, T // tm),
        in_specs=[pl.BlockSpec((None, tm, C), lambda b, i: (b, i, 0)), tspec, tspec] + aspecs,
        out_specs=ospecs,
        out_shape=oshapes,
        input_output_aliases={3 + j: 1 + j for j in range(len(extra))},
        compiler_params=_cp("parallel", "parallel"),
        name="diff_prepare",
    )(p_df, *tables, *extra)


def _softmax_pv(q, k_ref, v_ref):
    s = _dot(q, k_ref[...], NT)
    p = jnp.exp2((s - jnp.max(s, axis=-1, keepdims=True)).astype(BF16))
    acc = _dot(p, v_ref[...])
    return acc / acc[:, ONES_LANE:ONES_LANE + 1]


def _mla_attn_kernel(q_ref, k_ref, v_ref, o_ref):
    o_ref[...] = _softmax_pv(q_ref[...], k_ref, v_ref).astype(o_ref.dtype)


def _attn_specs(q, key_rows):
    tq = _tile(q.shape[2], ATTN_TQ)
    first, n = key_rows
    assert first % n == 0
    qspec = pl.BlockSpec((None, None, tq, HP), lambda b, h, i: (b, h, i, 0))
    kspec = pl.BlockSpec((None, None, n, HP), lambda b, h, i: (b, h, first // n, 0))
    return tq, qspec, kspec


def _mla_attention(q, k, v, key_rows):
    B, H, Tq, _ = q.shape
    tq, qspec, kspec = _attn_specs(q, key_rows)
    return pl.pallas_call(
        _mla_attn_kernel,
        grid=(B, H, Tq // tq),
        in_specs=[qspec, kspec, kspec],
        out_specs=qspec,
        out_shape=jax.ShapeDtypeStruct((B, H, Tq, HP), MIX_OUT),
        compiler_params=_cp("parallel", "parallel", "parallel"),
        name="mla_attention",
    )(q, k, v)


def _df_attn_kernel(lam_init, q_ref, k1_ref, k2_ref, v_ref, lq1, lk1, lq2, lk2, sub, o_ref):
    q = q_ref[...]
    lam = (jnp.exp(jnp.sum(lq1[...] * lk1[...], axis=-1, keepdims=True))
           - jnp.exp(jnp.sum(lq2[...] * lk2[...], axis=-1, keepdims=True)) + lam_init)
    o = _softmax_pv(q, k1_ref, v_ref) - lam * _softmax_pv(q, k2_ref, v_ref)
    lane = lax.broadcasted_iota(jnp.int32, o.shape, 1)
    o = jnp.where(lane < DF_V, o, 0.0)
    ms = jnp.sum(o * o, axis=-1, keepdims=True) * (1.0 / DF_V)
    o_ref[...] = (o * lax.rsqrt(ms + DF_SUBLN_EPS) * sub[...] * (1.0 - lam_init)).astype(o_ref.dtype)


def _df_attention(q, k1, k2, v, key_rows, lp, lam_init):
    B, H, Tq, _ = q.shape
    tq, qspec, kspec = _attn_specs(q, key_rows)
    lspec = pl.BlockSpec((1, DF_QK), lambda b, h, i: (0, 0))
    sub = jnp.pad(lp['df_subln'], (0, HP - DF_V)).reshape(1, HP)
    return pl.pallas_call(
        functools.partial(_df_attn_kernel, lam_init),
        grid=(B, H, Tq // tq),
        in_specs=[qspec, kspec, kspec, kspec, lspec, lspec, lspec, lspec,
                  pl.BlockSpec((1, HP), lambda b, h, i: (0, 0))],
        out_specs=qspec,
        out_shape=jax.ShapeDtypeStruct((B, H, Tq, HP), MIX_OUT),
        compiler_params=_cp("parallel", "parallel", "parallel"),
        name="diff_attention",
    )(q, k1, k2, v, lp['df_lq1'].reshape(1, DF_QK), lp['df_lk1'].reshape(1, DF_QK),
      lp['df_lq2'].reshape(1, DF_QK), lp['df_lk2'].reshape(1, DF_QK), sub)


def _merge_kernel(x_ref, sh_ref, sc_ref, gt_ref, gpre_ref, gpost_ref, orw, ohy, omla, odf,
                  wg, bg, urw, uhy, umla, udf, wo, o_ref):
    x = x_ref[...]
    h = (_rms(x, gpre_ref[...]) * (1.0 + sc_ref[...]) + sh_ref[...]).astype(BF16)
    ups = (_dot1(orw[...], urw[...]),
           _dot1(ohy[...], uhy[...]),
           sum(_dot1(omla[hh], umla[hh]) for hh in range(N_HEADS)),
           sum(_dot1(odf[hh], udf[hh]) for hh in range(N_HEADS)))
    acc = None
    for n in range(4):
        t = _sigmoid(_dot(h, wg[n]) + bg[n]) * ups[n]
        acc = t if acc is None else acc + t
    y = _dot(acc.astype(BF16), wo[...])
    o_ref[...] = x + gt_ref[...] * _rms(y, gpost_ref[...])


def _pad_head_rows(w):
    return jnp.pad(w.reshape(N_HEADS, HEAD_DIM, -1), ((0, 0), (0, HP - HEAD_DIM), (0, 0)))


def _merge(x, mod3, g_pre, g_post, outs, mw):
    B, T, D = x.shape
    o_rw, o_hy, o_mla, o_df = outs
    tm = _tile(T, 512)
    xspec = pl.BlockSpec((None, tm, D), lambda b, i: (b, i, 0))
    bspec = pl.BlockSpec((None, tm, BR), lambda b, i: (b, i, 0))
    hspec = pl.BlockSpec((None, N_HEADS, tm, HP), lambda b, i: (b, 0, i, 0))

    def full(a):
        nd = a.ndim
        return pl.BlockSpec(a.shape, lambda b, i: (0,) * nd)

    ws = [mw['wg'], mw['bg'], mw['urw'], mw['uhy'], mw['umla'], mw['udf'], mw['wo']]
    return pl.pallas_call(
        _merge_kernel,
        grid=(B, T // tm),
        in_specs=[xspec, _mod_spec(mod3[0], 2), _mod_spec(mod3[1], 2), _mod_spec(mod3[2], 2),
                  _row_spec(D), _row_spec(D), bspec, bspec, hspec, hspec] + [full(a) for a in ws],
        out_specs=xspec,
        out_shape=jax.ShapeDtypeStruct((B, T, D), F32),
        compiler_params=_cp("parallel", "parallel"),
        name="gated_merge",
    )(x, mod3[0], mod3[1], mod3[2], g_pre.reshape(1, D), g_post.reshape(1, D), o_rw, o_hy, o_mla, o_df, *ws)


def _mix_weights(w_in):
    src, sign = _rot_index()
    w_rw = w_in[:, :RW_COLS]
    w_hy = w_in[:, OFF_HY:OFF_HY + HY_COLS]
    kpe0 = OFF_MLA + MLA_Q_RANK + MLA_KV_RANK
    mla_idx = (list(OFF_MLA + np.arange(MLA_Q_RANK)) + [-1] * (256 - MLA_Q_RANK)
               + list(OFF_MLA + MLA_Q_RANK + np.arange(MLA_KV_RANK))
               + list(kpe0 + np.arange(MLA_ROPE)) + list(kpe0 + src) + [-1] * 64)
    mla_sign = [1.0] * 256 + [1.0] * MLA_KV_RANK + [1.0] * MLA_ROPE + list(sign) + [0.0] * 64
    w_mla = _take_cols(w_in, mla_idx, mla_sign)
    idx, sg = [], []
    for base, rot in ((OFF_DF, False), (OFF_DF, True), (OFF_DF + 256, False), (OFF_DF + 256, True),
                      (OFF_DF + 512, False)):
        for g in range(BR // DF_QK):
            idx += list(base + g * DF_QK + (src if rot else np.arange(DF_QK)))
            sg += list(sign) if rot else [1.0] * DF_QK
    w_df = _take_cols(w_in, idx, sg)
    return [w.astype(BF16) for w in (w_rw, w_hy, w_mla, w_df)]


def _merge_weights(lp):
    return dict(wg=lp['w_gate'].astype(BF16), bg=lp['b_gate'].reshape(4, 1, D_MODEL),
                urw=lp['w_up'][0].astype(BF16), uhy=lp['w_up'][1].astype(BF16),
                umla=_pad_head_rows(lp['w_up'][2]).astype(BF16), udf=_pad_head_rows(lp['w_up'][3]).astype(BF16),
                wo=lp['w_o'].astype(BF16))


def _token_mixing(x, xc, mod, modc, g_pre, lam_init, with_ctx_out, lp, rope):
    B, T, _ = x.shape
    ws = _mix_weights(lp['w_in'])
    p_rw, p_hy, p_mla, p_df = _project(x, mod[0], mod[1], g_pre, ws)
    c_rw, c_hy, c_mla, c_df = _project(xc, modc[0], modc[1], g_pre, ws)

    prep = _rwkv_prepare(p_rw, lp)
    prep_c = _rwkv_prepare(c_rw, lp)
    s0 = jnp.zeros((B, 2, N_HEADS, HEAD_DIM, HEAD_DIM), F32)
    yf_c, yb_c, s_c = _rwkv_scan(prep_c, s0)
    yf, yb, _ = _rwkv_scan(prep, s_c)
    o_rw = _rwkv_output(yf, yb, prep, lp)

    o_hy = _hyena(p_hy, lp)

    cos, sin = rope
    Tc = xc.shape[1]
    ones, zeros = jnp.ones((Tc, MLA_ROPE), F32), jnp.zeros((Tc, MLA_ROPE), F32)
    mla_w = _mla_weights(lp)
    q, k, v = _mla_prep(p_mla, _mla_tables(cos, sin), mla_w, T + Tc)
    qc, k, v = _mla_prep(c_mla, _mla_tables(ones, zeros), mla_w, T + Tc, bufs=(k, v))
    o_mla = _mla_attention(q, k, v, (0, T + Tc))

    dq, dk1, dk2, dv = _df_prep(p_df, _df_tables(cos, sin), T + Tc)
    dqc, dk1, dk2, dv = _df_prep(c_df, _df_tables(ones, zeros), T + Tc, bufs=(dk1, dk2, dv))
    o_df = _df_attention(dq, dk1, dk2, dv, (0, T + Tc), lp, lam_init)

    outs = (o_rw, o_hy, o_mla, o_df)
    if not with_ctx_out:
        return outs, None
    oc_rw = _rwkv_output(yf_c, yb_c, prep_c, lp)
    oc_hy = _hyena(c_hy, lp)
    oc_mla = _mla_attention(qc, k, v, (T, Tc))
    oc_df = _df_attention(dqc, dk1, dk2, dv, (T, Tc), lp, lam_init)
    return outs, (oc_rw, oc_hy, oc_mla, oc_df)


def kernel(x, c, ctx, c_ctx, w_mod, b_mod, norm_pre, norm_post, ffn_w_in, ffn_w_out, w_in, rw_mu, rw_w0, rw_w2, rw_a0, rw_a2, rw_g2, rw_kk, rw_ka, rw_rk, rw_ln_w, rw_ln_b, hy_conv_w, hy_conv_b, hy_w1, hy_b1, hy_f1, hy_w2, hy_b2, hy_f2, hy_w3, hy_bias, mla_q_norm, mla_wq_b, mla_kv_norm, mla_wkv_b, df_lq1, df_lk1, df_lq2, df_lk2, df_subln, w_up, w_gate, b_gate, w_o):
    params = dict(w_in=w_in, rw_mu=rw_mu, rw_w0=rw_w0, rw_w2=rw_w2, rw_a0=rw_a0, rw_a2=rw_a2, rw_g2=rw_g2,
                  rw_kk=rw_kk, rw_ka=rw_ka, rw_rk=rw_rk, rw_ln_w=rw_ln_w, rw_ln_b=rw_ln_b, hy_conv_w=hy_conv_w,
                  hy_conv_b=hy_conv_b, hy_w1=hy_w1, hy_b1=hy_b1, hy_f1=hy_f1, hy_w2=hy_w2, hy_b2=hy_b2,
                  hy_f2=hy_f2, hy_w3=hy_w3, hy_bias=hy_bias, mla_q_norm=mla_q_norm, mla_wq_b=mla_wq_b,
                  mla_kv_norm=mla_kv_norm, mla_wkv_b=mla_wkv_b, df_lq1=df_lq1, df_lk1=df_lk1, df_lq2=df_lq2,
                  df_lk2=df_lk2, df_subln=df_subln, w_up=w_up, w_gate=w_gate, b_gate=b_gate, w_o=w_o)
    B, T, D = x.shape
    depth = w_mod.shape[0]
    assert B <= 7 and D == D_MODEL and T % 128 == 0 and ctx.shape[1] % CHUNK == 0 and T % ctx.shape[1] == 0
    rope = _rope_tables(T)

    s_rows = jnp.concatenate([c, c_ctx[None], jnp.zeros((8 - B - 1, D), F32)], 0)
    mod_all = _modulation(s_rows, w_mod, b_mod).reshape(depth, 8, 3, 3, D)
    ffn_in = ffn_w_in.astype(BF16)
    ffn_out = ffn_w_out.astype(BF16)

    xc = ctx
    for l in range(depth):
        last = l == depth - 1
        lp = {k_: v_[l] for k_, v_ in params.items()}
        mod = [[mod_all[l, :B, s, m][:, None, :] for m in range(3)] for s in range(3)]
        modc = [[mod_all[l, B:B + 1, s, m][:, None, :] for m in range(3)] for s in range(3)]
        ffn_a = (norm_pre[l, 0], norm_post[l, 0], ffn_in[l, 0], ffn_out[l, 0])
        ffn_b = (norm_pre[l, 2], norm_post[l, 2], ffn_in[l, 1], ffn_out[l, 1])

        x = _ffn(x, mod[0], *ffn_a)
        xc = _ffn(xc, modc[0], *ffn_a)

        lam_init = 0.8 - 0.6 * math.exp(-0.3 * l)
        outs, outs_c = _token_mixing(x, xc, mod[1], modc[1], norm_pre[l, 1], lam_init, not last, lp, rope)
        mw = _merge_weights(lp)
        x = _merge(x, mod[1], norm_pre[l, 1], norm_post[l, 1], outs, mw)
        x = _ffn(x, mod[2], *ffn_b)
        if not last:
            xc = _merge(xc, modc[1], norm_pre[l, 1], norm_post[l, 1], outs_c, mw)
            xc = _ffn(xc, modc[2], *ffn_b)
    return x
```

```python
import functools
import math

import numpy as np
import jax
import jax.numpy as jnp
from jax import lax
from jax.experimental import pallas as pl
from jax.experimental.pallas import tpu as pltpu

F32 = jnp.float32
BF16 = jnp.bfloat16
MIX_OUT = BF16

D_MODEL = 1024
GRID_W = 64
N_HEADS = 4
HEAD_DIM = 64
BR = N_HEADS * HEAD_DIM
D_FF = 2816
MACARON_W = 0.5
EPS = 1e-6
ROPE_BASE = 10000.0
RW_LN_EPS = 64e-5
RW_COLS = 1152
HY_COLS = 768
HY_EMB = 33
HY_FAST_DECAY = 0.3
HY_SLOW_DECAY = 1.5
HY_TARGET = 1e-2
MLA_Q_RANK = 192
MLA_KV_RANK = 128
MLA_NOPE = 64
MLA_ROPE = 32
DF_QK = 32
DF_V = 64
DF_SUBLN_EPS = 1e-5
OFF_HY = RW_COLS
OFF_MLA = RW_COLS + HY_COLS
OFF_DF = OFF_MLA + MLA_Q_RANK + MLA_KV_RANK + MLA_ROPE

LANES = 128
HP = 128
ONES_LANE = 64
LOG2E = 1.4426950408889634
FFN_TM = 256
FFN_TF = 2816
ATTN_TQ = 256
MERGE_SPLIT = 2
FFT_DATA_PASSES = 1
CHUNK = 64
RW_P_GRAM = 1
RW_P_INV = (3, 3)
RW_P_STATE = 3
RW_P_APPLY = 1
VMEM_LIMIT = 56 * 1024 * 1024


def _cp(*sem):
    return pltpu.CompilerParams(dimension_semantics=sem, vmem_limit_bytes=VMEM_LIMIT)


def _tile(n, pref, mult=8):
    if n <= pref:
        return n
    t = (pref // mult) * mult
    while t > mult and n % t:
        t -= mult
    assert n % t == 0, (n, pref)
    return t


def _dot(a, b, dims=None):
    if dims is None:
        dims = (((a.ndim - 1,), (0,)), ((), ()))
    return lax.dot_general(a, b, dims, preferred_element_type=F32)


NT = (((1,), (1,)), ((), ()))
TN = (((0,), (0,)), ((), ()))


def _split(x):
    hi = x.astype(BF16)
    lo = (x - hi.astype(F32)).astype(BF16)
    return hi, lo


def _dot3(a, b, dims=None):
    ah, al = _split(a)
    bh, bl = _split(b)
    return _dot(ah, bh, dims) + (_dot(ah, bl, dims) + _dot(al, bh, dims))


def _dot1(a, b, dims=None):
    return _dot(a.astype(BF16), b.astype(BF16), dims)


def _dotc(ch, cl, x, passes=3):
    if passes == 1:
        return _dot(ch, x.astype(BF16))
    xh, xl = _split(x)
    return _dot(ch, xh) + (_dot(ch, xl) + _dot(cl, xh))


def _dot_exact_lhs(m, x):
    x1 = x.astype(BF16)
    r1 = x - x1.astype(F32)
    x2 = r1.astype(BF16)
    x3 = (r1 - x2.astype(F32)).astype(BF16)
    return _dot(m, x1) + (_dot(m, x2) + _dot(m, x3))


def _rms(x, g, eps=EPS):
    return x * lax.rsqrt(jnp.mean(x * x, axis=-1, keepdims=True) + eps) * g


def _sigmoid(x):
    return 1.0 / (1.0 + jnp.exp(-x))


def _silu(x):
    return x * _sigmoid(x)


def _softplus(z):
    return jnp.maximum(z, 0.0) + jnp.log(1.0 + jnp.exp(-jnp.abs(z)))


def _np_split(a):
    a = jnp.asarray(a, F32)
    hi = a.astype(BF16)
    lo = (a - hi.astype(F32)).astype(BF16)
    return hi, lo


def _mod_kernel(s_ref, w_ref, b_ref, o_ref):
    s = _silu(s_ref[...])
    o_ref[...] = _dot3(s, w_ref[...]) + b_ref[...]


def _modulation(s_rows, w_mod, b_mod):
    L, D, NM = w_mod.shape
    tn = _tile(NM, 1152, LANES)
    return pl.pallas_call(
        _mod_kernel,
        grid=(L, NM // tn),
        in_specs=[pl.BlockSpec((8, D), lambda l, j: (0, 0)),
                  pl.BlockSpec((None, D, tn), lambda l, j: (l, 0, j)),
                  pl.BlockSpec((None, 1, tn), lambda l, j: (l, 0, j))],
        out_specs=pl.BlockSpec((None, 8, tn), lambda l, j: (l, 0, j)),
        out_shape=jax.ShapeDtypeStruct((L, 8, NM), F32),
        compiler_params=_cp("parallel", "parallel"),
        name="modulation",
    )(s_rows, w_mod, b_mod.reshape(L, 1, NM))


def _mod_spec(m, nargs):
    if m.shape[0] == 1:
        return pl.BlockSpec((None, 1, D_MODEL), lambda *a: (0, 0, 0))
    return pl.BlockSpec((None, 1, D_MODEL), lambda *a: (a[0], 0, 0))


def _row_spec(n):
    return pl.BlockSpec((1, n), lambda *a: (0, 0))


def _ffn_kernel(x_ref, sh_ref, sc_ref, gt_ref, gpre_ref, gpost_ref, wa_ref, wb_ref, wo_ref, o_ref, h_sc, acc_sc):
    f = pl.program_id(2)

    @pl.when(f == 0)
    def _():
        h = _rms(x_ref[...], gpre_ref[...]) * (1.0 + sc_ref[...]) + sh_ref[...]
        h_sc[...] = h.astype(BF16)
        acc_sc[...] = jnp.zeros_like(acc_sc)

    h = h_sc[...]
    a = _dot(h, wa_ref[...])
    b = _dot(h, wb_ref[...])
    acc_sc[...] += _dot((_silu(a) * b).astype(BF16), wo_ref[...])

    @pl.when(f == pl.num_programs(2) - 1)
    def _():
        o_ref[...] = x_ref[...] + MACARON_W * gt_ref[...] * _rms(acc_sc[...], gpost_ref[...])


def _ffn(x, mod3, g_pre, g_post, w_in, w_out):
    B, T, D = x.shape
    F = w_out.shape[0]
    tm = _tile(T, FFN_TM)
    tf = _tile(F, FFN_TF, LANES)
    nf = F // tf
    xspec = pl.BlockSpec((None, tm, D), lambda b, i, f: (b, i, 0))
    return pl.pallas_call(
        _ffn_kernel,
        grid=(B, T // tm, nf),
        in_specs=[xspec, _mod_spec(mod3[0], 3), _mod_spec(mod3[1], 3), _mod_spec(mod3[2], 3),
                  _row_spec(D), _row_spec(D),
                  pl.BlockSpec((D, tf), lambda b, i, f: (0, f)),
                  pl.BlockSpec((D, tf), lambda b, i, f: (0, nf + f)),
                  pl.BlockSpec((tf, D), lambda b, i, f: (f, 0))],
        out_specs=xspec,
        out_shape=jax.ShapeDtypeStruct((B, T, D), F32),
        scratch_shapes=[pltpu.VMEM((tm, D), BF16), pltpu.VMEM((tm, D), F32)],
        compiler_params=_cp("parallel", "parallel", "arbitrary"),
        name="ffn",
    )(x, mod3[0], mod3[1], mod3[2], g_pre.reshape(1, D), g_post.reshape(1, D), w_in, w_in, w_out)


def _proj_kernel(x_ref, sh_ref, sc_ref, gpre_ref, wrw, why, wmla, wdf, orw, ohy, omla, odf):
    h = (_rms(x_ref[...], gpre_ref[...]) * (1.0 + sc_ref[...]) + sh_ref[...]).astype(BF16)
    orw[...] = _dot(h, wrw[...])
    ohy[...] = _dot(h, why[...])
    omla[...] = _dot(h, wmla[...]).astype(omla.dtype)
    odf[...] = _dot(h, wdf[...]).astype(odf.dtype)


def _project(x, shift, scale, g_pre, ws):
    B, T, D = x.shape
    tm = _tile(T, 256)
    xspec = pl.BlockSpec((None, tm, D), lambda b, i: (b, i, 0))
    wspecs = [pl.BlockSpec(w.shape, lambda b, i: (0, 0)) for w in ws]
    ospecs = [pl.BlockSpec((None, tm, w.shape[1]), lambda b, i: (b, i, 0)) for w in ws]
    oshapes = [jax.ShapeDtypeStruct((B, T, w.shape[1]), dt) for w, dt in zip(ws, (F32, F32, BF16, BF16))]
    return pl.pallas_call(
        _proj_kernel,
        grid=(B, T // tm),
        in_specs=[xspec, _mod_spec(shift, 2), _mod_spec(scale, 2), _row_spec(D)] + wspecs,
        out_specs=ospecs,
        out_shape=oshapes,
        compiler_params=_cp("parallel", "parallel"),
        name="mix_project",
    )(x, shift, scale, g_pre.reshape(1, D), *ws)


def _halo_specs(tm, T, C):
    nb8 = T // 8
    r = tm // 8
    main = pl.BlockSpec((None, tm, C), lambda b, i: (b, i, 0))
    prev = pl.BlockSpec((None, 8, C), lambda b, i: (b, jnp.maximum(i * r - 1, 0), 0))
    nxt = pl.BlockSpec((None, 8, C), lambda b, i: (b, jnp.minimum((i + 1) * r, nb8 - 1), 0))
    return main, prev, nxt


def _conv3_tile(p, prev8, next8, w0, w1, w2):
    i = pl.program_id(1)
    n_i = pl.num_programs(1)
    tm = p.shape[0]
    row = lax.broadcasted_iota(jnp.int32, p.shape, 0)
    prev_row = jnp.where(i > 0, prev8[7:8, :], 0.0)
    next_row = jnp.where(i < n_i - 1, next8[0:1, :], 0.0)
    pm = jnp.where(row == 0, prev_row, pltpu.roll(p, 1, 0))
    pp = jnp.where(row == tm - 1, next_row, pltpu.roll(p, tm - 1, 0))
    return pm * w0 + p * w1 + pp * w2


def _rwkv_prep_kernel(p_ref, pv_ref, nx_ref, mu_ref, w0_ref, w2_ref, a0_ref, a2_ref, g2_ref, kk_ref, ka_ref,
                      rk_ref, bd_ref, trif_ref, trib_ref, onec_ref, v_o, g_o, bonus_o, *dir_outs):
    mu = mu_ref[...]
    p = _conv3_tile(p_ref[...], pv_ref[...], nx_ref[...], 0.5 * mu, 1.0 - mu, 0.5 * mu)
    r = p[:, 0:BR]
    k = p[:, BR:2 * BR]
    v = p[:, 2 * BR:3 * BR]
    g_a = p[:, 3 * BR:3 * BR + 128]
    wa = p[:, 896:1024]
    aa = p[:, 1024:1152]
    bd = bd_ref[...]
    kkraw = k * kk_ref[...]
    ss = _dot_exact_lhs_rhs(kkraw * kkraw, bd)
    kk = kkraw / jnp.maximum(jnp.sqrt(ss), 1e-12)
    wl = w0_ref[...] + _dot3(jnp.tanh(wa), w2_ref[...])
    lw = -jnp.exp(-_softplus(-wl) - 0.5)
    a = _sigmoid(a0_ref[...] + _dot3(aa, a2_ref[...]))
    a_f = a[:, :BR]
    a_b = a[:, BR:]
    ka = ka_ref[...]
    k_f = k * (1.0 + (a_f - 1.0) * ka)
    k_b = k * (1.0 + (a_b - 1.0) * ka)
    v_o[...] = v
    g_o[...] = _dot1(_sigmoid(g_a), g2_ref[...])
    bonus_o[...] = _dot_exact_lhs_rhs(r * (k_f + k_b) * rk_ref[...], bd) * v
    onec = onec_ref[...]
    for d, (tri_ref, k_d, a_d) in enumerate(((trif_ref, k_f, a_f), (trib_ref, k_b, a_b))):
        rt_o, at_o, bt_o, kt_o, bh_o, kh_o, eg_o = dir_outs[7 * d:7 * d + 7]
        lw_d = lw[:, d * BR:(d + 1) * BR]
        b_d = kk * a_d
        g = _dot_exact_lhs(tri_ref[...], lw_d)
        gtot = _dot_exact_lhs(onec, lw_d)
        eng = jnp.exp(-g)
        et = jnp.exp(gtot - g)
        rt_o[...] = r * jnp.exp(g)
        at_o[...] = -kk * jnp.exp(g - lw_d)
        bt_o[...] = (b_d * eng).astype(bt_o.dtype)
        kt_o[...] = (k_d * eng).astype(kt_o.dtype)
        bh_o[...] = b_d * et
        kh_o[...] = k_d * et
        eg = jnp.exp(gtot)
        for j in range(eg_o.shape[0]):
            eg_o[j] = eg[j * CHUNK:j * CHUNK + 1, :]


def _dot_exact_lhs_rhs(x, m):
    x1 = x.astype(BF16)
    r1 = x - x1.astype(F32)
    x2 = r1.astype(BF16)
    x3 = (r1 - x2.astype(F32)).astype(BF16)
    return _dot(x1, m) + (_dot(x2, m) + _dot(x3, m))


def _block_diag_ones():
    h = np.arange(BR) // HEAD_DIM
    return jnp.asarray((h[:, None] == h[None, :]).astype(np.float32), BF16)


def _rwkv_prepare(p_rw, lp):
    B, T, C = p_rw.shape
    tm = _tile(T, 256)
    main, prev, nxt = _halo_specs(tm, T, C)
    z = jnp.zeros((64, BR), F32)
    w2 = jnp.concatenate([jnp.concatenate([lp['rw_w2'][0], z], 1), jnp.concatenate([z, lp['rw_w2'][1]], 1)], 0)
    a2 = jnp.concatenate([jnp.concatenate([lp['rw_a2'][0], z], 1), jnp.concatenate([z, lp['rw_a2'][1]], 1)], 0)
    params = [lp['rw_mu'].reshape(1, C), lp['rw_w0'].reshape(1, 2 * BR), w2, lp['rw_a0'].reshape(1, 2 * BR), a2,
              lp['rw_g2'].astype(BF16), lp['rw_kk'].reshape(1, BR), lp['rw_ka'].reshape(1, BR),
              lp['rw_rk'].reshape(1, BR), _block_diag_ones()]
    pos = np.arange(tm)
    same = (pos[:, None] // CHUNK) == (pos[None, :] // CHUNK)
    params += [jnp.asarray((same & (pos[None, :] <= pos[:, None])).astype(np.float32), BF16),
               jnp.asarray((same & (pos[None, :] >= pos[:, None])).astype(np.float32), BF16),
               jnp.asarray(same.astype(np.float32), BF16)]
    pspecs = [pl.BlockSpec(a.shape, lambda b, i: (0, 0)) for a in params]
    ospec = pl.BlockSpec((None, tm, BR), lambda b, i: (b, i, 0))
    names = ('v', 'g', 'bonus') + tuple(n + str(d) for d in (0, 1) for n in ('rt', 'at', 'bt', 'kt', 'bh', 'kh', 'eg'))
    gram_dt = BF16 if RW_P_GRAM == 1 else F32
    egspec = pl.BlockSpec((None, tm // CHUNK, 1, BR), lambda b, i: (b, i, 0, 0))
    ospecs = [egspec if n[:2] == 'eg' else ospec for n in names]
    oshapes = [jax.ShapeDtypeStruct((B, T // CHUNK, 1, BR), F32) if n[:2] == 'eg' else
               jax.ShapeDtypeStruct((B, T, BR), gram_dt if n[:2] in ('bt', 'kt') else F32) for n in names]
    outs = pl.pallas_call(
        _rwkv_prep_kernel,
        grid=(B, T // tm),
        in_specs=[main, prev, nxt] + pspecs,
        out_specs=ospecs,
        out_shape=oshapes,
        compiler_params=_cp("parallel", "parallel"),
        name="rwkv_prepare",
    )(p_rw, p_rw, p_rw, *params)
    return dict(zip(names, outs))


def _dot3s(a, b, dims=None):
    if dims is None:
        dims = (((1,), (0,)), ((), ()))
    ca, cb = dims[0][0][0], dims[0][1][0]
    a_lo = a - a.astype(BF16).astype(F32)
    b_lo = b - b.astype(BF16).astype(F32)
    sa = jnp.concatenate([a, a_lo, a], axis=ca).astype(BF16)
    sb = jnp.concatenate([b, b, b_lo], axis=cb).astype(BF16)
    return _dot(sa, sb, dims)


def _mmp(a, b, dims, passes):
    return _dot3s(a, b, dims) if passes == 3 else _dot1(a, b, dims)


def _rwkv_chunks(chains, eye):
    n = range(len(chains))
    rt, at, bt, kt, bh, kh, eg, v, ht, mask_s, mask_i = zip(*chains)
    C = rt[0].shape[0]
    gm = [_mmp(jnp.concatenate([at[i], rt[i]], 0), jnp.concatenate([bt[i], kt[i]], 0), NT, RW_P_GRAM) for i in n]
    top = [jnp.where(mask_s[i], gm[i][:C], 0.0) for i in n]
    bot = [jnp.where(mask_i[i], gm[i][C:], 0.0) for i in n]
    pw = [top[i][:, :C] for i in n]
    tinv = [eye + pw[i] for i in n]
    for _ in range(int(math.log2(C)) - 1):
        pw = [_mmp(pw[i], pw[i], None, RW_P_INV[0]) for i in n]
        upd = [_mmp(tinv[i], pw[i], None, RW_P_INV[1]) for i in n]
        tinv = [tinv[i] + upd[i] for i in n]
    w0 = [_mmp(at[i], ht[i], NT, RW_P_STATE) for i in n]
    w1 = [_mmp(top[i][:, C:], v[i], None, RW_P_APPLY) for i in n]
    u = [_mmp(tinv[i], w0[i] + w1[i], None, RW_P_STATE) for i in n]
    uv = [jnp.concatenate([u[i], v[i]], 0) for i in n]
    y0 = [_mmp(rt[i], ht[i], NT, RW_P_STATE) for i in n]
    y1 = [_mmp(bot[i], uv[i], None, RW_P_APPLY) for i in n]
    hu = [_mmp(uv[i], jnp.concatenate([bh[i], kh[i]], 0), TN, RW_P_STATE) for i in n]
    return [(y0[i] + y1[i], ht[i] * eg[i] + hu[i]) for i in n]


def _rwkv_scan_kernel(*refs):
    fwd_in, bwd_in = refs[0:8], refs[8:16]
    s0_ref, yf_o, yb_o, s_o, st = refs[16:]
    c = pl.program_id(0)
    nb, C = yf_o.shape[0], yf_o.shape[1]

    @pl.when(c == 0)
    def _():
        st[...] = s0_ref[...]

    row = lax.broadcasted_iota(jnp.int32, (C, 2 * C), 0)
    col = lax.broadcasted_iota(jnp.int32, (C, 2 * C), 1)
    col = jnp.where(col >= C, col - C, col)
    eye = (lax.broadcasted_iota(jnp.int32, (C, C), 0) == lax.broadcasted_iota(jnp.int32, (C, C), 1)).astype(F32)
    chains = []
    for b in range(nb):
        for d, ins in enumerate((fwd_in, bwd_in)):
            if d == 0:
                mask_i = col <= row
                mask_s = col < row
            else:
                mask_i = col >= row
                mask_s = col > row
            for h in range(N_HEADS):
                sl = pl.ds(h * HEAD_DIM, HEAD_DIM)
                rt, at, bt, kt, bh, kh, eg, v = (ref[b, :, sl] for ref in ins)
                chains.append((rt, at, bt, kt, bh, kh, eg, v, st[b, d, h], mask_s, mask_i))
    results = iter(_rwkv_chunks(chains, eye))
    for b in range(nb):
        for d, y_o in enumerate((yf_o, yb_o)):
            for h in range(N_HEADS):
                y, hn = next(results)
                y_o[b, :, pl.ds(h * HEAD_DIM, HEAD_DIM)] = y
                st[b, d, h] = hn

    @pl.when(c == pl.num_programs(0) - 1)
    def _():
        s_o[...] = st[...]


def _rwkv_scan(prep, s0):
    B, T, _ = prep['v'].shape
    nc = T // CHUNK
    fwd = pl.BlockSpec((B, CHUNK, BR), lambda c: (0, c, 0))
    bwd = pl.BlockSpec((B, CHUNK, BR), lambda c: (0, nc - 1 - c, 0))
    sspec = pl.BlockSpec(s0.shape, lambda c: (0, 0, 0, 0, 0))
    ins = [prep[n + str(d)] for d in (0, 1) for n in ('rt', 'at', 'bt', 'kt', 'bh', 'kh', 'eg')]
    return pl.pallas_call(
        _rwkv_scan_kernel,
        grid=(nc,),
        in_specs=[fwd] * 6 + [pl.BlockSpec((B, None, 1, BR), lambda c: (0, c, 0, 0)), fwd]
        + [bwd] * 6 + [pl.BlockSpec((B, None, 1, BR), lambda c: (0, nc - 1 - c, 0, 0)), bwd] + [sspec],
        out_specs=[fwd, bwd, sspec],
        out_shape=[jax.ShapeDtypeStruct((B, T, BR), F32), jax.ShapeDtypeStruct((B, T, BR), F32),
                   jax.ShapeDtypeStruct(s0.shape, F32)],
        scratch_shapes=[pltpu.VMEM(s0.shape, F32)],
        compiler_params=_cp("arbitrary"),
        name="rwkv_scan",
    )(*ins[:7], prep['v'], *ins[7:], prep['v'], s0)


def _rwkv_out_kernel(yf, yb, g, bonus, lnw, lnb, bd, o):
    y = yf[...] + yb[...]
    m = bd[...]
    inv_n = 1.0 / HEAD_DIM
    mean = _dot_exact_lhs_rhs(y, m) * inv_n
    yc = y - mean
    var = _dot_exact_lhs_rhs(yc * yc, m) * inv_n
    yn = yc * lax.rsqrt(var + RW_LN_EPS) * lnw[...] + lnb[...]
    o[...] = ((yn + bonus[...]) * g[...]).astype(o.dtype)


def _rwkv_output(yf, yb, prep, lp):
    B, T, _ = yf.shape
    tm = _tile(T, 512)
    spec = pl.BlockSpec((None, tm, BR), lambda b, i: (b, i, 0))
    return pl.pallas_call(
        _rwkv_out_kernel,
        grid=(B, T // tm),
        in_specs=[spec] * 4 + [_row_spec(BR), _row_spec(BR), pl.BlockSpec((BR, BR), lambda b, i: (0, 0))],
        out_specs=spec,
        out_shape=jax.ShapeDtypeStruct((B, T, BR), MIX_OUT),
        compiler_params=_cp("parallel", "parallel"),
        name="rwkv_output",
    )(yf, yb, prep['g'], prep['bonus'], lp['rw_ln_w'].reshape(1, BR), lp['rw_ln_b'].reshape(1, BR),
      _block_diag_ones())


def _hy_pre_kernel(p_ref, pv_ref, nx_ref, w_ref, b_ref, x0_o, z_o):
    w = w_ref[...]
    u = _conv3_tile(p_ref[...], pv_ref[...], nx_ref[...], w[0:1, :], w[1:2, :], w[2:3, :]) + b_ref[...]
    x0_o[...] = u[:, 0:BR]
    z_o[...] = u[:, 2 * BR:3 * BR] * u[:, BR:2 * BR]


def _hyena_pre(p_hy, lp):
    B, T, C = p_hy.shape
    tm = _tile(T, 512)
    main, prev, nxt = _halo_specs(tm, T, C)
    ospec = pl.BlockSpec((None, tm, BR), lambda b, i: (b, i, 0))
    return pl.pallas_call(
        _hy_pre_kernel,
        grid=(B, T // tm),
        in_specs=[main, prev, nxt, pl.BlockSpec((3, C), lambda b, i: (0, 0)), _row_spec(C)],
        out_specs=[ospec, ospec],
        out_shape=[jax.ShapeDtypeStruct((B, T, BR), F32)] * 2,
        compiler_params=_cp("parallel", "parallel"),
        name="hyena_pre",
    )(p_hy, p_hy, p_hy, lp['hy_conv_w'], lp['hy_conv_b'].reshape(1, C))


def _hy_filter_kernel(z_ref, t_ref, mf_ref, mb_ref, w1, b1, f1, w2, b2, f2, w3, dl, k_o, ss_o):
    h = jnp.sin(f1[...] * (_dot3(z_ref[...], w1[...]) + b1[...]))
    h = jnp.sin(f2[...] * (_dot3(h, w2[...]) + b2[...]))
    h = _dot1(h, w3[...]) * jnp.exp(-t_ref[...] * dl[...])
    kern = mf_ref[...] * h[:, :BR] + mb_ref[...] * h[:, BR:]
    k_o[...] = kern

    @pl.when(pl.program_id(0) == 0)
    def _():
        ss_o[...] = jnp.zeros_like(ss_o)

    ss_o[...] += jnp.sum(kern * kern, axis=0, keepdims=True)


def _hyena_filter(n, lp):
    t = jnp.linspace(0.0, 1.0, n, dtype=F32)
    bands = (HY_EMB - 1) // 2
    wpos = 2.0 * math.pi * jnp.arange(n, dtype=F32) / n
    fr = jnp.linspace(1e-4, bands - 1, bands, dtype=F32)
    t_ext = jnp.concatenate([t, t[::-1]])[:, None]
    fr_lane = jnp.concatenate([jnp.zeros((1,), F32), fr, fr, jnp.zeros((LANES - HY_EMB,), F32)])[None]
    ang = jnp.concatenate([wpos, wpos[::-1]])[:, None] * fr_lane
    lane = lax.broadcasted_iota(jnp.int32, ang.shape, 1)
    z_ext = jnp.where(lane == 0, t_ext, jnp.where(lane <= bands, jnp.cos(ang),
                                                  jnp.where(lane < HY_EMB, -jnp.sin(ang), 0.0)))
    rho = np.arange(2 * n)
    mf = jnp.asarray((rho < n).astype(np.float32))[:, None]
    mb = jnp.asarray((rho > n).astype(np.float32))[:, None]
    max_decay = math.log(HY_TARGET) / HY_FAST_DECAY
    min_decay = math.log(HY_TARGET) / HY_SLOW_DECAY
    deltas = jnp.abs(jnp.linspace(min_decay, max_decay, BR, dtype=F32))
    dl = jnp.tile(deltas, 2)[None]
    w1 = jnp.pad(lp['hy_w1'], ((0, LANES - HY_EMB), (0, 0)))
    params = [w1, lp['hy_b1'][None], lp['hy_f1'][None], lp['hy_w2'], lp['hy_b2'][None], lp['hy_f2'][None],
              lp['hy_w3'], dl]
    tm = _tile(2 * n, 512)
    return pl.pallas_call(
        _hy_filter_kernel,
        grid=(2 * n // tm,),
        in_specs=[pl.BlockSpec((tm, LANES), lambda i: (i, 0))] + [pl.BlockSpec((tm, 1), lambda i: (i, 0))] * 3
        + [pl.BlockSpec(a.shape, lambda i: (0, 0)) for a in params],
        out_specs=[pl.BlockSpec((tm, BR), lambda i: (i, 0)), pl.BlockSpec((1, BR), lambda i: (0, 0))],
        out_shape=[jax.ShapeDtypeStruct((2 * n, BR), F32), jax.ShapeDtypeStruct((1, BR), F32)],
        compiler_params=_cp("arbitrary"),
        name="hyena_filter",
    )(z_ext, t_ext, mf, mb, *params)


def _dft_tables(N1, N2):
    N = N1 * N2

    def mat(n):
        a = -2.0 * np.pi * np.outer(np.arange(n), np.arange(n)) / n
        return np.cos(a), np.sin(a)

    f1r, f1i = mat(N1)
    f2r, f2i = mat(N2)
    a = -2.0 * np.pi * np.outer(np.arange(N1), np.arange(N2)) / N
    return dict(f1r=f1r, f1i=f1i, f2r=f2r, f2i=f2i,
                twr=jnp.asarray(np.cos(a)[:, :, None], F32), twi=jnp.asarray(np.sin(a)[:, :, None], F32))


def _fft_in_kernel(passes, x_ref, frh, frl, fih, fil, ar_o, ai_o):
    x = x_ref[...]
    ar_o[...] = _dotc(frh[...], frl[...], x, passes).astype(ar_o.dtype)
    ai_o[...] = _dotc(fih[...], fil[...], x, passes).astype(ai_o.dtype)


def _fft_in(x2d, tabs, N1, passes):
    B, n1u, cols = x2d.shape
    consts = list(_np_split(tabs['f1r'][:, :n1u])) + list(_np_split(tabs['f1i'][:, :n1u]))
    tc = _tile(cols, 8192, LANES)
    ospec = pl.BlockSpec((None, N1, tc), lambda b, j: (b, 0, j))
    return pl.pallas_call(
        functools.partial(_fft_in_kernel, passes),
        grid=(B, cols // tc),
        in_specs=[pl.BlockSpec((None, n1u, tc), lambda b, j: (b, 0, j))]
        + [pl.BlockSpec((N1, n1u), lambda b, j: (0, 0))] * 4,
        out_specs=[ospec, ospec],
        out_shape=[jax.ShapeDtypeStruct((B, N1, cols), F32 if passes == 3 else BF16)] * 2,
        compiler_params=_cp("parallel", "parallel"),
        name="fft_first_axis",
    )(x2d, *consts)


def _cplx_dft(frh, frl, fih, fil, xr, xi, conj, passes):
    rr = _dotc(frh, frl, xr, passes)
    ii = _dotc(fih, fil, xi, passes)
    ri = _dotc(frh, frl, xi, passes)
    ir = _dotc(fih, fil, xr, passes)
    if conj:
        return rr + ii, ri - ir
    return rr - ii, ri + ir


def _fft_spec_kernel(ar, ai, twr, twi, frh, frl, fih, fil, sc, kr_o, ki_o):
    tr, ti = twr[...], twi[...]
    xr = ar[...] * tr - ai[...] * ti
    xi = ar[...] * ti + ai[...] * tr
    br, bi = _cplx_dft(frh[...], frl[...], fih[...], fil[...], xr, xi, False, 3)
    kr_o[...] = br * sc[...]
    ki_o[...] = bi * sc[...]


def _fft_conv_kernel(ar, ai, twr, twi, frh, frl, fih, fil, kr, ki, dr_o, di_o):
    tr, ti = twr[...], twi[...]
    f = (frh[...], frl[...], fih[...], fil[...])
    a_r, a_i = ar[...].astype(F32), ai[...].astype(F32)
    xr = a_r * tr - a_i * ti
    xi = a_r * ti + a_i * tr
    br, bi = _cplx_dft(*f, xr, xi, False, FFT_DATA_PASSES)
    cr = br * kr[...] - bi * ki[...]
    ci = br * ki[...] + bi * kr[...]
    dr, di = _cplx_dft(*f, cr, ci, True, FFT_DATA_PASSES)
    dr_o[...] = (dr * tr + di * ti).astype(dr_o.dtype)
    di_o[...] = (di * tr - dr * ti).astype(di_o.dtype)


def _fft_mid(a_re, a_im, tabs, kf=None, scale=None):
    B, N1, N2, C = a_re.shape
    consts = list(_np_split(tabs['f2r'])) + list(_np_split(tabs['f2i']))
    aspec = pl.BlockSpec((None, None, N2, C), lambda b, k: (b, k, 0, 0))
    tspec = pl.BlockSpec((None, N2, 1), lambda b, k: (k, 0, 0))
    fspec = pl.BlockSpec((N2, N2), lambda b, k: (0, 0))
    if kf is None:
        extra, especs, kern, nm = [scale], [_row_spec(C)], _fft_spec_kernel, "fft_filter_spectrum"
    else:
        kspec = pl.BlockSpec((None, None, N2, C), lambda b, k: (0, k, 0, 0))
        extra, especs, kern, nm = list(kf), [kspec, kspec], _fft_conv_kernel, "fft_second_axis_conv"
    return pl.pallas_call(
        kern,
        grid=(B, N1),
        in_specs=[aspec, aspec, tspec, tspec] + [fspec] * 4 + especs,
        out_specs=[aspec, aspec],
        out_shape=[jax.ShapeDtypeStruct((B, N1, N2, C), a_re.dtype)] * 2,
        compiler_params=_cp("parallel", "parallel"),
        name=nm,
    )(a_re, a_im, tabs['twr'], tabs['twi'], *consts, *extra)


def _fft_out_kernel(dr, di, frh, frl, fih, fil, x0, z, bias, o):
    zc = (_dotc(frh[...], frl[...], dr[...].astype(F32), FFT_DATA_PASSES)
          + _dotc(fih[...], fil[...], di[...].astype(F32), FFT_DATA_PASSES))
    zz = z[...]
    o[...] = (x0[...] * (zc + zz * bias[...])).astype(o.dtype)


def _fft_out(d_re, d_im, tabs, x0_2d, z_2d, bias_row):
    B, N1, cols = d_re.shape
    n1u = x0_2d.shape[1]
    consts = list(_np_split(tabs['f1r'][:n1u, :])) + list(_np_split(tabs['f1i'][:n1u, :]))
    tc = _tile(cols, 8192, LANES)
    dspec = pl.BlockSpec((None, N1, tc), lambda b, j: (b, 0, j))
    xspec = pl.BlockSpec((None, n1u, tc), lambda b, j: (b, 0, j))
    return pl.pallas_call(
        _fft_out_kernel,
        grid=(B, cols // tc),
        in_specs=[dspec, dspec] + [pl.BlockSpec((n1u, N1), lambda b, j: (0, 0))] * 4
        + [xspec, xspec, pl.BlockSpec((1, tc), lambda b, j: (0, j))],
        out_specs=xspec,
        out_shape=jax.ShapeDtypeStruct((B, n1u, cols), MIX_OUT),
        compiler_params=_cp("parallel", "parallel"),
        name="fft_last_axis_gate",
    )(d_re, d_im, *consts, x0_2d, z_2d, bias_row)


def _hy_gate_kernel(zc, x0, z, bias, o):
    o[...] = (x0[...] * (zc[...] + z[...] * bias[...])).astype(o.dtype)


def _hyena(p_hy, lp):
    B, n, _ = p_hy.shape
    x0, z = _hyena_pre(p_hy, lp)
    kern, ss = _hyena_filter(n, lp)
    N = 2 * n
    N2 = min(256, n)
    N1 = N // N2
    tabs = _dft_tables(N1, N2)
    scale = lax.rsqrt(ss) * (1.0 / N)
    bias = lp['hy_bias'].reshape(1, BR)
    if N1 >= 4:
        n1u = N1 // 2
        k_re, k_im = _fft_in(kern.reshape(1, N1, N2 * BR), tabs, N1, 3)
        kf = _fft_mid(k_re.reshape(1, N1, N2, BR), k_im.reshape(1, N1, N2, BR), tabs, scale=scale)
        a_re, a_im = _fft_in(z.reshape(B, n1u, N2 * BR), tabs, N1, FFT_DATA_PASSES)
        d_re, d_im = _fft_mid(a_re.reshape(B, N1, N2, BR), a_im.reshape(B, N1, N2, BR), tabs, kf=kf)
        o = _fft_out(d_re.reshape(B, N1, N2 * BR), d_im.reshape(B, N1, N2 * BR), tabs,
                     x0.reshape(B, n1u, N2 * BR), z.reshape(B, n1u, N2 * BR), jnp.tile(bias, (1, N2)))
        return o.reshape(B, n, BR)
    tabs = _dft_tables(1, N)
    kf = _fft_mid(kern.reshape(1, 1, N, BR), jnp.zeros((1, 1, N, BR), F32), tabs, scale=scale)
    zp = jnp.pad(z, ((0, 0), (0, n), (0, 0))).reshape(B, 1, N, BR)
    d_re, _ = _fft_mid(zp, jnp.zeros_like(zp), tabs, kf=kf)
    zc = d_re.reshape(B, N, BR)[:, :n]
    tm = _tile(n, 512)
    spec = pl.BlockSpec((None, tm, BR), lambda b, i: (b, i, 0))
    return pl.pallas_call(
        _hy_gate_kernel,
        grid=(B, n // tm),
        in_specs=[spec, spec, spec, _row_spec(BR)],
        out_specs=spec,
        out_shape=jax.ShapeDtypeStruct((B, n, BR), MIX_OUT),
        compiler_params=_cp("parallel", "parallel"),
        name="hyena_gate",
    )(zc, x0, z, bias)


def _rope_tables(n_lat):
    t = lax.iota(jnp.int32, n_lat)
    row = (t // GRID_W).astype(F32)[:, None]
    col = (t % GRID_W).astype(F32)[:, None]
    half = MLA_ROPE // 2
    inv = ROPE_BASE ** (-jnp.arange(0, half, 2, dtype=F32) / half)
    j = np.arange(HP) % MLA_ROPE
    inv_lane = jnp.take(inv, jnp.asarray(j % (half // 2)))[None]
    ang = jnp.where(jnp.asarray(j < half)[None], row, col) * inv_lane
    return jnp.cos(ang), jnp.sin(ang)


def _rot_index():
    j = np.arange(32)
    first = (j % 16) < 8
    src = np.where(first, j + 8, j - 8)
    sign = np.where(first, -1.0, 1.0)
    return src, sign


def _take_cols(w, idx, sign=None):
    idx = np.asarray(idx)
    sg = np.where(idx >= 0, 1.0 if sign is None else np.asarray(sign, np.float64), 0.0)
    pieces, i, n = [], 0, len(idx)
    while i < n:
        j = i + 1
        if sg[i] == 0.0:
            while j < n and sg[j] == 0.0:
                j += 1
            pieces.append(jnp.zeros((w.shape[0], j - i), w.dtype))
        else:
            while j < n and sg[j] == sg[i] and idx[j] == idx[j - 1] + 1:
                j += 1
            piece = w[:, int(idx[i]):int(idx[i]) + (j - i)]
            pieces.append(piece if sg[i] == 1.0 else piece * sg[i])
        i = j
    return jnp.concatenate(pieces, axis=1)


def _kv_buffers(B, T, tm, n_kv, bufs, total_rows):
    qshape = jax.ShapeDtypeStruct((B, N_HEADS, T, HP), BF16)
    qspec = pl.BlockSpec((None, N_HEADS, tm, HP), lambda b, i: (b, 0, i, 0))
    if bufs is None:
        kvshape = jax.ShapeDtypeStruct((B, N_HEADS, total_rows, HP), BF16)
        return [], [qspec] * (1 + n_kv), [qshape] + [kvshape] * n_kv, []
    off = (bufs[0].shape[2] - T) // tm
    kvspec = pl.BlockSpec((None, N_HEADS, tm, HP), lambda b, i: (b, 0, off + i, 0))
    kvshape = jax.ShapeDtypeStruct(bufs[0].shape, BF16)
    return ([pl.BlockSpec(memory_space=pl.ANY)] * n_kv, [qspec] + [kvspec] * n_kv, [qshape] + [kvshape] * n_kv,
            list(bufs))


def _mla_prep_kernel(p_ref, cq_t, sq_t, ck_t, qn, kvn, wqc, wqs, wk, wv, epe, *rest):
    q_o, k_o, v_o = rest[-3:]
    p = p_ref[...].astype(F32)
    cq = p[:, 0:256]
    cqn = (cq * lax.rsqrt(jnp.sum(cq * cq, axis=-1, keepdims=True) * (1.0 / MLA_Q_RANK) + EPS) * qn[...]).astype(BF16)
    ckv = p[:, 256:384]
    ckvn = _rms(ckv, kvn[...]).astype(BF16)
    cq_all = jnp.concatenate([cq_t[...]] * N_HEADS, axis=1)
    sq_all = jnp.concatenate([sq_t[...]] * N_HEADS, axis=1)
    q = _dot(cqn, wqc[...]) * cq_all + _dot(cqn, wqs[...]) * sq_all
    pe = (p[:, 384:512] * ck_t[...]).astype(BF16)
    k = _dot(ckvn, wk[...]) + _dot(pe, epe[...])
    v = _dot(ckvn, wv[...])
    lane = lax.broadcasted_iota(jnp.int32, (p.shape[0], HP), 1)
    for h in range(N_HEADS):
        q_o[h] = q[:, h * HP:(h + 1) * HP].astype(BF16)
        k_o[h] = k[:, h * HP:(h + 1) * HP].astype(BF16)
        v_o[h] = jnp.where(lane == ONES_LANE, 1.0, v[:, h * HP:(h + 1) * HP]).astype(BF16)


def _mla_weights(lp):
    s = LOG2E / math.sqrt(MLA_NOPE + MLA_ROPE)
    src, sign = _rot_index()
    hd = MLA_NOPE + MLA_ROPE
    qc_idx, qs_idx, qs_sign, k_idx, v_idx = [], [], [], [], []
    for h in range(N_HEADS):
        qc_idx += list(h * hd + np.arange(hd)) + [-1] * (HP - hd)
        qs_idx += [-1] * MLA_NOPE + list(h * hd + MLA_NOPE + src) + [-1] * (HP - hd)
        qs_sign += [0.0] * MLA_NOPE + list(sign) + [0.0] * (HP - hd)
        k_idx += list(h * 128 + np.arange(MLA_NOPE)) + [-1] * (HP - MLA_NOPE)
        v_idx += list(h * 128 + MLA_NOPE + np.arange(64)) + [-1] * (HP - 64)
    wq = jnp.pad(lp['mla_wq_b'] * s, ((0, 256 - MLA_Q_RANK), (0, 0)))
    wqc = _take_cols(wq, qc_idx).astype(BF16)
    wqs = _take_cols(wq, qs_idx, qs_sign).astype(BF16)
    wk = _take_cols(lp['mla_wkv_b'], k_idx).astype(BF16)
    wv = _take_cols(lp['mla_wkv_b'], v_idx).astype(BF16)
    epe = np.zeros((128, N_HEADS * HP), np.float32)
    for h in range(N_HEADS):
        for j in range(MLA_ROPE):
            epe[j, h * HP + MLA_NOPE + j] = 1.0
            epe[MLA_ROPE + j, h * HP + MLA_NOPE + j] = 1.0
    qn = jnp.pad(lp['mla_q_norm'], (0, 256 - MLA_Q_RANK)).reshape(1, 256)
    return qn, lp['mla_kv_norm'].reshape(1, MLA_KV_RANK), wqc, wqs, wk, wv, jnp.asarray(epe, BF16)


def _mla_tables(cos, sin):
    lane = lax.broadcasted_iota(jnp.int32, cos.shape, 1)
    pe = (lane >= MLA_NOPE) & (lane < MLA_NOPE + MLA_ROPE)
    cq = jnp.where(lane < MLA_NOPE, 1.0, jnp.where(pe, cos, 0.0))
    sq = jnp.where(pe, sin, 0.0)
    ck = jnp.where(lane < MLA_ROPE, cos, jnp.where(lane < 2 * MLA_ROPE, sin, 0.0))
    return cq, sq, ck


def _mla_prep(p_mla, tables, weights, total_rows, bufs=None):
    B, T, C = p_mla.shape
    tm = _tile(T, 256)
    W = N_HEADS * HP
    tspecs = [pl.BlockSpec((tm, HP), lambda b, i: (i, 0))] * 3
    wspecs = [pl.BlockSpec(w.shape, lambda b, i: (0, 0)) for w in weights]
    aspecs, ospecs, oshapes, extra = _kv_buffers(B, T, tm, 2, bufs, total_rows)
    n_in = 1 + len(tables) + len(weights)
    return pl.pallas_call(
        _mla_prep_kernel,
        grid=(B, T // tm),
        in_specs=[pl.BlockSpec((None, tm, C), lambda b, i: (b, i, 0))] + tspecs + wspecs + aspecs,
        out_specs=ospecs,
        out_shape=oshapes,
        input_output_aliases={n_in + j: 1 + j for j in range(len(extra))},
        compiler_params=_cp("parallel", "parallel"),
        name="mla_prepare",
    )(p_mla, *tables, *weights, *extra)


def _df_prep_kernel(p_ref, c_t, s_t, *rest):
    q_o, k1_o, k2_o, v_o = rest[-4:]
    p = p_ref[...].astype(F32)
    c = jnp.concatenate([c_t[...]] * 2, axis=1)
    s = jnp.concatenate([s_t[...]] * 2, axis=1)
    q = (p[:, 0:BR] * c + p[:, BR:2 * BR] * s) * (LOG2E / math.sqrt(DF_QK))
    k = p[:, 2 * BR:3 * BR] * c + p[:, 3 * BR:4 * BR] * s
    v = p[:, 4 * BR:5 * BR]
    lane = lax.broadcasted_iota(jnp.int32, (p.shape[0], HP), 1)
    for h in range(N_HEADS):
        pair = slice((h // 2) * HP, (h // 2 + 1) * HP)

        def head(x):
            return x[:, pair] if h % 2 == 0 else pltpu.roll(x[:, pair], HEAD_DIM, 1)

        kh = head(k)
        q_o[h] = jnp.where(lane < 2 * DF_QK, head(q), 0.0).astype(BF16)
        k1_o[h] = jnp.where(lane < DF_QK, kh, 0.0).astype(BF16)
        k2_o[h] = jnp.where((lane >= DF_QK) & (lane < 2 * DF_QK), kh, 0.0).astype(BF16)
        v_o[h] = jnp.where(lane == ONES_LANE, 1.0, jnp.where(lane < DF_V, head(v), 0.0)).astype(BF16)


def _df_tables(cos, sin):
    return cos, sin


def _df_prep(p_df, tables, total_rows, bufs=None):
    B, T, C = p_df.shape
    tm = _tile(T, 256)
    W = N_HEADS * HP
    tspec = pl.BlockSpec((tm, HP), lambda b, i: (i, 0))
    aspecs, ospecs, oshapes, extra = _kv_buffers(B, T, tm, 3, bufs, total_rows)
    return pl.pallas_call(
        _df_prep_kernel,
        grid=(B, T // tm),
        in_specs=[pl.BlockSpec((None, tm, C), lambda b, i: (b, i, 0)), tspec, tspec] + aspecs,
        out_specs=ospecs,
        out_shape=oshapes,
        input_output_aliases={3 + j: 1 + j for j in range(len(extra))},
        compiler_params=_cp("parallel", "parallel"),
        name="diff_prepare",
    )(p_df, *tables, *extra)


def _softmax_pv(q, k_ref, v_ref):
    s = _dot(q, k_ref[...], NT)
    p = jnp.exp2((s - jnp.max(s, axis=-1, keepdims=True)).astype(BF16))
    acc = _dot(p, v_ref[...])
    return acc / acc[:, ONES_LANE:ONES_LANE + 1]


def _mla_attn_kernel(q_ref, k_ref, v_ref, o_ref):
    o = _softmax_pv(q_ref[...], k_ref, v_ref)
    lane = lax.broadcasted_iota(jnp.int32, o.shape, 1)
    o_ref[...] = jnp.where(lane < HEAD_DIM, o, 0.0).astype(o_ref.dtype)


def _attn_specs(q, key_rows):
    tq = _tile(q.shape[2], ATTN_TQ)
    first, n = key_rows
    assert first % n == 0
    qspec = pl.BlockSpec((None, None, tq, HP), lambda b, h, i: (b, h, i, 0))
    kspec = pl.BlockSpec((None, None, n, HP), lambda b, h, i: (b, h, first // n, 0))
    return tq, qspec, kspec


def _mla_attention(q, k, v, key_rows):
    B, H, Tq, _ = q.shape
    tq, qspec, kspec = _attn_specs(q, key_rows)
    return pl.pallas_call(
        _mla_attn_kernel,
        grid=(B, H, Tq // tq),
        in_specs=[qspec, kspec, kspec],
        out_specs=qspec,
        out_shape=jax.ShapeDtypeStruct((B, H, Tq, HP), MIX_OUT),
        compiler_params=_cp("parallel", "parallel", "parallel"),
        name="mla_attention",
    )(q, k, v)


def _df_attn_kernel(lam_init, q_ref, k1_ref, k2_ref, v_ref, lq1, lk1, lq2, lk2, sub, o_ref):
    q = q_ref[...]
    lam = (jnp.exp(jnp.sum(lq1[...] * lk1[...], axis=-1, keepdims=True))
           - jnp.exp(jnp.sum(lq2[...] * lk2[...], axis=-1, keepdims=True)) + lam_init)
    o = _softmax_pv(q, k1_ref, v_ref) - lam * _softmax_pv(q, k2_ref, v_ref)
    lane = lax.broadcasted_iota(jnp.int32, o.shape, 1)
    o = jnp.where(lane < DF_V, o, 0.0)
    ms = jnp.sum(o * o, axis=-1, keepdims=True) * (1.0 / DF_V)
    o_ref[...] = (o * lax.rsqrt(ms + DF_SUBLN_EPS) * sub[...] * (1.0 - lam_init)).astype(o_ref.dtype)


def _df_attention(q, k1, k2, v, key_rows, lp, lam_init):
    B, H, Tq, _ = q.shape
    tq, qspec, kspec = _attn_specs(q, key_rows)
    lspec = pl.BlockSpec((1, DF_QK), lambda b, h, i: (0, 0))
    sub = jnp.pad(lp['df_subln'], (0, HP - DF_V)).reshape(1, HP)
    return pl.pallas_call(
        functools.partial(_df_attn_kernel, lam_init),
        grid=(B, H, Tq // tq),
        in_specs=[qspec, kspec, kspec, kspec, lspec, lspec, lspec, lspec,
                  pl.BlockSpec((1, HP), lambda b, h, i: (0, 0))],
        out_specs=qspec,
        out_shape=jax.ShapeDtypeStruct((B, H, Tq, HP), MIX_OUT),
        compiler_params=_cp("parallel", "parallel", "parallel"),
        name="diff_attention",
    )(q, k1, k2, v, lp['df_lq1'].reshape(1, DF_QK), lp['df_lk1'].reshape(1, DF_QK),
      lp['df_lq2'].reshape(1, DF_QK), lp['df_lk2'].reshape(1, DF_QK), sub)


def _merge_kernel(x_ref, sh_ref, sc_ref, gt_ref, gpre_ref, gpost_ref, orw, ohy, omla, odf,
                  wg, bg, urw, uhy, umla, udf, wo, o_ref):
    rows = x_ref.shape[0] // MERGE_SPLIT
    grp = [pl.ds(i * rows, rows) for i in range(MERGE_SPLIT)]
    xs = [x_ref[g, :] for g in grp]
    hs = [(_rms(x, gpre_ref[...]) * (1.0 + sc_ref[...]) + sh_ref[...]).astype(BF16) for x in xs]
    accs = [None] * MERGE_SPLIT
    for n in range(4):
        for i, g in enumerate(grp):
            if n == 0:
                up = _dot1(orw[g, :], urw[...])
            elif n == 1:
                up = _dot1(ohy[g, :], uhy[...])
            else:
                o_ref_n, u_ref_n = (omla, umla) if n == 2 else (odf, udf)
                pairs = [o_ref_n[hh, g, :] + pltpu.roll(o_ref_n[hh + 1, g, :].astype(F32), HEAD_DIM, 1).astype(BF16)
                         for hh in range(0, N_HEADS, 2)]
                up = _dot(jnp.concatenate(pairs, axis=1), u_ref_n[...])
            t = _sigmoid(_dot(hs[i], wg[n]) + bg[n]) * up
            accs[i] = t if accs[i] is None else accs[i] + t
    ys = [_dot(acc.astype(BF16), wo[...]) for acc in accs]
    for i, g in enumerate(grp):
        o_ref[g, :] = xs[i] + gt_ref[...] * _rms(ys[i], gpost_ref[...])


def _merge(x, mod3, g_pre, g_post, outs, mw):
    B, T, D = x.shape
    o_rw, o_hy, o_mla, o_df = outs
    tm = _tile(T, 512)
    xspec = pl.BlockSpec((None, tm, D), lambda b, i: (b, i, 0))
    bspec = pl.BlockSpec((None, tm, BR), lambda b, i: (b, i, 0))
    hspec = pl.BlockSpec((None, N_HEADS, tm, HP), lambda b, i: (b, 0, i, 0))

    def full(a):
        nd = a.ndim
        return pl.BlockSpec(a.shape, lambda b, i: (0,) * nd)

    ws = [mw['wg'], mw['bg'], mw['urw'], mw['uhy'], mw['umla'], mw['udf'], mw['wo']]
    return pl.pallas_call(
        _merge_kernel,
        grid=(B, T // tm),
        in_specs=[xspec, _mod_spec(mod3[0], 2), _mod_spec(mod3[1], 2), _mod_spec(mod3[2], 2),
                  _row_spec(D), _row_spec(D), bspec, bspec, hspec, hspec] + [full(a) for a in ws],
        out_specs=xspec,
        out_shape=jax.ShapeDtypeStruct((B, T, D), F32),
        compiler_params=_cp("parallel", "parallel"),
        name="gated_merge",
    )(x, mod3[0], mod3[1], mod3[2], g_pre.reshape(1, D), g_post.reshape(1, D), o_rw, o_hy, o_mla, o_df, *ws)


def _mix_weights(w_in):
    src, sign = _rot_index()
    w_rw = w_in[:, :RW_COLS]
    w_hy = w_in[:, OFF_HY:OFF_HY + HY_COLS]
    kpe0 = OFF_MLA + MLA_Q_RANK + MLA_KV_RANK
    mla_idx = (list(OFF_MLA + np.arange(MLA_Q_RANK)) + [-1] * (256 - MLA_Q_RANK)
               + list(OFF_MLA + MLA_Q_RANK + np.arange(MLA_KV_RANK))
               + list(kpe0 + np.arange(MLA_ROPE)) + list(kpe0 + src) + [-1] * 64)
    mla_sign = [1.0] * 256 + [1.0] * MLA_KV_RANK + [1.0] * MLA_ROPE + list(sign) + [0.0] * 64
    w_mla = _take_cols(w_in, mla_idx, mla_sign)
    idx, sg = [], []
    for base, rot in ((OFF_DF, False), (OFF_DF, True), (OFF_DF + 256, False), (OFF_DF + 256, True),
                      (OFF_DF + 512, False)):
        for g in range(BR // DF_QK):
            idx += list(base + g * DF_QK + (src if rot else np.arange(DF_QK)))
            sg += list(sign) if rot else [1.0] * DF_QK
    w_df = _take_cols(w_in, idx, sg)
    return [w.astype(BF16) for w in (w_rw, w_hy, w_mla, w_df)]


def _merge_weights(lp):
    return dict(wg=lp['w_gate'].astype(BF16), bg=lp['b_gate'].reshape(4, 1, D_MODEL),
                urw=lp['w_up'][0].astype(BF16), uhy=lp['w_up'][1].astype(BF16),
                umla=lp['w_up'][2].astype(BF16), udf=lp['w_up'][3].astype(BF16),
                wo=lp['w_o'].astype(BF16))


def _token_mixing(x, xc, mod, modc, g_pre, lam_init, with_ctx_out, lp, rope):
    B, T, _ = x.shape
    ws = _mix_weights(lp['w_in'])
    p_rw, p_hy, p_mla, p_df = _project(x, mod[0], mod[1], g_pre, ws)
    c_rw, c_hy, c_mla, c_df = _project(xc, modc[0], modc[1], g_pre, ws)

    prep = _rwkv_prepare(p_rw, lp)
    prep_c = _rwkv_prepare(c_rw, lp)
    s0 = jnp.zeros((B, 2, N_HEADS, HEAD_DIM, HEAD_DIM), F32)
    yf_c, yb_c, s_c = _rwkv_scan(prep_c, s0)
    yf, yb, _ = _rwkv_scan(prep, s_c)
    o_rw = _rwkv_output(yf, yb, prep, lp)

    o_hy = _hyena(p_hy, lp)

    cos, sin = rope
    Tc = xc.shape[1]
    ones, zeros = jnp.ones((Tc, HP), F32), jnp.zeros((Tc, HP), F32)
    mla_w = _mla_weights(lp)
    q, k, v = _mla_prep(p_mla, _mla_tables(cos, sin), mla_w, T + Tc)
    qc, k, v = _mla_prep(c_mla, _mla_tables(ones, zeros), mla_w, T + Tc, bufs=(k, v))
    o_mla = _mla_attention(q, k, v, (0, T + Tc))

    dq, dk1, dk2, dv = _df_prep(p_df, _df_tables(cos, sin), T + Tc)
    dqc, dk1, dk2, dv = _df_prep(c_df, _df_tables(ones, zeros), T + Tc, bufs=(dk1, dk2, dv))
    o_df = _df_attention(dq, dk1, dk2, dv, (0, T + Tc), lp, lam_init)

    outs = (o_rw, o_hy, o_mla, o_df)
    if not with_ctx_out:
        return outs, None
    oc_rw = _rwkv_output(yf_c, yb_c, prep_c, lp)
    oc_hy = _hyena(c_hy, lp)
    oc_mla = _mla_attention(qc, k, v, (T, Tc))
    oc_df = _df_attention(dqc, dk1, dk2, dv, (T, Tc), lp, lam_init)
    return outs, (oc_rw, oc_hy, oc_mla, oc_df)


def kernel(x, c, ctx, c_ctx, w_mod, b_mod, norm_pre, norm_post, ffn_w_in, ffn_w_out, w_in, rw_mu, rw_w0, rw_w2, rw_a0, rw_a2, rw_g2, rw_kk, rw_ka, rw_rk, rw_ln_w, rw_ln_b, hy_conv_w, hy_conv_b, hy_w1, hy_b1, hy_f1, hy_w2, hy_b2, hy_f2, hy_w3, hy_bias, mla_q_norm, mla_wq_b, mla_kv_norm, mla_wkv_b, df_lq1, df_lk1, df_lq2, df_lk2, df_subln, w_up, w_gate, b_gate, w_o):
    params = dict(w_in=w_in, rw_mu=rw_mu, rw_w0=rw_w0, rw_w2=rw_w2, rw_a0=rw_a0, rw_a2=rw_a2, rw_g2=rw_g2,
                  rw_kk=rw_kk, rw_ka=rw_ka, rw_rk=rw_rk, rw_ln_w=rw_ln_w, rw_ln_b=rw_ln_b, hy_conv_w=hy_conv_w,
                  hy_conv_b=hy_conv_b, hy_w1=hy_w1, hy_b1=hy_b1, hy_f1=hy_f1, hy_w2=hy_w2, hy_b2=hy_b2,
                  hy_f2=hy_f2, hy_w3=hy_w3, hy_bias=hy_bias, mla_q_norm=mla_q_norm, mla_wq_b=mla_wq_b,
                  mla_kv_norm=mla_kv_norm, mla_wkv_b=mla_wkv_b, df_lq1=df_lq1, df_lk1=df_lk1, df_lq2=df_lq2,
                  df_lk2=df_lk2, df_subln=df_subln, w_up=w_up, w_gate=w_gate, b_gate=b_gate, w_o=w_o)
    B, T, D = x.shape
    depth = w_mod.shape[0]
    assert B <= 7 and D == D_MODEL and T % 128 == 0 and ctx.shape[1] % CHUNK == 0 and T % ctx.shape[1] == 0
    rope = _rope_tables(T)

    s_rows = jnp.concatenate([c, c_ctx[None], jnp.zeros((8 - B - 1, D), F32)], 0)
    mod_all = _modulation(s_rows, w_mod, b_mod).reshape(depth, 8, 3, 3, D)
    ffn_in = ffn_w_in.astype(BF16)
    ffn_out = ffn_w_out.astype(BF16)

    xc = ctx
    for l in range(depth):
        last = l == depth - 1
        lp = {k_: v_[l] for k_, v_ in params.items()}
        mod = [[mod_all[l, :B, s, m][:, None, :] for m in range(3)] for s in range(3)]
        modc = [[mod_all[l, B:B + 1, s, m][:, None, :] for m in range(3)] for s in range(3)]
        ffn_a = (norm_pre[l, 0], norm_post[l, 0], ffn_in[l, 0], ffn_out[l, 0])
        ffn_b = (norm_pre[l, 2], norm_post[l, 2], ffn_in[l, 1], ffn_out[l, 1])

        x = _ffn(x, mod[0], *ffn_a)
        xc = _ffn(xc, modc[0], *ffn_a)

        lam_init = 0.8 - 0.6 * math.exp(-0.3 * l)
        outs, outs_c = _token_mixing(x, xc, mod[1], modc[1], norm_pre[l, 1], lam_init, not last, lp, rope)
        mw = _merge_weights(lp)
        x = _merge(x, mod[1], norm_pre[l, 1], norm_post[l, 1], outs, mw)
        x = _ffn(x, mod[2], *ffn_b)
        if not last:
            xc = _merge(xc, modc[1], norm_pre[l, 1], norm_post[l, 1], outs_c, mw)
            xc = _ffn(xc, modc[2], *ffn_b)
    return x
```

```python
import functools
import math

import numpy as np
import jax
import jax.numpy as jnp
from jax import lax
from jax.experimental import pallas as pl
from jax.experimental.pallas import tpu as pltpu

F32 = jnp.float32
BF16 = jnp.bfloat16
MIX_OUT = BF16

D_MODEL = 1024
GRID_W = 64
N_HEADS = 4
HEAD_DIM = 64
BR = N_HEADS * HEAD_DIM
D_FF = 2816
MACARON_W = 0.5
EPS = 1e-6
ROPE_BASE = 10000.0
RW_LN_EPS = 64e-5
RW_COLS = 1152
HY_COLS = 768
HY_EMB = 33
HY_FAST_DECAY = 0.3
HY_SLOW_DECAY = 1.5
HY_TARGET = 1e-2
MLA_Q_RANK = 192
MLA_KV_RANK = 128
MLA_NOPE = 64
MLA_ROPE = 32
DF_QK = 32
DF_V = 64
DF_SUBLN_EPS = 1e-5
OFF_HY = RW_COLS
OFF_MLA = RW_COLS + HY_COLS
OFF_DF = OFF_MLA + MLA_Q_RANK + MLA_KV_RANK + MLA_ROPE

LANES = 128
HP = 128
ONES_LANE = 64
LOG2E = 1.4426950408889634
FFN_TM = 256
FFN_TF = 2816
ATTN_TQ = 256
MERGE_SPLIT = 2
FFT_DATA_PASSES = 1
CHUNK = 64
RW_P_GRAM = 1
RW_P_INV = 3
RW_P_STATE = 3
RW_P_APPLY = 1
VMEM_LIMIT = 56 * 1024 * 1024


def _cp(*sem):
    return pltpu.CompilerParams(dimension_semantics=sem, vmem_limit_bytes=VMEM_LIMIT)


def _tile(n, pref, mult=8):
    if n <= pref:
        return n
    t = (pref // mult) * mult
    while t > mult and n % t:
        t -= mult
    assert n % t == 0, (n, pref)
    return t


def _dot(a, b, dims=None):
    if dims is None:
        dims = (((a.ndim - 1,), (0,)), ((), ()))
    return lax.dot_general(a, b, dims, preferred_element_type=F32)


NT = (((1,), (1,)), ((), ()))
TN = (((0,), (0,)), ((), ()))


def _split(x):
    hi = x.astype(BF16)
    lo = (x - hi.astype(F32)).astype(BF16)
    return hi, lo


def _dot3(a, b, dims=None):
    ah, al = _split(a)
    bh, bl = _split(b)
    return _dot(ah, bh, dims) + (_dot(ah, bl, dims) + _dot(al, bh, dims))


def _dot1(a, b, dims=None):
    return _dot(a.astype(BF16), b.astype(BF16), dims)


def _dotc(ch, cl, x, passes=3):
    if passes == 1:
        return _dot(ch, x.astype(BF16))
    xh, xl = _split(x)
    return _dot(ch, xh) + (_dot(ch, xl) + _dot(cl, xh))


def _dot_exact_lhs(m, x):
    x1 = x.astype(BF16)
    r1 = x - x1.astype(F32)
    x2 = r1.astype(BF16)
    x3 = (r1 - x2.astype(F32)).astype(BF16)
    return _dot(m, x1) + (_dot(m, x2) + _dot(m, x3))


def _rms(x, g, eps=EPS):
    return x * lax.rsqrt(jnp.mean(x * x, axis=-1, keepdims=True) + eps) * g


def _sigmoid(x):
    return 1.0 / (1.0 + jnp.exp(-x))


def _silu(x):
    return x * _sigmoid(x)


def _softplus(z):
    return jnp.maximum(z, 0.0) + jnp.log(1.0 + jnp.exp(-jnp.abs(z)))


def _np_split(a):
    a = jnp.asarray(a, F32)
    hi = a.astype(BF16)
    lo = (a - hi.astype(F32)).astype(BF16)
    return hi, lo


def _mod_kernel(s_ref, w_ref, b_ref, o_ref):
    s = _silu(s_ref[...])
    o_ref[...] = _dot3(s, w_ref[...]) + b_ref[...]


def _modulation(s_rows, w_mod, b_mod):
    L, D, NM = w_mod.shape
    tn = _tile(NM, 1152, LANES)
    return pl.pallas_call(
        _mod_kernel,
        grid=(L, NM // tn),
        in_specs=[pl.BlockSpec((8, D), lambda l, j: (0, 0)),
                  pl.BlockSpec((None, D, tn), lambda l, j: (l, 0, j)),
                  pl.BlockSpec((None, 1, tn), lambda l, j: (l, 0, j))],
        out_specs=pl.BlockSpec((None, 8, tn), lambda l, j: (l, 0, j)),
        out_shape=jax.ShapeDtypeStruct((L, 8, NM), F32),
        compiler_params=_cp("parallel", "parallel"),
        name="modulation",
    )(s_rows, w_mod, b_mod.reshape(L, 1, NM))


def _mod_spec(m, nargs):
    if m.shape[0] == 1:
        return pl.BlockSpec((None, 1, D_MODEL), lambda *a: (0, 0, 0))
    return pl.BlockSpec((None, 1, D_MODEL), lambda *a: (a[0], 0, 0))


def _row_spec(n):
    return pl.BlockSpec((1, n), lambda *a: (0, 0))


def _ffn_kernel(x_ref, sh_ref, sc_ref, gt_ref, gpre_ref, gpost_ref, wa_ref, wb_ref, wo_ref, o_ref, h_sc, acc_sc):
    f = pl.program_id(2)

    @pl.when(f == 0)
    def _():
        h = _rms(x_ref[...], gpre_ref[...]) * (1.0 + sc_ref[...]) + sh_ref[...]
        h_sc[...] = h.astype(BF16)
        acc_sc[...] = jnp.zeros_like(acc_sc)

    h = h_sc[...]
    a = _dot(h, wa_ref[...])
    b = _dot(h, wb_ref[...])
    acc_sc[...] += _dot((_silu(a) * b).astype(BF16), wo_ref[...])

    @pl.when(f == pl.num_programs(2) - 1)
    def _():
        o_ref[...] = x_ref[...] + MACARON_W * gt_ref[...] * _rms(acc_sc[...], gpost_ref[...])


def _ffn(x, mod3, g_pre, g_post, w_in, w_out):
    B, T, D = x.shape
    F = w_out.shape[0]
    tm = _tile(T, FFN_TM)
    tf = _tile(F, FFN_TF, LANES)
    nf = F // tf
    xspec = pl.BlockSpec((None, tm, D), lambda b, i, f: (b, i, 0))
    return pl.pallas_call(
        _ffn_kernel,
        grid=(B, T // tm, nf),
        in_specs=[xspec, _mod_spec(mod3[0], 3), _mod_spec(mod3[1], 3), _mod_spec(mod3[2], 3),
                  _row_spec(D), _row_spec(D),
                  pl.BlockSpec((D, tf), lambda b, i, f: (0, f)),
                  pl.BlockSpec((D, tf), lambda b, i, f: (0, nf + f)),
                  pl.BlockSpec((tf, D), lambda b, i, f: (f, 0))],
        out_specs=xspec,
        out_shape=jax.ShapeDtypeStruct((B, T, D), F32),
        scratch_shapes=[pltpu.VMEM((tm, D), BF16), pltpu.VMEM((tm, D), F32)],
        compiler_params=_cp("parallel", "parallel", "arbitrary"),
        name="ffn",
    )(x, mod3[0], mod3[1], mod3[2], g_pre.reshape(1, D), g_post.reshape(1, D), w_in, w_in, w_out)


def _proj_kernel(x_ref, sh_ref, sc_ref, gpre_ref, wrw, why, wmla, wdf, orw, ohy, omla, odf):
    h = (_rms(x_ref[...], gpre_ref[...]) * (1.0 + sc_ref[...]) + sh_ref[...]).astype(BF16)
    orw[...] = _dot(h, wrw[...])
    ohy[...] = _dot(h, why[...])
    omla[...] = _dot(h, wmla[...]).astype(omla.dtype)
    odf[...] = _dot(h, wdf[...]).astype(odf.dtype)


def _project(x, shift, scale, g_pre, ws):
    B, T, D = x.shape
    tm = _tile(T, 256)
    xspec = pl.BlockSpec((None, tm, D), lambda b, i: (b, i, 0))
    wspecs = [pl.BlockSpec(w.shape, lambda b, i: (0, 0)) for w in ws]
    ospecs = [pl.BlockSpec((None, tm, w.shape[1]), lambda b, i: (b, i, 0)) for w in ws]
    oshapes = [jax.ShapeDtypeStruct((B, T, w.shape[1]), dt) for w, dt in zip(ws, (F32, F32, BF16, BF16))]
    return pl.pallas_call(
        _proj_kernel,
        grid=(B, T // tm),
        in_specs=[xspec, _mod_spec(shift, 2), _mod_spec(scale, 2), _row_spec(D)] + wspecs,
        out_specs=ospecs,
        out_shape=oshapes,
        compiler_params=_cp("parallel", "parallel"),
        name="mix_project",
    )(x, shift, scale, g_pre.reshape(1, D), *ws)


def _halo_specs(tm, T, C):
    nb8 = T // 8
    r = tm // 8
    main = pl.BlockSpec((None, tm, C), lambda b, i: (b, i, 0))
    prev = pl.BlockSpec((None, 8, C), lambda b, i: (b, jnp.maximum(i * r - 1, 0), 0))
    nxt = pl.BlockSpec((None, 8, C), lambda b, i: (b, jnp.minimum((i + 1) * r, nb8 - 1), 0))
    return main, prev, nxt


def _conv3_tile(p, prev8, next8, w0, w1, w2):
    i = pl.program_id(1)
    n_i = pl.num_programs(1)
    tm = p.shape[0]
    row = lax.broadcasted_iota(jnp.int32, p.shape, 0)
    prev_row = jnp.where(i > 0, prev8[7:8, :], 0.0)
    next_row = jnp.where(i < n_i - 1, next8[0:1, :], 0.0)
    pm = jnp.where(row == 0, prev_row, pltpu.roll(p, 1, 0))
    pp = jnp.where(row == tm - 1, next_row, pltpu.roll(p, tm - 1, 0))
    return pm * w0 + p * w1 + pp * w2


def _rwkv_prep_kernel(p_ref, pv_ref, nx_ref, mu_ref, w0_ref, w2_ref, a0_ref, a2_ref, g2_ref, kk_ref, ka_ref,
                      rk_ref, bd_ref, trif_ref, trib_ref, onec_ref, v_o, g_o, bonus_o, *dir_outs):
    mu = mu_ref[...]
    p = _conv3_tile(p_ref[...], pv_ref[...], nx_ref[...], 0.5 * mu, 1.0 - mu, 0.5 * mu)
    r = p[:, 0:BR]
    k = p[:, BR:2 * BR]
    v = p[:, 2 * BR:3 * BR]
    g_a = p[:, 3 * BR:3 * BR + 128]
    wa = p[:, 896:1024]
    aa = p[:, 1024:1152]
    bd = bd_ref[...]
    kkraw = k * kk_ref[...]
    ss = _dot_exact_lhs_rhs(kkraw * kkraw, bd)
    kk = kkraw / jnp.maximum(jnp.sqrt(ss), 1e-12)
    wl = w0_ref[...] + _dot3(jnp.tanh(wa), w2_ref[...])
    lw = -jnp.exp(-_softplus(-wl) - 0.5)
    a = _sigmoid(a0_ref[...] + _dot3(aa, a2_ref[...]))
    a_f = a[:, :BR]
    a_b = a[:, BR:]
    ka = ka_ref[...]
    k_f = k * (1.0 + (a_f - 1.0) * ka)
    k_b = k * (1.0 + (a_b - 1.0) * ka)
    v_o[...] = v
    g_o[...] = _dot1(_sigmoid(g_a), g2_ref[...])
    bonus_o[...] = _dot_exact_lhs_rhs(r * (k_f + k_b) * rk_ref[...], bd) * v
    onec = onec_ref[...]
    for d, (tri_ref, k_d, a_d) in enumerate(((trif_ref, k_f, a_f), (trib_ref, k_b, a_b))):
        rt_o, at_o, bt_o, kt_o, bh_o, kh_o, eg_o = dir_outs[7 * d:7 * d + 7]
        lw_d = lw[:, d * BR:(d + 1) * BR]
        b_d = kk * a_d
        g = _dot_exact_lhs(tri_ref[...], lw_d)
        gtot = _dot_exact_lhs(onec, lw_d)
        eng = jnp.exp(-g)
        et = jnp.exp(gtot - g)
        rt_o[...] = r * jnp.exp(g)
        at_o[...] = -kk * jnp.exp(g - lw_d)
        bt_o[...] = (b_d * eng).astype(bt_o.dtype)
        kt_o[...] = (k_d * eng).astype(kt_o.dtype)
        bh_o[...] = b_d * et
        kh_o[...] = k_d * et
        eg = jnp.exp(gtot)
        for j in range(eg_o.shape[0]):
            eg_o[j] = eg[j * CHUNK:j * CHUNK + 1, :]


def _dot_exact_lhs_rhs(x, m):
    x1 = x.astype(BF16)
    r1 = x - x1.astype(F32)
    x2 = r1.astype(BF16)
    x3 = (r1 - x2.astype(F32)).astype(BF16)
    return _dot(x1, m) + (_dot(x2, m) + _dot(x3, m))


def _block_diag_ones():
    h = np.arange(BR) // HEAD_DIM
    return jnp.asarray((h[:, None] == h[None, :]).astype(np.float32), BF16)


def _rwkv_prepare(p_rw, lp):
    B, T, C = p_rw.shape
    tm = _tile(T, 256)
    main, prev, nxt = _halo_specs(tm, T, C)
    z = jnp.zeros((64, BR), F32)
    w2 = jnp.concatenate([jnp.concatenate([lp['rw_w2'][0], z], 1), jnp.concatenate([z, lp['rw_w2'][1]], 1)], 0)
    a2 = jnp.concatenate([jnp.concatenate([lp['rw_a2'][0], z], 1), jnp.concatenate([z, lp['rw_a2'][1]], 1)], 0)
    params = [lp['rw_mu'].reshape(1, C), lp['rw_w0'].reshape(1, 2 * BR), w2, lp['rw_a0'].reshape(1, 2 * BR), a2,
              lp['rw_g2'].astype(BF16), lp['rw_kk'].reshape(1, BR), lp['rw_ka'].reshape(1, BR),
              lp['rw_rk'].reshape(1, BR), _block_diag_ones()]
    pos = np.arange(tm)
    same = (pos[:, None] // CHUNK) == (pos[None, :] // CHUNK)
    params += [jnp.asarray((same & (pos[None, :] <= pos[:, None])).astype(np.float32), BF16),
               jnp.asarray((same & (pos[None, :] >= pos[:, None])).astype(np.float32), BF16),
               jnp.asarray(same.astype(np.float32), BF16)]
    pspecs = [pl.BlockSpec(a.shape, lambda b, i: (0, 0)) for a in params]
    ospec = pl.BlockSpec((None, tm, BR), lambda b, i: (b, i, 0))
    names = ('v', 'g', 'bonus') + tuple(n + str(d) for d in (0, 1) for n in ('rt', 'at', 'bt', 'kt', 'bh', 'kh', 'eg'))
    gram_dt = BF16 if RW_P_GRAM == 1 else F32
    egspec = pl.BlockSpec((None, tm // CHUNK, 1, BR), lambda b, i: (b, i, 0, 0))
    ospecs = [egspec if n[:2] == 'eg' else ospec for n in names]
    oshapes = [jax.ShapeDtypeStruct((B, T // CHUNK, 1, BR), F32) if n[:2] == 'eg' else
               jax.ShapeDtypeStruct((B, T, BR), gram_dt if n[:2] in ('bt', 'kt') else F32) for n in names]
    outs = pl.pallas_call(
        _rwkv_prep_kernel,
        grid=(B, T // tm),
        in_specs=[main, prev, nxt] + pspecs,
        out_specs=ospecs,
        out_shape=oshapes,
        compiler_params=_cp("parallel", "parallel"),
        name="rwkv_prepare",
    )(p_rw, p_rw, p_rw, *params)
    return dict(zip(names, outs))


def _dot3s(a, b, dims=None):
    if dims is None:
        dims = (((1,), (0,)), ((), ()))
    ca, cb = dims[0][0][0], dims[0][1][0]
    a_lo = a - a.astype(BF16).astype(F32)
    b_lo = b - b.astype(BF16).astype(F32)
    sa = jnp.concatenate([a, a_lo, a], axis=ca).astype(BF16)
    sb = jnp.concatenate([b, b, b_lo], axis=cb).astype(BF16)
    return _dot(sa, sb, dims)


def _mmp(a, b, dims, passes):
    return _dot3s(a, b, dims) if passes == 3 else _dot1(a, b, dims)


def _rwkv_chunks(chains, eye):
    n = range(len(chains))
    rt, at, bt, kt, bh, kh, eg, v, ht, mask_s, mask_i = zip(*chains)
    C = rt[0].shape[0]
    ar = [jnp.concatenate([at[i], rt[i]], 0) for i in n]
    gm = [_mmp(ar[i], jnp.concatenate([bt[i], kt[i]], 0), NT, RW_P_GRAM) for i in n]
    top = [jnp.where(mask_s[i], gm[i][:C], 0.0) for i in n]
    bot = [jnp.where(mask_i[i], gm[i][C:], 0.0) for i in n]
    lab = [top[i][:, :C] for i in n]
    pw = [_mmp(lab[i], lab[i], None, RW_P_INV) for i in n]
    tinv = [eye + lab[i] for i in n]
    for _ in range(int(math.log2(C)) - 2):
        both = [_mmp(pw[i], jnp.concatenate([pw[i], tinv[i]], 1), None, RW_P_INV) for i in n]
        pw = [both[i][:, :C] for i in n]
        tinv = [tinv[i] + both[i][:, C:] for i in n]
    last = [_mmp(pw[i], tinv[i], None, RW_P_INV) for i in n]
    tinv = [tinv[i] + last[i] for i in n]
    wy = [_mmp(ar[i], ht[i], NT, RW_P_STATE) for i in n]
    w1 = [_mmp(top[i][:, C:], v[i], None, RW_P_APPLY) for i in n]
    u = [_mmp(tinv[i], wy[i][:C] + w1[i], None, RW_P_STATE) for i in n]
    uv = [jnp.concatenate([u[i], v[i]], 0) for i in n]
    y1 = [_mmp(bot[i], uv[i], None, RW_P_APPLY) for i in n]
    hu = [_mmp(uv[i], jnp.concatenate([bh[i], kh[i]], 0), TN, RW_P_STATE) for i in n]
    return [(wy[i][C:] + y1[i], ht[i] * eg[i] + hu[i]) for i in n]


def _rwkv_scan_kernel(*refs):
    fwd_in, bwd_in = refs[0:8], refs[8:16]
    s0_ref, yf_o, yb_o, s_o, st = refs[16:]
    c = pl.program_id(0)
    nb, C = yf_o.shape[0], yf_o.shape[1]

    @pl.when(c == 0)
    def _():
        st[...] = s0_ref[...]

    row = lax.broadcasted_iota(jnp.int32, (C, 2 * C), 0)
    col = lax.broadcasted_iota(jnp.int32, (C, 2 * C), 1)
    col = jnp.where(col >= C, col - C, col)
    eye = (lax.broadcasted_iota(jnp.int32, (C, C), 0) == lax.broadcasted_iota(jnp.int32, (C, C), 1)).astype(F32)
    chains = []
    for b in range(nb):
        for d, ins in enumerate((fwd_in, bwd_in)):
            if d == 0:
                mask_i = col <= row
                mask_s = col < row
            else:
                mask_i = col >= row
                mask_s = col > row
            for h in range(N_HEADS):
                sl = pl.ds(h * HEAD_DIM, HEAD_DIM)
                rt, at, bt, kt, bh, kh, eg, v = (ref[b, :, sl] for ref in ins)
                chains.append((rt, at, bt, kt, bh, kh, eg, v, st[b, d, h], mask_s, mask_i))
    results = iter(_rwkv_chunks(chains, eye))
    for b in range(nb):
        for d, y_o in enumerate((yf_o, yb_o)):
            for h in range(N_HEADS):
                y, hn = next(results)
                y_o[b, :, pl.ds(h * HEAD_DIM, HEAD_DIM)] = y
                st[b, d, h] = hn

    @pl.when(c == pl.num_programs(0) - 1)
    def _():
        s_o[...] = st[...]


def _rwkv_scan(prep, s0):
    B, T, _ = prep['v'].shape
    nc = T // CHUNK
    fwd = pl.BlockSpec((B, CHUNK, BR), lambda c: (0, c, 0))
    bwd = pl.BlockSpec((B, CHUNK, BR), lambda c: (0, nc - 1 - c, 0))
    sspec = pl.BlockSpec(s0.shape, lambda c: (0, 0, 0, 0, 0))
    ins = [prep[n + str(d)] for d in (0, 1) for n in ('rt', 'at', 'bt', 'kt', 'bh', 'kh', 'eg')]
    return pl.pallas_call(
        _rwkv_scan_kernel,
        grid=(nc,),
        in_specs=[fwd] * 6 + [pl.BlockSpec((B, None, 1, BR), lambda c: (0, c, 0, 0)), fwd]
        + [bwd] * 6 + [pl.BlockSpec((B, None, 1, BR), lambda c: (0, nc - 1 - c, 0, 0)), bwd] + [sspec],
        out_specs=[fwd, bwd, sspec],
        out_shape=[jax.ShapeDtypeStruct((B, T, BR), F32), jax.ShapeDtypeStruct((B, T, BR), F32),
                   jax.ShapeDtypeStruct(s0.shape, F32)],
        scratch_shapes=[pltpu.VMEM(s0.shape, F32)],
        compiler_params=_cp("arbitrary"),
        name="rwkv_scan",
    )(*ins[:7], prep['v'], *ins[7:], prep['v'], s0)


def _rwkv_out_kernel(yf, yb, g, bonus, lnw, lnb, bd, o):
    y = yf[...] + yb[...]
    m = bd[...]
    inv_n = 1.0 / HEAD_DIM
    mean = _dot_exact_lhs_rhs(y, m) * inv_n
    yc = y - mean
    var = _dot_exact_lhs_rhs(yc * yc, m) * inv_n
    yn = yc * lax.rsqrt(var + RW_LN_EPS) * lnw[...] + lnb[...]
    o[...] = ((yn + bonus[...]) * g[...]).astype(o.dtype)


def _rwkv_output(yf, yb, prep, lp):
    B, T, _ = yf.shape
    tm = _tile(T, 512)
    spec = pl.BlockSpec((None, tm, BR), lambda b, i: (b, i, 0))
    return pl.pallas_call(
        _rwkv_out_kernel,
        grid=(B, T // tm),
        in_specs=[spec] * 4 + [_row_spec(BR), _row_spec(BR), pl.BlockSpec((BR, BR), lambda b, i: (0, 0))],
        out_specs=spec,
        out_shape=jax.ShapeDtypeStruct((B, T, BR), MIX_OUT),
        compiler_params=_cp("parallel", "parallel"),
        name="rwkv_output",
    )(yf, yb, prep['g'], prep['bonus'], lp['rw_ln_w'].reshape(1, BR), lp['rw_ln_b'].reshape(1, BR),
      _block_diag_ones())


def _hy_pre_kernel(p_ref, pv_ref, nx_ref, w_ref, b_ref, x0_o, z_o):
    w = w_ref[...]
    u = _conv3_tile(p_ref[...], pv_ref[...], nx_ref[...], w[0:1, :], w[1:2, :], w[2:3, :]) + b_ref[...]
    x0_o[...] = u[:, 0:BR]
    z_o[...] = u[:, 2 * BR:3 * BR] * u[:, BR:2 * BR]


def _hyena_pre(p_hy, lp):
    B, T, C = p_hy.shape
    tm = _tile(T, 512)
    main, prev, nxt = _halo_specs(tm, T, C)
    ospec = pl.BlockSpec((None, tm, BR), lambda b, i: (b, i, 0))
    return pl.pallas_call(
        _hy_pre_kernel,
        grid=(B, T // tm),
        in_specs=[main, prev, nxt, pl.BlockSpec((3, C), lambda b, i: (0, 0)), _row_spec(C)],
        out_specs=[ospec, ospec],
        out_shape=[jax.ShapeDtypeStruct((B, T, BR), F32)] * 2,
        compiler_params=_cp("parallel", "parallel"),
        name="hyena_pre",
    )(p_hy, p_hy, p_hy, lp['hy_conv_w'], lp['hy_conv_b'].reshape(1, C))


def _hy_filter_kernel(z_ref, t_ref, mf_ref, mb_ref, w1, b1, f1, w2, b2, f2, w3, dl, k_o, ss_o):
    h = jnp.sin(f1[...] * (_dot3(z_ref[...], w1[...]) + b1[...]))
    h = jnp.sin(f2[...] * (_dot3(h, w2[...]) + b2[...]))
    h = _dot1(h, w3[...]) * jnp.exp(-t_ref[...] * dl[...])
    kern = mf_ref[...] * h[:, :BR] + mb_ref[...] * h[:, BR:]
    k_o[...] = kern

    @pl.when(pl.program_id(0) == 0)
    def _():
        ss_o[...] = jnp.zeros_like(ss_o)

    ss_o[...] += jnp.sum(kern * kern, axis=0, keepdims=True)


def _hyena_filter(n, lp):
    t = jnp.linspace(0.0, 1.0, n, dtype=F32)
    bands = (HY_EMB - 1) // 2
    wpos = 2.0 * math.pi * jnp.arange(n, dtype=F32) / n
    fr = jnp.linspace(1e-4, bands - 1, bands, dtype=F32)
    t_ext = jnp.concatenate([t, t[::-1]])[:, None]
    fr_lane = jnp.concatenate([jnp.zeros((1,), F32), fr, fr, jnp.zeros((LANES - HY_EMB,), F32)])[None]
    ang = jnp.concatenate([wpos, wpos[::-1]])[:, None] * fr_lane
    lane = lax.broadcasted_iota(jnp.int32, ang.shape, 1)
    z_ext = jnp.where(lane == 0, t_ext, jnp.where(lane <= bands, jnp.cos(ang),
                                                  jnp.where(lane < HY_EMB, -jnp.sin(ang), 0.0)))
    rho = np.arange(2 * n)
    mf = jnp.asarray((rho < n).astype(np.float32))[:, None]
    mb = jnp.asarray((rho > n).astype(np.float32))[:, None]
    max_decay = math.log(HY_TARGET) / HY_FAST_DECAY
    min_decay = math.log(HY_TARGET) / HY_SLOW_DECAY
    deltas = jnp.abs(jnp.linspace(min_decay, max_decay, BR, dtype=F32))
    dl = jnp.tile(deltas, 2)[None]
    w1 = jnp.pad(lp['hy_w1'], ((0, LANES - HY_EMB), (0, 0)))
    params = [w1, lp['hy_b1'][None], lp['hy_f1'][None], lp['hy_w2'], lp['hy_b2'][None], lp['hy_f2'][None],
              lp['hy_w3'], dl]
    tm = _tile(2 * n, 512)
    return pl.pallas_call(
        _hy_filter_kernel,
        grid=(2 * n // tm,),
        in_specs=[pl.BlockSpec((tm, LANES), lambda i: (i, 0))] + [pl.BlockSpec((tm, 1), lambda i: (i, 0))] * 3
        + [pl.BlockSpec(a.shape, lambda i: (0, 0)) for a in params],
        out_specs=[pl.BlockSpec((tm, BR), lambda i: (i, 0)), pl.BlockSpec((1, BR), lambda i: (0, 0))],
        out_shape=[jax.ShapeDtypeStruct((2 * n, BR), F32), jax.ShapeDtypeStruct((1, BR), F32)],
        compiler_params=_cp("arbitrary"),
        name="hyena_filter",
    )(z_ext, t_ext, mf, mb, *params)


def _dft_tables(N1, N2):
    N = N1 * N2

    def mat(n):
        a = -2.0 * np.pi * np.outer(np.arange(n), np.arange(n)) / n
        return np.cos(a), np.sin(a)

    f1r, f1i = mat(N1)
    f2r, f2i = mat(N2)
    a = -2.0 * np.pi * np.outer(np.arange(N1), np.arange(N2)) / N
    return dict(f1r=f1r, f1i=f1i, f2r=f2r, f2i=f2i,
                twr=jnp.asarray(np.cos(a)[:, :, None], F32), twi=jnp.asarray(np.sin(a)[:, :, None], F32))


def _fft_in_kernel(passes, x_ref, frh, frl, fih, fil, ar_o, ai_o):
    x = x_ref[...]
    ar_o[...] = _dotc(frh[...], frl[...], x, passes).astype(ar_o.dtype)
    ai_o[...] = _dotc(fih[...], fil[...], x, passes).astype(ai_o.dtype)


def _fft_in(x2d, tabs, N1, passes):
    B, n1u, cols = x2d.shape
    consts = list(_np_split(tabs['f1r'][:, :n1u])) + list(_np_split(tabs['f1i'][:, :n1u]))
    tc = _tile(cols, 8192, LANES)
    ospec = pl.BlockSpec((None, N1, tc), lambda b, j: (b, 0, j))
    return pl.pallas_call(
        functools.partial(_fft_in_kernel, passes),
        grid=(B, cols // tc),
        in_specs=[pl.BlockSpec((None, n1u, tc), lambda b, j: (b, 0, j))]
        + [pl.BlockSpec((N1, n1u), lambda b, j: (0, 0))] * 4,
        out_specs=[ospec, ospec],
        out_shape=[jax.ShapeDtypeStruct((B, N1, cols), F32 if passes == 3 else BF16)] * 2,
        compiler_params=_cp("parallel", "parallel"),
        name="fft_first_axis",
    )(x2d, *consts)


def _cplx_dft(frh, frl, fih, fil, xr, xi, conj, passes):
    rr = _dotc(frh, frl, xr, passes)
    ii = _dotc(fih, fil, xi, passes)
    ri = _dotc(frh, frl, xi, passes)
    ir = _dotc(fih, fil, xr, passes)
    if conj:
        return rr + ii, ri - ir
    return rr - ii, ri + ir


def _fft_spec_kernel(ar, ai, twr, twi, frh, frl, fih, fil, sc, kr_o, ki_o):
    tr, ti = twr[...], twi[...]
    xr = ar[...] * tr - ai[...] * ti
    xi = ar[...] * ti + ai[...] * tr
    br, bi = _cplx_dft(frh[...], frl[...], fih[...], fil[...], xr, xi, False, 3)
    kr_o[...] = br * sc[...]
    ki_o[...] = bi * sc[...]


def _fft_conv_kernel(ar, ai, twr, twi, frh, frl, fih, fil, kr, ki, dr_o, di_o):
    tr, ti = twr[...], twi[...]
    f = (frh[...], frl[...], fih[...], fil[...])
    a_r, a_i = ar[...].astype(F32), ai[...].astype(F32)
    xr = a_r * tr - a_i * ti
    xi = a_r * ti + a_i * tr
    br, bi = _cplx_dft(*f, xr, xi, False, FFT_DATA_PASSES)
    cr = br * kr[...] - bi * ki[...]
    ci = br * ki[...] + bi * kr[...]
    dr, di = _cplx_dft(*f, cr, ci, True, FFT_DATA_PASSES)
    dr_o[...] = (dr * tr + di * ti).astype(dr_o.dtype)
    di_o[...] = (di * tr - dr * ti).astype(di_o.dtype)


def _fft_mid(a_re, a_im, tabs, kf=None, scale=None):
    B, N1, N2, C = a_re.shape
    consts = list(_np_split(tabs['f2r'])) + list(_np_split(tabs['f2i']))
    aspec = pl.BlockSpec((None, None, N2, C), lambda b, k: (b, k, 0, 0))
    tspec = pl.BlockSpec((None, N2, 1), lambda b, k: (k, 0, 0))
    fspec = pl.BlockSpec((N2, N2), lambda b, k: (0, 0))
    if kf is None:
        extra, especs, kern, nm = [scale], [_row_spec(C)], _fft_spec_kernel, "fft_filter_spectrum"
    else:
        kspec = pl.BlockSpec((None, None, N2, C), lambda b, k: (0, k, 0, 0))
        extra, especs, kern, nm = list(kf), [kspec, kspec], _fft_conv_kernel, "fft_second_axis_conv"
    return pl.pallas_call(
        kern,
        grid=(B, N1),
        in_specs=[aspec, aspec, tspec, tspec] + [fspec] * 4 + especs,
        out_specs=[aspec, aspec],
        out_shape=[jax.ShapeDtypeStruct((B, N1, N2, C), a_re.dtype)] * 2,
        compiler_params=_cp("parallel", "parallel"),
        name=nm,
    )(a_re, a_im, tabs['twr'], tabs['twi'], *consts, *extra)


def _fft_out_kernel(dr, di, frh, frl, fih, fil, x0, z, bias, o):
    zc = (_dotc(frh[...], frl[...], dr[...].astype(F32), FFT_DATA_PASSES)
          + _dotc(fih[...], fil[...], di[...].astype(F32), FFT_DATA_PASSES))
    zz = z[...]
    o[...] = (x0[...] * (zc + zz * bias[...])).astype(o.dtype)


def _fft_out(d_re, d_im, tabs, x0_2d, z_2d, bias_row):
    B, N1, cols = d_re.shape
    n1u = x0_2d.shape[1]
    consts = list(_np_split(tabs['f1r'][:n1u, :])) + list(_np_split(tabs['f1i'][:n1u, :]))
    tc = _tile(cols, 8192, LANES)
    dspec = pl.BlockSpec((None, N1, tc), lambda b, j: (b, 0, j))
    xspec = pl.BlockSpec((None, n1u, tc), lambda b, j: (b, 0, j))
    return pl.pallas_call(
        _fft_out_kernel,
        grid=(B, cols // tc),
        in_specs=[dspec, dspec] + [pl.BlockSpec((n1u, N1), lambda b, j: (0, 0))] * 4
        + [xspec, xspec, pl.BlockSpec((1, tc), lambda b, j: (0, j))],
        out_specs=xspec,
        out_shape=jax.ShapeDtypeStruct((B, n1u, cols), MIX_OUT),
        compiler_params=_cp("parallel", "parallel"),
        name="fft_last_axis_gate",
    )(d_re, d_im, *consts, x0_2d, z_2d, bias_row)


def _hy_gate_kernel(zc, x0, z, bias, o):
    o[...] = (x0[...] * (zc[...] + z[...] * bias[...])).astype(o.dtype)


def _hyena(p_hy, lp):
    B, n, _ = p_hy.shape
    x0, z = _hyena_pre(p_hy, lp)
    kern, ss = _hyena_filter(n, lp)
    N = 2 * n
    N2 = min(256, n)
    N1 = N // N2
    tabs = _dft_tables(N1, N2)
    scale = lax.rsqrt(ss) * (1.0 / N)
    bias = lp['hy_bias'].reshape(1, BR)
    if N1 >= 4:
        n1u = N1 // 2
        k_re, k_im = _fft_in(kern.reshape(1, N1, N2 * BR), tabs, N1, 3)
        kf = _fft_mid(k_re.reshape(1, N1, N2, BR), k_im.reshape(1, N1, N2, BR), tabs, scale=scale)
        a_re, a_im = _fft_in(z.reshape(B, n1u, N2 * BR), tabs, N1, FFT_DATA_PASSES)
        d_re, d_im = _fft_mid(a_re.reshape(B, N1, N2, BR), a_im.reshape(B, N1, N2, BR), tabs, kf=kf)
        o = _fft_out(d_re.reshape(B, N1, N2 * BR), d_im.reshape(B, N1, N2 * BR), tabs,
                     x0.reshape(B, n1u, N2 * BR), z.reshape(B, n1u, N2 * BR), jnp.tile(bias, (1, N2)))
        return o.reshape(B, n, BR)
    tabs = _dft_tables(1, N)
    kf = _fft_mid(kern.reshape(1, 1, N, BR), jnp.zeros((1, 1, N, BR), F32), tabs, scale=scale)
    zp = jnp.pad(z, ((0, 0), (0, n), (0, 0))).reshape(B, 1, N, BR)
    d_re, _ = _fft_mid(zp, jnp.zeros_like(zp), tabs, kf=kf)
    zc = d_re.reshape(B, N, BR)[:, :n]
    tm = _tile(n, 512)
    spec = pl.BlockSpec((None, tm, BR), lambda b, i: (b, i, 0))
    return pl.pallas_call(
        _hy_gate_kernel,
        grid=(B, n // tm),
        in_specs=[spec, spec, spec, _row_spec(BR)],
        out_specs=spec,
        out_shape=jax.ShapeDtypeStruct((B, n, BR), MIX_OUT),
        compiler_params=_cp("parallel", "parallel"),
        name="hyena_gate",
    )(zc, x0, z, bias)


def _rope_tables(n_lat):
    t = lax.iota(jnp.int32, n_lat)
    row = (t // GRID_W).astype(F32)[:, None]
    col = (t % GRID_W).astype(F32)[:, None]
    half = MLA_ROPE // 2
    inv = ROPE_BASE ** (-jnp.arange(0, half, 2, dtype=F32) / half)
    j = np.arange(HP) % MLA_ROPE
    inv_lane = jnp.take(inv, jnp.asarray(j % (half // 2)))[None]
    ang = jnp.where(jnp.asarray(j < half)[None], row, col) * inv_lane
    return jnp.cos(ang), jnp.sin(ang)


def _rot_index():
    j = np.arange(32)
    first = (j % 16) < 8
    src = np.where(first, j + 8, j - 8)
    sign = np.where(first, -1.0, 1.0)
    return src, sign


def _take_cols(w, idx, sign=None):
    idx = np.asarray(idx)
    sg = np.where(idx >= 0, 1.0 if sign is None else np.asarray(sign, np.float64), 0.0)
    pieces, i, n = [], 0, len(idx)
    while i < n:
        j = i + 1
        if sg[i] == 0.0:
            while j < n and sg[j] == 0.0:
                j += 1
            pieces.append(jnp.zeros((w.shape[0], j - i), w.dtype))
        else:
            while j < n and sg[j] == sg[i] and idx[j] == idx[j - 1] + 1:
                j += 1
            piece = w[:, int(idx[i]):int(idx[i]) + (j - i)]
            pieces.append(piece if sg[i] == 1.0 else piece * sg[i])
        i = j
    return jnp.concatenate(pieces, axis=1)


def _kv_buffers(B, T, tm, n_kv, bufs, total_rows):
    qshape = jax.ShapeDtypeStruct((B, N_HEADS, T, HP), BF16)
    qspec = pl.BlockSpec((None, N_HEADS, tm, HP), lambda b, i: (b, 0, i, 0))
    if bufs is None:
        kvshape = jax.ShapeDtypeStruct((B, N_HEADS, total_rows, HP), BF16)
        return [], [qspec] * (1 + n_kv), [qshape] + [kvshape] * n_kv, []
    off = (bufs[0].shape[2] - T) // tm
    kvspec = pl.BlockSpec((None, N_HEADS, tm, HP), lambda b, i: (b, 0, off + i, 0))
    kvshape = jax.ShapeDtypeStruct(bufs[0].shape, BF16)
    return ([pl.BlockSpec(memory_space=pl.ANY)] * n_kv, [qspec] + [kvspec] * n_kv, [qshape] + [kvshape] * n_kv,
            list(bufs))


def _mla_prep_kernel(p_ref, cq_t, sq_t, ck_t, qn, kvn, wqc, wqs, wk, wv, epe, *rest):
    q_o, k_o, v_o = rest[-3:]
    p = p_ref[...].astype(F32)
    cq = p[:, 0:256]
    cqn = (cq * lax.rsqrt(jnp.sum(cq * cq, axis=-1, keepdims=True) * (1.0 / MLA_Q_RANK) + EPS) * qn[...]).astype(BF16)
    ckv = p[:, 256:384]
    ckvn = _rms(ckv, kvn[...]).astype(BF16)
    cq_all = jnp.concatenate([cq_t[...]] * N_HEADS, axis=1)
    sq_all = jnp.concatenate([sq_t[...]] * N_HEADS, axis=1)
    q = _dot(cqn, wqc[...]) * cq_all + _dot(cqn, wqs[...]) * sq_all
    pe = (p[:, 384:512] * ck_t[...]).astype(BF16)
    k = _dot(ckvn, wk[...]) + _dot(pe, epe[...])
    v = _dot(ckvn, wv[...])
    lane = lax.broadcasted_iota(jnp.int32, (p.shape[0], HP), 1)
    for h in range(N_HEADS):
        q_o[h] = q[:, h * HP:(h + 1) * HP].astype(BF16)
        k_o[h] = k[:, h * HP:(h + 1) * HP].astype(BF16)
        v_o[h] = jnp.where(lane == ONES_LANE, 1.0, v[:, h * HP:(h + 1) * HP]).astype(BF16)


def _mla_weights(lp):
    s = LOG2E / math.sqrt(MLA_NOPE + MLA_ROPE)
    src, sign = _rot_index()
    hd = MLA_NOPE + MLA_ROPE
    qc_idx, qs_idx, qs_sign, k_idx, v_idx = [], [], [], [], []
    for h in range(N_HEADS):
        qc_idx += list(h * hd + np.arange(hd)) + [-1] * (HP - hd)
        qs_idx += [-1] * MLA_NOPE + list(h * hd + MLA_NOPE + src) + [-1] * (HP - hd)
        qs_sign += [0.0] * MLA_NOPE + list(sign) + [0.0] * (HP - hd)
        k_idx += list(h * 128 + np.arange(MLA_NOPE)) + [-1] * (HP - MLA_NOPE)
        v_idx += list(h * 128 + MLA_NOPE + np.arange(64)) + [-1] * (HP - 64)
    wq = jnp.pad(lp['mla_wq_b'] * s, ((0, 256 - MLA_Q_RANK), (0, 0)))
    wqc = _take_cols(wq, qc_idx).astype(BF16)
    wqs = _take_cols(wq, qs_idx, qs_sign).astype(BF16)
    wk = _take_cols(lp['mla_wkv_b'], k_idx).astype(BF16)
    wv = _take_cols(lp['mla_wkv_b'], v_idx).astype(BF16)
    epe = np.zeros((128, N_HEADS * HP), np.float32)
    for h in range(N_HEADS):
        for j in range(MLA_ROPE):
            epe[j, h * HP + MLA_NOPE + j] = 1.0
            epe[MLA_ROPE + j, h * HP + MLA_NOPE + j] = 1.0
    qn = jnp.pad(lp['mla_q_norm'], (0, 256 - MLA_Q_RANK)).reshape(1, 256)
    return qn, lp['mla_kv_norm'].reshape(1, MLA_KV_RANK), wqc, wqs, wk, wv, jnp.asarray(epe, BF16)


def _mla_tables(cos, sin):
    lane = lax.broadcasted_iota(jnp.int32, cos.shape, 1)
    pe = (lane >= MLA_NOPE) & (lane < MLA_NOPE + MLA_ROPE)
    cq = jnp.where(lane < MLA_NOPE, 1.0, jnp.where(pe, cos, 0.0))
    sq = jnp.where(pe, sin, 0.0)
    ck = jnp.where(lane < MLA_ROPE, cos, jnp.where(lane < 2 * MLA_ROPE, sin, 0.0))
    return cq, sq, ck


def _mla_prep(p_mla, tables, weights, total_rows, bufs=None):
    B, T, C = p_mla.shape
    tm = _tile(T, 256)
    W = N_HEADS * HP
    tspecs = [pl.BlockSpec((tm, HP), lambda b, i: (i, 0))] * 3
    wspecs = [pl.BlockSpec(w.shape, lambda b, i: (0, 0)) for w in weights]
    aspecs, ospecs, oshapes, extra = _kv_buffers(B, T, tm, 2, bufs, total_rows)
    n_in = 1 + len(tables) + len(weights)
    return pl.pallas_call(
        _mla_prep_kernel,
        grid=(B, T // tm),
        in_specs=[pl.BlockSpec((None, tm, C), lambda b, i: (b, i, 0))] + tspecs + wspecs + aspecs,
        out_specs=ospecs,
        out_shape=oshapes,
        input_output_aliases={n_in + j: 1 + j for j in range(len(extra))},
        compiler_params=_cp("parallel", "parallel"),
        name="mla_prepare",
    )(p_mla, *tables, *weights, *extra)


def _df_prep_kernel(p_ref, c_t, s_t, *rest):
    q_o, k1_o, k2_o, v_o = rest[-4:]
    p = p_ref[...].astype(F32)
    c = jnp.concatenate([c_t[...]] * 2, axis=1)
    s = jnp.concatenate([s_t[...]] * 2, axis=1)
    q = (p[:, 0:BR] * c + p[:, BR:2 * BR] * s) * (LOG2E / math.sqrt(DF_QK))
    k = p[:, 2 * BR:3 * BR] * c + p[:, 3 * BR:4 * BR] * s
    v = p[:, 4 * BR:5 * BR]
    lane = lax.broadcasted_iota(jnp.int32, (p.shape[0], HP), 1)
    for h in range(N_HEADS):
        pair = slice((h // 2) * HP, (h // 2 + 1) * HP)

        def head(x):
            return x[:, pair] if h % 2 == 0 else pltpu.roll(x[:, pair], HEAD_DIM, 1)

        kh = head(k)
        q_o[h] = jnp.where(lane < 2 * DF_QK, head(q), 0.0).astype(BF16)
        k1_o[h] = jnp.where(lane < DF_QK, kh, 0.0).astype(BF16)
        k2_o[h] = jnp.where((lane >= DF_QK) & (lane < 2 * DF_QK), kh, 0.0).astype(BF16)
        v_o[h] = jnp.where(lane == ONES_LANE, 1.0, jnp.where(lane < DF_V, head(v), 0.0)).astype(BF16)


def _df_tables(cos, sin):
    return cos, sin


def _df_prep(p_df, tables, total_rows, bufs=None):
    B, T, C = p_df.shape
    tm = _tile(T, 256)
    W = N_HEADS * HP
    tspec = pl.BlockSpec((tm, HP), lambda b, i: (i, 0))
    aspecs, ospecs, oshapes, extra = _kv_buffers(B, T, tm, 3, bufs, total_rows)
    return pl.pallas_call(
        _df_prep_kernel,
        grid=(B, T // tm),
        in_specs=[pl.BlockSpec((None, tm, C), lambda b, i: (b, i, 0)), tspec, tspec] + aspecs,
        out_specs=ospecs,
        out_shape=oshapes,
        input_output_aliases={3 + j: 1 + j for j in range(len(extra))},
        compiler_params=_cp("parallel", "parallel"),
        name="diff_prepare",
    )(p_df, *tables, *extra)


def _softmax_pv(q, k_ref, v_ref):
    s = _dot(q, k_ref[...], NT)
    p = jnp.exp2((s - jnp.max(s, axis=-1, keepdims=True)).astype(BF16))
    acc = _dot(p, v_ref[...])
    return acc / acc[:, ONES_LANE:ONES_LANE + 1]


def _mla_attn_kernel(q_ref, k_ref, v_ref, o_ref):
    o = _softmax_pv(q_ref[...], k_ref, v_ref)
    lane = lax.broadcasted_iota(jnp.int32, o.shape, 1)
    o_ref[...] = jnp.where(lane < HEAD_DIM, o, 0.0).astype(o_ref.dtype)


def _attn_specs(q, key_rows):
    tq = _tile(q.shape[2], ATTN_TQ)
    first, n = key_rows
    assert first % n == 0
    qspec = pl.BlockSpec((None, None, tq, HP), lambda b, h, i: (b, h, i, 0))
    kspec = pl.BlockSpec((None, None, n, HP), lambda b, h, i: (b, h, first // n, 0))
    return tq, qspec, kspec


def _mla_attention(q, k, v, key_rows):
    B, H, Tq, _ = q.shape
    tq, qspec, kspec = _attn_specs(q, key_rows)
    return pl.pallas_call(
        _mla_attn_kernel,
        grid=(B, H, Tq // tq),
        in_specs=[qspec, kspec, kspec],
        out_specs=qspec,
        out_shape=jax.ShapeDtypeStruct((B, H, Tq, HP), MIX_OUT),
        compiler_params=_cp("parallel", "parallel", "parallel"),
        name="mla_attention",
    )(q, k, v)


def _df_attn_kernel(lam_init, q_ref, k1_ref, k2_ref, v_ref, lq1, lk1, lq2, lk2, sub, o_ref):
    q = q_ref[...]
    lam = (jnp.exp(jnp.sum(lq1[...] * lk1[...], axis=-1, keepdims=True))
           - jnp.exp(jnp.sum(lq2[...] * lk2[...], axis=-1, keepdims=True)) + lam_init)
    o = _softmax_pv(q, k1_ref, v_ref) - lam * _softmax_pv(q, k2_ref, v_ref)
    lane = lax.broadcasted_iota(jnp.int32, o.shape, 1)
    o = jnp.where(lane < DF_V, o, 0.0)
    ms = jnp.sum(o * o, axis=-1, keepdims=True) * (1.0 / DF_V)
    o_ref[...] = (o * lax.rsqrt(ms + DF_SUBLN_EPS) * sub[...] * (1.0 - lam_init)).astype(o_ref.dtype)


def _df_attention(q, k1, k2, v, key_rows, lp, lam_init):
    B, H, Tq, _ = q.shape
    tq, qspec, kspec = _attn_specs(q, key_rows)
    lspec = pl.BlockSpec((1, DF_QK), lambda b, h, i: (0, 0))
    sub = jnp.pad(lp['df_subln'], (0, HP - DF_V)).reshape(1, HP)
    return pl.pallas_call(
        functools.partial(_df_attn_kernel, lam_init),
        grid=(B, H, Tq // tq),
        in_specs=[qspec, kspec, kspec, kspec, lspec, lspec, lspec, lspec,
                  pl.BlockSpec((1, HP), lambda b, h, i: (0, 0))],
        out_specs=qspec,
        out_shape=jax.ShapeDtypeStruct((B, H, Tq, HP), MIX_OUT),
        compiler_params=_cp("parallel", "parallel", "parallel"),
        name="diff_attention",
    )(q, k1, k2, v, lp['df_lq1'].reshape(1, DF_QK), lp['df_lk1'].reshape(1, DF_QK),
      lp['df_lq2'].reshape(1, DF_QK), lp['df_lk2'].reshape(1, DF_QK), sub)


def _merge_kernel(x_ref, sh_ref, sc_ref, gt_ref, gpre_ref, gpost_ref, orw, ohy, omla, odf,
                  wg, bg, urw, uhy, umla, udf, wo, o_ref):
    rows = x_ref.shape[0] // MERGE_SPLIT
    grp = [pl.ds(i * rows, rows) for i in range(MERGE_SPLIT)]
    xs = [x_ref[g, :] for g in grp]
    hs = [(_rms(x, gpre_ref[...]) * (1.0 + sc_ref[...]) + sh_ref[...]).astype(BF16) for x in xs]
    accs = [None] * MERGE_SPLIT
    for n in range(4):
        for i, g in enumerate(grp):
            if n == 0:
                up = _dot1(orw[g, :], urw[...])
            elif n == 1:
                up = _dot1(ohy[g, :], uhy[...])
            else:
                o_ref_n, u_ref_n = (omla, umla) if n == 2 else (odf, udf)
                pairs = [o_ref_n[hh, g, :] + pltpu.roll(o_ref_n[hh + 1, g, :].astype(F32), HEAD_DIM, 1).astype(BF16)
                         for hh in range(0, N_HEADS, 2)]
                up = _dot(jnp.concatenate(pairs, axis=1), u_ref_n[...])
            t = _sigmoid(_dot(hs[i], wg[n]) + bg[n]) * up
            accs[i] = t if accs[i] is None else accs[i] + t
    ys = [_dot(acc.astype(BF16), wo[...]) for acc in accs]
    for i, g in enumerate(grp):
        o_ref[g, :] = xs[i] + gt_ref[...] * _rms(ys[i], gpost_ref[...])


def _merge(x, mod3, g_pre, g_post, outs, mw):
    B, T, D = x.shape
    o_rw, o_hy, o_mla, o_df = outs
    tm = _tile(T, 512)
    xspec = pl.BlockSpec((None, tm, D), lambda b, i: (b, i, 0))
    bspec = pl.BlockSpec((None, tm, BR), lambda b, i: (b, i, 0))
    hspec = pl.BlockSpec((None, N_HEADS, tm, HP), lambda b, i: (b, 0, i, 0))

    def full(a):
        nd = a.ndim
        return pl.BlockSpec(a.shape, lambda b, i: (0,) * nd)

    ws = [mw['wg'], mw['bg'], mw['urw'], mw['uhy'], mw['umla'], mw['udf'], mw['wo']]
    return pl.pallas_call(
        _merge_kernel,
        grid=(B, T // tm),
        in_specs=[xspec, _mod_spec(mod3[0], 2), _mod_spec(mod3[1], 2), _mod_spec(mod3[2], 2),
                  _row_spec(D), _row_spec(D), bspec, bspec, hspec, hspec] + [full(a) for a in ws],
        out_specs=xspec,
        out_shape=jax.ShapeDtypeStruct((B, T, D), F32),
        compiler_params=_cp("parallel", "parallel"),
        name="gated_merge",
    )(x, mod3[0], mod3[1], mod3[2], g_pre.reshape(1, D), g_post.reshape(1, D), o_rw, o_hy, o_mla, o_df, *ws)


def _mix_weights(w_in):
    src, sign = _rot_index()
    w_rw = w_in[:, :RW_COLS]
    w_hy = w_in[:, OFF_HY:OFF_HY + HY_COLS]
    kpe0 = OFF_MLA + MLA_Q_RANK + MLA_KV_RANK
    mla_idx = (list(OFF_MLA + np.arange(MLA_Q_RANK)) + [-1] * (256 - MLA_Q_RANK)
               + list(OFF_MLA + MLA_Q_RANK + np.arange(MLA_KV_RANK))
               + list(kpe0 + np.arange(MLA_ROPE)) + list(kpe0 + src) + [-1] * 64)
    mla_sign = [1.0] * 256 + [1.0] * MLA_KV_RANK + [1.0] * MLA_ROPE + list(sign) + [0.0] * 64
    w_mla = _take_cols(w_in, mla_idx, mla_sign)
    idx, sg = [], []
    for base, rot in ((OFF_DF, False), (OFF_DF, True), (OFF_DF + 256, False), (OFF_DF + 256, True),
                      (OFF_DF + 512, False)):
        for g in range(BR // DF_QK):
            idx += list(base + g * DF_QK + (src if rot else np.arange(DF_QK)))
            sg += list(sign) if rot else [1.0] * DF_QK
    w_df = _take_cols(w_in, idx, sg)
    return [w.astype(BF16) for w in (w_rw, w_hy, w_mla, w_df)]


def _merge_weights(lp):
    return dict(wg=lp['w_gate'].astype(BF16), bg=lp['b_gate'].reshape(4, 1, D_MODEL),
                urw=lp['w_up'][0].astype(BF16), uhy=lp['w_up'][1].astype(BF16),
                umla=lp['w_up'][2].astype(BF16), udf=lp['w_up'][3].astype(BF16),
                wo=lp['w_o'].astype(BF16))


def _token_mixing(x, xc, mod, modc, g_pre, lam_init, with_ctx_out, lp, rope):
    B, T, _ = x.shape
    ws = _mix_weights(lp['w_in'])
    p_rw, p_hy, p_mla, p_df = _project(x, mod[0], mod[1], g_pre, ws)
    c_rw, c_hy, c_mla, c_df = _project(xc, modc[0], modc[1], g_pre, ws)

    prep = _rwkv_prepare(p_rw, lp)
    prep_c = _rwkv_prepare(c_rw, lp)
    s0 = jnp.zeros((B, 2, N_HEADS, HEAD_DIM, HEAD_DIM), F32)
    yf_c, yb_c, s_c = _rwkv_scan(prep_c, s0)
    yf, yb, _ = _rwkv_scan(prep, s_c)
    o_rw = _rwkv_output(yf, yb, prep, lp)

    o_hy = _hyena(p_hy, lp)

    cos, sin = rope
    Tc = xc.shape[1]
    ones, zeros = jnp.ones((Tc, HP), F32), jnp.zeros((Tc, HP), F32)
    mla_w = _mla_weights(lp)
    q, k, v = _mla_prep(p_mla, _mla_tables(cos, sin), mla_w, T + Tc)
    qc, k, v = _mla_prep(c_mla, _mla_tables(ones, zeros), mla_w, T + Tc, bufs=(k, v))
    o_mla = _mla_attention(q, k, v, (0, T + Tc))

    dq, dk1, dk2, dv = _df_prep(p_df, _df_tables(cos, sin), T + Tc)
    dqc, dk1, dk2, dv = _df_prep(c_df, _df_tables(ones, zeros), T + Tc, bufs=(dk1, dk2, dv))
    o_df = _df_attention(dq, dk1, dk2, dv, (0, T + Tc), lp, lam_init)

    outs = (o_rw, o_hy, o_mla, o_df)
    if not with_ctx_out:
        return outs, None
    oc_rw = _rwkv_output(yf_c, yb_c, prep_c, lp)
    oc_hy = _hyena(c_hy, lp)
    oc_mla = _mla_attention(qc, k, v, (T, Tc))
    oc_df = _df_attention(dqc, dk1, dk2, dv, (T, Tc), lp, lam_init)
    return outs, (oc_rw, oc_hy, oc_mla, oc_df)


def kernel(x, c, ctx, c_ctx, w_mod, b_mod, norm_pre, norm_post, ffn_w_in, ffn_w_out, w_in, rw_mu, rw_w0, rw_w2, rw_a0, rw_a2, rw_g2, rw_kk, rw_ka, rw_rk, rw_ln_w, rw_ln_b, hy_conv_w, hy_conv_b, hy_w1, hy_b1, hy_f1, hy_w2, hy_b2, hy_f2, hy_w3, hy_bias, mla_q_norm, mla_wq_b, mla_kv_norm, mla_wkv_b, df_lq1, df_lk1, df_lq2, df_lk2, df_subln, w_up, w_gate, b_gate, w_o):
    params = dict(w_in=w_in, rw_mu=rw_mu, rw_w0=rw_w0, rw_w2=rw_w2, rw_a0=rw_a0, rw_a2=rw_a2, rw_g2=rw_g2,
                  rw_kk=rw_kk, rw_ka=rw_ka, rw_rk=rw_rk, rw_ln_w=rw_ln_w, rw_ln_b=rw_ln_b, hy_conv_w=hy_conv_w,
                  hy_conv_b=hy_conv_b, hy_w1=hy_w1, hy_b1=hy_b1, hy_f1=hy_f1, hy_w2=hy_w2, hy_b2=hy_b2,
                  hy_f2=hy_f2, hy_w3=hy_w3, hy_bias=hy_bias, mla_q_norm=mla_q_norm, mla_wq_b=mla_wq_b,
                  mla_kv_norm=mla_kv_norm, mla_wkv_b=mla_wkv_b, df_lq1=df_lq1, df_lk1=df_lk1, df_lq2=df_lq2,
                  df_lk2=df_lk2, df_subln=df_subln, w_up=w_up, w_gate=w_gate, b_gate=b_gate, w_o=w_o)
    B, T, D = x.shape
    depth = w_mod.shape[0]
    assert B <= 7 and D == D_MODEL and T % 128 == 0 and ctx.shape[1] % CHUNK == 0 and T % ctx.shape[1] == 0
    rope = _rope_tables(T)

    s_rows = jnp.concatenate([c, c_ctx[None], jnp.zeros((8 - B - 1, D), F32)], 0)
    mod_all = _modulation(s_rows, w_mod, b_mod).reshape(depth, 8, 3, 3, D)
    ffn_in = ffn_w_in.astype(BF16)
    ffn_out = ffn_w_out.astype(BF16)

    xc = ctx
    for l in range(depth):
        last = l == depth - 1
        lp = {k_: v_[l] for k_, v_ in params.items()}
        mod = [[mod_all[l, :B, s, m][:, None, :] for m in range(3)] for s in range(3)]
        modc = [[mod_all[l, B:B + 1, s, m][:, None, :] for m in range(3)] for s in range(3)]
        ffn_a = (norm_pre[l, 0], norm_post[l, 0], ffn_in[l, 0], ffn_out[l, 0])
        ffn_b = (norm_pre[l, 2], norm_post[l, 2], ffn_in[l, 1], ffn_out[l, 1])

        x = _ffn(x, mod[0], *ffn_a)
        xc = _ffn(xc, modc[0], *ffn_a)

        lam_init = 0.8 - 0.6 * math.exp(-0.3 * l)
        outs, outs_c = _token_mixing(x, xc, mod[1], modc[1], norm_pre[l, 1], lam_init, not last, lp, rope)
        mw = _merge_weights(lp)
        x = _merge(x, mod[1], norm_pre[l, 1], norm_post[l, 1], outs, mw)
        x = _ffn(x, mod[2], *ffn_b)
        if not last:
            xc = _merge(xc, modc[1], norm_pre[l, 1], norm_post[l, 1], outs_c, mw)
            xc = _ffn(xc, modc[2], *ffn_b)
    return x
```

```python
import functools
import math

import numpy as np
import jax
import jax.numpy as jnp
from jax import lax
from jax.experimental import pallas as pl
from jax.experimental.pallas import tpu as pltpu

F32 = jnp.float32
BF16 = jnp.bfloat16
MIX_OUT = BF16

D_MODEL = 1024
GRID_W = 64
N_HEADS = 4
HEAD_DIM = 64
BR = N_HEADS * HEAD_DIM
D_FF = 2816
MACARON_W = 0.5
EPS = 1e-6
ROPE_BASE = 10000.0
RW_LN_EPS = 64e-5
RW_COLS = 1152
HY_COLS = 768
HY_EMB = 33
HY_FAST_DECAY = 0.3
HY_SLOW_DECAY = 1.5
HY_TARGET = 1e-2
MLA_Q_RANK = 192
MLA_KV_RANK = 128
MLA_NOPE = 64
MLA_ROPE = 32
DF_QK = 32
DF_V = 64
DF_SUBLN_EPS = 1e-5
OFF_HY = RW_COLS
OFF_MLA = RW_COLS + HY_COLS
OFF_DF = OFF_MLA + MLA_Q_RANK + MLA_KV_RANK + MLA_ROPE

LANES = 128
HP = 128
ONES_LANE = 64
LOG2E = 1.4426950408889634
FFN_TM = 256
FFN_TF = 2816
ATTN_TQ = 256
FFT_FILTER_PASSES = 1
FFT_ROWS_PER_STEP = 4
MERGE_SPLIT = 2
FFT_DATA_PASSES = 1
CHUNK = 64
RW_P_GRAM = 1
RW_P_INV = 3
RW_P_STATE = 3
RW_P_APPLY = 1
VMEM_LIMIT = 56 * 1024 * 1024


def _cp(*sem):
    return pltpu.CompilerParams(dimension_semantics=sem, vmem_limit_bytes=VMEM_LIMIT)


def _tile(n, pref, mult=8):
    if n <= pref:
        return n
    t = (pref // mult) * mult
    while t > mult and n % t:
        t -= mult
    assert n % t == 0, (n, pref)
    return t


def _dot(a, b, dims=None):
    if dims is None:
        dims = (((a.ndim - 1,), (0,)), ((), ()))
    return lax.dot_general(a, b, dims, preferred_element_type=F32)


NT = (((1,), (1,)), ((), ()))
TN = (((0,), (0,)), ((), ()))


def _split(x):
    hi = x.astype(BF16)
    lo = (x - hi.astype(F32)).astype(BF16)
    return hi, lo


def _dot3(a, b, dims=None):
    ah, al = _split(a)
    bh, bl = _split(b)
    return _dot(ah, bh, dims) + (_dot(ah, bl, dims) + _dot(al, bh, dims))


def _dot1(a, b, dims=None):
    return _dot(a.astype(BF16), b.astype(BF16), dims)


def _dotc(ch, cl, x, passes=3):
    if passes == 1:
        return _dot(ch, x.astype(BF16))
    xh, xl = _split(x)
    return _dot(ch, xh) + (_dot(ch, xl) + _dot(cl, xh))


def _dot_exact_lhs(m, x):
    x1 = x.astype(BF16)
    r1 = x - x1.astype(F32)
    x2 = r1.astype(BF16)
    x3 = (r1 - x2.astype(F32)).astype(BF16)
    return _dot(m, x1) + (_dot(m, x2) + _dot(m, x3))


def _rms(x, g, eps=EPS):
    return x * lax.rsqrt(jnp.mean(x * x, axis=-1, keepdims=True) + eps) * g


def _sigmoid(x):
    return 1.0 / (1.0 + jnp.exp(-x))


def _silu(x):
    return x * _sigmoid(x)


def _softplus(z):
    return jnp.maximum(z, 0.0) + jnp.log(1.0 + jnp.exp(-jnp.abs(z)))


def _np_split(a):
    a = jnp.asarray(a, F32)
    hi = a.astype(BF16)
    lo = (a - hi.astype(F32)).astype(BF16)
    return hi, lo


def _mod_kernel(s_ref, w_ref, b_ref, o_ref):
    s = _silu(s_ref[...])
    o_ref[...] = _dot3(s, w_ref[...]) + b_ref[...]


def _modulation(s_rows, w_mod, b_mod):
    L, D, NM = w_mod.shape
    tn = _tile(NM, 1152, LANES)
    return pl.pallas_call(
        _mod_kernel,
        grid=(L, NM // tn),
        in_specs=[pl.BlockSpec((8, D), lambda l, j: (0, 0)),
                  pl.BlockSpec((None, D, tn), lambda l, j: (l, 0, j)),
                  pl.BlockSpec((None, 1, tn), lambda l, j: (l, 0, j))],
        out_specs=pl.BlockSpec((None, 8, tn), lambda l, j: (l, 0, j)),
        out_shape=jax.ShapeDtypeStruct((L, 8, NM), F32),
        compiler_params=_cp("parallel", "parallel"),
        name="modulation",
    )(s_rows, w_mod, b_mod.reshape(L, 1, NM))


def _mod_spec(m, nargs):
    if m.shape[0] == 1:
        return pl.BlockSpec((None, 1, D_MODEL), lambda *a: (0, 0, 0))
    return pl.BlockSpec((None, 1, D_MODEL), lambda *a: (a[0], 0, 0))


def _row_spec(n):
    return pl.BlockSpec((1, n), lambda *a: (0, 0))


def _ffn_kernel(x_ref, sh_ref, sc_ref, gt_ref, gpre_ref, gpost_ref, wa_ref, wb_ref, wo_ref, o_ref, h_sc, acc_sc):
    f = pl.program_id(2)

    @pl.when(f == 0)
    def _():
        h = _rms(x_ref[...], gpre_ref[...]) * (1.0 + sc_ref[...]) + sh_ref[...]
        h_sc[...] = h.astype(BF16)
        acc_sc[...] = jnp.zeros_like(acc_sc)

    h = h_sc[...]
    a = _dot(h, wa_ref[...])
    b = _dot(h, wb_ref[...])
    acc_sc[...] += _dot((_silu(a) * b).astype(BF16), wo_ref[...])

    @pl.when(f == pl.num_programs(2) - 1)
    def _():
        o_ref[...] = x_ref[...] + MACARON_W * gt_ref[...] * _rms(acc_sc[...], gpost_ref[...])


def _ffn(x, mod3, g_pre, g_post, w_in, w_out):
    B, T, D = x.shape
    F = w_out.shape[0]
    tm = _tile(T, FFN_TM)
    tf = _tile(F, FFN_TF, LANES)
    nf = F // tf
    xspec = pl.BlockSpec((None, tm, D), lambda b, i, f: (b, i, 0))
    return pl.pallas_call(
        _ffn_kernel,
        grid=(B, T // tm, nf),
        in_specs=[xspec, _mod_spec(mod3[0], 3), _mod_spec(mod3[1], 3), _mod_spec(mod3[2], 3),
                  _row_spec(D), _row_spec(D),
                  pl.BlockSpec((D, tf), lambda b, i, f: (0, f)),
                  pl.BlockSpec((D, tf), lambda b, i, f: (0, nf + f)),
                  pl.BlockSpec((tf, D), lambda b, i, f: (f, 0))],
        out_specs=xspec,
        out_shape=jax.ShapeDtypeStruct((B, T, D), F32),
        scratch_shapes=[pltpu.VMEM((tm, D), BF16), pltpu.VMEM((tm, D), F32)],
        compiler_params=_cp("parallel", "parallel", "arbitrary"),
        name="ffn",
    )(x, mod3[0], mod3[1], mod3[2], g_pre.reshape(1, D), g_post.reshape(1, D), w_in, w_in, w_out)


def _proj_kernel(x_ref, sh_ref, sc_ref, gpre_ref, wrw, why, wmla, wdf, orw, ohy, omla, odf):
    h = (_rms(x_ref[...], gpre_ref[...]) * (1.0 + sc_ref[...]) + sh_ref[...]).astype(BF16)
    orw[...] = _dot(h, wrw[...])
    ohy[...] = _dot(h, why[...])
    omla[...] = _dot(h, wmla[...]).astype(omla.dtype)
    odf[...] = _dot(h, wdf[...]).astype(odf.dtype)


def _project(x, shift, scale, g_pre, ws):
    B, T, D = x.shape
    tm = _tile(T, 256)
    xspec = pl.BlockSpec((None, tm, D), lambda b, i: (b, i, 0))
    wspecs = [pl.BlockSpec(w.shape, lambda b, i: (0, 0)) for w in ws]
    ospecs = [pl.BlockSpec((None, tm, w.shape[1]), lambda b, i: (b, i, 0)) for w in ws]
    oshapes = [jax.ShapeDtypeStruct((B, T, w.shape[1]), dt) for w, dt in zip(ws, (F32, F32, BF16, BF16))]
    return pl.pallas_call(
        _proj_kernel,
        grid=(B, T // tm),
        in_specs=[xspec, _mod_spec(shift, 2), _mod_spec(scale, 2), _row_spec(D)] + wspecs,
        out_specs=ospecs,
        out_shape=oshapes,
        compiler_params=_cp("parallel", "parallel"),
        name="mix_project",
    )(x, shift, scale, g_pre.reshape(1, D), *ws)


def _halo_specs(tm, T, C):
    nb8 = T // 8
    r = tm // 8
    main = pl.BlockSpec((None, tm, C), lambda b, i: (b, i, 0))
    prev = pl.BlockSpec((None, 8, C), lambda b, i: (b, jnp.maximum(i * r - 1, 0), 0))
    nxt = pl.BlockSpec((None, 8, C), lambda b, i: (b, jnp.minimum((i + 1) * r, nb8 - 1), 0))
    return main, prev, nxt


def _conv3_tile(p, prev8, next8, w0, w1, w2):
    i = pl.program_id(1)
    n_i = pl.num_programs(1)
    tm = p.shape[0]
    row = lax.broadcasted_iota(jnp.int32, p.shape, 0)
    prev_row = jnp.where(i > 0, prev8[7:8, :], 0.0)
    next_row = jnp.where(i < n_i - 1, next8[0:1, :], 0.0)
    pm = jnp.where(row == 0, prev_row, pltpu.roll(p, 1, 0))
    pp = jnp.where(row == tm - 1, next_row, pltpu.roll(p, tm - 1, 0))
    return pm * w0 + p * w1 + pp * w2


def _rwkv_prep_kernel(p_ref, pv_ref, nx_ref, mu_ref, w0_ref, w2_ref, a0_ref, a2_ref, g2_ref, kk_ref, ka_ref,
                      rk_ref, bd_ref, trif_ref, trib_ref, onec_ref, v_o, g_o, bonus_o, *dir_outs):
    mu = mu_ref[...]
    p = _conv3_tile(p_ref[...], pv_ref[...], nx_ref[...], 0.5 * mu, 1.0 - mu, 0.5 * mu)
    r = p[:, 0:BR]
    k = p[:, BR:2 * BR]
    v = p[:, 2 * BR:3 * BR]
    g_a = p[:, 3 * BR:3 * BR + 128]
    wa = p[:, 896:1024]
    aa = p[:, 1024:1152]
    bd = bd_ref[...]
    kkraw = k * kk_ref[...]
    ss = _dot_exact_lhs_rhs(kkraw * kkraw, bd)
    kk = kkraw / jnp.maximum(jnp.sqrt(ss), 1e-12)
    wl = w0_ref[...] + _dot3(jnp.tanh(wa), w2_ref[...])
    lw = -jnp.exp(-_softplus(-wl) - 0.5)
    a = _sigmoid(a0_ref[...] + _dot3(aa, a2_ref[...]))
    a_f = a[:, :BR]
    a_b = a[:, BR:]
    ka = ka_ref[...]
    k_f = k * (1.0 + (a_f - 1.0) * ka)
    k_b = k * (1.0 + (a_b - 1.0) * ka)
    v_o[...] = v
    g_o[...] = _dot1(_sigmoid(g_a), g2_ref[...])
    bonus_o[...] = _dot_exact_lhs_rhs(r * (k_f + k_b) * rk_ref[...], bd) * v
    onec = onec_ref[...]
    for d, (tri_ref, k_d, a_d) in enumerate(((trif_ref, k_f, a_f), (trib_ref, k_b, a_b))):
        rt_o, at_o, bt_o, kt_o, bh_o, kh_o, eg_o = dir_outs[7 * d:7 * d + 7]
        lw_d = lw[:, d * BR:(d + 1) * BR]
        b_d = kk * a_d
        g = _dot_exact_lhs(tri_ref[...], lw_d)
        gtot = _dot_exact_lhs(onec, lw_d)
        eng = jnp.exp(-g)
        et = jnp.exp(gtot - g)
        rt_o[...] = r * jnp.exp(g)
        at_o[...] = -kk * jnp.exp(g - lw_d)
        bt_o[...] = (b_d * eng).astype(bt_o.dtype)
        kt_o[...] = (k_d * eng).astype(kt_o.dtype)
        bh_o[...] = b_d * et
        kh_o[...] = k_d * et
        eg = jnp.exp(gtot)
        for j in range(eg_o.shape[0]):
            eg_o[j] = eg[j * CHUNK:j * CHUNK + 1, :]


def _dot_exact_lhs_rhs(x, m):
    x1 = x.astype(BF16)
    r1 = x - x1.astype(F32)
    x2 = r1.astype(BF16)
    x3 = (r1 - x2.astype(F32)).astype(BF16)
    return _dot(x1, m) + (_dot(x2, m) + _dot(x3, m))


def _block_diag_ones():
    h = np.arange(BR) // HEAD_DIM
    return jnp.asarray((h[:, None] == h[None, :]).astype(np.float32), BF16)


def _rwkv_prepare(p_rw, lp):
    B, T, C = p_rw.shape
    tm = _tile(T, 256)
    main, prev, nxt = _halo_specs(tm, T, C)
    z = jnp.zeros((64, BR), F32)
    w2 = jnp.concatenate([jnp.concatenate([lp['rw_w2'][0], z], 1), jnp.concatenate([z, lp['rw_w2'][1]], 1)], 0)
    a2 = jnp.concatenate([jnp.concatenate([lp['rw_a2'][0], z], 1), jnp.concatenate([z, lp['rw_a2'][1]], 1)], 0)
    params = [lp['rw_mu'].reshape(1, C), lp['rw_w0'].reshape(1, 2 * BR), w2, lp['rw_a0'].reshape(1, 2 * BR), a2,
              lp['rw_g2'].astype(BF16), lp['rw_kk'].reshape(1, BR), lp['rw_ka'].reshape(1, BR),
              lp['rw_rk'].reshape(1, BR), _block_diag_ones()]
    pos = np.arange(tm)
    same = (pos[:, None] // CHUNK) == (pos[None, :] // CHUNK)
    params += [jnp.asarray((same & (pos[None, :] <= pos[:, None])).astype(np.float32), BF16),
               jnp.asarray((same & (pos[None, :] >= pos[:, None])).astype(np.float32), BF16),
               jnp.asarray(same.astype(np.float32), BF16)]
    pspecs = [pl.BlockSpec(a.shape, lambda b, i: (0, 0)) for a in params]
    ospec = pl.BlockSpec((None, tm, BR), lambda b, i: (b, i, 0))
    names = ('v', 'g', 'bonus') + tuple(n + str(d) for d in (0, 1) for n in ('rt', 'at', 'bt', 'kt', 'bh', 'kh', 'eg'))
    gram_dt = BF16 if RW_P_GRAM == 1 else F32
    egspec = pl.BlockSpec((None, tm // CHUNK, 1, BR), lambda b, i: (b, i, 0, 0))
    ospecs = [egspec if n[:2] == 'eg' else ospec for n in names]
    oshapes = [jax.ShapeDtypeStruct((B, T // CHUNK, 1, BR), F32) if n[:2] == 'eg' else
               jax.ShapeDtypeStruct((B, T, BR), gram_dt if n[:2] in ('bt', 'kt') else F32) for n in names]
    outs = pl.pallas_call(
        _rwkv_prep_kernel,
        grid=(B, T // tm),
        in_specs=[main, prev, nxt] + pspecs,
        out_specs=ospecs,
        out_shape=oshapes,
        compiler_params=_cp("parallel", "parallel"),
        name="rwkv_prepare",
    )(p_rw, p_rw, p_rw, *params)
    return dict(zip(names, outs))


def _dot3s(a, b, dims=None):
    if dims is None:
        dims = (((1,), (0,)), ((), ()))
    ca, cb = dims[0][0][0], dims[0][1][0]
    a_lo = a - a.astype(BF16).astype(F32)
    b_lo = b - b.astype(BF16).astype(F32)
    sa = jnp.concatenate([a, a_lo, a], axis=ca).astype(BF16)
    sb = jnp.concatenate([b, b, b_lo], axis=cb).astype(BF16)
    return _dot(sa, sb, dims)


def _mmp(a, b, dims, passes):
    return _dot3s(a, b, dims) if passes == 3 else _dot1(a, b, dims)


def _rwkv_chunks(chains, eye):
    n = range(len(chains))
    rt, at, bt, kt, bh, kh, eg, v, ht, mask_s, mask_i = zip(*chains)
    C = rt[0].shape[0]
    ar = [jnp.concatenate([at[i], rt[i]], 0) for i in n]
    gm = [_mmp(ar[i], jnp.concatenate([bt[i], kt[i]], 0), NT, RW_P_GRAM) for i in n]
    top = [jnp.where(mask_s[i], gm[i][:C], 0.0) for i in n]
    bot = [jnp.where(mask_i[i], gm[i][C:], 0.0) for i in n]
    lab = [top[i][:, :C] for i in n]
    pw = [_mmp(lab[i], lab[i], None, RW_P_INV) for i in n]
    tinv = [eye + lab[i] for i in n]
    for _ in range(int(math.log2(C)) - 2):
        both = [_mmp(pw[i], jnp.concatenate([pw[i], tinv[i]], 1), None, RW_P_INV) for i in n]
        pw = [both[i][:, :C] for i in n]
        tinv = [tinv[i] + both[i][:, C:] for i in n]
    last = [_mmp(pw[i], tinv[i], None, RW_P_INV) for i in n]
    tinv = [tinv[i] + last[i] for i in n]
    wy = [_mmp(ar[i], ht[i], NT, RW_P_STATE) for i in n]
    w1 = [_mmp(top[i][:, C:], v[i], None, RW_P_APPLY) for i in n]
    u = [_mmp(tinv[i], wy[i][:C] + w1[i], None, RW_P_STATE) for i in n]
    uv = [jnp.concatenate([u[i], v[i]], 0) for i in n]
    y1 = [_mmp(bot[i], uv[i], None, RW_P_APPLY) for i in n]
    hu = [_mmp(uv[i], jnp.concatenate([bh[i], kh[i]], 0), TN, RW_P_STATE) for i in n]
    return [(wy[i][C:] + y1[i], ht[i] * eg[i] + hu[i]) for i in n]


def _rwkv_scan_kernel(*refs):
    fwd_in, bwd_in = refs[0:8], refs[8:16]
    s0_ref, yf_o, yb_o, s_o, st = refs[16:]
    c = pl.program_id(0)
    nb, C = yf_o.shape[0], yf_o.shape[1]

    @pl.when(c == 0)
    def _():
        st[...] = s0_ref[...]

    row = lax.broadcasted_iota(jnp.int32, (C, 2 * C), 0)
    col = lax.broadcasted_iota(jnp.int32, (C, 2 * C), 1)
    col = jnp.where(col >= C, col - C, col)
    eye = (lax.broadcasted_iota(jnp.int32, (C, C), 0) == lax.broadcasted_iota(jnp.int32, (C, C), 1)).astype(F32)
    chains = []
    for b in range(nb):
        for d, ins in enumerate((fwd_in, bwd_in)):
            if d == 0:
                mask_i = col <= row
                mask_s = col < row
            else:
                mask_i = col >= row
                mask_s = col > row
            for h in range(N_HEADS):
                sl = pl.ds(h * HEAD_DIM, HEAD_DIM)
                rt, at, bt, kt, bh, kh, eg, v = (ref[b, :, sl] for ref in ins)
                chains.append((rt, at, bt, kt, bh, kh, eg, v, st[b, d, h], mask_s, mask_i))
    results = iter(_rwkv_chunks(chains, eye))
    for b in range(nb):
        for d, y_o in enumerate((yf_o, yb_o)):
            for h in range(N_HEADS):
                y, hn = next(results)
                y_o[b, :, pl.ds(h * HEAD_DIM, HEAD_DIM)] = y
                st[b, d, h] = hn

    @pl.when(c == pl.num_programs(0) - 1)
    def _():
        s_o[...] = st[...]


def _rwkv_scan(prep, s0):
    B, T, _ = prep['v'].shape
    nc = T // CHUNK
    fwd = pl.BlockSpec((B, CHUNK, BR), lambda c: (0, c, 0))
    bwd = pl.BlockSpec((B, CHUNK, BR), lambda c: (0, nc - 1 - c, 0))
    sspec = pl.BlockSpec(s0.shape, lambda c: (0, 0, 0, 0, 0))
    ins = [prep[n + str(d)] for d in (0, 1) for n in ('rt', 'at', 'bt', 'kt', 'bh', 'kh', 'eg')]
    return pl.pallas_call(
        _rwkv_scan_kernel,
        grid=(nc,),
        in_specs=[fwd] * 6 + [pl.BlockSpec((B, None, 1, BR), lambda c: (0, c, 0, 0)), fwd]
        + [bwd] * 6 + [pl.BlockSpec((B, None, 1, BR), lambda c: (0, nc - 1 - c, 0, 0)), bwd] + [sspec],
        out_specs=[fwd, bwd, sspec],
        out_shape=[jax.ShapeDtypeStruct((B, T, BR), F32), jax.ShapeDtypeStruct((B, T, BR), F32),
                   jax.ShapeDtypeStruct(s0.shape, F32)],
        scratch_shapes=[pltpu.VMEM(s0.shape, F32)],
        compiler_params=_cp("arbitrary"),
        name="rwkv_scan",
    )(*ins[:7], prep['v'], *ins[7:], prep['v'], s0)


def _rwkv_out_kernel(yf, yb, g, bonus, lnw, lnb, bd, o):
    y = yf[...] + yb[...]
    m = bd[...]
    inv_n = 1.0 / HEAD_DIM
    mean = _dot_exact_lhs_rhs(y, m) * inv_n
    yc = y - mean
    var = _dot_exact_lhs_rhs(yc * yc, m) * inv_n
    yn = yc * lax.rsqrt(var + RW_LN_EPS) * lnw[...] + lnb[...]
    o[...] = ((yn + bonus[...]) * g[...]).astype(o.dtype)


def _rwkv_output(yf, yb, prep, lp):
    B, T, _ = yf.shape
    tm = _tile(T, 512)
    spec = pl.BlockSpec((None, tm, BR), lambda b, i: (b, i, 0))
    return pl.pallas_call(
        _rwkv_out_kernel,
        grid=(B, T // tm),
        in_specs=[spec] * 4 + [_row_spec(BR), _row_spec(BR), pl.BlockSpec((BR, BR), lambda b, i: (0, 0))],
        out_specs=spec,
        out_shape=jax.ShapeDtypeStruct((B, T, BR), MIX_OUT),
        compiler_params=_cp("parallel", "parallel"),
        name="rwkv_output",
    )(yf, yb, prep['g'], prep['bonus'], lp['rw_ln_w'].reshape(1, BR), lp['rw_ln_b'].reshape(1, BR),
      _block_diag_ones())


def _hy_pre_kernel(p_ref, pv_ref, nx_ref, w_ref, b_ref, x0_o, z_o):
    w = w_ref[...]
    u = _conv3_tile(p_ref[...], pv_ref[...], nx_ref[...], w[0:1, :], w[1:2, :], w[2:3, :]) + b_ref[...]
    x0_o[...] = u[:, 0:BR]
    z_o[...] = u[:, 2 * BR:3 * BR] * u[:, BR:2 * BR]


def _hyena_pre(p_hy, lp):
    B, T, C = p_hy.shape
    tm = _tile(T, 512)
    main, prev, nxt = _halo_specs(tm, T, C)
    ospec = pl.BlockSpec((None, tm, BR), lambda b, i: (b, i, 0))
    return pl.pallas_call(
        _hy_pre_kernel,
        grid=(B, T // tm),
        in_specs=[main, prev, nxt, pl.BlockSpec((3, C), lambda b, i: (0, 0)), _row_spec(C)],
        out_specs=[ospec, ospec],
        out_shape=[jax.ShapeDtypeStruct((B, T, BR), F32)] * 2,
        compiler_params=_cp("parallel", "parallel"),
        name="hyena_pre",
    )(p_hy, p_hy, p_hy, lp['hy_conv_w'], lp['hy_conv_b'].reshape(1, C))


def _hy_filter_kernel(z_ref, t_ref, mf_ref, mb_ref, w1, b1, f1, w2, b2, f2, w3, dl, k_o, ss_o):
    h = jnp.sin(f1[...] * (_dot3(z_ref[...], w1[...]) + b1[...]))
    h = jnp.sin(f2[...] * (_dot3(h, w2[...]) + b2[...]))
    h = _dot1(h, w3[...]) * jnp.exp(-t_ref[...] * dl[...])
    kern = mf_ref[...] * h[:, :BR] + mb_ref[...] * h[:, BR:]
    k_o[...] = kern

    @pl.when(pl.program_id(0) == 0)
    def _():
        ss_o[...] = jnp.zeros_like(ss_o)

    ss_o[...] += jnp.sum(kern * kern, axis=0, keepdims=True)


def _hyena_filter(n, lp):
    t = jnp.linspace(0.0, 1.0, n, dtype=F32)
    bands = (HY_EMB - 1) // 2
    wpos = 2.0 * math.pi * jnp.arange(n, dtype=F32) / n
    fr = jnp.linspace(1e-4, bands - 1, bands, dtype=F32)
    t_ext = jnp.concatenate([t, t[::-1]])[:, None]
    fr_lane = jnp.concatenate([jnp.zeros((1,), F32), fr, fr, jnp.zeros((LANES - HY_EMB,), F32)])[None]
    ang = jnp.concatenate([wpos, wpos[::-1]])[:, None] * fr_lane
    lane = lax.broadcasted_iota(jnp.int32, ang.shape, 1)
    z_ext = jnp.where(lane == 0, t_ext, jnp.where(lane <= bands, jnp.cos(ang),
                                                  jnp.where(lane < HY_EMB, -jnp.sin(ang), 0.0)))
    rho = np.arange(2 * n)
    mf = jnp.asarray((rho < n).astype(np.float32))[:, None]
    mb = jnp.asarray((rho > n).astype(np.float32))[:, None]
    max_decay = math.log(HY_TARGET) / HY_FAST_DECAY
    min_decay = math.log(HY_TARGET) / HY_SLOW_DECAY
    deltas = jnp.abs(jnp.linspace(min_decay, max_decay, BR, dtype=F32))
    dl = jnp.tile(deltas, 2)[None]
    w1 = jnp.pad(lp['hy_w1'], ((0, LANES - HY_EMB), (0, 0)))
    params = [w1, lp['hy_b1'][None], lp['hy_f1'][None], lp['hy_w2'], lp['hy_b2'][None], lp['hy_f2'][None],
              lp['hy_w3'], dl]
    tm = _tile(2 * n, 512)
    return pl.pallas_call(
        _hy_filter_kernel,
        grid=(2 * n // tm,),
        in_specs=[pl.BlockSpec((tm, LANES), lambda i: (i, 0))] + [pl.BlockSpec((tm, 1), lambda i: (i, 0))] * 3
        + [pl.BlockSpec(a.shape, lambda i: (0, 0)) for a in params],
        out_specs=[pl.BlockSpec((tm, BR), lambda i: (i, 0)), pl.BlockSpec((1, BR), lambda i: (0, 0))],
        out_shape=[jax.ShapeDtypeStruct((2 * n, BR), F32), jax.ShapeDtypeStruct((1, BR), F32)],
        compiler_params=_cp("arbitrary"),
        name="hyena_filter",
    )(z_ext, t_ext, mf, mb, *params)


def _dft_tables(N1, N2):
    N = N1 * N2

    def mat(n):
        a = -2.0 * np.pi * np.outer(np.arange(n), np.arange(n)) / n
        return np.cos(a), np.sin(a)

    f1r, f1i = mat(N1)
    f2r, f2i = mat(N2)
    a = -2.0 * np.pi * np.outer(np.arange(N1), np.arange(N2)) / N
    return dict(f1r=f1r, f1i=f1i, f2r=f2r, f2i=f2i,
                twr=jnp.asarray(np.cos(a)[:, :, None], F32), twi=jnp.asarray(np.sin(a)[:, :, None], F32))


def _fft_in_kernel(passes, x_ref, frh, frl, fih, fil, ar_o, ai_o):
    x = x_ref[...]
    ar_o[...] = _dotc(frh[...], frl[...], x, passes).astype(ar_o.dtype)
    ai_o[...] = _dotc(fih[...], fil[...], x, passes).astype(ai_o.dtype)


def _fft_in(x2d, tabs, N1, passes):
    B, n1u, cols = x2d.shape
    consts = list(_np_split(tabs['f1r'][:, :n1u])) + list(_np_split(tabs['f1i'][:, :n1u]))
    tc = _tile(cols, 8192, LANES)
    ospec = pl.BlockSpec((None, N1, tc), lambda b, j: (b, 0, j))
    return pl.pallas_call(
        functools.partial(_fft_in_kernel, passes),
        grid=(B, cols // tc),
        in_specs=[pl.BlockSpec((None, n1u, tc), lambda b, j: (b, 0, j))]
        + [pl.BlockSpec((N1, n1u), lambda b, j: (0, 0))] * 4,
        out_specs=[ospec, ospec],
        out_shape=[jax.ShapeDtypeStruct((B, N1, cols), F32 if passes == 3 else BF16)] * 2,
        compiler_params=_cp("parallel", "parallel"),
        name="fft_first_axis",
    )(x2d, *consts)


def _cplx_dft(frh, frl, fih, fil, xr, xi, conj, passes):
    rr = _dotc(frh, frl, xr, passes)
    ii = _dotc(fih, fil, xi, passes)
    ri = _dotc(frh, frl, xi, passes)
    ir = _dotc(fih, fil, xr, passes)
    if conj:
        return rr + ii, ri - ir
    return rr - ii, ri + ir


def _fft_spec_kernel(ar, ai, twr, twi, frh, frl, fih, fil, sc, kr_o, ki_o):
    f = (frh[...], frl[...], fih[...], fil[...])
    for j in range(ar.shape[0]):
        tr, ti = twr[j], twi[j]
        xr = ar[j] * tr - ai[j] * ti
        xi = ar[j] * ti + ai[j] * tr
        br, bi = _cplx_dft(*f, xr, xi, False, FFT_FILTER_PASSES)
        kr_o[j] = br * sc[...]
        ki_o[j] = bi * sc[...]


def _fft_conv_kernel(ar, ai, twr, twi, frh, frl, fih, fil, kr, ki, dr_o, di_o):
    f = (frh[...], frl[...], fih[...], fil[...])
    rows = range(ar.shape[0])
    tw = [(twr[j], twi[j]) for j in rows]
    a = [(ar[j].astype(F32), ai[j].astype(F32)) for j in rows]
    x = [(a[j][0] * tw[j][0] - a[j][1] * tw[j][1], a[j][0] * tw[j][1] + a[j][1] * tw[j][0]) for j in rows]
    bf = [_cplx_dft(*f, x[j][0], x[j][1], False, FFT_DATA_PASSES) for j in rows]
    c = [(bf[j][0] * kr[j] - bf[j][1] * ki[j], bf[j][0] * ki[j] + bf[j][1] * kr[j]) for j in rows]
    d = [_cplx_dft(*f, c[j][0], c[j][1], True, FFT_DATA_PASSES) for j in rows]
    for j in rows:
        (dr, di), (tr, ti) = d[j], tw[j]
        dr_o[j] = (dr * tr + di * ti).astype(dr_o.dtype)
        di_o[j] = (di * tr - dr * ti).astype(di_o.dtype)


def _fft_mid(a_re, a_im, tabs, kf=None, scale=None):
    B, N1, N2, C = a_re.shape
    consts = list(_np_split(tabs['f2r'])) + list(_np_split(tabs['f2i']))
    kb = _tile(N1, FFT_ROWS_PER_STEP, 1)
    aspec = pl.BlockSpec((None, kb, N2, C), lambda b, k: (b, k, 0, 0))
    tspec = pl.BlockSpec((kb, N2, 1), lambda b, k: (k, 0, 0))
    fspec = pl.BlockSpec((N2, N2), lambda b, k: (0, 0))
    if kf is None:
        extra, especs, kern, nm = [scale], [_row_spec(C)], _fft_spec_kernel, "fft_filter_spectrum"
    else:
        kspec = pl.BlockSpec((None, kb, N2, C), lambda b, k: (0, k, 0, 0))
        extra, especs, kern, nm = list(kf), [kspec, kspec], _fft_conv_kernel, "fft_second_axis_conv"
    return pl.pallas_call(
        kern,
        grid=(B, N1 // kb),
        in_specs=[aspec, aspec, tspec, tspec] + [fspec] * 4 + especs,
        out_specs=[aspec, aspec],
        out_shape=[jax.ShapeDtypeStruct((B, N1, N2, C), a_re.dtype)] * 2,
        compiler_params=_cp("parallel", "parallel"),
        name=nm,
    )(a_re, a_im, tabs['twr'], tabs['twi'], *consts, *extra)


def _fft_out_kernel(dr, di, frh, frl, fih, fil, x0, z, bias, o):
    zc = (_dotc(frh[...], frl[...], dr[...].astype(F32), FFT_DATA_PASSES)
          + _dotc(fih[...], fil[...], di[...].astype(F32), FFT_DATA_PASSES))
    zz = z[...]
    o[...] = (x0[...] * (zc + zz * bias[...])).astype(o.dtype)


def _fft_out(d_re, d_im, tabs, x0_2d, z_2d, bias_row):
    B, N1, cols = d_re.shape
    n1u = x0_2d.shape[1]
    consts = list(_np_split(tabs['f1r'][:n1u, :])) + list(_np_split(tabs['f1i'][:n1u, :]))
    tc = _tile(cols, 8192, LANES)
    dspec = pl.BlockSpec((None, N1, tc), lambda b, j: (b, 0, j))
    xspec = pl.BlockSpec((None, n1u, tc), lambda b, j: (b, 0, j))
    return pl.pallas_call(
        _fft_out_kernel,
        grid=(B, cols // tc),
        in_specs=[dspec, dspec] + [pl.BlockSpec((n1u, N1), lambda b, j: (0, 0))] * 4
        + [xspec, xspec, pl.BlockSpec((1, tc), lambda b, j: (0, j))],
        out_specs=xspec,
        out_shape=jax.ShapeDtypeStruct((B, n1u, cols), MIX_OUT),
        compiler_params=_cp("parallel", "parallel"),
        name="fft_last_axis_gate",
    )(d_re, d_im, *consts, x0_2d, z_2d, bias_row)


def _hy_gate_kernel(zc, x0, z, bias, o):
    o[...] = (x0[...] * (zc[...] + z[...] * bias[...])).astype(o.dtype)


def _hyena(p_hy, lp):
    B, n, _ = p_hy.shape
    x0, z = _hyena_pre(p_hy, lp)
    kern, ss = _hyena_filter(n, lp)
    N = 2 * n
    N2 = min(256, n)
    N1 = N // N2
    tabs = _dft_tables(N1, N2)
    scale = lax.rsqrt(ss) * (1.0 / N)
    bias = lp['hy_bias'].reshape(1, BR)
    if N1 >= 4:
        n1u = N1 // 2
        k_re, k_im = _fft_in(kern.reshape(1, N1, N2 * BR), tabs, N1, 3)
        kf = _fft_mid(k_re.reshape(1, N1, N2, BR), k_im.reshape(1, N1, N2, BR), tabs, scale=scale)
        a_re, a_im = _fft_in(z.reshape(B, n1u, N2 * BR), tabs, N1, FFT_DATA_PASSES)
        d_re, d_im = _fft_mid(a_re.reshape(B, N1, N2, BR), a_im.reshape(B, N1, N2, BR), tabs, kf=kf)
        o = _fft_out(d_re.reshape(B, N1, N2 * BR), d_im.reshape(B, N1, N2 * BR), tabs,
                     x0.reshape(B, n1u, N2 * BR), z.reshape(B, n1u, N2 * BR), jnp.tile(bias, (1, N2)))
        return o.reshape(B, n, BR)
    tabs = _dft_tables(1, N)
    kf = _fft_mid(kern.reshape(1, 1, N, BR), jnp.zeros((1, 1, N, BR), F32), tabs, scale=scale)
    zp = jnp.pad(z, ((0, 0), (0, n), (0, 0))).reshape(B, 1, N, BR)
    d_re, _ = _fft_mid(zp, jnp.zeros_like(zp), tabs, kf=kf)
    zc = d_re.reshape(B, N, BR)[:, :n]
    tm = _tile(n, 512)
    spec = pl.BlockSpec((None, tm, BR), lambda b, i: (b, i, 0))
    return pl.pallas_call(
        _hy_gate_kernel,
        grid=(B, n // tm),
        in_specs=[spec, spec, spec, _row_spec(BR)],
        out_specs=spec,
        out_shape=jax.ShapeDtypeStruct((B, n, BR), MIX_OUT),
        compiler_params=_cp("parallel", "parallel"),
        name="hyena_gate",
    )(zc, x0, z, bias)


def _rope_tables(n_lat):
    t = lax.iota(jnp.int32, n_lat)
    row = (t // GRID_W).astype(F32)[:, None]
    col = (t % GRID_W).astype(F32)[:, None]
    half = MLA_ROPE // 2
    inv = ROPE_BASE ** (-jnp.arange(0, half, 2, dtype=F32) / half)
    j = np.arange(HP) % MLA_ROPE
    inv_lane = jnp.take(inv, jnp.asarray(j % (half // 2)))[None]
    ang = jnp.where(jnp.asarray(j < half)[None], row, col) * inv_lane
    return jnp.cos(ang), jnp.sin(ang)


def _rot_index():
    j = np.arange(32)
    first = (j % 16) < 8
    src = np.where(first, j + 8, j - 8)
    sign = np.where(first, -1.0, 1.0)
    return src, sign


def _take_cols(w, idx, sign=None):
    idx = np.asarray(idx)
    sg = np.where(idx >= 0, 1.0 if sign is None else np.asarray(sign, np.float64), 0.0)
    pieces, i, n = [], 0, len(idx)
    while i < n:
        j = i + 1
        if sg[i] == 0.0:
            while j < n and sg[j] == 0.0:
                j += 1
            pieces.append(jnp.zeros((w.shape[0], j - i), w.dtype))
        else:
            while j < n and sg[j] == sg[i] and idx[j] == idx[j - 1] + 1:
                j += 1
            piece = w[:, int(idx[i]):int(idx[i]) + (j - i)]
            pieces.append(piece if sg[i] == 1.0 else piece * sg[i])
        i = j
    return jnp.concatenate(pieces, axis=1)


def _kv_buffers(B, T, tm, n_kv, bufs, total_rows):
    qshape = jax.ShapeDtypeStruct((B, N_HEADS, T, HP), BF16)
    qspec = pl.BlockSpec((None, N_HEADS, tm, HP), lambda b, i: (b, 0, i, 0))
    if bufs is None:
        kvshape = jax.ShapeDtypeStruct((B, N_HEADS, total_rows, HP), BF16)
        return [], [qspec] * (1 + n_kv), [qshape] + [kvshape] * n_kv, []
    off = (bufs[0].shape[2] - T) // tm
    kvspec = pl.BlockSpec((None, N_HEADS, tm, HP), lambda b, i: (b, 0, off + i, 0))
    kvshape = jax.ShapeDtypeStruct(bufs[0].shape, BF16)
    return ([pl.BlockSpec(memory_space=pl.ANY)] * n_kv, [qspec] + [kvspec] * n_kv, [qshape] + [kvshape] * n_kv,
            list(bufs))


def _mla_prep_kernel(p_ref, cq_t, sq_t, ck_t, qn, kvn, wqc, wqs, wk, wv, epe, *rest):
    q_o, k_o, v_o = rest[-3:]
    p = p_ref[...].astype(F32)
    cq = p[:, 0:256]
    cqn = (cq * lax.rsqrt(jnp.sum(cq * cq, axis=-1, keepdims=True) * (1.0 / MLA_Q_RANK) + EPS) * qn[...]).astype(BF16)
    ckv = p[:, 256:384]
    ckvn = _rms(ckv, kvn[...]).astype(BF16)
    cq_all = jnp.concatenate([cq_t[...]] * N_HEADS, axis=1)
    sq_all = jnp.concatenate([sq_t[...]] * N_HEADS, axis=1)
    q = _dot(cqn, wqc[...]) * cq_all + _dot(cqn, wqs[...]) * sq_all
    pe = (p[:, 384:512] * ck_t[...]).astype(BF16)
    k = _dot(ckvn, wk[...]) + _dot(pe, epe[...])
    v = _dot(ckvn, wv[...])
    lane = lax.broadcasted_iota(jnp.int32, (p.shape[0], HP), 1)
    for h in range(N_HEADS):
        q_o[h] = q[:, h * HP:(h + 1) * HP].astype(BF16)
        k_o[h] = k[:, h * HP:(h + 1) * HP].astype(BF16)
        v_o[h] = jnp.where(lane == ONES_LANE, 1.0, v[:, h * HP:(h + 1) * HP]).astype(BF16)


def _mla_weights(lp):
    s = LOG2E / math.sqrt(MLA_NOPE + MLA_ROPE)
    src, sign = _rot_index()
    hd = MLA_NOPE + MLA_ROPE
    qc_idx, qs_idx, qs_sign, k_idx, v_idx = [], [], [], [], []
    for h in range(N_HEADS):
        qc_idx += list(h * hd + np.arange(hd)) + [-1] * (HP - hd)
        qs_idx += [-1] * MLA_NOPE + list(h * hd + MLA_NOPE + src) + [-1] * (HP - hd)
        qs_sign += [0.0] * MLA_NOPE + list(sign) + [0.0] * (HP - hd)
        k_idx += list(h * 128 + np.arange(MLA_NOPE)) + [-1] * (HP - MLA_NOPE)
        v_idx += list(h * 128 + MLA_NOPE + np.arange(64)) + [-1] * (HP - 64)
    wq = jnp.pad(lp['mla_wq_b'] * s, ((0, 256 - MLA_Q_RANK), (0, 0)))
    wqc = _take_cols(wq, qc_idx).astype(BF16)
    wqs = _take_cols(wq, qs_idx, qs_sign).astype(BF16)
    wk = _take_cols(lp['mla_wkv_b'], k_idx).astype(BF16)
    wv = _take_cols(lp['mla_wkv_b'], v_idx).astype(BF16)
    epe = np.zeros((128, N_HEADS * HP), np.float32)
    for h in range(N_HEADS):
        for j in range(MLA_ROPE):
            epe[j, h * HP + MLA_NOPE + j] = 1.0
            epe[MLA_ROPE + j, h * HP + MLA_NOPE + j] = 1.0
    qn = jnp.pad(lp['mla_q_norm'], (0, 256 - MLA_Q_RANK)).reshape(1, 256)
    return qn, lp['mla_kv_norm'].reshape(1, MLA_KV_RANK), wqc, wqs, wk, wv, jnp.asarray(epe, BF16)


def _mla_tables(cos, sin):
    lane = lax.broadcasted_iota(jnp.int32, cos.shape, 1)
    pe = (lane >= MLA_NOPE) & (lane < MLA_NOPE + MLA_ROPE)
    cq = jnp.where(lane < MLA_NOPE, 1.0, jnp.where(pe, cos, 0.0))
    sq = jnp.where(pe, sin, 0.0)
    ck = jnp.where(lane < MLA_ROPE, cos, jnp.where(lane < 2 * MLA_ROPE, sin, 0.0))
    return cq, sq, ck


def _mla_prep(p_mla, tables, weights, total_rows, bufs=None):
    B, T, C = p_mla.shape
    tm = _tile(T, 256)
    W = N_HEADS * HP
    tspecs = [pl.BlockSpec((tm, HP), lambda b, i: (i, 0))] * 3
    wspecs = [pl.BlockSpec(w.shape, lambda b, i: (0, 0)) for w in weights]
    aspecs, ospecs, oshapes, extra = _kv_buffers(B, T, tm, 2, bufs, total_rows)
    n_in = 1 + len(tables) + len(weights)
    return pl.pallas_call(
        _mla_prep_kernel,
        grid=(B, T // tm),
        in_specs=[pl.BlockSpec((None, tm, C), lambda b, i: (b, i, 0))] + tspecs + wspecs + aspecs,
        out_specs=ospecs,
        out_shape=oshapes,
        input_output_aliases={n_in + j: 1 + j for j in range(len(extra))},
        compiler_params=_cp("parallel", "parallel"),
        name="mla_prepare",
    )(p_mla, *tables, *weights, *extra)


def _df_prep_kernel(p_ref, c_t, s_t, *rest):
    q_o, k1_o, k2_o, v_o = rest[-4:]
    p = p_ref[...].astype(F32)
    c = jnp.concatenate([c_t[...]] * 2, axis=1)
    s = jnp.concatenate([s_t[...]] * 2, axis=1)
    q = (p[:, 0:BR] * c + p[:, BR:2 * BR] * s) * (LOG2E / math.sqrt(DF_QK))
    k = p[:, 2 * BR:3 * BR] * c + p[:, 3 * BR:4 * BR] * s
    v = p[:, 4 * BR:5 * BR]
    lane = lax.broadcasted_iota(jnp.int32, (p.shape[0], HP), 1)
    for h in range(N_HEADS):
        pair = slice((h // 2) * HP, (h // 2 + 1) * HP)

        def head(x):
            return x[:, pair] if h % 2 == 0 else pltpu.roll(x[:, pair], HEAD_DIM, 1)

        kh = head(k)
        q_o[h] = jnp.where(lane < 2 * DF_QK, head(q), 0.0).astype(BF16)
        k1_o[h] = jnp.where(lane < DF_QK, kh, 0.0).astype(BF16)
        k2_o[h] = jnp.where((lane >= DF_QK) & (lane < 2 * DF_QK), kh, 0.0).astype(BF16)
        v_o[h] = jnp.where(lane == ONES_LANE, 1.0, jnp.where(lane < DF_V, head(v), 0.0)).astype(BF16)


def _df_tables(cos, sin):
    return cos, sin


def _df_prep(p_df, tables, total_rows, bufs=None):
    B, T, C = p_df.shape
    tm = _tile(T, 256)
    W = N_HEADS * HP
    tspec = pl.BlockSpec((tm, HP), lambda b, i: (i, 0))
    aspecs, ospecs, oshapes, extra = _kv_buffers(B, T, tm, 3, bufs, total_rows)
    return pl.pallas_call(
        _df_prep_kernel,
        grid=(B, T // tm),
        in_specs=[pl.BlockSpec((None, tm, C), lambda b, i: (b, i, 0)), tspec, tspec] + aspecs,
        out_specs=ospecs,
        out_shape=oshapes,
        input_output_aliases={3 + j: 1 + j for j in range(len(extra))},
        compiler_params=_cp("parallel", "parallel"),
        name="diff_prepare",
    )(p_df, *tables, *extra)


def _softmax_pv(q, k_ref, v_ref):
    s = _dot(q, k_ref[...], NT)
    p = jnp.exp2((s - jnp.max(s, axis=-1, keepdims=True)).astype(BF16))
    acc = _dot(p, v_ref[...])
    return acc / acc[:, ONES_LANE:ONES_LANE + 1]


def _mla_attn_kernel(q_ref, k_ref, v_ref, o_ref):
    o = _softmax_pv(q_ref[...], k_ref, v_ref)
    lane = lax.broadcasted_iota(jnp.int32, o.shape, 1)
    o_ref[...] = jnp.where(lane < HEAD_DIM, o, 0.0).astype(o_ref.dtype)


def _attn_specs(q, key_rows):
    tq = _tile(q.shape[2], ATTN_TQ)
    first, n = key_rows
    assert first % n == 0
    qspec = pl.BlockSpec((None, None, tq, HP), lambda b, h, i: (b, h, i, 0))
    kspec = pl.BlockSpec((None, None, n, HP), lambda b, h, i: (b, h, first // n, 0))
    return tq, qspec, kspec


def _mla_attention(q, k, v, key_rows):
    B, H, Tq, _ = q.shape
    tq, qspec, kspec = _attn_specs(q, key_rows)
    return pl.pallas_call(
        _mla_attn_kernel,
        grid=(B, H, Tq // tq),
        in_specs=[qspec, kspec, kspec],
        out_specs=qspec,
        out_shape=jax.ShapeDtypeStruct((B, H, Tq, HP), MIX_OUT),
        compiler_params=_cp("parallel", "parallel", "parallel"),
        name="mla_attention",
    )(q, k, v)


def _df_attn_kernel(lam_init, q_ref, k1_ref, k2_ref, v_ref, lq1, lk1, lq2, lk2, sub, o_ref):
    q = q_ref[...]
    lam = (jnp.exp(jnp.sum(lq1[...] * lk1[...], axis=-1, keepdims=True))
           - jnp.exp(jnp.sum(lq2[...] * lk2[...], axis=-1, keepdims=True)) + lam_init)
    o = _softmax_pv(q, k1_ref, v_ref) - lam * _softmax_pv(q, k2_ref, v_ref)
    lane = lax.broadcasted_iota(jnp.int32, o.shape, 1)
    o = jnp.where(lane < DF_V, o, 0.0)
    ms = jnp.sum(o * o, axis=-1, keepdims=True) * (1.0 / DF_V)
    o_ref[...] = (o * lax.rsqrt(ms + DF_SUBLN_EPS) * sub[...] * (1.0 - lam_init)).astype(o_ref.dtype)


def _df_attention(q, k1, k2, v, key_rows, lp, lam_init):
    B, H, Tq, _ = q.shape
    tq, qspec, kspec = _attn_specs(q, key_rows)
    lspec = pl.BlockSpec((1, DF_QK), lambda b, h, i: (0, 0))
    sub = jnp.pad(lp['df_subln'], (0, HP - DF_V)).reshape(1, HP)
    return pl.pallas_call(
        functools.partial(_df_attn_kernel, lam_init),
        grid=(B, H, Tq // tq),
        in_specs=[qspec, kspec, kspec, kspec, lspec, lspec, lspec, lspec,
                  pl.BlockSpec((1, HP), lambda b, h, i: (0, 0))],
        out_specs=qspec,
        out_shape=jax.ShapeDtypeStruct((B, H, Tq, HP), MIX_OUT),
        compiler_params=_cp("parallel", "parallel", "parallel"),
        name="diff_attention",
    )(q, k1, k2, v, lp['df_lq1'].reshape(1, DF_QK), lp['df_lk1'].reshape(1, DF_QK),
      lp['df_lq2'].reshape(1, DF_QK), lp['df_lk2'].reshape(1, DF_QK), sub)


def _merge_kernel(x_ref, sh_ref, sc_ref, gt_ref, gpre_ref, gpost_ref, orw, ohy, omla, odf,
                  wg, bg, urw, uhy, umla, udf, wo, o_ref):
    rows = x_ref.shape[0] // MERGE_SPLIT
    grp = [pl.ds(i * rows, rows) for i in range(MERGE_SPLIT)]
    xs = [x_ref[g, :] for g in grp]
    hs = [(_rms(x, gpre_ref[...]) * (1.0 + sc_ref[...]) + sh_ref[...]).astype(BF16) for x in xs]
    accs = [None] * MERGE_SPLIT
    for n in range(4):
        for i, g in enumerate(grp):
            if n == 0:
                up = _dot1(orw[g, :], urw[...])
            elif n == 1:
                up = _dot1(ohy[g, :], uhy[...])
            else:
                o_ref_n, u_ref_n = (omla, umla) if n == 2 else (odf, udf)
                pairs = [o_ref_n[hh, g, :] + pltpu.roll(o_ref_n[hh + 1, g, :].astype(F32), HEAD_DIM, 1).astype(BF16)
                         for hh in range(0, N_HEADS, 2)]
                up = _dot(jnp.concatenate(pairs, axis=1), u_ref_n[...])
            t = _sigmoid(_dot(hs[i], wg[n]) + bg[n]) * up
            accs[i] = t if accs[i] is None else accs[i] + t
    ys = [_dot(acc.astype(BF16), wo[...]) for acc in accs]
    for i, g in enumerate(grp):
        o_ref[g, :] = xs[i] + gt_ref[...] * _rms(ys[i], gpost_ref[...])


def _merge(x, mod3, g_pre, g_post, outs, mw):
    B, T, D = x.shape
    o_rw, o_hy, o_mla, o_df = outs
    tm = _tile(T, 512)
    xspec = pl.BlockSpec((None, tm, D), lambda b, i: (b, i, 0))
    bspec = pl.BlockSpec((None, tm, BR), lambda b, i: (b, i, 0))
    hspec = pl.BlockSpec((None, N_HEADS, tm, HP), lambda b, i: (b, 0, i, 0))

    def full(a):
        nd = a.ndim
        return pl.BlockSpec(a.shape, lambda b, i: (0,) * nd)

    ws = [mw['wg'], mw['bg'], mw['urw'], mw['uhy'], mw['umla'], mw['udf'], mw['wo']]
    return pl.pallas_call(
        _merge_kernel,
        grid=(B, T // tm),
        in_specs=[xspec, _mod_spec(mod3[0], 2), _mod_spec(mod3[1], 2), _mod_spec(mod3[2], 2),
                  _row_spec(D), _row_spec(D), bspec, bspec, hspec, hspec] + [full(a) for a in ws],
        out_specs=xspec,
        out_shape=jax.ShapeDtypeStruct((B, T, D), F32),
        compiler_params=_cp("parallel", "parallel"),
        name="gated_merge",
    )(x, mod3[0], mod3[1], mod3[2], g_pre.reshape(1, D), g_post.reshape(1, D), o_rw, o_hy, o_mla, o_df, *ws)


def _mix_weights(w_in):
    src, sign = _rot_index()
    w_rw = w_in[:, :RW_COLS]
    w_hy = w_in[:, OFF_HY:OFF_HY + HY_COLS]
    kpe0 = OFF_MLA + MLA_Q_RANK + MLA_KV_RANK
    mla_idx = (list(OFF_MLA + np.arange(MLA_Q_RANK)) + [-1] * (256 - MLA_Q_RANK)
               + list(OFF_MLA + MLA_Q_RANK + np.arange(MLA_KV_RANK))
               + list(kpe0 + np.arange(MLA_ROPE)) + list(kpe0 + src) + [-1] * 64)
    mla_sign = [1.0] * 256 + [1.0] * MLA_KV_RANK + [1.0] * MLA_ROPE + list(sign) + [0.0] * 64
    w_mla = _take_cols(w_in, mla_idx, mla_sign)
    idx, sg = [], []
    for base, rot in ((OFF_DF, False), (OFF_DF, True), (OFF_DF + 256, False), (OFF_DF + 256, True),
                      (OFF_DF + 512, False)):
        for g in range(BR // DF_QK):
            idx += list(base + g * DF_QK + (src if rot else np.arange(DF_QK)))
            sg += list(sign) if rot else [1.0] * DF_QK
    w_df = _take_cols(w_in, idx, sg)
    return [w.astype(BF16) for w in (w_rw, w_hy, w_mla, w_df)]


def _merge_weights(lp):
    return dict(wg=lp['w_gate'].astype(BF16), bg=lp['b_gate'].reshape(4, 1, D_MODEL),
                urw=lp['w_up'][0].astype(BF16), uhy=lp['w_up'][1].astype(BF16),
                umla=lp['w_up'][2].astype(BF16), udf=lp['w_up'][3].astype(BF16),
                wo=lp['w_o'].astype(BF16))


def _token_mixing(x, xc, mod, modc, g_pre, lam_init, with_ctx_out, lp, rope):
    B, T, _ = x.shape
    ws = _mix_weights(lp['w_in'])
    p_rw, p_hy, p_mla, p_df = _project(x, mod[0], mod[1], g_pre, ws)
    c_rw, c_hy, c_mla, c_df = _project(xc, modc[0], modc[1], g_pre, ws)

    prep = _rwkv_prepare(p_rw, lp)
    prep_c = _rwkv_prepare(c_rw, lp)
    s0 = jnp.zeros((B, 2, N_HEADS, HEAD_DIM, HEAD_DIM), F32)
    yf_c, yb_c, s_c = _rwkv_scan(prep_c, s0)
    yf, yb, _ = _rwkv_scan(prep, s_c)
    o_rw = _rwkv_output(yf, yb, prep, lp)

    o_hy = _hyena(p_hy, lp)

    cos, sin = rope
    Tc = xc.shape[1]
    ones, zeros = jnp.ones((Tc, HP), F32), jnp.zeros((Tc, HP), F32)
    mla_w = _mla_weights(lp)
    q, k, v = _mla_prep(p_mla, _mla_tables(cos, sin), mla_w, T + Tc)
    qc, k, v = _mla_prep(c_mla, _mla_tables(ones, zeros), mla_w, T + Tc, bufs=(k, v))
    o_mla = _mla_attention(q, k, v, (0, T + Tc))

    dq, dk1, dk2, dv = _df_prep(p_df, _df_tables(cos, sin), T + Tc)
    dqc, dk1, dk2, dv = _df_prep(c_df, _df_tables(ones, zeros), T + Tc, bufs=(dk1, dk2, dv))
    o_df = _df_attention(dq, dk1, dk2, dv, (0, T + Tc), lp, lam_init)

    outs = (o_rw, o_hy, o_mla, o_df)
    if not with_ctx_out:
        return outs, None
    oc_rw = _rwkv_output(yf_c, yb_c, prep_c, lp)
    oc_hy = _hyena(c_hy, lp)
    oc_mla = _mla_attention(qc, k, v, (T, Tc))
    oc_df = _df_attention(dqc, dk1, dk2, dv, (T, Tc), lp, lam_init)
    return outs, (oc_rw, oc_hy, oc_mla, oc_df)


def kernel(x, c, ctx, c_ctx, w_mod, b_mod, norm_pre, norm_post, ffn_w_in, ffn_w_out, w_in, rw_mu, rw_w0, rw_w2, rw_a0, rw_a2, rw_g2, rw_kk, rw_ka, rw_rk, rw_ln_w, rw_ln_b, hy_conv_w, hy_conv_b, hy_w1, hy_b1, hy_f1, hy_w2, hy_b2, hy_f2, hy_w3, hy_bias, mla_q_norm, mla_wq_b, mla_kv_norm, mla_wkv_b, df_lq1, df_lk1, df_lq2, df_lk2, df_subln, w_up, w_gate, b_gate, w_o):
    params = dict(w_in=w_in, rw_mu=rw_mu, rw_w0=rw_w0, rw_w2=rw_w2, rw_a0=rw_a0, rw_a2=rw_a2, rw_g2=rw_g2,
                  rw_kk=rw_kk, rw_ka=rw_ka, rw_rk=rw_rk, rw_ln_w=rw_ln_w, rw_ln_b=rw_ln_b, hy_conv_w=hy_conv_w,
                  hy_conv_b=hy_conv_b, hy_w1=hy_w1, hy_b1=hy_b1, hy_f1=hy_f1, hy_w2=hy_w2, hy_b2=hy_b2,
                  hy_f2=hy_f2, hy_w3=hy_w3, hy_bias=hy_bias, mla_q_norm=mla_q_norm, mla_wq_b=mla_wq_b,
                  mla_kv_norm=mla_kv_norm, mla_wkv_b=mla_wkv_b, df_lq1=df_lq1, df_lk1=df_lk1, df_lq2=df_lq2,
                  df_lk2=df_lk2, df_subln=df_subln, w_up=w_up, w_gate=w_gate, b_gate=b_gate, w_o=w_o)
    B, T, D = x.shape
    depth = w_mod.shape[0]
    assert B <= 7 and D == D_MODEL and T % 128 == 0 and ctx.shape[1] % CHUNK == 0 and T % ctx.shape[1] == 0
    rope = _rope_tables(T)

    s_rows = jnp.concatenate([c, c_ctx[None], jnp.zeros((8 - B - 1, D), F32)], 0)
    mod_all = _modulation(s_rows, w_mod, b_mod).reshape(depth, 8, 3, 3, D)
    ffn_in = ffn_w_in.astype(BF16)
    ffn_out = ffn_w_out.astype(BF16)

    xc = ctx
    for l in range(depth):
        last = l == depth - 1
        lp = {k_: v_[l] for k_, v_ in params.items()}
        mod = [[mod_all[l, :B, s, m][:, None, :] for m in range(3)] for s in range(3)]
        modc = [[mod_all[l, B:B + 1, s, m][:, None, :] for m in range(3)] for s in range(3)]
        ffn_a = (norm_pre[l, 0], norm_post[l, 0], ffn_in[l, 0], ffn_out[l, 0])
        ffn_b = (norm_pre[l, 2], norm_post[l, 2], ffn_in[l, 1], ffn_out[l, 1])

        x = _ffn(x, mod[0], *ffn_a)
        xc = _ffn(xc, modc[0], *ffn_a)

        lam_init = 0.8 - 0.6 * math.exp(-0.3 * l)
        outs, outs_c = _token_mixing(x, xc, mod[1], modc[1], norm_pre[l, 1], lam_init, not last, lp, rope)
        mw = _merge_weights(lp)
        x = _merge(x, mod[1], norm_pre[l, 1], norm_post[l, 1], outs, mw)
        x = _ffn(x, mod[2], *ffn_b)
        if not last:
            xc = _merge(xc, modc[1], norm_pre[l, 1], norm_post[l, 1], outs_c, mw)
            xc = _ffn(xc, modc[2], *ffn_b)
    return x
```

```python
import functools
import math

import numpy as np
import jax
import jax.numpy as jnp
from jax import lax
from jax.experimental import pallas as pl
from jax.experimental.pallas import tpu as pltpu

F32 = jnp.float32
BF16 = jnp.bfloat16
MIX_OUT = BF16

D_MODEL = 1024
GRID_W = 64
N_HEADS = 4
HEAD_DIM = 64
BR = N_HEADS * HEAD_DIM
D_FF = 2816
MACARON_W = 0.5
EPS = 1e-6
ROPE_BASE = 10000.0
RW_LN_EPS = 64e-5
RW_COLS = 1152
HY_COLS = 768
HY_EMB = 33
HY_FAST_DECAY = 0.3
HY_SLOW_DECAY = 1.5
HY_TARGET = 1e-2
MLA_Q_RANK = 192
MLA_KV_RANK = 128
MLA_NOPE = 64
MLA_ROPE = 32
DF_QK = 32
DF_V = 64
DF_SUBLN_EPS = 1e-5
OFF_HY = RW_COLS
OFF_MLA = RW_COLS + HY_COLS
OFF_DF = OFF_MLA + MLA_Q_RANK + MLA_KV_RANK + MLA_ROPE

LANES = 128
HP = 128
ONES_LANE = 64
LOG2E = 1.4426950408889634
FFN_TM = 256
FFN_TF = 2816
ATTN_TQ = 256
FFT_FILTER_PASSES = 1
FFT_ROWS_PER_STEP = 4
MERGE_SPLIT = 2
FFT_DATA_PASSES = 1
CHUNK = 64
RW_P_GRAM = 1
RW_P_INV = 3
RW_P_STATE = 3
RW_P_APPLY = 1
VMEM_LIMIT = 56 * 1024 * 1024


def _cp(*sem):
    return pltpu.CompilerParams(dimension_semantics=sem, vmem_limit_bytes=VMEM_LIMIT)


def _tile(n, pref, mult=8):
    if n <= pref:
        return n
    t = (pref // mult) * mult
    while t > mult and n % t:
        t -= mult
    assert n % t == 0, (n, pref)
    return t


def _dot(a, b, dims=None):
    if dims is None:
        dims = (((a.ndim - 1,), (0,)), ((), ()))
    return lax.dot_general(a, b, dims, preferred_element_type=F32)


NT = (((1,), (1,)), ((), ()))
TN = (((0,), (0,)), ((), ()))


def _split(x):
    hi = x.astype(BF16)
    lo = (x - hi.astype(F32)).astype(BF16)
    return hi, lo


def _dot3(a, b, dims=None):
    ah, al = _split(a)
    bh, bl = _split(b)
    return _dot(ah, bh, dims) + (_dot(ah, bl, dims) + _dot(al, bh, dims))


def _dot1(a, b, dims=None):
    return _dot(a.astype(BF16), b.astype(BF16), dims)


def _dotc(ch, cl, x, passes=3):
    if passes == 1:
        return _dot(ch, x.astype(BF16))
    xh, xl = _split(x)
    return _dot(ch, xh) + (_dot(ch, xl) + _dot(cl, xh))


def _dot_exact_lhs(m, x):
    x1 = x.astype(BF16)
    r1 = x - x1.astype(F32)
    x2 = r1.astype(BF16)
    x3 = (r1 - x2.astype(F32)).astype(BF16)
    return _dot(m, x1) + (_dot(m, x2) + _dot(m, x3))


def _rms(x, g, eps=EPS):
    return x * lax.rsqrt(jnp.mean(x * x, axis=-1, keepdims=True) + eps) * g


def _sigmoid(x):
    return 1.0 / (1.0 + jnp.exp(-x))


def _silu(x):
    return x * _sigmoid(x)


def _softplus(z):
    return jnp.maximum(z, 0.0) + jnp.log(1.0 + jnp.exp(-jnp.abs(z)))


def _np_split(a):
    a = jnp.asarray(a, F32)
    hi = a.astype(BF16)
    lo = (a - hi.astype(F32)).astype(BF16)
    return hi, lo


def _mod_kernel(s_ref, w_ref, b_ref, o_ref):
    s = _silu(s_ref[...])
    o_ref[...] = _dot3(s, w_ref[...]) + b_ref[...]


def _modulation(s_rows, w_mod, b_mod):
    L, D, NM = w_mod.shape
    tn = _tile(NM, 1152, LANES)
    return pl.pallas_call(
        _mod_kernel,
        grid=(L, NM // tn),
        in_specs=[pl.BlockSpec((8, D), lambda l, j: (0, 0)),
                  pl.BlockSpec((None, D, tn), lambda l, j: (l, 0, j)),
                  pl.BlockSpec((None, 1, tn), lambda l, j: (l, 0, j))],
        out_specs=pl.BlockSpec((None, 8, tn), lambda l, j: (l, 0, j)),
        out_shape=jax.ShapeDtypeStruct((L, 8, NM), F32),
        compiler_params=_cp("parallel", "parallel"),
        name="modulation",
    )(s_rows, w_mod, b_mod.reshape(L, 1, NM))


def _mod_spec(m, nargs):
    if m.shape[0] == 1:
        return pl.BlockSpec((None, 1, D_MODEL), lambda *a: (0, 0, 0))
    return pl.BlockSpec((None, 1, D_MODEL), lambda *a: (a[0], 0, 0))


def _row_spec(n):
    return pl.BlockSpec((1, n), lambda *a: (0, 0))


def _ffn_kernel(x_ref, sh_ref, sc_ref, gt_ref, gpre_ref, gpost_ref, wa_ref, wb_ref, wo_ref, o_ref, h_sc, acc_sc):
    f = pl.program_id(2)

    @pl.when(f == 0)
    def _():
        h = _rms(x_ref[...], gpre_ref[...]) * (1.0 + sc_ref[...]) + sh_ref[...]
        h_sc[...] = h.astype(BF16)
        acc_sc[...] = jnp.zeros_like(acc_sc)

    h = h_sc[...]
    a = _dot(h, wa_ref[...])
    b = _dot(h, wb_ref[...])
    acc_sc[...] += _dot((_silu(a) * b).astype(BF16), wo_ref[...])

    @pl.when(f == pl.num_programs(2) - 1)
    def _():
        o_ref[...] = x_ref[...] + MACARON_W * gt_ref[...] * _rms(acc_sc[...], gpost_ref[...])


def _ffn(x, mod3, g_pre, g_post, w_in, w_out):
    B, T, D = x.shape
    F = w_out.shape[0]
    tm = _tile(T, FFN_TM)
    tf = _tile(F, FFN_TF, LANES)
    nf = F // tf
    xspec = pl.BlockSpec((None, tm, D), lambda b, i, f: (b, i, 0))
    return pl.pallas_call(
        _ffn_kernel,
        grid=(B, T // tm, nf),
        in_specs=[xspec, _mod_spec(mod3[0], 3), _mod_spec(mod3[1], 3), _mod_spec(mod3[2], 3),
                  _row_spec(D), _row_spec(D),
                  pl.BlockSpec((D, tf), lambda b, i, f: (0, f)),
                  pl.BlockSpec((D, tf), lambda b, i, f: (0, nf + f)),
                  pl.BlockSpec((tf, D), lambda b, i, f: (f, 0))],
        out_specs=xspec,
        out_shape=jax.ShapeDtypeStruct((B, T, D), F32),
        scratch_shapes=[pltpu.VMEM((tm, D), BF16), pltpu.VMEM((tm, D), F32)],
        compiler_params=_cp("parallel", "parallel", "arbitrary"),
        name="ffn",
    )(x, mod3[0], mod3[1], mod3[2], g_pre.reshape(1, D), g_post.reshape(1, D), w_in, w_in, w_out)


def _proj_kernel(x_ref, sh_ref, sc_ref, gpre_ref, wrw, why, wmla, wdf, orw, ohy, omla, odf):
    h = (_rms(x_ref[...], gpre_ref[...]) * (1.0 + sc_ref[...]) + sh_ref[...]).astype(BF16)
    orw[...] = _dot(h, wrw[...])
    ohy[...] = _dot(h, why[...])
    omla[...] = _dot(h, wmla[...]).astype(omla.dtype)
    odf[...] = _dot(h, wdf[...]).astype(odf.dtype)


def _project(x, shift, scale, g_pre, ws):
    B, T, D = x.shape
    tm = _tile(T, 256)
    xspec = pl.BlockSpec((None, tm, D), lambda b, i: (b, i, 0))
    wspecs = [pl.BlockSpec(w.shape, lambda b, i: (0, 0)) for w in ws]
    ospecs = [pl.BlockSpec((None, tm, w.shape[1]), lambda b, i: (b, i, 0)) for w in ws]
    oshapes = [jax.ShapeDtypeStruct((B, T, w.shape[1]), dt) for w, dt in zip(ws, (F32, F32, BF16, BF16))]
    return pl.pallas_call(
        _proj_kernel,
        grid=(B, T // tm),
        in_specs=[xspec, _mod_spec(shift, 2), _mod_spec(scale, 2), _row_spec(D)] + wspecs,
        out_specs=ospecs,
        out_shape=oshapes,
        compiler_params=_cp("parallel", "parallel"),
        name="mix_project",
    )(x, shift, scale, g_pre.reshape(1, D), *ws)


def _halo_specs(tm, T, C):
    nb8 = T // 8
    r = tm // 8
    main = pl.BlockSpec((None, tm, C), lambda b, i: (b, i, 0))
    prev = pl.BlockSpec((None, 8, C), lambda b, i: (b, jnp.maximum(i * r - 1, 0), 0))
    nxt = pl.BlockSpec((None, 8, C), lambda b, i: (b, jnp.minimum((i + 1) * r, nb8 - 1), 0))
    return main, prev, nxt


def _conv3_tile(p, prev8, next8, w0, w1, w2):
    i = pl.program_id(1)
    n_i = pl.num_programs(1)
    tm = p.shape[0]
    row = lax.broadcasted_iota(jnp.int32, p.shape, 0)
    prev_row = jnp.where(i > 0, prev8[7:8, :], 0.0)
    next_row = jnp.where(i < n_i - 1, next8[0:1, :], 0.0)
    pm = jnp.where(row == 0, prev_row, pltpu.roll(p, 1, 0))
    pp = jnp.where(row == tm - 1, next_row, pltpu.roll(p, tm - 1, 0))
    return pm * w0 + p * w1 + pp * w2


def _rwkv_prep_kernel(p_ref, pv_ref, nx_ref, mu_ref, w0_ref, w2_ref, a0_ref, a2_ref, g2_ref, kk_ref, ka_ref,
                      rk_ref, bd_ref, trif_ref, trib_ref, v_o, g_o, bonus_o, *dir_outs):
    mu = mu_ref[...]
    p = _conv3_tile(p_ref[...], pv_ref[...], nx_ref[...], 0.5 * mu, 1.0 - mu, 0.5 * mu)
    r = p[:, 0:BR]
    k = p[:, BR:2 * BR]
    v = p[:, 2 * BR:3 * BR]
    g_a = p[:, 3 * BR:3 * BR + 128]
    wa = p[:, 896:1024]
    aa = p[:, 1024:1152]
    bd = bd_ref[...]
    kkraw = k * kk_ref[...]
    ss = _dot_exact_lhs_rhs(kkraw * kkraw, bd)
    kk = kkraw / jnp.maximum(jnp.sqrt(ss), 1e-12)
    wl = w0_ref[...] + _dot3s(jnp.tanh(wa), w2_ref[...])
    lw = -jnp.exp(-_softplus(-wl) - 0.5)
    a = _sigmoid(a0_ref[...] + _dot3s(aa, a2_ref[...]))
    a_f = a[:, :BR]
    a_b = a[:, BR:]
    ka = ka_ref[...]
    k_f = k * (1.0 + (a_f - 1.0) * ka)
    k_b = k * (1.0 + (a_b - 1.0) * ka)
    v_o[...] = v
    g_o[...] = _dot1(_sigmoid(g_a), g2_ref[...])
    bonus_o[...] = _dot_exact_lhs_rhs(r * (k_f + k_b) * rk_ref[...], bd) * v
    for d, (tri_ref, k_d, a_d) in enumerate(((trif_ref, k_f, a_f), (trib_ref, k_b, a_b))):
        rt_o, at_o, bt_o, kt_o, bh_o, kh_o, eg_o = dir_outs[7 * d:7 * d + 7]
        lw_d = lw[:, d * BR:(d + 1) * BR]
        b_d = kk * a_d
        g = _dot_exact_lhs(tri_ref[...], lw_d)
        g4 = g.reshape(g.shape[0] // CHUNK, CHUNK, BR)
        edge = g4[:, CHUNK - 1:CHUNK, :] if d == 0 else g4[:, 0:1, :]
        gtot = jnp.broadcast_to(edge, g4.shape).reshape(g.shape)
        eng = jnp.exp(-g)
        et = jnp.exp(gtot - g)
        rt_o[...] = r * jnp.exp(g)
        at_o[...] = -kk * jnp.exp(g - lw_d)
        bt_o[...] = (b_d * eng).astype(bt_o.dtype)
        kt_o[...] = (k_d * eng).astype(kt_o.dtype)
        bh_o[...] = b_d * et
        kh_o[...] = k_d * et
        eg = jnp.exp(gtot)
        for j in range(eg_o.shape[0]):
            eg_o[j] = eg[j * CHUNK:j * CHUNK + 1, :]


def _dot_exact_lhs_rhs(x, m):
    x1 = x.astype(BF16)
    r1 = x - x1.astype(F32)
    x2 = r1.astype(BF16)
    x3 = (r1 - x2.astype(F32)).astype(BF16)
    return _dot(x1, m) + (_dot(x2, m) + _dot(x3, m))


def _block_diag_ones():
    h = np.arange(BR) // HEAD_DIM
    return jnp.asarray((h[:, None] == h[None, :]).astype(np.float32), BF16)


def _rwkv_prepare(p_rw, lp):
    B, T, C = p_rw.shape
    tm = _tile(T, 256)
    main, prev, nxt = _halo_specs(tm, T, C)
    z = jnp.zeros((64, BR), F32)
    w2 = jnp.concatenate([jnp.concatenate([lp['rw_w2'][0], z], 1), jnp.concatenate([z, lp['rw_w2'][1]], 1)], 0)
    a2 = jnp.concatenate([jnp.concatenate([lp['rw_a2'][0], z], 1), jnp.concatenate([z, lp['rw_a2'][1]], 1)], 0)
    params = [lp['rw_mu'].reshape(1, C), lp['rw_w0'].reshape(1, 2 * BR), w2, lp['rw_a0'].reshape(1, 2 * BR), a2,
              lp['rw_g2'].astype(BF16), lp['rw_kk'].reshape(1, BR), lp['rw_ka'].reshape(1, BR),
              lp['rw_rk'].reshape(1, BR), _block_diag_ones()]
    pos = np.arange(tm)
    same = (pos[:, None] // CHUNK) == (pos[None, :] // CHUNK)
    params += [jnp.asarray((same & (pos[None, :] <= pos[:, None])).astype(np.float32), BF16),
               jnp.asarray((same & (pos[None, :] >= pos[:, None])).astype(np.float32), BF16)]
    pspecs = [pl.BlockSpec(a.shape, lambda b, i: (0, 0)) for a in params]
    ospec = pl.BlockSpec((None, tm, BR), lambda b, i: (b, i, 0))
    names = ('v', 'g', 'bonus') + tuple(n + str(d) for d in (0, 1) for n in ('rt', 'at', 'bt', 'kt', 'bh', 'kh', 'eg'))
    gram_dt = BF16 if RW_P_GRAM == 1 else F32
    egspec = pl.BlockSpec((None, tm // CHUNK, 1, BR), lambda b, i: (b, i, 0, 0))
    ospecs = [egspec if n[:2] == 'eg' else ospec for n in names]
    oshapes = [jax.ShapeDtypeStruct((B, T // CHUNK, 1, BR), F32) if n[:2] == 'eg' else
               jax.ShapeDtypeStruct((B, T, BR), gram_dt if n[:2] in ('bt', 'kt') else F32) for n in names]
    outs = pl.pallas_call(
        _rwkv_prep_kernel,
        grid=(B, T // tm),
        in_specs=[main, prev, nxt] + pspecs,
        out_specs=ospecs,
        out_shape=oshapes,
        compiler_params=_cp("parallel", "parallel"),
        name="rwkv_prepare",
    )(p_rw, p_rw, p_rw, *params)
    return dict(zip(names, outs))


def _dot3s(a, b, dims=None):
    if dims is None:
        dims = (((1,), (0,)), ((), ()))
    ca, cb = dims[0][0][0], dims[0][1][0]
    a_lo = a - a.astype(BF16).astype(F32)
    b_lo = b - b.astype(BF16).astype(F32)
    sa = jnp.concatenate([a, a_lo, a], axis=ca).astype(BF16)
    sb = jnp.concatenate([b, b, b_lo], axis=cb).astype(BF16)
    return _dot(sa, sb, dims)


def _mmp(a, b, dims, passes):
    return _dot3s(a, b, dims) if passes == 3 else _dot1(a, b, dims)


def _rwkv_chunks(chains, eye):
    n = range(len(chains))
    rt, at, bt, kt, bh, kh, eg, v, ht, mask_s, mask_i = zip(*chains)
    C = rt[0].shape[0]
    ar = [jnp.concatenate([at[i], rt[i]], 0) for i in n]
    gm = [_mmp(ar[i], jnp.concatenate([bt[i], kt[i]], 0), NT, RW_P_GRAM) for i in n]
    top = [jnp.where(mask_s[i], gm[i][:C], 0.0) for i in n]
    bot = [jnp.where(mask_i[i], gm[i][C:], 0.0) for i in n]
    lab = [top[i][:, :C] for i in n]
    pw = [_mmp(lab[i], lab[i], None, RW_P_INV) for i in n]
    tinv = [eye + lab[i] for i in n]
    for _ in range(int(math.log2(C)) - 2):
        both = [_mmp(pw[i], jnp.concatenate([pw[i], tinv[i]], 1), None, RW_P_INV) for i in n]
        pw = [both[i][:, :C] for i in n]
        tinv = [tinv[i] + both[i][:, C:] for i in n]
    last = [_mmp(pw[i], tinv[i], None, RW_P_INV) for i in n]
    tinv = [tinv[i] + last[i] for i in n]
    wy = [_mmp(ar[i], ht[i], NT, RW_P_STATE) for i in n]
    w1 = [_mmp(top[i][:, C:], v[i], None, RW_P_APPLY) for i in n]
    u = [_mmp(tinv[i], wy[i][:C] + w1[i], None, RW_P_STATE) for i in n]
    uv = [jnp.concatenate([u[i], v[i]], 0) for i in n]
    y1 = [_mmp(bot[i], uv[i], None, RW_P_APPLY) for i in n]
    hu = [_mmp(uv[i], jnp.concatenate([bh[i], kh[i]], 0), TN, RW_P_STATE) for i in n]
    return [(wy[i][C:] + y1[i], ht[i] * eg[i] + hu[i]) for i in n]


def _rwkv_scan_kernel(*refs):
    fwd_in, bwd_in = refs[0:8], refs[8:16]
    s0_ref, yf_o, yb_o, s_o, st = refs[16:]
    c = pl.program_id(0)
    nb, C = yf_o.shape[0], yf_o.shape[1]

    @pl.when(c == 0)
    def _():
        st[...] = s0_ref[...]

    row = lax.broadcasted_iota(jnp.int32, (C, 2 * C), 0)
    col = lax.broadcasted_iota(jnp.int32, (C, 2 * C), 1)
    col = jnp.where(col >= C, col - C, col)
    eye = (lax.broadcasted_iota(jnp.int32, (C, C), 0) == lax.broadcasted_iota(jnp.int32, (C, C), 1)).astype(F32)
    chains = []
    for b in range(nb):
        for d, ins in enumerate((fwd_in, bwd_in)):
            if d == 0:
                mask_i = col <= row
                mask_s = col < row
            else:
                mask_i = col >= row
                mask_s = col > row
            for h in range(N_HEADS):
                sl = pl.ds(h * HEAD_DIM, HEAD_DIM)
                rt, at, bt, kt, bh, kh, eg, v = (ref[b, :, sl] for ref in ins)
                chains.append((rt, at, bt, kt, bh, kh, eg, v, st[b, d, h], mask_s, mask_i))
    results = iter(_rwkv_chunks(chains, eye))
    for b in range(nb):
        for d, y_o in enumerate((yf_o, yb_o)):
            for h in range(N_HEADS):
                y, hn = next(results)
                y_o[b, :, pl.ds(h * HEAD_DIM, HEAD_DIM)] = y
                st[b, d, h] = hn

    @pl.when(c == pl.num_programs(0) - 1)
    def _():
        s_o[...] = st[...]


def _rwkv_scan(prep, s0):
    B, T, _ = prep['v'].shape
    nc = T // CHUNK
    fwd = pl.BlockSpec((B, CHUNK, BR), lambda c: (0, c, 0))
    bwd = pl.BlockSpec((B, CHUNK, BR), lambda c: (0, nc - 1 - c, 0))
    sspec = pl.BlockSpec(s0.shape, lambda c: (0, 0, 0, 0, 0))
    ins = [prep[n + str(d)] for d in (0, 1) for n in ('rt', 'at', 'bt', 'kt', 'bh', 'kh', 'eg')]
    return pl.pallas_call(
        _rwkv_scan_kernel,
        grid=(nc,),
        in_specs=[fwd] * 6 + [pl.BlockSpec((B, None, 1, BR), lambda c: (0, c, 0, 0)), fwd]
        + [bwd] * 6 + [pl.BlockSpec((B, None, 1, BR), lambda c: (0, nc - 1 - c, 0, 0)), bwd] + [sspec],
        out_specs=[fwd, bwd, sspec],
        out_shape=[jax.ShapeDtypeStruct((B, T, BR), F32), jax.ShapeDtypeStruct((B, T, BR), F32),
                   jax.ShapeDtypeStruct(s0.shape, F32)],
        scratch_shapes=[pltpu.VMEM(s0.shape, F32)],
        compiler_params=_cp("arbitrary"),
        name="rwkv_scan",
    )(*ins[:7], prep['v'], *ins[7:], prep['v'], s0)


def _rwkv_out_kernel(yf, yb, g, bonus, lnw, lnb, bd, o):
    y = yf[...] + yb[...]
    m = bd[...]
    inv_n = 1.0 / HEAD_DIM
    mean = _dot_exact_lhs_rhs(y, m) * inv_n
    yc = y - mean
    var = _dot_exact_lhs_rhs(yc * yc, m) * inv_n
    yn = yc * lax.rsqrt(var + RW_LN_EPS) * lnw[...] + lnb[...]
    o[...] = ((yn + bonus[...]) * g[...]).astype(o.dtype)


def _rwkv_output(yf, yb, prep, lp):
    B, T, _ = yf.shape
    tm = _tile(T, 512)
    spec = pl.BlockSpec((None, tm, BR), lambda b, i: (b, i, 0))
    return pl.pallas_call(
        _rwkv_out_kernel,
        grid=(B, T // tm),
        in_specs=[spec] * 4 + [_row_spec(BR), _row_spec(BR), pl.BlockSpec((BR, BR), lambda b, i: (0, 0))],
        out_specs=spec,
        out_shape=jax.ShapeDtypeStruct((B, T, BR), MIX_OUT),
        compiler_params=_cp("parallel", "parallel"),
        name="rwkv_output",
    )(yf, yb, prep['g'], prep['bonus'], lp['rw_ln_w'].reshape(1, BR), lp['rw_ln_b'].reshape(1, BR),
      _block_diag_ones())


def _hy_pre_kernel(p_ref, pv_ref, nx_ref, w_ref, b_ref, x0_o, z_o):
    w = w_ref[...]
    u = _conv3_tile(p_ref[...], pv_ref[...], nx_ref[...], w[0:1, :], w[1:2, :], w[2:3, :]) + b_ref[...]
    x0_o[...] = u[:, 0:BR]
    z_o[...] = u[:, 2 * BR:3 * BR] * u[:, BR:2 * BR]


def _hyena_pre(p_hy, lp):
    B, T, C = p_hy.shape
    tm = _tile(T, 512)
    main, prev, nxt = _halo_specs(tm, T, C)
    ospec = pl.BlockSpec((None, tm, BR), lambda b, i: (b, i, 0))
    return pl.pallas_call(
        _hy_pre_kernel,
        grid=(B, T // tm),
        in_specs=[main, prev, nxt, pl.BlockSpec((3, C), lambda b, i: (0, 0)), _row_spec(C)],
        out_specs=[ospec, ospec],
        out_shape=[jax.ShapeDtypeStruct((B, T, BR), F32)] * 2,
        compiler_params=_cp("parallel", "parallel"),
        name="hyena_pre",
    )(p_hy, p_hy, p_hy, lp['hy_conv_w'], lp['hy_conv_b'].reshape(1, C))


def _hy_filter_kernel(z_ref, t_ref, mf_ref, mb_ref, w1, b1, f1, w2, b2, f2, w3, dl, k_o, ss_o):
    h = jnp.sin(f1[...] * (_dot3s(z_ref[...], w1[...]) + b1[...]))
    h = jnp.sin(f2[...] * (_dot3s(h, w2[...]) + b2[...]))
    h = _dot1(h, w3[...]) * jnp.exp(-t_ref[...] * dl[...])
    kern = mf_ref[...] * h[:, :BR] + mb_ref[...] * h[:, BR:]
    k_o[...] = kern

    @pl.when(pl.program_id(0) == 0)
    def _():
        ss_o[...] = jnp.zeros_like(ss_o)

    ss_o[...] += jnp.sum(kern * kern, axis=0, keepdims=True)


def _hyena_filter(n, lp):
    t = jnp.linspace(0.0, 1.0, n, dtype=F32)
    bands = (HY_EMB - 1) // 2
    wpos = 2.0 * math.pi * jnp.arange(n, dtype=F32) / n
    fr = jnp.linspace(1e-4, bands - 1, bands, dtype=F32)
    t_ext = jnp.concatenate([t, t[::-1]])[:, None]
    fr_lane = jnp.concatenate([jnp.zeros((1,), F32), fr, fr, jnp.zeros((LANES - HY_EMB,), F32)])[None]
    ang = jnp.concatenate([wpos, wpos[::-1]])[:, None] * fr_lane
    lane = lax.broadcasted_iota(jnp.int32, ang.shape, 1)
    z_ext = jnp.where(lane == 0, t_ext, jnp.where(lane <= bands, jnp.cos(ang),
                                                  jnp.where(lane < HY_EMB, -jnp.sin(ang), 0.0)))
    rho = np.arange(2 * n)
    mf = jnp.asarray((rho < n).astype(np.float32))[:, None]
    mb = jnp.asarray((rho > n).astype(np.float32))[:, None]
    max_decay = math.log(HY_TARGET) / HY_FAST_DECAY
    min_decay = math.log(HY_TARGET) / HY_SLOW_DECAY
    deltas = jnp.abs(jnp.linspace(min_decay, max_decay, BR, dtype=F32))
    dl = jnp.tile(deltas, 2)[None]
    w1 = jnp.pad(lp['hy_w1'], ((0, LANES - HY_EMB), (0, 0)))
    params = [w1, lp['hy_b1'][None], lp['hy_f1'][None], lp['hy_w2'], lp['hy_b2'][None], lp['hy_f2'][None],
              lp['hy_w3'], dl]
    tm = _tile(2 * n, 512)
    return pl.pallas_call(
        _hy_filter_kernel,
        grid=(2 * n // tm,),
        in_specs=[pl.BlockSpec((tm, LANES), lambda i: (i, 0))] + [pl.BlockSpec((tm, 1), lambda i: (i, 0))] * 3
        + [pl.BlockSpec(a.shape, lambda i: (0, 0)) for a in params],
        out_specs=[pl.BlockSpec((tm, BR), lambda i: (i, 0)), pl.BlockSpec((1, BR), lambda i: (0, 0))],
        out_shape=[jax.ShapeDtypeStruct((2 * n, BR), F32), jax.ShapeDtypeStruct((1, BR), F32)],
        compiler_params=_cp("arbitrary"),
        name="hyena_filter",
    )(z_ext, t_ext, mf, mb, *params)


def _dft_tables(N1, N2):
    N = N1 * N2

    def mat(n):
        a = -2.0 * np.pi * np.outer(np.arange(n), np.arange(n)) / n
        return np.cos(a), np.sin(a)

    f1r, f1i = mat(N1)
    f2r, f2i = mat(N2)
    a = -2.0 * np.pi * np.outer(np.arange(N1), np.arange(N2)) / N
    return dict(f1r=f1r, f1i=f1i, f2r=f2r, f2i=f2i,
                twr=jnp.asarray(np.cos(a)[:, :, None], F32), twi=jnp.asarray(np.sin(a)[:, :, None], F32))


def _fft_in_kernel(passes, x_ref, frh, frl, fih, fil, ar_o, ai_o):
    x = x_ref[...]
    ar_o[...] = _dotc(frh[...], frl[...], x, passes).astype(ar_o.dtype)
    ai_o[...] = _dotc(fih[...], fil[...], x, passes).astype(ai_o.dtype)


def _fft_in(x2d, tabs, N1, passes):
    B, n1u, cols = x2d.shape
    consts = list(_np_split(tabs['f1r'][:, :n1u])) + list(_np_split(tabs['f1i'][:, :n1u]))
    tc = _tile(cols, 8192, LANES)
    ospec = pl.BlockSpec((None, N1, tc), lambda b, j: (b, 0, j))
    return pl.pallas_call(
        functools.partial(_fft_in_kernel, passes),
        grid=(B, cols // tc),
        in_specs=[pl.BlockSpec((None, n1u, tc), lambda b, j: (b, 0, j))]
        + [pl.BlockSpec((N1, n1u), lambda b, j: (0, 0))] * 4,
        out_specs=[ospec, ospec],
        out_shape=[jax.ShapeDtypeStruct((B, N1, cols), F32 if passes == 3 else BF16)] * 2,
        compiler_params=_cp("parallel", "parallel"),
        name="fft_first_axis",
    )(x2d, *consts)


def _cplx_dft(frh, frl, fih, fil, xr, xi, conj, passes):
    rr = _dotc(frh, frl, xr, passes)
    ii = _dotc(fih, fil, xi, passes)
    ri = _dotc(frh, frl, xi, passes)
    ir = _dotc(fih, fil, xr, passes)
    if conj:
        return rr + ii, ri - ir
    return rr - ii, ri + ir


def _fft_spec_kernel(ar, ai, twr, twi, frh, frl, fih, fil, sc, kr_o, ki_o):
    f = (frh[...], frl[...], fih[...], fil[...])
    for j in range(ar.shape[0]):
        tr, ti = twr[j], twi[j]
        xr = ar[j] * tr - ai[j] * ti
        xi = ar[j] * ti + ai[j] * tr
        br, bi = _cplx_dft(*f, xr, xi, False, FFT_FILTER_PASSES)
        kr_o[j] = br * sc[...]
        ki_o[j] = bi * sc[...]


def _fft_conv_kernel(ar, ai, twr, twi, frh, frl, fih, fil, kr, ki, dr_o, di_o):
    f = (frh[...], frl[...], fih[...], fil[...])
    rows = range(ar.shape[0])
    tw = [(twr[j], twi[j]) for j in rows]
    a = [(ar[j].astype(F32), ai[j].astype(F32)) for j in rows]
    x = [(a[j][0] * tw[j][0] - a[j][1] * tw[j][1], a[j][0] * tw[j][1] + a[j][1] * tw[j][0]) for j in rows]
    bf = [_cplx_dft(*f, x[j][0], x[j][1], False, FFT_DATA_PASSES) for j in rows]
    c = [(bf[j][0] * kr[j] - bf[j][1] * ki[j], bf[j][0] * ki[j] + bf[j][1] * kr[j]) for j in rows]
    d = [_cplx_dft(*f, c[j][0], c[j][1], True, FFT_DATA_PASSES) for j in rows]
    for j in rows:
        (dr, di), (tr, ti) = d[j], tw[j]
        dr_o[j] = (dr * tr + di * ti).astype(dr_o.dtype)
        di_o[j] = (di * tr - dr * ti).astype(di_o.dtype)


def _fft_mid(a_re, a_im, tabs, kf=None, scale=None):
    B, N1, N2, C = a_re.shape
    consts = list(_np_split(tabs['f2r'])) + list(_np_split(tabs['f2i']))
    kb = _tile(N1, FFT_ROWS_PER_STEP, 1)
    aspec = pl.BlockSpec((None, kb, N2, C), lambda b, k: (b, k, 0, 0))
    tspec = pl.BlockSpec((kb, N2, 1), lambda b, k: (k, 0, 0))
    fspec = pl.BlockSpec((N2, N2), lambda b, k: (0, 0))
    if kf is None:
        extra, especs, kern, nm = [scale], [_row_spec(C)], _fft_spec_kernel, "fft_filter_spectrum"
    else:
        kspec = pl.BlockSpec((None, kb, N2, C), lambda b, k: (0, k, 0, 0))
        extra, especs, kern, nm = list(kf), [kspec, kspec], _fft_conv_kernel, "fft_second_axis_conv"
    return pl.pallas_call(
        kern,
        grid=(B, N1 // kb),
        in_specs=[aspec, aspec, tspec, tspec] + [fspec] * 4 + especs,
        out_specs=[aspec, aspec],
        out_shape=[jax.ShapeDtypeStruct((B, N1, N2, C), a_re.dtype)] * 2,
        compiler_params=_cp("parallel", "parallel"),
        name=nm,
    )(a_re, a_im, tabs['twr'], tabs['twi'], *consts, *extra)


def _fft_out_kernel(dr, di, frh, frl, fih, fil, x0, z, bias, o):
    zc = (_dotc(frh[...], frl[...], dr[...].astype(F32), FFT_DATA_PASSES)
          + _dotc(fih[...], fil[...], di[...].astype(F32), FFT_DATA_PASSES))
    zz = z[...]
    o[...] = (x0[...] * (zc + zz * bias[...])).astype(o.dtype)


def _fft_out(d_re, d_im, tabs, x0_2d, z_2d, bias_row):
    B, N1, cols = d_re.shape
    n1u = x0_2d.shape[1]
    consts = list(_np_split(tabs['f1r'][:n1u, :])) + list(_np_split(tabs['f1i'][:n1u, :]))
    tc = _tile(cols, 8192, LANES)
    dspec = pl.BlockSpec((None, N1, tc), lambda b, j: (b, 0, j))
    xspec = pl.BlockSpec((None, n1u, tc), lambda b, j: (b, 0, j))
    return pl.pallas_call(
        _fft_out_kernel,
        grid=(B, cols // tc),
        in_specs=[dspec, dspec] + [pl.BlockSpec((n1u, N1), lambda b, j: (0, 0))] * 4
        + [xspec, xspec, pl.BlockSpec((1, tc), lambda b, j: (0, j))],
        out_specs=xspec,
        out_shape=jax.ShapeDtypeStruct((B, n1u, cols), MIX_OUT),
        compiler_params=_cp("parallel", "parallel"),
        name="fft_last_axis_gate",
    )(d_re, d_im, *consts, x0_2d, z_2d, bias_row)


def _hy_gate_kernel(zc, x0, z, bias, o):
    o[...] = (x0[...] * (zc[...] + z[...] * bias[...])).astype(o.dtype)


def _hyena(p_hy, lp):
    B, n, _ = p_hy.shape
    x0, z = _hyena_pre(p_hy, lp)
    kern, ss = _hyena_filter(n, lp)
    N = 2 * n
    N2 = min(256, n)
    N1 = N // N2
    tabs = _dft_tables(N1, N2)
    scale = lax.rsqrt(ss) * (1.0 / N)
    bias = lp['hy_bias'].reshape(1, BR)
    if N1 >= 4:
        n1u = N1 // 2
        k_re, k_im = _fft_in(kern.reshape(1, N1, N2 * BR), tabs, N1, 3)
        kf = _fft_mid(k_re.reshape(1, N1, N2, BR), k_im.reshape(1, N1, N2, BR), tabs, scale=scale)
        a_re, a_im = _fft_in(z.reshape(B, n1u, N2 * BR), tabs, N1, FFT_DATA_PASSES)
        d_re, d_im = _fft_mid(a_re.reshape(B, N1, N2, BR), a_im.reshape(B, N1, N2, BR), tabs, kf=kf)
        o = _fft_out(d_re.reshape(B, N1, N2 * BR), d_im.reshape(B, N1, N2 * BR), tabs,
                     x0.reshape(B, n1u, N2 * BR), z.reshape(B, n1u, N2 * BR), jnp.tile(bias, (1, N2)))
        return o.reshape(B, n, BR)
    tabs = _dft_tables(1, N)
    kf = _fft_mid(kern.reshape(1, 1, N, BR), jnp.zeros((1, 1, N, BR), F32), tabs, scale=scale)
    zp = jnp.pad(z, ((0, 0), (0, n), (0, 0))).reshape(B, 1, N, BR)
    d_re, _ = _fft_mid(zp, jnp.zeros_like(zp), tabs, kf=kf)
    zc = d_re.reshape(B, N, BR)[:, :n]
    tm = _tile(n, 512)
    spec = pl.BlockSpec((None, tm, BR), lambda b, i: (b, i, 0))
    return pl.pallas_call(
        _hy_gate_kernel,
        grid=(B, n // tm),
        in_specs=[spec, spec, spec, _row_spec(BR)],
        out_specs=spec,
        out_shape=jax.ShapeDtypeStruct((B, n, BR), MIX_OUT),
        compiler_params=_cp("parallel", "parallel"),
        name="hyena_gate",
    )(zc, x0, z, bias)


def _rope_tables(n_lat):
    t = lax.iota(jnp.int32, n_lat)
    row = (t // GRID_W).astype(F32)[:, None]
    col = (t % GRID_W).astype(F32)[:, None]
    half = MLA_ROPE // 2
    inv = ROPE_BASE ** (-jnp.arange(0, half, 2, dtype=F32) / half)
    j = np.arange(HP) % MLA_ROPE
    inv_lane = jnp.take(inv, jnp.asarray(j % (half // 2)))[None]
    ang = jnp.where(jnp.asarray(j < half)[None], row, col) * inv_lane
    return jnp.cos(ang), jnp.sin(ang)


def _rot_index():
    j = np.arange(32)
    first = (j % 16) < 8
    src = np.where(first, j + 8, j - 8)
    sign = np.where(first, -1.0, 1.0)
    return src, sign


def _take_cols(w, idx, sign=None):
    idx = np.asarray(idx)
    sg = np.where(idx >= 0, 1.0 if sign is None else np.asarray(sign, np.float64), 0.0)
    pieces, i, n = [], 0, len(idx)
    while i < n:
        j = i + 1
        if sg[i] == 0.0:
            while j < n and sg[j] == 0.0:
                j += 1
            pieces.append(jnp.zeros((w.shape[0], j - i), w.dtype))
        else:
            while j < n and sg[j] == sg[i] and idx[j] == idx[j - 1] + 1:
                j += 1
            piece = w[:, int(idx[i]):int(idx[i]) + (j - i)]
            pieces.append(piece if sg[i] == 1.0 else piece * sg[i])
        i = j
    return jnp.concatenate(pieces, axis=1)


def _kv_buffers(B, T, tm, n_kv, bufs, total_rows):
    qshape = jax.ShapeDtypeStruct((B, N_HEADS, T, HP), BF16)
    qspec = pl.BlockSpec((None, N_HEADS, tm, HP), lambda b, i: (b, 0, i, 0))
    if bufs is None:
        kvshape = jax.ShapeDtypeStruct((B, N_HEADS, total_rows, HP), BF16)
        return [], [qspec] * (1 + n_kv), [qshape] + [kvshape] * n_kv, []
    off = (bufs[0].shape[2] - T) // tm
    kvspec = pl.BlockSpec((None, N_HEADS, tm, HP), lambda b, i: (b, 0, off + i, 0))
    kvshape = jax.ShapeDtypeStruct(bufs[0].shape, BF16)
    return ([pl.BlockSpec(memory_space=pl.ANY)] * n_kv, [qspec] + [kvspec] * n_kv, [qshape] + [kvshape] * n_kv,
            list(bufs))


def _mla_prep_kernel(p_ref, cq_t, sq_t, ck_t, qn, kvn, wqc, wqs, wk, wv, epe, *rest):
    q_o, k_o, v_o = rest[-3:]
    p = p_ref[...].astype(F32)
    cq = p[:, 0:256]
    cqn = (cq * lax.rsqrt(jnp.sum(cq * cq, axis=-1, keepdims=True) * (1.0 / MLA_Q_RANK) + EPS) * qn[...]).astype(BF16)
    ckv = p[:, 256:384]
    ckvn = _rms(ckv, kvn[...]).astype(BF16)
    cq_all = jnp.concatenate([cq_t[...]] * N_HEADS, axis=1)
    sq_all = jnp.concatenate([sq_t[...]] * N_HEADS, axis=1)
    q = _dot(cqn, wqc[...]) * cq_all + _dot(cqn, wqs[...]) * sq_all
    pe = (p[:, 384:512] * ck_t[...]).astype(BF16)
    k = _dot(ckvn, wk[...]) + _dot(pe, epe[...])
    v = _dot(ckvn, wv[...])
    lane = lax.broadcasted_iota(jnp.int32, (p.shape[0], HP), 1)
    for h in range(N_HEADS):
        q_o[h] = q[:, h * HP:(h + 1) * HP].astype(BF16)
        k_o[h] = k[:, h * HP:(h + 1) * HP].astype(BF16)
        v_o[h] = jnp.where(lane == ONES_LANE, 1.0, v[:, h * HP:(h + 1) * HP]).astype(BF16)


def _mla_weights(lp):
    s = LOG2E / math.sqrt(MLA_NOPE + MLA_ROPE)
    src, sign = _rot_index()
    hd = MLA_NOPE + MLA_ROPE
    qc_idx, qs_idx, qs_sign, k_idx, v_idx = [], [], [], [], []
    for h in range(N_HEADS):
        qc_idx += list(h * hd + np.arange(hd)) + [-1] * (HP - hd)
        qs_idx += [-1] * MLA_NOPE + list(h * hd + MLA_NOPE + src) + [-1] * (HP - hd)
        qs_sign += [0.0] * MLA_NOPE + list(sign) + [0.0] * (HP - hd)
        k_idx += list(h * 128 + np.arange(MLA_NOPE)) + [-1] * (HP - MLA_NOPE)
        v_idx += list(h * 128 + MLA_NOPE + np.arange(64)) + [-1] * (HP - 64)
    wq = jnp.pad(lp['mla_wq_b'] * s, ((0, 256 - MLA_Q_RANK), (0, 0)))
    wqc = _take_cols(wq, qc_idx).astype(BF16)
    wqs = _take_cols(wq, qs_idx, qs_sign).astype(BF16)
    wk = _take_cols(lp['mla_wkv_b'], k_idx).astype(BF16)
    wv = _take_cols(lp['mla_wkv_b'], v_idx).astype(BF16)
    epe = np.zeros((128, N_HEADS * HP), np.float32)
    for h in range(N_HEADS):
        for j in range(MLA_ROPE):
            epe[j, h * HP + MLA_NOPE + j] = 1.0
            epe[MLA_ROPE + j, h * HP + MLA_NOPE + j] = 1.0
    qn = jnp.pad(lp['mla_q_norm'], (0, 256 - MLA_Q_RANK)).reshape(1, 256)
    return qn, lp['mla_kv_norm'].reshape(1, MLA_KV_RANK), wqc, wqs, wk, wv, jnp.asarray(epe, BF16)


def _mla_tables(cos, sin):
    lane = lax.broadcasted_iota(jnp.int32, cos.shape, 1)
    pe = (lane >= MLA_NOPE) & (lane < MLA_NOPE + MLA_ROPE)
    cq = jnp.where(lane < MLA_NOPE, 1.0, jnp.where(pe, cos, 0.0))
    sq = jnp.where(pe, sin, 0.0)
    ck = jnp.where(lane < MLA_ROPE, cos, jnp.where(lane < 2 * MLA_ROPE, sin, 0.0))
    return cq, sq, ck


def _mla_prep(p_mla, tables, weights, total_rows, bufs=None):
    B, T, C = p_mla.shape
    tm = _tile(T, 256)
    tspecs = [pl.BlockSpec((tm, HP), lambda b, i: (i, 0))] * 3
    wspecs = [pl.BlockSpec(w.shape, lambda b, i: (0, 0)) for w in weights]
    aspecs, ospecs, oshapes, extra = _kv_buffers(B, T, tm, 2, bufs, total_rows)
    n_in = 1 + len(tables) + len(weights)
    return pl.pallas_call(
        _mla_prep_kernel,
        grid=(B, T // tm),
        in_specs=[pl.BlockSpec((None, tm, C), lambda b, i: (b, i, 0))] + tspecs + wspecs + aspecs,
        out_specs=ospecs,
        out_shape=oshapes,
        input_output_aliases={n_in + j: 1 + j for j in range(len(extra))},
        compiler_params=_cp("parallel", "parallel"),
        name="mla_prepare",
    )(p_mla, *tables, *weights, *extra)


def _df_prep_kernel(p_ref, c_t, s_t, *rest):
    q_o, k1_o, k2_o, v_o = rest[-4:]
    p = p_ref[...].astype(F32)
    c = jnp.concatenate([c_t[...]] * 2, axis=1)
    s = jnp.concatenate([s_t[...]] * 2, axis=1)
    q = (p[:, 0:BR] * c + p[:, BR:2 * BR] * s) * (LOG2E / math.sqrt(DF_QK))
    k = p[:, 2 * BR:3 * BR] * c + p[:, 3 * BR:4 * BR] * s
    v = p[:, 4 * BR:5 * BR]
    lane = lax.broadcasted_iota(jnp.int32, (p.shape[0], HP), 1)
    for h in range(N_HEADS):
        pair = slice((h // 2) * HP, (h // 2 + 1) * HP)

        def head(x):
            return x[:, pair] if h % 2 == 0 else pltpu.roll(x[:, pair], HEAD_DIM, 1)

        kh = head(k)
        q_o[h] = jnp.where(lane < 2 * DF_QK, head(q), 0.0).astype(BF16)
        k1_o[h] = jnp.where(lane < DF_QK, kh, 0.0).astype(BF16)
        k2_o[h] = jnp.where((lane >= DF_QK) & (lane < 2 * DF_QK), kh, 0.0).astype(BF16)
        v_o[h] = jnp.where(lane == ONES_LANE, 1.0, jnp.where(lane < DF_V, head(v), 0.0)).astype(BF16)


def _df_tables(cos, sin):
    return cos, sin


def _df_prep(p_df, tables, total_rows, bufs=None):
    B, T, C = p_df.shape
    tm = _tile(T, 256)
    tspec = pl.BlockSpec((tm, HP), lambda b, i: (i, 0))
    aspecs, ospecs, oshapes, extra = _kv_buffers(B, T, tm, 3, bufs, total_rows)
    return pl.pallas_call(
        _df_prep_kernel,
        grid=(B, T // tm),
        in_specs=[pl.BlockSpec((None, tm, C), lambda b, i: (b, i, 0)), tspec, tspec] + aspecs,
        out_specs=ospecs,
        out_shape=oshapes,
        input_output_aliases={3 + j: 1 + j for j in range(len(extra))},
        compiler_params=_cp("parallel", "parallel"),
        name="diff_prepare",
    )(p_df, *tables, *extra)


def _softmax_pv(q, k_ref, v_ref):
    s = _dot(q, k_ref[...], NT)
    p = jnp.exp2((s - jnp.max(s, axis=-1, keepdims=True)).astype(BF16))
    acc = _dot(p, v_ref[...])
    return acc / acc[:, ONES_LANE:ONES_LANE + 1]


def _mla_attn_kernel(q_ref, k_ref, v_ref, o_ref):
    o = _softmax_pv(q_ref[...], k_ref, v_ref)
    lane = lax.broadcasted_iota(jnp.int32, o.shape, 1)
    o_ref[...] = jnp.where(lane < HEAD_DIM, o, 0.0).astype(o_ref.dtype)


def _attn_specs(q, key_rows):
    tq = _tile(q.shape[2], ATTN_TQ)
    first, n = key_rows
    assert first % n == 0
    qspec = pl.BlockSpec((None, None, tq, HP), lambda b, h, i: (b, h, i, 0))
    kspec = pl.BlockSpec((None, None, n, HP), lambda b, h, i: (b, h, first // n, 0))
    return tq, qspec, kspec


def _mla_attention(q, k, v, key_rows):
    B, H, Tq, _ = q.shape
    tq, qspec, kspec = _attn_specs(q, key_rows)
    return pl.pallas_call(
        _mla_attn_kernel,
        grid=(B, H, Tq // tq),
        in_specs=[qspec, kspec, kspec],
        out_specs=qspec,
        out_shape=jax.ShapeDtypeStruct((B, H, Tq, HP), MIX_OUT),
        compiler_params=_cp("parallel", "parallel", "parallel"),
        name="mla_attention",
    )(q, k, v)


def _df_attn_kernel(lam_init, q_ref, k1_ref, k2_ref, v_ref, lq1, lk1, lq2, lk2, sub, o_ref):
    q = q_ref[...]
    lam = (jnp.exp(jnp.sum(lq1[...] * lk1[...], axis=-1, keepdims=True))
           - jnp.exp(jnp.sum(lq2[...] * lk2[...], axis=-1, keepdims=True)) + lam_init)
    o = _softmax_pv(q, k1_ref, v_ref) - lam * _softmax_pv(q, k2_ref, v_ref)
    lane = lax.broadcasted_iota(jnp.int32, o.shape, 1)
    o = jnp.where(lane < DF_V, o, 0.0)
    ms = jnp.sum(o * o, axis=-1, keepdims=True) * (1.0 / DF_V)
    o_ref[...] = (o * lax.rsqrt(ms + DF_SUBLN_EPS) * sub[...] * (1.0 - lam_init)).astype(o_ref.dtype)


def _df_attention(q, k1, k2, v, key_rows, lp, lam_init):
    B, H, Tq, _ = q.shape
    tq, qspec, kspec = _attn_specs(q, key_rows)
    lspec = pl.BlockSpec((1, DF_QK), lambda b, h, i: (0, 0))
    sub = jnp.pad(lp['df_subln'], (0, HP - DF_V)).reshape(1, HP)
    return pl.pallas_call(
        functools.partial(_df_attn_kernel, lam_init),
        grid=(B, H, Tq // tq),
        in_specs=[qspec, kspec, kspec, kspec, lspec, lspec, lspec, lspec,
                  pl.BlockSpec((1, HP), lambda b, h, i: (0, 0))],
        out_specs=qspec,
        out_shape=jax.ShapeDtypeStruct((B, H, Tq, HP), MIX_OUT),
        compiler_params=_cp("parallel", "parallel", "parallel"),
        name="diff_attention",
    )(q, k1, k2, v, lp['df_lq1'].reshape(1, DF_QK), lp['df_lk1'].reshape(1, DF_QK),
      lp['df_lq2'].reshape(1, DF_QK), lp['df_lk2'].reshape(1, DF_QK), sub)


def _merge_kernel(x_ref, sh_ref, sc_ref, gt_ref, gpre_ref, gpost_ref, orw, ohy, omla, odf,
                  wg, bg, urw, uhy, umla, udf, wo, o_ref):
    rows = x_ref.shape[0] // MERGE_SPLIT
    grp = [pl.ds(i * rows, rows) for i in range(MERGE_SPLIT)]
    xs = [x_ref[g, :] for g in grp]
    hs = [(_rms(x, gpre_ref[...]) * (1.0 + sc_ref[...]) + sh_ref[...]).astype(BF16) for x in xs]
    accs = [None] * MERGE_SPLIT
    for n in range(4):
        for i, g in enumerate(grp):
            if n == 0:
                up = _dot1(orw[g, :], urw[...])
            elif n == 1:
                up = _dot1(ohy[g, :], uhy[...])
            else:
                o_ref_n, u_ref_n = (omla, umla) if n == 2 else (odf, udf)
                pairs = [o_ref_n[hh, g, :] + pltpu.roll(o_ref_n[hh + 1, g, :].astype(F32), HEAD_DIM, 1).astype(BF16)
                         for hh in range(0, N_HEADS, 2)]
                up = _dot(jnp.concatenate(pairs, axis=1), u_ref_n[...])
            t = _sigmoid(_dot(hs[i], wg[n]) + bg[n]) * up
            accs[i] = t if accs[i] is None else accs[i] + t
    ys = [_dot(acc.astype(BF16), wo[...]) for acc in accs]
    for i, g in enumerate(grp):
        o_ref[g, :] = xs[i] + gt_ref[...] * _rms(ys[i], gpost_ref[...])


def _merge(x, mod3, g_pre, g_post, outs, mw):
    B, T, D = x.shape
    o_rw, o_hy, o_mla, o_df = outs
    tm = _tile(T, 512)
    xspec = pl.BlockSpec((None, tm, D), lambda b, i: (b, i, 0))
    bspec = pl.BlockSpec((None, tm, BR), lambda b, i: (b, i, 0))
    hspec = pl.BlockSpec((None, N_HEADS, tm, HP), lambda b, i: (b, 0, i, 0))

    def full(a):
        nd = a.ndim
        return pl.BlockSpec(a.shape, lambda b, i: (0,) * nd)

    ws = [mw['wg'], mw['bg'], mw['urw'], mw['uhy'], mw['umla'], mw['udf'], mw['wo']]
    return pl.pallas_call(
        _merge_kernel,
        grid=(B, T // tm),
        in_specs=[xspec, _mod_spec(mod3[0], 2), _mod_spec(mod3[1], 2), _mod_spec(mod3[2], 2),
                  _row_spec(D), _row_spec(D), bspec, bspec, hspec, hspec] + [full(a) for a in ws],
        out_specs=xspec,
        out_shape=jax.ShapeDtypeStruct((B, T, D), F32),
        compiler_params=_cp("parallel", "parallel"),
        name="gated_merge",
    )(x, mod3[0], mod3[1], mod3[2], g_pre.reshape(1, D), g_post.reshape(1, D), o_rw, o_hy, o_mla, o_df, *ws)


def _mix_weights(w_in):
    src, sign = _rot_index()
    w_rw = w_in[:, :RW_COLS]
    w_hy = w_in[:, OFF_HY:OFF_HY + HY_COLS]
    kpe0 = OFF_MLA + MLA_Q_RANK + MLA_KV_RANK
    mla_idx = (list(OFF_MLA + np.arange(MLA_Q_RANK)) + [-1] * (256 - MLA_Q_RANK)
               + list(OFF_MLA + MLA_Q_RANK + np.arange(MLA_KV_RANK))
               + list(kpe0 + np.arange(MLA_ROPE)) + list(kpe0 + src) + [-1] * 64)
    mla_sign = [1.0] * 256 + [1.0] * MLA_KV_RANK + [1.0] * MLA_ROPE + list(sign) + [0.0] * 64
    w_mla = _take_cols(w_in, mla_idx, mla_sign)
    idx, sg = [], []
    for base, rot in ((OFF_DF, False), (OFF_DF, True), (OFF_DF + 256, False), (OFF_DF + 256, True),
                      (OFF_DF + 512, False)):
        for g in range(BR // DF_QK):
            idx += list(base + g * DF_QK + (src if rot else np.arange(DF_QK)))
            sg += list(sign) if rot else [1.0] * DF_QK
    w_df = _take_cols(w_in, idx, sg)
    return [w.astype(BF16) for w in (w_rw, w_hy, w_mla, w_df)]


def _merge_weights(lp):
    return dict(wg=lp['w_gate'].astype(BF16), bg=lp['b_gate'].reshape(4, 1, D_MODEL),
                urw=lp['w_up'][0].astype(BF16), uhy=lp['w_up'][1].astype(BF16),
                umla=lp['w_up'][2].astype(BF16), udf=lp['w_up'][3].astype(BF16),
                wo=lp['w_o'].astype(BF16))


def _token_mixing(x, xc, mod, modc, g_pre, lam_init, with_ctx_out, lp, rope):
    B, T, _ = x.shape
    ws = _mix_weights(lp['w_in'])
    p_rw, p_hy, p_mla, p_df = _project(x, mod[0], mod[1], g_pre, ws)
    c_rw, c_hy, c_mla, c_df = _project(xc, modc[0], modc[1], g_pre, ws)

    prep = _rwkv_prepare(p_rw, lp)
    prep_c = _rwkv_prepare(c_rw, lp)
    s0 = jnp.zeros((B, 2, N_HEADS, HEAD_DIM, HEAD_DIM), F32)
    yf_c, yb_c, s_c = _rwkv_scan(prep_c, s0)
    yf, yb, _ = _rwkv_scan(prep, s_c)
    o_rw = _rwkv_output(yf, yb, prep, lp)

    o_hy = _hyena(p_hy, lp)

    cos, sin = rope
    Tc = xc.shape[1]
    ones, zeros = jnp.ones((Tc, HP), F32), jnp.zeros((Tc, HP), F32)
    mla_w = _mla_weights(lp)
    q, k, v = _mla_prep(p_mla, _mla_tables(cos, sin), mla_w, T + Tc)
    qc, k, v = _mla_prep(c_mla, _mla_tables(ones, zeros), mla_w, T + Tc, bufs=(k, v))
    o_mla = _mla_attention(q, k, v, (0, T + Tc))

    dq, dk1, dk2, dv = _df_prep(p_df, _df_tables(cos, sin), T + Tc)
    dqc, dk1, dk2, dv = _df_prep(c_df, _df_tables(ones, zeros), T + Tc, bufs=(dk1, dk2, dv))
    o_df = _df_attention(dq, dk1, dk2, dv, (0, T + Tc), lp, lam_init)

    outs = (o_rw, o_hy, o_mla, o_df)
    if not with_ctx_out:
        return outs, None
    oc_rw = _rwkv_output(yf_c, yb_c, prep_c, lp)
    oc_hy = _hyena(c_hy, lp)
    oc_mla = _mla_attention(qc, k, v, (T, Tc))
    oc_df = _df_attention(dqc, dk1, dk2, dv, (T, Tc), lp, lam_init)
    return outs, (oc_rw, oc_hy, oc_mla, oc_df)


def kernel(x, c, ctx, c_ctx, w_mod, b_mod, norm_pre, norm_post, ffn_w_in, ffn_w_out, w_in, rw_mu, rw_w0, rw_w2, rw_a0, rw_a2, rw_g2, rw_kk, rw_ka, rw_rk, rw_ln_w, rw_ln_b, hy_conv_w, hy_conv_b, hy_w1, hy_b1, hy_f1, hy_w2, hy_b2, hy_f2, hy_w3, hy_bias, mla_q_norm, mla_wq_b, mla_kv_norm, mla_wkv_b, df_lq1, df_lk1, df_lq2, df_lk2, df_subln, w_up, w_gate, b_gate, w_o):
    params = dict(w_in=w_in, rw_mu=rw_mu, rw_w0=rw_w0, rw_w2=rw_w2, rw_a0=rw_a0, rw_a2=rw_a2, rw_g2=rw_g2,
                  rw_kk=rw_kk, rw_ka=rw_ka, rw_rk=rw_rk, rw_ln_w=rw_ln_w, rw_ln_b=rw_ln_b, hy_conv_w=hy_conv_w,
                  hy_conv_b=hy_conv_b, hy_w1=hy_w1, hy_b1=hy_b1, hy_f1=hy_f1, hy_w2=hy_w2, hy_b2=hy_b2,
                  hy_f2=hy_f2, hy_w3=hy_w3, hy_bias=hy_bias, mla_q_norm=mla_q_norm, mla_wq_b=mla_wq_b,
                  mla_kv_norm=mla_kv_norm, mla_wkv_b=mla_wkv_b, df_lq1=df_lq1, df_lk1=df_lk1, df_lq2=df_lq2,
                  df_lk2=df_lk2, df_subln=df_subln, w_up=w_up, w_gate=w_gate, b_gate=b_gate, w_o=w_o)
    B, T, D = x.shape
    depth = w_mod.shape[0]
    assert B <= 7 and D == D_MODEL and T % 128 == 0 and ctx.shape[1] % CHUNK == 0 and T % ctx.shape[1] == 0
    rope = _rope_tables(T)

    s_rows = jnp.concatenate([c, c_ctx[None], jnp.zeros((8 - B - 1, D), F32)], 0)
    mod_all = _modulation(s_rows, w_mod, b_mod).reshape(depth, 8, 3, 3, D)
    ffn_in = ffn_w_in.astype(BF16)
    ffn_out = ffn_w_out.astype(BF16)

    xc = ctx
    for l in range(depth):
        last = l == depth - 1
        lp = {k_: v_[l] for k_, v_ in params.items()}
        mod = [[mod_all[l, :B, s, m][:, None, :] for m in range(3)] for s in range(3)]
        modc = [[mod_all[l, B:B + 1, s, m][:, None, :] for m in range(3)] for s in range(3)]
        ffn_a = (norm_pre[l, 0], norm_post[l, 0], ffn_in[l, 0], ffn_out[l, 0])
        ffn_b = (norm_pre[l, 2], norm_post[l, 2], ffn_in[l, 1], ffn_out[l, 1])

        x = _ffn(x, mod[0], *ffn_a)
        xc = _ffn(xc, modc[0], *ffn_a)

        lam_init = 0.8 - 0.6 * math.exp(-0.3 * l)
        outs, outs_c = _token_mixing(x, xc, mod[1], modc[1], norm_pre[l, 1], lam_init, not last, lp, rope)
        mw = _merge_weights(lp)
        x = _merge(x, mod[1], norm_pre[l, 1], norm_post[l, 1], outs, mw)
        x = _ffn(x, mod[2], *ffn_b)
        if not last:
            xc = _merge(xc, modc[1], norm_pre[l, 1], norm_post[l, 1], outs_c, mw)
            xc = _ffn(xc, modc[2], *ffn_b)
    return x
```

```python
import functools
import math

import numpy as np
import jax
import jax.numpy as jnp
from jax import lax
from jax.experimental import pallas as pl
from jax.experimental.pallas import tpu as pltpu

F32 = jnp.float32
BF16 = jnp.bfloat16
MIX_OUT = BF16

D_MODEL = 1024
GRID_W = 64
N_HEADS = 4
HEAD_DIM = 64
BR = N_HEADS * HEAD_DIM
D_FF = 2816
MACARON_W = 0.5
EPS = 1e-6
ROPE_BASE = 10000.0
RW_LN_EPS = 64e-5
RW_COLS = 1152
HY_COLS = 768
HY_EMB = 33
HY_FAST_DECAY = 0.3
HY_SLOW_DECAY = 1.5
HY_TARGET = 1e-2
MLA_Q_RANK = 192
MLA_KV_RANK = 128
MLA_NOPE = 64
MLA_ROPE = 32
DF_QK = 32
DF_V = 64
DF_SUBLN_EPS = 1e-5
OFF_HY = RW_COLS
OFF_MLA = RW_COLS + HY_COLS
OFF_DF = OFF_MLA + MLA_Q_RANK + MLA_KV_RANK + MLA_ROPE

LANES = 128
HP = 128
ONES_LANE = 64
LOG2E = 1.4426950408889634
FFN_TM = 512
FFN_TF = 2816
ATTN_TQ = 256
FFT_FILTER_PASSES = 1
FFT_ROWS_PER_STEP = 4
MERGE_SPLIT = 2
FFT_DATA_PASSES = 1
CHUNK = 64
RW_P_GRAM = 1
RW_P_INV = 3
RW_P_STATE = 3
RW_P_APPLY = 1
VMEM_LIMIT = 56 * 1024 * 1024


def _cp(*sem):
    return pltpu.CompilerParams(dimension_semantics=sem, vmem_limit_bytes=VMEM_LIMIT)


def _tile(n, pref, mult=8):
    if n <= pref:
        return n
    t = (pref // mult) * mult
    while t > mult and n % t:
        t -= mult
    assert n % t == 0, (n, pref)
    return t


def _dot(a, b, dims=None):
    if dims is None:
        dims = (((a.ndim - 1,), (0,)), ((), ()))
    return lax.dot_general(a, b, dims, preferred_element_type=F32)


NT = (((1,), (1,)), ((), ()))
TN = (((0,), (0,)), ((), ()))


def _split(x):
    hi = x.astype(BF16)
    lo = (x - hi.astype(F32)).astype(BF16)
    return hi, lo


def _dot3(a, b, dims=None):
    ah, al = _split(a)
    bh, bl = _split(b)
    return _dot(ah, bh, dims) + (_dot(ah, bl, dims) + _dot(al, bh, dims))


def _dot1(a, b, dims=None):
    return _dot(a.astype(BF16), b.astype(BF16), dims)


def _dotc(ch, cl, x, passes=3):
    if passes == 1:
        return _dot(ch, x.astype(BF16))
    xh, xl = _split(x)
    return _dot(ch, xh) + (_dot(ch, xl) + _dot(cl, xh))


def _dot_exact_lhs(m, x):
    x1 = x.astype(BF16)
    r1 = x - x1.astype(F32)
    x2 = r1.astype(BF16)
    x3 = (r1 - x2.astype(F32)).astype(BF16)
    return _dot(m, x1) + (_dot(m, x2) + _dot(m, x3))


def _rms(x, g, eps=EPS):
    return x * lax.rsqrt(jnp.mean(x * x, axis=-1, keepdims=True) + eps) * g


def _sigmoid(x):
    return 1.0 / (1.0 + jnp.exp(-x))


def _silu(x):
    return x * _sigmoid(x)


def _softplus(z):
    return jnp.maximum(z, 0.0) + jnp.log(1.0 + jnp.exp(-jnp.abs(z)))


def _np_split(a):
    a = jnp.asarray(a, F32)
    hi = a.astype(BF16)
    lo = (a - hi.astype(F32)).astype(BF16)
    return hi, lo


def _mod_kernel(s_ref, w_ref, b_ref, o_ref):
    s = _silu(s_ref[...])
    o_ref[...] = _dot3(s, w_ref[...]) + b_ref[...]


def _modulation(s_rows, w_mod, b_mod):
    L, D, NM = w_mod.shape
    tn = _tile(NM, 1152, LANES)
    return pl.pallas_call(
        _mod_kernel,
        grid=(L, NM // tn),
        in_specs=[pl.BlockSpec((8, D), lambda l, j: (0, 0)),
                  pl.BlockSpec((None, D, tn), lambda l, j: (l, 0, j)),
                  pl.BlockSpec((None, 1, tn), lambda l, j: (l, 0, j))],
        out_specs=pl.BlockSpec((None, 8, tn), lambda l, j: (l, 0, j)),
        out_shape=jax.ShapeDtypeStruct((L, 8, NM), F32),
        compiler_params=_cp("parallel", "parallel"),
        name="modulation",
    )(s_rows, w_mod, b_mod.reshape(L, 1, NM))


def _mod_spec(m, nargs):
    if m.shape[0] == 1:
        return pl.BlockSpec((None, 1, D_MODEL), lambda *a: (0, 0, 0))
    return pl.BlockSpec((None, 1, D_MODEL), lambda *a: (a[0], 0, 0))


def _row_spec(n):
    return pl.BlockSpec((1, n), lambda *a: (0, 0))


def _ffn_kernel(x_ref, sh_ref, sc_ref, gt_ref, gpre_ref, gpost_ref, wa_ref, wb_ref, wo_ref, o_ref, h_sc, acc_sc):
    f = pl.program_id(2)

    @pl.when(f == 0)
    def _():
        h = _rms(x_ref[...], gpre_ref[...]) * (1.0 + sc_ref[...]) + sh_ref[...]
        h_sc[...] = h.astype(BF16)
        acc_sc[...] = jnp.zeros_like(acc_sc)

    h = h_sc[...]
    a = _dot(h, wa_ref[...])
    b = _dot(h, wb_ref[...])
    acc_sc[...] += _dot((_silu(a) * b).astype(BF16), wo_ref[...])

    @pl.when(f == pl.num_programs(2) - 1)
    def _():
        o_ref[...] = x_ref[...] + MACARON_W * gt_ref[...] * _rms(acc_sc[...], gpost_ref[...])


def _ffn(x, mod3, g_pre, g_post, w_in, w_out):
    B, T, D = x.shape
    F = w_out.shape[0]
    tm = _tile(T, FFN_TM)
    tf = _tile(F, FFN_TF, LANES)
    nf = F // tf
    wbuf = dict(pipeline_mode=pl.Buffered(1)) if nf == 1 else {}
    xspec = pl.BlockSpec((None, tm, D), lambda b, i, f: (b, i, 0))
    return pl.pallas_call(
        _ffn_kernel,
        grid=(B, T // tm, nf),
        in_specs=[xspec, _mod_spec(mod3[0], 3), _mod_spec(mod3[1], 3), _mod_spec(mod3[2], 3),
                  _row_spec(D), _row_spec(D),
                  pl.BlockSpec((D, tf), lambda b, i, f: (0, f), **wbuf),
                  pl.BlockSpec((D, tf), lambda b, i, f: (0, nf + f), **wbuf),
                  pl.BlockSpec((tf, D), lambda b, i, f: (f, 0), **wbuf)],
        out_specs=xspec,
        out_shape=jax.ShapeDtypeStruct((B, T, D), F32),
        scratch_shapes=[pltpu.VMEM((tm, D), BF16), pltpu.VMEM((tm, D), F32)],
        compiler_params=_cp("parallel", "parallel", "arbitrary"),
        name="ffn",
    )(x, mod3[0], mod3[1], mod3[2], g_pre.reshape(1, D), g_post.reshape(1, D), w_in, w_in, w_out)


def _proj_kernel(x_ref, sh_ref, sc_ref, gpre_ref, wrw, why, wmla, wdf, orw, ohy, omla, odf):
    h = (_rms(x_ref[...], gpre_ref[...]) * (1.0 + sc_ref[...]) + sh_ref[...]).astype(BF16)
    orw[...] = _dot(h, wrw[...])
    ohy[...] = _dot(h, why[...])
    omla[...] = _dot(h, wmla[...]).astype(omla.dtype)
    odf[...] = _dot(h, wdf[...]).astype(odf.dtype)


def _project(x, shift, scale, g_pre, ws):
    B, T, D = x.shape
    tm = _tile(T, 256)
    xspec = pl.BlockSpec((None, tm, D), lambda b, i: (b, i, 0))
    wspecs = [pl.BlockSpec(w.shape, lambda b, i: (0, 0)) for w in ws]
    ospecs = [pl.BlockSpec((None, tm, w.shape[1]), lambda b, i: (b, i, 0)) for w in ws]
    oshapes = [jax.ShapeDtypeStruct((B, T, w.shape[1]), dt) for w, dt in zip(ws, (F32, F32, BF16, BF16))]
    return pl.pallas_call(
        _proj_kernel,
        grid=(B, T // tm),
        in_specs=[xspec, _mod_spec(shift, 2), _mod_spec(scale, 2), _row_spec(D)] + wspecs,
        out_specs=ospecs,
        out_shape=oshapes,
        compiler_params=_cp("parallel", "parallel"),
        name="mix_project",
    )(x, shift, scale, g_pre.reshape(1, D), *ws)


def _halo_specs(tm, T, C):
    nb8 = T // 8
    r = tm // 8
    main = pl.BlockSpec((None, tm, C), lambda b, i: (b, i, 0))
    prev = pl.BlockSpec((None, 8, C), lambda b, i: (b, jnp.maximum(i * r - 1, 0), 0))
    nxt = pl.BlockSpec((None, 8, C), lambda b, i: (b, jnp.minimum((i + 1) * r, nb8 - 1), 0))
    return main, prev, nxt


def _conv3_tile(p, prev8, next8, w0, w1, w2):
    i = pl.program_id(1)
    n_i = pl.num_programs(1)
    tm = p.shape[0]
    row = lax.broadcasted_iota(jnp.int32, p.shape, 0)
    prev_row = jnp.where(i > 0, prev8[7:8, :], 0.0)
    next_row = jnp.where(i < n_i - 1, next8[0:1, :], 0.0)
    pm = jnp.where(row == 0, prev_row, pltpu.roll(p, 1, 0))
    pp = jnp.where(row == tm - 1, next_row, pltpu.roll(p, tm - 1, 0))
    return pm * w0 + p * w1 + pp * w2


def _rwkv_prep_kernel(p_ref, pv_ref, nx_ref, mu_ref, w0_ref, w2_ref, a0_ref, a2_ref, g2_ref, kk_ref, ka_ref,
                      rk_ref, bd_ref, trif_ref, trib_ref, v_o, g_o, bonus_o, *dir_outs):
    mu = mu_ref[...]
    p = _conv3_tile(p_ref[...], pv_ref[...], nx_ref[...], 0.5 * mu, 1.0 - mu, 0.5 * mu)
    r = p[:, 0:BR]
    k = p[:, BR:2 * BR]
    v = p[:, 2 * BR:3 * BR]
    g_a = p[:, 3 * BR:3 * BR + 128]
    wa = p[:, 896:1024]
    aa = p[:, 1024:1152]
    bd = bd_ref[...]
    kkraw = k * kk_ref[...]
    ss = _dot_exact_lhs_rhs(kkraw * kkraw, bd)
    kk = kkraw / jnp.maximum(jnp.sqrt(ss), 1e-12)
    wl = w0_ref[...] + _dot3s(jnp.tanh(wa), w2_ref[...])
    lw = -jnp.exp(-_softplus(-wl) - 0.5)
    a = _sigmoid(a0_ref[...] + _dot3s(aa, a2_ref[...]))
    a_f = a[:, :BR]
    a_b = a[:, BR:]
    ka = ka_ref[...]
    k_f = k * (1.0 + (a_f - 1.0) * ka)
    k_b = k * (1.0 + (a_b - 1.0) * ka)
    v_o[...] = v
    g_o[...] = _dot1(_sigmoid(g_a), g2_ref[...])
    bonus_o[...] = _dot_exact_lhs_rhs(r * (k_f + k_b) * rk_ref[...], bd) * v
    for d, (tri_ref, k_d, a_d) in enumerate(((trif_ref, k_f, a_f), (trib_ref, k_b, a_b))):
        rt_o, at_o, bt_o, kt_o, bh_o, kh_o, eg_o = dir_outs[7 * d:7 * d + 7]
        lw_d = lw[:, d * BR:(d + 1) * BR]
        b_d = kk * a_d
        g = _dot_exact_lhs(tri_ref[...], lw_d)
        g4 = g.reshape(g.shape[0] // CHUNK, CHUNK, BR)
        edge = g4[:, CHUNK - 1:CHUNK, :] if d == 0 else g4[:, 0:1, :]
        gtot = jnp.broadcast_to(edge, g4.shape).reshape(g.shape)
        eng = jnp.exp(-g)
        et = jnp.exp(gtot - g)
        rt_o[...] = r * jnp.exp(g)
        at_o[...] = -kk * jnp.exp(g - lw_d)
        bt_o[...] = (b_d * eng).astype(bt_o.dtype)
        kt_o[...] = (k_d * eng).astype(kt_o.dtype)
        bh_o[...] = b_d * et
        kh_o[...] = k_d * et
        eg = jnp.exp(gtot)
        for j in range(eg_o.shape[0]):
            eg_o[j] = eg[j * CHUNK:j * CHUNK + 1, :]


def _dot_exact_lhs_rhs(x, m):
    x1 = x.astype(BF16)
    r1 = x - x1.astype(F32)
    x2 = r1.astype(BF16)
    x3 = (r1 - x2.astype(F32)).astype(BF16)
    return _dot(x1, m) + (_dot(x2, m) + _dot(x3, m))


def _block_diag_ones():
    h = np.arange(BR) // HEAD_DIM
    return jnp.asarray((h[:, None] == h[None, :]).astype(np.float32), BF16)


def _rwkv_prepare(p_rw, lp):
    B, T, C = p_rw.shape
    tm = _tile(T, 256)
    main, prev, nxt = _halo_specs(tm, T, C)
    z = jnp.zeros((64, BR), F32)
    w2 = jnp.concatenate([jnp.concatenate([lp['rw_w2'][0], z], 1), jnp.concatenate([z, lp['rw_w2'][1]], 1)], 0)
    a2 = jnp.concatenate([jnp.concatenate([lp['rw_a2'][0], z], 1), jnp.concatenate([z, lp['rw_a2'][1]], 1)], 0)
    params = [lp['rw_mu'].reshape(1, C), lp['rw_w0'].reshape(1, 2 * BR), w2, lp['rw_a0'].reshape(1, 2 * BR), a2,
              lp['rw_g2'].astype(BF16), lp['rw_kk'].reshape(1, BR), lp['rw_ka'].reshape(1, BR),
              lp['rw_rk'].reshape(1, BR), _block_diag_ones()]
    pos = np.arange(tm)
    same = (pos[:, None] // CHUNK) == (pos[None, :] // CHUNK)
    params += [jnp.asarray((same & (pos[None, :] <= pos[:, None])).astype(np.float32), BF16),
               jnp.asarray((same & (pos[None, :] >= pos[:, None])).astype(np.float32), BF16)]
    pspecs = [pl.BlockSpec(a.shape, lambda b, i: (0, 0)) for a in params]
    ospec = pl.BlockSpec((None, tm, BR), lambda b, i: (b, i, 0))
    names = ('v', 'g', 'bonus') + tuple(n + str(d) for d in (0, 1) for n in ('rt', 'at', 'bt', 'kt', 'bh', 'kh', 'eg'))
    gram_dt = BF16 if RW_P_GRAM == 1 else F32
    egspec = pl.BlockSpec((None, tm // CHUNK, 1, BR), lambda b, i: (b, i, 0, 0))
    ospecs = [egspec if n[:2] == 'eg' else ospec for n in names]
    oshapes = [jax.ShapeDtypeStruct((B, T // CHUNK, 1, BR), F32) if n[:2] == 'eg' else
               jax.ShapeDtypeStruct((B, T, BR), gram_dt if n[:2] in ('bt', 'kt') else F32) for n in names]
    outs = pl.pallas_call(
        _rwkv_prep_kernel,
        grid=(B, T // tm),
        in_specs=[main, prev, nxt] + pspecs,
        out_specs=ospecs,
        out_shape=oshapes,
        compiler_params=_cp("parallel", "parallel"),
        name="rwkv_prepare",
    )(p_rw, p_rw, p_rw, *params)
    return dict(zip(names, outs))


def _dot3s(a, b, dims=None):
    if dims is None:
        dims = (((1,), (0,)), ((), ()))
    ca, cb = dims[0][0][0], dims[0][1][0]
    a_lo = a - a.astype(BF16).astype(F32)
    b_lo = b - b.astype(BF16).astype(F32)
    sa = jnp.concatenate([a, a_lo, a], axis=ca).astype(BF16)
    sb = jnp.concatenate([b, b, b_lo], axis=cb).astype(BF16)
    return _dot(sa, sb, dims)


def _mmp(a, b, dims, passes):
    return _dot3s(a, b, dims) if passes == 3 else _dot1(a, b, dims)


def _rwkv_chunks(chains, eye):
    n = range(len(chains))
    rt, at, bt, kt, bh, kh, eg, v, ht, mask_s, mask_i = zip(*chains)
    C = rt[0].shape[0]
    ar = [jnp.concatenate([at[i], rt[i]], 0) for i in n]
    gm = [_mmp(ar[i], jnp.concatenate([bt[i], kt[i]], 0), NT, RW_P_GRAM) for i in n]
    top = [jnp.where(mask_s[i], gm[i][:C], 0.0) for i in n]
    bot = [jnp.where(mask_i[i], gm[i][C:], 0.0) for i in n]
    lab = [top[i][:, :C] for i in n]
    pw = [_mmp(lab[i], lab[i], None, RW_P_INV) for i in n]
    tinv = [eye + lab[i] for i in n]
    for _ in range(int(math.log2(C)) - 2):
        both = [_mmp(pw[i], jnp.concatenate([pw[i], tinv[i]], 1), None, RW_P_INV) for i in n]
        pw = [both[i][:, :C] for i in n]
        tinv = [tinv[i] + both[i][:, C:] for i in n]
    last = [_mmp(pw[i], tinv[i], None, RW_P_INV) for i in n]
    tinv = [tinv[i] + last[i] for i in n]
    wy = [_mmp(ar[i], ht[i], NT, RW_P_STATE) for i in n]
    w1 = [_mmp(top[i][:, C:], v[i], None, RW_P_APPLY) for i in n]
    u = [_mmp(tinv[i], wy[i][:C] + w1[i], None, RW_P_STATE) for i in n]
    uv = [jnp.concatenate([u[i], v[i]], 0) for i in n]
    y1 = [_mmp(bot[i], uv[i], None, RW_P_APPLY) for i in n]
    hu = [_mmp(uv[i], jnp.concatenate([bh[i], kh[i]], 0), TN, RW_P_STATE) for i in n]
    return [(wy[i][C:] + y1[i], ht[i] * eg[i] + hu[i]) for i in n]


def _rwkv_scan_kernel(*refs):
    fwd_in, bwd_in = refs[0:8], refs[8:16]
    s0_ref, yf_o, yb_o, s_o, st = refs[16:]
    c = pl.program_id(0)
    nb, C = yf_o.shape[0], yf_o.shape[1]

    @pl.when(c == 0)
    def _():
        st[...] = s0_ref[...]

    row = lax.broadcasted_iota(jnp.int32, (C, 2 * C), 0)
    col = lax.broadcasted_iota(jnp.int32, (C, 2 * C), 1)
    col = jnp.where(col >= C, col - C, col)
    eye = (lax.broadcasted_iota(jnp.int32, (C, C), 0) == lax.broadcasted_iota(jnp.int32, (C, C), 1)).astype(F32)
    chains = []
    for b in range(nb):
        for d, ins in enumerate((fwd_in, bwd_in)):
            if d == 0:
                mask_i = col <= row
                mask_s = col < row
            else:
                mask_i = col >= row
                mask_s = col > row
            for h in range(N_HEADS):
                sl = pl.ds(h * HEAD_DIM, HEAD_DIM)
                rt, at, bt, kt, bh, kh, eg, v = (ref[b, :, sl] for ref in ins)
                chains.append((rt, at, bt, kt, bh, kh, eg, v, st[b, d, h], mask_s, mask_i))
    results = iter(_rwkv_chunks(chains, eye))
    for b in range(nb):
        for d, y_o in enumerate((yf_o, yb_o)):
            for h in range(N_HEADS):
                y, hn = next(results)
                y_o[b, :, pl.ds(h * HEAD_DIM, HEAD_DIM)] = y
                st[b, d, h] = hn

    @pl.when(c == pl.num_programs(0) - 1)
    def _():
        s_o[...] = st[...]


def _rwkv_scan(prep, s0):
    B, T, _ = prep['v'].shape
    nc = T // CHUNK
    fwd = pl.BlockSpec((B, CHUNK, BR), lambda c: (0, c, 0))
    bwd = pl.BlockSpec((B, CHUNK, BR), lambda c: (0, nc - 1 - c, 0))
    sspec = pl.BlockSpec(s0.shape, lambda c: (0, 0, 0, 0, 0))
    ins = [prep[n + str(d)] for d in (0, 1) for n in ('rt', 'at', 'bt', 'kt', 'bh', 'kh', 'eg')]
    return pl.pallas_call(
        _rwkv_scan_kernel,
        grid=(nc,),
        in_specs=[fwd] * 6 + [pl.BlockSpec((B, None, 1, BR), lambda c: (0, c, 0, 0)), fwd]
        + [bwd] * 6 + [pl.BlockSpec((B, None, 1, BR), lambda c: (0, nc - 1 - c, 0, 0)), bwd] + [sspec],
        out_specs=[fwd, bwd, sspec],
        out_shape=[jax.ShapeDtypeStruct((B, T, BR), F32), jax.ShapeDtypeStruct((B, T, BR), F32),
                   jax.ShapeDtypeStruct(s0.shape, F32)],
        scratch_shapes=[pltpu.VMEM(s0.shape, F32)],
        compiler_params=_cp("arbitrary"),
        name="rwkv_scan",
    )(*ins[:7], prep['v'], *ins[7:], prep['v'], s0)


def _rwkv_out_kernel(yf, yb, g, bonus, lnw, lnb, bd, o):
    y = yf[...] + yb[...]
    m = bd[...]
    inv_n = 1.0 / HEAD_DIM
    mean = _dot_exact_lhs_rhs(y, m) * inv_n
    yc = y - mean
    var = _dot_exact_lhs_rhs(yc * yc, m) * inv_n
    yn = yc * lax.rsqrt(var + RW_LN_EPS) * lnw[...] + lnb[...]
    o[...] = ((yn + bonus[...]) * g[...]).astype(o.dtype)


def _rwkv_output(yf, yb, prep, lp):
    B, T, _ = yf.shape
    tm = _tile(T, 512)
    spec = pl.BlockSpec((None, tm, BR), lambda b, i: (b, i, 0))
    return pl.pallas_call(
        _rwkv_out_kernel,
        grid=(B, T // tm),
        in_specs=[spec] * 4 + [_row_spec(BR), _row_spec(BR), pl.BlockSpec((BR, BR), lambda b, i: (0, 0))],
        out_specs=spec,
        out_shape=jax.ShapeDtypeStruct((B, T, BR), MIX_OUT),
        compiler_params=_cp("parallel", "parallel"),
        name="rwkv_output",
    )(yf, yb, prep['g'], prep['bonus'], lp['rw_ln_w'].reshape(1, BR), lp['rw_ln_b'].reshape(1, BR),
      _block_diag_ones())


def _hy_pre_kernel(p_ref, pv_ref, nx_ref, w_ref, b_ref, x0_o, z_o):
    w = w_ref[...]
    u = _conv3_tile(p_ref[...], pv_ref[...], nx_ref[...], w[0:1, :], w[1:2, :], w[2:3, :]) + b_ref[...]
    x0_o[...] = u[:, 0:BR]
    z_o[...] = u[:, 2 * BR:3 * BR] * u[:, BR:2 * BR]


def _hyena_pre(p_hy, lp):
    B, T, C = p_hy.shape
    tm = _tile(T, 512)
    main, prev, nxt = _halo_specs(tm, T, C)
    ospec = pl.BlockSpec((None, tm, BR), lambda b, i: (b, i, 0))
    return pl.pallas_call(
        _hy_pre_kernel,
        grid=(B, T // tm),
        in_specs=[main, prev, nxt, pl.BlockSpec((3, C), lambda b, i: (0, 0)), _row_spec(C)],
        out_specs=[ospec, ospec],
        out_shape=[jax.ShapeDtypeStruct((B, T, BR), F32)] * 2,
        compiler_params=_cp("parallel", "parallel"),
        name="hyena_pre",
    )(p_hy, p_hy, p_hy, lp['hy_conv_w'], lp['hy_conv_b'].reshape(1, C))


def _hy_filter_kernel(z_ref, t_ref, mf_ref, mb_ref, w1, b1, f1, w2, b2, f2, w3, dl, k_o, ss_o):
    h = jnp.sin(f1[...] * (_dot3s(z_ref[...], w1[...]) + b1[...]))
    h = jnp.sin(f2[...] * (_dot3s(h, w2[...]) + b2[...]))
    h = _dot1(h, w3[...]) * jnp.exp(-t_ref[...] * dl[...])
    kern = mf_ref[...] * h[:, :BR] + mb_ref[...] * h[:, BR:]
    k_o[...] = kern

    @pl.when(pl.program_id(0) == 0)
    def _():
        ss_o[...] = jnp.zeros_like(ss_o)

    ss_o[...] += jnp.sum(kern * kern, axis=0, keepdims=True)


def _hyena_filter(n, lp):
    t = jnp.linspace(0.0, 1.0, n, dtype=F32)
    bands = (HY_EMB - 1) // 2
    wpos = 2.0 * math.pi * jnp.arange(n, dtype=F32) / n
    fr = jnp.linspace(1e-4, bands - 1, bands, dtype=F32)
    t_ext = jnp.concatenate([t, t[::-1]])[:, None]
    fr_lane = jnp.concatenate([jnp.zeros((1,), F32), fr, fr, jnp.zeros((LANES - HY_EMB,), F32)])[None]
    ang = jnp.concatenate([wpos, wpos[::-1]])[:, None] * fr_lane
    lane = lax.broadcasted_iota(jnp.int32, ang.shape, 1)
    z_ext = jnp.where(lane == 0, t_ext, jnp.where(lane <= bands, jnp.cos(ang),
                                                  jnp.where(lane < HY_EMB, -jnp.sin(ang), 0.0)))
    rho = np.arange(2 * n)
    mf = jnp.asarray((rho < n).astype(np.float32))[:, None]
    mb = jnp.asarray((rho > n).astype(np.float32))[:, None]
    max_decay = math.log(HY_TARGET) / HY_FAST_DECAY
    min_decay = math.log(HY_TARGET) / HY_SLOW_DECAY
    deltas = jnp.abs(jnp.linspace(min_decay, max_decay, BR, dtype=F32))
    dl = jnp.tile(deltas, 2)[None]
    w1 = jnp.pad(lp['hy_w1'], ((0, LANES - HY_EMB), (0, 0)))
    params = [w1, lp['hy_b1'][None], lp['hy_f1'][None], lp['hy_w2'], lp['hy_b2'][None], lp['hy_f2'][None],
              lp['hy_w3'], dl]
    tm = _tile(2 * n, 512)
    return pl.pallas_call(
        _hy_filter_kernel,
        grid=(2 * n // tm,),
        in_specs=[pl.BlockSpec((tm, LANES), lambda i: (i, 0))] + [pl.BlockSpec((tm, 1), lambda i: (i, 0))] * 3
        + [pl.BlockSpec(a.shape, lambda i: (0, 0)) for a in params],
        out_specs=[pl.BlockSpec((tm, BR), lambda i: (i, 0)), pl.BlockSpec((1, BR), lambda i: (0, 0))],
        out_shape=[jax.ShapeDtypeStruct((2 * n, BR), F32), jax.ShapeDtypeStruct((1, BR), F32)],
        compiler_params=_cp("arbitrary"),
        name="hyena_filter",
    )(z_ext, t_ext, mf, mb, *params)


def _dft_tables(N1, N2):
    N = N1 * N2

    def mat(n):
        a = -2.0 * np.pi * np.outer(np.arange(n), np.arange(n)) / n
        return np.cos(a), np.sin(a)

    f1r, f1i = mat(N1)
    f2r, f2i = mat(N2)
    a = -2.0 * np.pi * np.outer(np.arange(N1), np.arange(N2)) / N
    return dict(f1r=f1r, f1i=f1i, f2r=f2r, f2i=f2i,
                twr=jnp.asarray(np.cos(a)[:, :, None], F32), twi=jnp.asarray(np.sin(a)[:, :, None], F32))


def _fft_in_kernel(passes, x_ref, frh, frl, fih, fil, ar_o, ai_o):
    x = x_ref[...]
    ar_o[...] = _dotc(frh[...], frl[...], x, passes).astype(ar_o.dtype)
    ai_o[...] = _dotc(fih[...], fil[...], x, passes).astype(ai_o.dtype)


def _fft_in(x2d, tabs, N1, passes):
    B, n1u, cols = x2d.shape
    consts = list(_np_split(tabs['f1r'][:, :n1u])) + list(_np_split(tabs['f1i'][:, :n1u]))
    tc = _tile(cols, 8192, LANES)
    ospec = pl.BlockSpec((None, N1, tc), lambda b, j: (b, 0, j))
    return pl.pallas_call(
        functools.partial(_fft_in_kernel, passes),
        grid=(B, cols // tc),
        in_specs=[pl.BlockSpec((None, n1u, tc), lambda b, j: (b, 0, j))]
        + [pl.BlockSpec((N1, n1u), lambda b, j: (0, 0))] * 4,
        out_specs=[ospec, ospec],
        out_shape=[jax.ShapeDtypeStruct((B, N1, cols), F32 if passes == 3 else BF16)] * 2,
        compiler_params=_cp("parallel", "parallel"),
        name="fft_first_axis",
    )(x2d, *consts)


def _cplx_dft(frh, frl, fih, fil, xr, xi, conj, passes):
    rr = _dotc(frh, frl, xr, passes)
    ii = _dotc(fih, fil, xi, passes)
    ri = _dotc(frh, frl, xi, passes)
    ir = _dotc(fih, fil, xr, passes)
    if conj:
        return rr + ii, ri - ir
    return rr - ii, ri + ir


def _fft_spec_kernel(ar, ai, twr, twi, frh, frl, fih, fil, sc, kr_o, ki_o):
    f = (frh[...], frl[...], fih[...], fil[...])
    for j in range(ar.shape[0]):
        tr, ti = twr[j], twi[j]
        xr = ar[j] * tr - ai[j] * ti
        xi = ar[j] * ti + ai[j] * tr
        br, bi = _cplx_dft(*f, xr, xi, False, FFT_FILTER_PASSES)
        kr_o[j] = br * sc[...]
        ki_o[j] = bi * sc[...]


def _fft_conv_kernel(ar, ai, twr, twi, frh, frl, fih, fil, kr, ki, dr_o, di_o):
    f = (frh[...], frl[...], fih[...], fil[...])
    rows = range(ar.shape[0])
    tw = [(twr[j], twi[j]) for j in rows]
    a = [(ar[j].astype(F32), ai[j].astype(F32)) for j in rows]
    x = [(a[j][0] * tw[j][0] - a[j][1] * tw[j][1], a[j][0] * tw[j][1] + a[j][1] * tw[j][0]) for j in rows]
    bf = [_cplx_dft(*f, x[j][0], x[j][1], False, FFT_DATA_PASSES) for j in rows]
    c = [(bf[j][0] * kr[j] - bf[j][1] * ki[j], bf[j][0] * ki[j] + bf[j][1] * kr[j]) for j in rows]
    d = [_cplx_dft(*f, c[j][0], c[j][1], True, FFT_DATA_PASSES) for j in rows]
    for j in rows:
        (dr, di), (tr, ti) = d[j], tw[j]
        dr_o[j] = (dr * tr + di * ti).astype(dr_o.dtype)
        di_o[j] = (di * tr - dr * ti).astype(di_o.dtype)


def _fft_mid(a_re, a_im, tabs, kf=None, scale=None):
    B, N1, N2, C = a_re.shape
    consts = list(_np_split(tabs['f2r'])) + list(_np_split(tabs['f2i']))
    kb = _tile(N1, FFT_ROWS_PER_STEP, 1)
    aspec = pl.BlockSpec((None, kb, N2, C), lambda b, k: (b, k, 0, 0))
    tspec = pl.BlockSpec((kb, N2, 1), lambda b, k: (k, 0, 0))
    fspec = pl.BlockSpec((N2, N2), lambda b, k: (0, 0))
    if kf is None:
        extra, especs, kern, nm = [scale], [_row_spec(C)], _fft_spec_kernel, "fft_filter_spectrum"
    else:
        kspec = pl.BlockSpec((None, kb, N2, C), lambda b, k: (0, k, 0, 0))
        extra, especs, kern, nm = list(kf), [kspec, kspec], _fft_conv_kernel, "fft_second_axis_conv"
    return pl.pallas_call(
        kern,
        grid=(B, N1 // kb),
        in_specs=[aspec, aspec, tspec, tspec] + [fspec] * 4 + especs,
        out_specs=[aspec, aspec],
        out_shape=[jax.ShapeDtypeStruct((B, N1, N2, C), a_re.dtype)] * 2,
        compiler_params=_cp("parallel", "parallel"),
        name=nm,
    )(a_re, a_im, tabs['twr'], tabs['twi'], *consts, *extra)


def _fft_out_kernel(dr, di, frh, frl, fih, fil, x0, z, bias, o):
    zc = (_dotc(frh[...], frl[...], dr[...].astype(F32), FFT_DATA_PASSES)
          + _dotc(fih[...], fil[...], di[...].astype(F32), FFT_DATA_PASSES))
    zz = z[...]
    o[...] = (x0[...] * (zc + zz * bias[...])).astype(o.dtype)


def _fft_out(d_re, d_im, tabs, x0_2d, z_2d, bias_row):
    B, N1, cols = d_re.shape
    n1u = x0_2d.shape[1]
    consts = list(_np_split(tabs['f1r'][:n1u, :])) + list(_np_split(tabs['f1i'][:n1u, :]))
    tc = _tile(cols, 8192, LANES)
    dspec = pl.BlockSpec((None, N1, tc), lambda b, j: (b, 0, j))
    xspec = pl.BlockSpec((None, n1u, tc), lambda b, j: (b, 0, j))
    return pl.pallas_call(
        _fft_out_kernel,
        grid=(B, cols // tc),
        in_specs=[dspec, dspec] + [pl.BlockSpec((n1u, N1), lambda b, j: (0, 0))] * 4
        + [xspec, xspec, pl.BlockSpec((1, tc), lambda b, j: (0, j))],
        out_specs=xspec,
        out_shape=jax.ShapeDtypeStruct((B, n1u, cols), MIX_OUT),
        compiler_params=_cp("parallel", "parallel"),
        name="fft_last_axis_gate",
    )(d_re, d_im, *consts, x0_2d, z_2d, bias_row)


def _hy_gate_kernel(zc, x0, z, bias, o):
    o[...] = (x0[...] * (zc[...] + z[...] * bias[...])).astype(o.dtype)


def _hyena(p_hy, lp):
    B, n, _ = p_hy.shape
    x0, z = _hyena_pre(p_hy, lp)
    kern, ss = _hyena_filter(n, lp)
    N = 2 * n
    N2 = min(256, n)
    N1 = N // N2
    tabs = _dft_tables(N1, N2)
    scale = lax.rsqrt(ss) * (1.0 / N)
    bias = lp['hy_bias'].reshape(1, BR)
    if N1 >= 4:
        n1u = N1 // 2
        k_re, k_im = _fft_in(kern.reshape(1, N1, N2 * BR), tabs, N1, 3)
        kf = _fft_mid(k_re.reshape(1, N1, N2, BR), k_im.reshape(1, N1, N2, BR), tabs, scale=scale)
        a_re, a_im = _fft_in(z.reshape(B, n1u, N2 * BR), tabs, N1, FFT_DATA_PASSES)
        d_re, d_im = _fft_mid(a_re.reshape(B, N1, N2, BR), a_im.reshape(B, N1, N2, BR), tabs, kf=kf)
        o = _fft_out(d_re.reshape(B, N1, N2 * BR), d_im.reshape(B, N1, N2 * BR), tabs,
                     x0.reshape(B, n1u, N2 * BR), z.reshape(B, n1u, N2 * BR), jnp.tile(bias, (1, N2)))
        return o.reshape(B, n, BR)
    tabs = _dft_tables(1, N)
    kf = _fft_mid(kern.reshape(1, 1, N, BR), jnp.zeros((1, 1, N, BR), F32), tabs, scale=scale)
    zp = jnp.pad(z, ((0, 0), (0, n), (0, 0))).reshape(B, 1, N, BR)
    d_re, _ = _fft_mid(zp, jnp.zeros_like(zp), tabs, kf=kf)
    zc = d_re.reshape(B, N, BR)[:, :n]
    tm = _tile(n, 512)
    spec = pl.BlockSpec((None, tm, BR), lambda b, i: (b, i, 0))
    return pl.pallas_call(
        _hy_gate_kernel,
        grid=(B, n // tm),
        in_specs=[spec, spec, spec, _row_spec(BR)],
        out_specs=spec,
        out_shape=jax.ShapeDtypeStruct((B, n, BR), MIX_OUT),
        compiler_params=_cp("parallel", "parallel"),
        name="hyena_gate",
    )(zc, x0, z, bias)


def _rope_tables(n_lat):
    t = lax.iota(jnp.int32, n_lat)
    row = (t // GRID_W).astype(F32)[:, None]
    col = (t % GRID_W).astype(F32)[:, None]
    half = MLA_ROPE // 2
    inv = ROPE_BASE ** (-jnp.arange(0, half, 2, dtype=F32) / half)
    j = np.arange(HP) % MLA_ROPE
    inv_lane = jnp.take(inv, jnp.asarray(j % (half // 2)))[None]
    ang = jnp.where(jnp.asarray(j < half)[None], row, col) * inv_lane
    return jnp.cos(ang), jnp.sin(ang)


def _rot_index():
    j = np.arange(32)
    first = (j % 16) < 8
    src = np.where(first, j + 8, j - 8)
    sign = np.where(first, -1.0, 1.0)
    return src, sign


def _take_cols(w, idx, sign=None):
    idx = np.asarray(idx)
    sg = np.where(idx >= 0, 1.0 if sign is None else np.asarray(sign, np.float64), 0.0)
    pieces, i, n = [], 0, len(idx)
    while i < n:
        j = i + 1
        if sg[i] == 0.0:
            while j < n and sg[j] == 0.0:
                j += 1
            pieces.append(jnp.zeros((w.shape[0], j - i), w.dtype))
        else:
            while j < n and sg[j] == sg[i] and idx[j] == idx[j - 1] + 1:
                j += 1
            piece = w[:, int(idx[i]):int(idx[i]) + (j - i)]
            pieces.append(piece if sg[i] == 1.0 else piece * sg[i])
        i = j
    return jnp.concatenate(pieces, axis=1)


def _kv_buffers(B, T, tm, n_kv, bufs, total_rows):
    qshape = jax.ShapeDtypeStruct((B, N_HEADS, T, HP), BF16)
    qspec = pl.BlockSpec((None, N_HEADS, tm, HP), lambda b, i: (b, 0, i, 0))
    if bufs is None:
        kvshape = jax.ShapeDtypeStruct((B, N_HEADS, total_rows, HP), BF16)
        return [], [qspec] * (1 + n_kv), [qshape] + [kvshape] * n_kv, []
    off = (bufs[0].shape[2] - T) // tm
    kvspec = pl.BlockSpec((None, N_HEADS, tm, HP), lambda b, i: (b, 0, off + i, 0))
    kvshape = jax.ShapeDtypeStruct(bufs[0].shape, BF16)
    return ([pl.BlockSpec(memory_space=pl.ANY)] * n_kv, [qspec] + [kvspec] * n_kv, [qshape] + [kvshape] * n_kv,
            list(bufs))


def _mla_prep_kernel(p_ref, cq_t, sq_t, ck_t, qn, kvn, wqc, wqs, wk, wv, epe, *rest):
    q_o, k_o, v_o = rest[-3:]
    p = p_ref[...].astype(F32)
    cq = p[:, 0:256]
    cqn = (cq * lax.rsqrt(jnp.sum(cq * cq, axis=-1, keepdims=True) * (1.0 / MLA_Q_RANK) + EPS) * qn[...]).astype(BF16)
    ckv = p[:, 256:384]
    ckvn = _rms(ckv, kvn[...]).astype(BF16)
    cq_all = jnp.concatenate([cq_t[...]] * N_HEADS, axis=1)
    sq_all = jnp.concatenate([sq_t[...]] * N_HEADS, axis=1)
    q = _dot(cqn, wqc[...]) * cq_all + _dot(cqn, wqs[...]) * sq_all
    pe = (p[:, 384:512] * ck_t[...]).astype(BF16)
    k = _dot(ckvn, wk[...]) + _dot(pe, epe[...])
    v = _dot(ckvn, wv[...])
    lane = lax.broadcasted_iota(jnp.int32, (p.shape[0], HP), 1)
    for h in range(N_HEADS):
        q_o[h] = q[:, h * HP:(h + 1) * HP].astype(BF16)
        k_o[h] = k[:, h * HP:(h + 1) * HP].astype(BF16)
        v_o[h] = jnp.where(lane == ONES_LANE, 1.0, v[:, h * HP:(h + 1) * HP]).astype(BF16)


def _mla_weights(lp):
    s = LOG2E / math.sqrt(MLA_NOPE + MLA_ROPE)
    src, sign = _rot_index()
    hd = MLA_NOPE + MLA_ROPE
    qc_idx, qs_idx, qs_sign, k_idx, v_idx = [], [], [], [], []
    for h in range(N_HEADS):
        qc_idx += list(h * hd + np.arange(hd)) + [-1] * (HP - hd)
        qs_idx += [-1] * MLA_NOPE + list(h * hd + MLA_NOPE + src) + [-1] * (HP - hd)
        qs_sign += [0.0] * MLA_NOPE + list(sign) + [0.0] * (HP - hd)
        k_idx += list(h * 128 + np.arange(MLA_NOPE)) + [-1] * (HP - MLA_NOPE)
        v_idx += list(h * 128 + MLA_NOPE + np.arange(64)) + [-1] * (HP - 64)
    wq = jnp.pad(lp['mla_wq_b'] * s, ((0, 256 - MLA_Q_RANK), (0, 0)))
    wqc = _take_cols(wq, qc_idx).astype(BF16)
    wqs = _take_cols(wq, qs_idx, qs_sign).astype(BF16)
    wk = _take_cols(lp['mla_wkv_b'], k_idx).astype(BF16)
    wv = _take_cols(lp['mla_wkv_b'], v_idx).astype(BF16)
    epe = np.zeros((128, N_HEADS * HP), np.float32)
    for h in range(N_HEADS):
        for j in range(MLA_ROPE):
            epe[j, h * HP + MLA_NOPE + j] = 1.0
            epe[MLA_ROPE + j, h * HP + MLA_NOPE + j] = 1.0
    qn = jnp.pad(lp['mla_q_norm'], (0, 256 - MLA_Q_RANK)).reshape(1, 256)
    return qn, lp['mla_kv_norm'].reshape(1, MLA_KV_RANK), wqc, wqs, wk, wv, jnp.asarray(epe, BF16)


def _mla_tables(cos, sin):
    lane = lax.broadcasted_iota(jnp.int32, cos.shape, 1)
    pe = (lane >= MLA_NOPE) & (lane < MLA_NOPE + MLA_ROPE)
    cq = jnp.where(lane < MLA_NOPE, 1.0, jnp.where(pe, cos, 0.0))
    sq = jnp.where(pe, sin, 0.0)
    ck = jnp.where(lane < MLA_ROPE, cos, jnp.where(lane < 2 * MLA_ROPE, sin, 0.0))
    return cq, sq, ck


def _mla_prep(p_mla, tables, weights, total_rows, bufs=None):
    B, T, C = p_mla.shape
    tm = _tile(T, 256)
    tspecs = [pl.BlockSpec((tm, HP), lambda b, i: (i, 0))] * 3
    wspecs = [pl.BlockSpec(w.shape, lambda b, i: (0, 0)) for w in weights]
    aspecs, ospecs, oshapes, extra = _kv_buffers(B, T, tm, 2, bufs, total_rows)
    n_in = 1 + len(tables) + len(weights)
    return pl.pallas_call(
        _mla_prep_kernel,
        grid=(B, T // tm),
        in_specs=[pl.BlockSpec((None, tm, C), lambda b, i: (b, i, 0))] + tspecs + wspecs + aspecs,
        out_specs=ospecs,
        out_shape=oshapes,
        input_output_aliases={n_in + j: 1 + j for j in range(len(extra))},
        compiler_params=_cp("parallel", "parallel"),
        name="mla_prepare",
    )(p_mla, *tables, *weights, *extra)


def _df_prep_kernel(p_ref, c_t, s_t, *rest):
    q_o, k1_o, k2_o, v_o = rest[-4:]
    p = p_ref[...].astype(F32)
    c = jnp.concatenate([c_t[...]] * 2, axis=1)
    s = jnp.concatenate([s_t[...]] * 2, axis=1)
    q = (p[:, 0:BR] * c + p[:, BR:2 * BR] * s) * (LOG2E / math.sqrt(DF_QK))
    k = p[:, 2 * BR:3 * BR] * c + p[:, 3 * BR:4 * BR] * s
    v = p[:, 4 * BR:5 * BR]
    lane = lax.broadcasted_iota(jnp.int32, (p.shape[0], HP), 1)
    for h in range(N_HEADS):
        pair = slice((h // 2) * HP, (h // 2 + 1) * HP)

        def head(x):
            return x[:, pair] if h % 2 == 0 else pltpu.roll(x[:, pair], HEAD_DIM, 1)

        kh = head(k)
        q_o[h] = jnp.where(lane < 2 * DF_QK, head(q), 0.0).astype(BF16)
        k1_o[h] = jnp.where(lane < DF_QK, kh, 0.0).astype(BF16)
        k2_o[h] = jnp.where((lane >= DF_QK) & (lane < 2 * DF_QK), kh, 0.0).astype(BF16)
        v_o[h] = jnp.where(lane == ONES_LANE, 1.0, jnp.where(lane < DF_V, head(v), 0.0)).astype(BF16)


def _df_tables(cos, sin):
    return cos, sin


def _df_prep(p_df, tables, total_rows, bufs=None):
    B, T, C = p_df.shape
    tm = _tile(T, 256)
    tspec = pl.BlockSpec((tm, HP), lambda b, i: (i, 0))
    aspecs, ospecs, oshapes, extra = _kv_buffers(B, T, tm, 3, bufs, total_rows)
    return pl.pallas_call(
        _df_prep_kernel,
        grid=(B, T // tm),
        in_specs=[pl.BlockSpec((None, tm, C), lambda b, i: (b, i, 0)), tspec, tspec] + aspecs,
        out_specs=ospecs,
        out_shape=oshapes,
        input_output_aliases={3 + j: 1 + j for j in range(len(extra))},
        compiler_params=_cp("parallel", "parallel"),
        name="diff_prepare",
    )(p_df, *tables, *extra)


def _softmax_pv(q, k_ref, v_ref):
    s = _dot(q, k_ref[...], NT)
    p = jnp.exp2((s - jnp.max(s, axis=-1, keepdims=True)).astype(BF16))
    acc = _dot(p, v_ref[...])
    return acc / acc[:, ONES_LANE:ONES_LANE + 1]


def _mla_attn_kernel(q_ref, k_ref, v_ref, o_ref):
    o = _softmax_pv(q_ref[...], k_ref, v_ref)
    lane = lax.broadcasted_iota(jnp.int32, o.shape, 1)
    o_ref[...] = jnp.where(lane < HEAD_DIM, o, 0.0).astype(o_ref.dtype)


def _attn_specs(q, key_rows):
    tq = _tile(q.shape[2], ATTN_TQ)
    first, n = key_rows
    assert first % n == 0
    qspec = pl.BlockSpec((None, None, tq, HP), lambda b, h, i: (b, h, i, 0))
    kspec = pl.BlockSpec((None, None, n, HP), lambda b, h, i: (b, h, first // n, 0))
    return tq, qspec, kspec


def _mla_attention(q, k, v, key_rows):
    B, H, Tq, _ = q.shape
    tq, qspec, kspec = _attn_specs(q, key_rows)
    return pl.pallas_call(
        _mla_attn_kernel,
        grid=(B, H, Tq // tq),
        in_specs=[qspec, kspec, kspec],
        out_specs=qspec,
        out_shape=jax.ShapeDtypeStruct((B, H, Tq, HP), MIX_OUT),
        compiler_params=_cp("parallel", "parallel", "parallel"),
        name="mla_attention",
    )(q, k, v)


def _df_attn_kernel(lam_init, q_ref, k1_ref, k2_ref, v_ref, lq1, lk1, lq2, lk2, sub, o_ref):
    q = q_ref[...]
    lam = (jnp.exp(jnp.sum(lq1[...] * lk1[...], axis=-1, keepdims=True))
           - jnp.exp(jnp.sum(lq2[...] * lk2[...], axis=-1, keepdims=True)) + lam_init)
    o = _softmax_pv(q, k1_ref, v_ref) - lam * _softmax_pv(q, k2_ref, v_ref)
    lane = lax.broadcasted_iota(jnp.int32, o.shape, 1)
    o = jnp.where(lane < DF_V, o, 0.0)
    ms = jnp.sum(o * o, axis=-1, keepdims=True) * (1.0 / DF_V)
    o_ref[...] = (o * lax.rsqrt(ms + DF_SUBLN_EPS) * sub[...] * (1.0 - lam_init)).astype(o_ref.dtype)


def _df_attention(q, k1, k2, v, key_rows, lp, lam_init):
    B, H, Tq, _ = q.shape
    tq, qspec, kspec = _attn_specs(q, key_rows)
    lspec = pl.BlockSpec((1, DF_QK), lambda b, h, i: (0, 0))
    sub = jnp.pad(lp['df_subln'], (0, HP - DF_V)).reshape(1, HP)
    return pl.pallas_call(
        functools.partial(_df_attn_kernel, lam_init),
        grid=(B, H, Tq // tq),
        in_specs=[qspec, kspec, kspec, kspec, lspec, lspec, lspec, lspec,
                  pl.BlockSpec((1, HP), lambda b, h, i: (0, 0))],
        out_specs=qspec,
        out_shape=jax.ShapeDtypeStruct((B, H, Tq, HP), MIX_OUT),
        compiler_params=_cp("parallel", "parallel", "parallel"),
        name="diff_attention",
    )(q, k1, k2, v, lp['df_lq1'].reshape(1, DF_QK), lp['df_lk1'].reshape(1, DF_QK),
      lp['df_lq2'].reshape(1, DF_QK), lp['df_lk2'].reshape(1, DF_QK), sub)


def _merge_kernel(x_ref, sh_ref, sc_ref, gt_ref, gpre_ref, gpost_ref, orw, ohy, omla, odf,
                  wg, bg, urw, uhy, umla, udf, wo, o_ref):
    rows = x_ref.shape[0] // MERGE_SPLIT
    grp = [pl.ds(i * rows, rows) for i in range(MERGE_SPLIT)]
    xs = [x_ref[g, :] for g in grp]
    hs = [(_rms(x, gpre_ref[...]) * (1.0 + sc_ref[...]) + sh_ref[...]).astype(BF16) for x in xs]
    accs = [None] * MERGE_SPLIT
    for n in range(4):
        for i, g in enumerate(grp):
            if n == 0:
                up = _dot1(orw[g, :], urw[...])
            elif n == 1:
                up = _dot1(ohy[g, :], uhy[...])
            else:
                o_ref_n, u_ref_n = (omla, umla) if n == 2 else (odf, udf)
                pairs = [o_ref_n[hh, g, :] + pltpu.roll(o_ref_n[hh + 1, g, :].astype(F32), HEAD_DIM, 1).astype(BF16)
                         for hh in range(0, N_HEADS, 2)]
                up = _dot(jnp.concatenate(pairs, axis=1), u_ref_n[...])
            t = _sigmoid(_dot(hs[i], wg[n]) + bg[n]) * up
            accs[i] = t if accs[i] is None else accs[i] + t
    ys = [_dot(acc.astype(BF16), wo[...]) for acc in accs]
    for i, g in enumerate(grp):
        o_ref[g, :] = xs[i] + gt_ref[...] * _rms(ys[i], gpost_ref[...])


def _merge(x, mod3, g_pre, g_post, outs, mw):
    B, T, D = x.shape
    o_rw, o_hy, o_mla, o_df = outs
    tm = _tile(T, 512)
    xspec = pl.BlockSpec((None, tm, D), lambda b, i: (b, i, 0))
    bspec = pl.BlockSpec((None, tm, BR), lambda b, i: (b, i, 0))
    hspec = pl.BlockSpec((None, N_HEADS, tm, HP), lambda b, i: (b, 0, i, 0))

    def full(a):
        nd = a.ndim
        return pl.BlockSpec(a.shape, lambda b, i: (0,) * nd)

    ws = [mw['wg'], mw['bg'], mw['urw'], mw['uhy'], mw['umla'], mw['udf'], mw['wo']]
    return pl.pallas_call(
        _merge_kernel,
        grid=(B, T // tm),
        in_specs=[xspec, _mod_spec(mod3[0], 2), _mod_spec(mod3[1], 2), _mod_spec(mod3[2], 2),
                  _row_spec(D), _row_spec(D), bspec, bspec, hspec, hspec] + [full(a) for a in ws],
        out_specs=xspec,
        out_shape=jax.ShapeDtypeStruct((B, T, D), F32),
        compiler_params=_cp("parallel", "parallel"),
        name="gated_merge",
    )(x, mod3[0], mod3[1], mod3[2], g_pre.reshape(1, D), g_post.reshape(1, D), o_rw, o_hy, o_mla, o_df, *ws)


def _mix_weights(w_in):
    src, sign = _rot_index()
    w_rw = w_in[:, :RW_COLS]
    w_hy = w_in[:, OFF_HY:OFF_HY + HY_COLS]
    kpe0 = OFF_MLA + MLA_Q_RANK + MLA_KV_RANK
    mla_idx = (list(OFF_MLA + np.arange(MLA_Q_RANK)) + [-1] * (256 - MLA_Q_RANK)
               + list(OFF_MLA + MLA_Q_RANK + np.arange(MLA_KV_RANK))
               + list(kpe0 + np.arange(MLA_ROPE)) + list(kpe0 + src) + [-1] * 64)
    mla_sign = [1.0] * 256 + [1.0] * MLA_KV_RANK + [1.0] * MLA_ROPE + list(sign) + [0.0] * 64
    w_mla = _take_cols(w_in, mla_idx, mla_sign)
    idx, sg = [], []
    for base, rot in ((OFF_DF, False), (OFF_DF, True), (OFF_DF + 256, False), (OFF_DF + 256, True),
                      (OFF_DF + 512, False)):
        for g in range(BR // DF_QK):
            idx += list(base + g * DF_QK + (src if rot else np.arange(DF_QK)))
            sg += list(sign) if rot else [1.0] * DF_QK
    w_df = _take_cols(w_in, idx, sg)
    return [w.astype(BF16) for w in (w_rw, w_hy, w_mla, w_df)]


def _merge_weights(lp):
    return dict(wg=lp['w_gate'].astype(BF16), bg=lp['b_gate'].reshape(4, 1, D_MODEL),
                urw=lp['w_up'][0].astype(BF16), uhy=lp['w_up'][1].astype(BF16),
                umla=lp['w_up'][2].astype(BF16), udf=lp['w_up'][3].astype(BF16),
                wo=lp['w_o'].astype(BF16))


def _token_mixing(x, xc, mod, modc, g_pre, lam_init, with_ctx_out, lp, rope):
    B, T, _ = x.shape
    ws = _mix_weights(lp['w_in'])
    p_rw, p_hy, p_mla, p_df = _project(x, mod[0], mod[1], g_pre, ws)
    c_rw, c_hy, c_mla, c_df = _project(xc, modc[0], modc[1], g_pre, ws)

    prep = _rwkv_prepare(p_rw, lp)
    prep_c = _rwkv_prepare(c_rw, lp)
    s0 = jnp.zeros((B, 2, N_HEADS, HEAD_DIM, HEAD_DIM), F32)
    yf_c, yb_c, s_c = _rwkv_scan(prep_c, s0)
    yf, yb, _ = _rwkv_scan(prep, s_c)
    o_rw = _rwkv_output(yf, yb, prep, lp)

    o_hy = _hyena(p_hy, lp)

    cos, sin = rope
    Tc = xc.shape[1]
    ones, zeros = jnp.ones((Tc, HP), F32), jnp.zeros((Tc, HP), F32)
    mla_w = _mla_weights(lp)
    q, k, v = _mla_prep(p_mla, _mla_tables(cos, sin), mla_w, T + Tc)
    qc, k, v = _mla_prep(c_mla, _mla_tables(ones, zeros), mla_w, T + Tc, bufs=(k, v))
    o_mla = _mla_attention(q, k, v, (0, T + Tc))

    dq, dk1, dk2, dv = _df_prep(p_df, _df_tables(cos, sin), T + Tc)
    dqc, dk1, dk2, dv = _df_prep(c_df, _df_tables(ones, zeros), T + Tc, bufs=(dk1, dk2, dv))
    o_df = _df_attention(dq, dk1, dk2, dv, (0, T + Tc), lp, lam_init)

    outs = (o_rw, o_hy, o_mla, o_df)
    if not with_ctx_out:
        return outs, None
    oc_rw = _rwkv_output(yf_c, yb_c, prep_c, lp)
    oc_hy = _hyena(c_hy, lp)
    oc_mla = _mla_attention(qc, k, v, (T, Tc))
    oc_df = _df_attention(dqc, dk1, dk2, dv, (T, Tc), lp, lam_init)
    return outs, (oc_rw, oc_hy, oc_mla, oc_df)


def kernel(x, c, ctx, c_ctx, w_mod, b_mod, norm_pre, norm_post, ffn_w_in, ffn_w_out, w_in, rw_mu, rw_w0, rw_w2, rw_a0, rw_a2, rw_g2, rw_kk, rw_ka, rw_rk, rw_ln_w, rw_ln_b, hy_conv_w, hy_conv_b, hy_w1, hy_b1, hy_f1, hy_w2, hy_b2, hy_f2, hy_w3, hy_bias, mla_q_norm, mla_wq_b, mla_kv_norm, mla_wkv_b, df_lq1, df_lk1, df_lq2, df_lk2, df_subln, w_up, w_gate, b_gate, w_o):
    params = dict(w_in=w_in, rw_mu=rw_mu, rw_w0=rw_w0, rw_w2=rw_w2, rw_a0=rw_a0, rw_a2=rw_a2, rw_g2=rw_g2,
                  rw_kk=rw_kk, rw_ka=rw_ka, rw_rk=rw_rk, rw_ln_w=rw_ln_w, rw_ln_b=rw_ln_b, hy_conv_w=hy_conv_w,
                  hy_conv_b=hy_conv_b, hy_w1=hy_w1, hy_b1=hy_b1, hy_f1=hy_f1, hy_w2=hy_w2, hy_b2=hy_b2,
                  hy_f2=hy_f2, hy_w3=hy_w3, hy_bias=hy_bias, mla_q_norm=mla_q_norm, mla_wq_b=mla_wq_b,
                  mla_kv_norm=mla_kv_norm, mla_wkv_b=mla_wkv_b, df_lq1=df_lq1, df_lk1=df_lk1, df_lq2=df_lq2,
                  df_lk2=df_lk2, df_subln=df_subln, w_up=w_up, w_gate=w_gate, b_gate=b_gate, w_o=w_o)
    B, T, D = x.shape
    depth = w_mod.shape[0]
    assert B <= 7 and D == D_MODEL and T % 128 == 0 and ctx.shape[1] % CHUNK == 0 and T % ctx.shape[1] == 0
    rope = _rope_tables(T)

    s_rows = jnp.concatenate([c, c_ctx[None], jnp.zeros((8 - B - 1, D), F32)], 0)
    mod_all = _modulation(s_rows, w_mod, b_mod).reshape(depth, 8, 3, 3, D)
    ffn_in = ffn_w_in.astype(BF16)
    ffn_out = ffn_w_out.astype(BF16)

    xc = ctx
    for l in range(depth):
        last = l == depth - 1
        lp = {k_: v_[l] for k_, v_ in params.items()}
        mod = [[mod_all[l, :B, s, m][:, None, :] for m in range(3)] for s in range(3)]
        modc = [[mod_all[l, B:B + 1, s, m][:, None, :] for m in range(3)] for s in range(3)]
        ffn_a = (norm_pre[l, 0], norm_post[l, 0], ffn_in[l, 0], ffn_out[l, 0])
        ffn_b = (norm_pre[l, 2], norm_post[l, 2], ffn_in[l, 1], ffn_out[l, 1])

        x = _ffn(x, mod[0], *ffn_a)
        xc = _ffn(xc, modc[0], *ffn_a)

        lam_init = 0.8 - 0.6 * math.exp(-0.3 * l)
        outs, outs_c = _token_mixing(x, xc, mod[1], modc[1], norm_pre[l, 1], lam_init, not last, lp, rope)
        mw = _merge_weights(lp)
        x = _merge(x, mod[1], norm_pre[l, 1], norm_post[l, 1], outs, mw)
        x = _ffn(x, mod[2], *ffn_b)
        if not last:
            xc = _merge(xc, modc[1], norm_pre[l, 1], norm_post[l, 1], outs_c, mw)
            xc = _ffn(xc, modc[2], *ffn_b)
    return x
```

```python
import functools
import math

import numpy as np
import jax
import jax.numpy as jnp
from jax import lax
from jax.experimental import pallas as pl
from jax.experimental.pallas import tpu as pltpu

F32 = jnp.float32
BF16 = jnp.bfloat16
MIX_OUT = BF16

D_MODEL = 1024
GRID_W = 64
N_HEADS = 4
HEAD_DIM = 64
BR = N_HEADS * HEAD_DIM
D_FF = 2816
MACARON_W = 0.5
EPS = 1e-6
ROPE_BASE = 10000.0
RW_LN_EPS = 64e-5
RW_COLS = 1152
HY_COLS = 768
HY_EMB = 33
HY_FAST_DECAY = 0.3
HY_SLOW_DECAY = 1.5
HY_TARGET = 1e-2
MLA_Q_RANK = 192
MLA_KV_RANK = 128
MLA_NOPE = 64
MLA_ROPE = 32
DF_QK = 32
DF_V = 64
DF_SUBLN_EPS = 1e-5
OFF_HY = RW_COLS
OFF_MLA = RW_COLS + HY_COLS
OFF_DF = OFF_MLA + MLA_Q_RANK + MLA_KV_RANK + MLA_ROPE

LANES = 128
HP = 128
ONES_LANE = 64
LOG2E = 1.4426950408889634
FFN_TM = 512
FFN_TF = 2816
ATTN_TQ = 256
FFT_FILTER_PASSES = 1
FFT_ROWS_PER_STEP = 4
MERGE_SPLIT = 2
FFT_DATA_PASSES = 1
CHUNK = 64
RW_P_GRAM = 1
RW_P_INV = 3
RW_P_STATE = 3
RW_P_APPLY = 1
VMEM_LIMIT = 56 * 1024 * 1024


def _cp(*sem):
    return pltpu.CompilerParams(dimension_semantics=sem, vmem_limit_bytes=VMEM_LIMIT)


def _tile(n, pref, mult=8):
    if n <= pref:
        return n
    t = (pref // mult) * mult
    while t > mult and n % t:
        t -= mult
    assert n % t == 0, (n, pref)
    return t


def _dot(a, b, dims=None):
    if dims is None:
        dims = (((a.ndim - 1,), (0,)), ((), ()))
    return lax.dot_general(a, b, dims, preferred_element_type=F32)


NT = (((1,), (1,)), ((), ()))
TN = (((0,), (0,)), ((), ()))


def _split(x):
    hi = x.astype(BF16)
    lo = (x - hi.astype(F32)).astype(BF16)
    return hi, lo


def _dot3(a, b, dims=None):
    ah, al = _split(a)
    bh, bl = _split(b)
    return _dot(ah, bh, dims) + (_dot(ah, bl, dims) + _dot(al, bh, dims))


def _dot1(a, b, dims=None):
    return _dot(a.astype(BF16), b.astype(BF16), dims)


def _dotc(ch, cl, x, passes=3):
    if passes == 1:
        return _dot(ch, x.astype(BF16))
    xh, xl = _split(x)
    return _dot(ch, xh) + (_dot(ch, xl) + _dot(cl, xh))


def _dot_exact_lhs(m, x):
    x1 = x.astype(BF16)
    r1 = x - x1.astype(F32)
    x2 = r1.astype(BF16)
    x3 = (r1 - x2.astype(F32)).astype(BF16)
    return _dot(m, x1) + (_dot(m, x2) + _dot(m, x3))


def _rms(x, g, eps=EPS):
    return x * lax.rsqrt(jnp.mean(x * x, axis=-1, keepdims=True) + eps) * g


def _sigmoid(x):
    return 1.0 / (1.0 + jnp.exp(-x))


def _silu(x):
    return x * _sigmoid(x)


def _softplus(z):
    return jnp.maximum(z, 0.0) + jnp.log(1.0 + jnp.exp(-jnp.abs(z)))


def _np_split(a):
    a = jnp.asarray(a, F32)
    hi = a.astype(BF16)
    lo = (a - hi.astype(F32)).astype(BF16)
    return hi, lo


def _mod_kernel(s_ref, w_ref, b_ref, o_ref):
    s = _silu(s_ref[...])
    o_ref[...] = _dot3(s, w_ref[...]) + b_ref[...]


def _modulation(s_rows, w_mod, b_mod):
    L, D, NM = w_mod.shape
    tn = _tile(NM, 1152, LANES)
    return pl.pallas_call(
        _mod_kernel,
        grid=(L, NM // tn),
        in_specs=[pl.BlockSpec((8, D), lambda l, j: (0, 0)),
                  pl.BlockSpec((None, D, tn), lambda l, j: (l, 0, j)),
                  pl.BlockSpec((None, 1, tn), lambda l, j: (l, 0, j))],
        out_specs=pl.BlockSpec((None, 8, tn), lambda l, j: (l, 0, j)),
        out_shape=jax.ShapeDtypeStruct((L, 8, NM), F32),
        compiler_params=_cp("parallel", "parallel"),
        name="modulation",
    )(s_rows, w_mod, b_mod.reshape(L, 1, NM))


def _mod_spec(m, nargs):
    if m.shape[0] == 1:
        return pl.BlockSpec((None, 1, D_MODEL), lambda *a: (0, 0, 0))
    return pl.BlockSpec((None, 1, D_MODEL), lambda *a: (a[0], 0, 0))


def _row_spec(n):
    return pl.BlockSpec((1, n), lambda *a: (0, 0))


def _ffn_kernel(x_ref, sh_ref, sc_ref, gt_ref, gpre_ref, gpost_ref, wa_ref, wb_ref, wo_ref, o_ref, h_sc, acc_sc):
    f = pl.program_id(2)

    @pl.when(f == 0)
    def _():
        h = _rms(x_ref[...], gpre_ref[...]) * (1.0 + sc_ref[...]) + sh_ref[...]
        h_sc[...] = h.astype(BF16)
        acc_sc[...] = jnp.zeros_like(acc_sc)

    h = h_sc[...]
    a = _dot(h, wa_ref[...])
    b = _dot(h, wb_ref[...])
    acc_sc[...] += _dot((_silu(a) * b).astype(BF16), wo_ref[...])

    @pl.when(f == pl.num_programs(2) - 1)
    def _():
        o_ref[...] = x_ref[...] + MACARON_W * gt_ref[...] * _rms(acc_sc[...], gpost_ref[...])


def _ffn(x, mod3, g_pre, g_post, w_in, w_out):
    B, T, D = x.shape
    F = w_out.shape[0]
    tm = _tile(T, FFN_TM)
    tf = _tile(F, FFN_TF, LANES)
    nf = F // tf
    wbuf = dict(pipeline_mode=pl.Buffered(1)) if nf == 1 else {}
    xspec = pl.BlockSpec((None, tm, D), lambda b, i, f: (b, i, 0))
    return pl.pallas_call(
        _ffn_kernel,
        grid=(B, T // tm, nf),
        in_specs=[xspec, _mod_spec(mod3[0], 3), _mod_spec(mod3[1], 3), _mod_spec(mod3[2], 3),
                  _row_spec(D), _row_spec(D),
                  pl.BlockSpec((D, tf), lambda b, i, f: (0, f), **wbuf),
                  pl.BlockSpec((D, tf), lambda b, i, f: (0, nf + f), **wbuf),
                  pl.BlockSpec((tf, D), lambda b, i, f: (f, 0), **wbuf)],
        out_specs=xspec,
        out_shape=jax.ShapeDtypeStruct((B, T, D), F32),
        scratch_shapes=[pltpu.VMEM((tm, D), BF16), pltpu.VMEM((tm, D), F32)],
        compiler_params=_cp("parallel", "parallel", "arbitrary"),
        name="ffn",
    )(x, mod3[0], mod3[1], mod3[2], g_pre.reshape(1, D), g_post.reshape(1, D), w_in, w_in, w_out)


def _proj_kernel(x_ref, sh_ref, sc_ref, gpre_ref, wrw, why, wmla, wdf, orw, ohy, omla, odf):
    h = (_rms(x_ref[...], gpre_ref[...]) * (1.0 + sc_ref[...]) + sh_ref[...]).astype(BF16)
    orw[...] = _dot(h, wrw[...])
    ohy[...] = _dot(h, why[...])
    omla[...] = _dot(h, wmla[...]).astype(omla.dtype)
    odf[...] = _dot(h, wdf[...]).astype(odf.dtype)


def _project(x, shift, scale, g_pre, ws):
    B, T, D = x.shape
    tm = _tile(T, 256)
    xspec = pl.BlockSpec((None, tm, D), lambda b, i: (b, i, 0))
    wspecs = [pl.BlockSpec(w.shape, lambda b, i: (0, 0)) for w in ws]
    ospecs = [pl.BlockSpec((None, tm, w.shape[1]), lambda b, i: (b, i, 0)) for w in ws]
    oshapes = [jax.ShapeDtypeStruct((B, T, w.shape[1]), dt) for w, dt in zip(ws, (F32, F32, BF16, BF16))]
    return pl.pallas_call(
        _proj_kernel,
        grid=(B, T // tm),
        in_specs=[xspec, _mod_spec(shift, 2), _mod_spec(scale, 2), _row_spec(D)] + wspecs,
        out_specs=ospecs,
        out_shape=oshapes,
        compiler_params=_cp("parallel", "parallel"),
        name="mix_project",
    )(x, shift, scale, g_pre.reshape(1, D), *ws)


def _halo_specs(tm, T, C):
    nb8 = T // 8
    r = tm // 8
    main = pl.BlockSpec((None, tm, C), lambda b, i: (b, i, 0))
    prev = pl.BlockSpec((None, 8, C), lambda b, i: (b, jnp.maximum(i * r - 1, 0), 0))
    nxt = pl.BlockSpec((None, 8, C), lambda b, i: (b, jnp.minimum((i + 1) * r, nb8 - 1), 0))
    return main, prev, nxt


def _conv3_tile(p, prev8, next8, w0, w1, w2):
    i = pl.program_id(1)
    n_i = pl.num_programs(1)
    tm = p.shape[0]
    row = lax.broadcasted_iota(jnp.int32, p.shape, 0)
    prev_row = jnp.where(i > 0, prev8[7:8, :], 0.0)
    next_row = jnp.where(i < n_i - 1, next8[0:1, :], 0.0)
    pm = jnp.where(row == 0, prev_row, pltpu.roll(p, 1, 0))
    pp = jnp.where(row == tm - 1, next_row, pltpu.roll(p, tm - 1, 0))
    return pm * w0 + p * w1 + pp * w2


def _rwkv_prep_kernel(p_ref, pv_ref, nx_ref, mu_ref, w0_ref, w2_ref, a0_ref, a2_ref, g2_ref, kk_ref, ka_ref,
                      rk_ref, bd_ref, trif_ref, trib_ref, v_o, g_o, bonus_o, *dir_outs):
    mu = mu_ref[...]
    p = _conv3_tile(p_ref[...], pv_ref[...], nx_ref[...], 0.5 * mu, 1.0 - mu, 0.5 * mu)
    r = p[:, 0:BR]
    k = p[:, BR:2 * BR]
    v = p[:, 2 * BR:3 * BR]
    g_a = p[:, 3 * BR:3 * BR + 128]
    wa = p[:, 896:1024]
    aa = p[:, 1024:1152]
    bd = bd_ref[...]
    kkraw = k * kk_ref[...]
    ss = _dot_exact_lhs_rhs(kkraw * kkraw, bd)
    kk = kkraw / jnp.maximum(jnp.sqrt(ss), 1e-12)
    wl = w0_ref[...] + _dot3s(jnp.tanh(wa), w2_ref[...])
    lw = -jnp.exp(-_softplus(-wl) - 0.5)
    a = _sigmoid(a0_ref[...] + _dot3s(aa, a2_ref[...]))
    a_f = a[:, :BR]
    a_b = a[:, BR:]
    ka = ka_ref[...]
    k_f = k * (1.0 + (a_f - 1.0) * ka)
    k_b = k * (1.0 + (a_b - 1.0) * ka)
    v_o[...] = v
    g_o[...] = _dot1(_sigmoid(g_a), g2_ref[...])
    bonus_o[...] = _dot_exact_lhs_rhs(r * (k_f + k_b) * rk_ref[...], bd) * v
    for d, (tri_ref, k_d, a_d) in enumerate(((trif_ref, k_f, a_f), (trib_ref, k_b, a_b))):
        rt_o, at_o, bt_o, kt_o, bh_o, kh_o, eg_o = dir_outs[7 * d:7 * d + 7]
        lw_d = lw[:, d * BR:(d + 1) * BR]
        b_d = kk * a_d
        g = _dot_exact_lhs(tri_ref[...], lw_d)
        g4 = g.reshape(g.shape[0] // CHUNK, CHUNK, BR)
        edge = g4[:, CHUNK - 1:CHUNK, :] if d == 0 else g4[:, 0:1, :]
        gtot = jnp.broadcast_to(edge, g4.shape).reshape(g.shape)
        eng = jnp.exp(-g)
        et = jnp.exp(gtot - g)
        rt_o[...] = r * jnp.exp(g)
        at_o[...] = -kk * jnp.exp(g - lw_d)
        bt_o[...] = (b_d * eng).astype(bt_o.dtype)
        kt_o[...] = (k_d * eng).astype(kt_o.dtype)
        bh_o[...] = b_d * et
        kh_o[...] = k_d * et
        eg = jnp.exp(gtot)
        for j in range(eg_o.shape[0]):
            eg_o[j] = eg[j * CHUNK:j * CHUNK + 1, :]


def _dot_exact_lhs_rhs(x, m):
    x1 = x.astype(BF16)
    r1 = x - x1.astype(F32)
    x2 = r1.astype(BF16)
    x3 = (r1 - x2.astype(F32)).astype(BF16)
    return _dot(x1, m) + (_dot(x2, m) + _dot(x3, m))


def _block_diag_ones():
    h = np.arange(BR) // HEAD_DIM
    return jnp.asarray((h[:, None] == h[None, :]).astype(np.float32), BF16)


def _rwkv_prepare(p_rw, lp):
    B, T, C = p_rw.shape
    tm = _tile(T, 256)
    main, prev, nxt = _halo_specs(tm, T, C)
    z = jnp.zeros((64, BR), F32)
    w2 = jnp.concatenate([jnp.concatenate([lp['rw_w2'][0], z], 1), jnp.concatenate([z, lp['rw_w2'][1]], 1)], 0)
    a2 = jnp.concatenate([jnp.concatenate([lp['rw_a2'][0], z], 1), jnp.concatenate([z, lp['rw_a2'][1]], 1)], 0)
    params = [lp['rw_mu'].reshape(1, C), lp['rw_w0'].reshape(1, 2 * BR), w2, lp['rw_a0'].reshape(1, 2 * BR), a2,
              lp['rw_g2'].astype(BF16), lp['rw_kk'].reshape(1, BR), lp['rw_ka'].reshape(1, BR),
              lp['rw_rk'].reshape(1, BR), _block_diag_ones()]
    pos = np.arange(tm)
    same = (pos[:, None] // CHUNK) == (pos[None, :] // CHUNK)
    params += [jnp.asarray((same & (pos[None, :] <= pos[:, None])).astype(np.float32), BF16),
               jnp.asarray((same & (pos[None, :] >= pos[:, None])).astype(np.float32), BF16)]
    pspecs = [pl.BlockSpec(a.shape, lambda b, i: (0, 0)) for a in params]
    ospec = pl.BlockSpec((None, tm, BR), lambda b, i: (b, i, 0))
    names = ('v', 'g', 'bonus') + tuple(n + str(d) for d in (0, 1) for n in ('rt', 'at', 'bt', 'kt', 'bh', 'kh', 'eg'))
    gram_dt = BF16 if RW_P_GRAM == 1 else F32
    egspec = pl.BlockSpec((None, tm // CHUNK, 1, BR), lambda b, i: (b, i, 0, 0))
    ospecs = [egspec if n[:2] == 'eg' else ospec for n in names]
    oshapes = [jax.ShapeDtypeStruct((B, T // CHUNK, 1, BR), F32) if n[:2] == 'eg' else
               jax.ShapeDtypeStruct((B, T, BR), gram_dt if n[:2] in ('bt', 'kt') else F32) for n in names]
    outs = pl.pallas_call(
        _rwkv_prep_kernel,
        grid=(B, T // tm),
        in_specs=[main, prev, nxt] + pspecs,
        out_specs=ospecs,
        out_shape=oshapes,
        compiler_params=_cp("parallel", "parallel"),
        name="rwkv_prepare",
    )(p_rw, p_rw, p_rw, *params)
    return dict(zip(names, outs))


def _dot3s(a, b, dims=None):
    if dims is None:
        dims = (((1,), (0,)), ((), ()))
    ca, cb = dims[0][0][0], dims[0][1][0]
    a_lo = a - a.astype(BF16).astype(F32)
    b_lo = b - b.astype(BF16).astype(F32)
    sa = jnp.concatenate([a, a_lo, a], axis=ca).astype(BF16)
    sb = jnp.concatenate([b, b, b_lo], axis=cb).astype(BF16)
    return _dot(sa, sb, dims)


def _mmp(a, b, dims, passes):
    return _dot3s(a, b, dims) if passes == 3 else _dot1(a, b, dims)


def _rwkv_chunks(chains, eye):
    n = range(len(chains))
    rt, at, bt, kt, bh, kh, eg, v, ht, mask_s, mask_i = zip(*chains)
    C = rt[0].shape[0]
    ar = [jnp.concatenate([at[i], rt[i]], 0) for i in n]
    gm = [_mmp(ar[i], jnp.concatenate([bt[i], kt[i]], 0), NT, RW_P_GRAM) for i in n]
    top = [jnp.where(mask_s[i], gm[i][:C], 0.0) for i in n]
    bot = [jnp.where(mask_i[i], gm[i][C:], 0.0) for i in n]
    lab = [top[i][:, :C] for i in n]
    pw = [_mmp(lab[i], lab[i], None, RW_P_INV) for i in n]
    tinv = [eye + lab[i] for i in n]
    for _ in range(int(math.log2(C)) - 2):
        both = [_mmp(pw[i], jnp.concatenate([pw[i], tinv[i]], 1), None, RW_P_INV) for i in n]
        pw = [both[i][:, :C] for i in n]
        tinv = [tinv[i] + both[i][:, C:] for i in n]
    last = [_mmp(pw[i], tinv[i], None, RW_P_INV) for i in n]
    tinv = [tinv[i] + last[i] for i in n]
    wy = [_mmp(ar[i], ht[i], NT, RW_P_STATE) for i in n]
    w1 = [_mmp(top[i][:, C:], v[i], None, RW_P_APPLY) for i in n]
    u = [_mmp(tinv[i], wy[i][:C] + w1[i], None, RW_P_STATE) for i in n]
    uv = [jnp.concatenate([u[i], v[i]], 0) for i in n]
    y1 = [_mmp(bot[i], uv[i], None, RW_P_APPLY) for i in n]
    hu = [_mmp(uv[i], jnp.concatenate([bh[i], kh[i]], 0), TN, RW_P_STATE) for i in n]
    return [(wy[i][C:] + y1[i], ht[i] * eg[i] + hu[i]) for i in n]


def _rwkv_scan_kernel(*refs):
    fwd_in, bwd_in = refs[0:8], refs[8:16]
    s0_ref, yf_o, yb_o, s_o, st = refs[16:]
    c = pl.program_id(0)
    nb, C = yf_o.shape[0], yf_o.shape[1]

    @pl.when(c == 0)
    def _():
        st[...] = s0_ref[...]

    row = lax.broadcasted_iota(jnp.int32, (C, 2 * C), 0)
    col = lax.broadcasted_iota(jnp.int32, (C, 2 * C), 1)
    col = jnp.where(col >= C, col - C, col)
    eye = (lax.broadcasted_iota(jnp.int32, (C, C), 0) == lax.broadcasted_iota(jnp.int32, (C, C), 1)).astype(F32)
    chains = []
    for b in range(nb):
        for d, ins in enumerate((fwd_in, bwd_in)):
            if d == 0:
                mask_i = col <= row
                mask_s = col < row
            else:
                mask_i = col >= row
                mask_s = col > row
            for h in range(N_HEADS):
                sl = pl.ds(h * HEAD_DIM, HEAD_DIM)
                rt, at, bt, kt, bh, kh, eg, v = (ref[b, :, sl] for ref in ins)
                chains.append((rt, at, bt, kt, bh, kh, eg, v, st[b, d, h], mask_s, mask_i))
    results = iter(_rwkv_chunks(chains, eye))
    for b in range(nb):
        for d, y_o in enumerate((yf_o, yb_o)):
            for h in range(N_HEADS):
                y, hn = next(results)
                y_o[b, :, pl.ds(h * HEAD_DIM, HEAD_DIM)] = y
                st[b, d, h] = hn

    @pl.when(c == pl.num_programs(0) - 1)
    def _():
        s_o[...] = st[...]


def _rwkv_scan(prep, s0):
    B, T, _ = prep['v'].shape
    nc = T // CHUNK
    fwd = pl.BlockSpec((B, CHUNK, BR), lambda c: (0, c, 0))
    bwd = pl.BlockSpec((B, CHUNK, BR), lambda c: (0, nc - 1 - c, 0))
    sspec = pl.BlockSpec(s0.shape, lambda c: (0, 0, 0, 0, 0))
    ins = [prep[n + str(d)] for d in (0, 1) for n in ('rt', 'at', 'bt', 'kt', 'bh', 'kh', 'eg')]
    return pl.pallas_call(
        _rwkv_scan_kernel,
        grid=(nc,),
        in_specs=[fwd] * 6 + [pl.BlockSpec((B, None, 1, BR), lambda c: (0, c, 0, 0)), fwd]
        + [bwd] * 6 + [pl.BlockSpec((B, None, 1, BR), lambda c: (0, nc - 1 - c, 0, 0)), bwd] + [sspec],
        out_specs=[fwd, bwd, sspec],
        out_shape=[jax.ShapeDtypeStruct((B, T, BR), F32), jax.ShapeDtypeStruct((B, T, BR), F32),
                   jax.ShapeDtypeStruct(s0.shape, F32)],
        scratch_shapes=[pltpu.VMEM(s0.shape, F32)],
        compiler_params=_cp("arbitrary"),
        name="rwkv_scan",
    )(*ins[:7], prep['v'], *ins[7:], prep['v'], s0)


def _rwkv_out_kernel(yf, yb, g, bonus, lnw, lnb, bd, o):
    y = yf[...] + yb[...]
    m = bd[...]
    inv_n = 1.0 / HEAD_DIM
    mean = _dot_exact_lhs_rhs(y, m) * inv_n
    yc = y - mean
    var = _dot_exact_lhs_rhs(yc * yc, m) * inv_n
    yn = yc * lax.rsqrt(var + RW_LN_EPS) * lnw[...] + lnb[...]
    o[...] = ((yn + bonus[...]) * g[...]).astype(o.dtype)


def _rwkv_output(yf, yb, prep, lp):
    B, T, _ = yf.shape
    tm = _tile(T, 512)
    spec = pl.BlockSpec((None, tm, BR), lambda b, i: (b, i, 0))
    return pl.pallas_call(
        _rwkv_out_kernel,
        grid=(B, T // tm),
        in_specs=[spec] * 4 + [_row_spec(BR), _row_spec(BR), pl.BlockSpec((BR, BR), lambda b, i: (0, 0))],
        out_specs=spec,
        out_shape=jax.ShapeDtypeStruct((B, T, BR), MIX_OUT),
        compiler_params=_cp("parallel", "parallel"),
        name="rwkv_output",
    )(yf, yb, prep['g'], prep['bonus'], lp['rw_ln_w'].reshape(1, BR), lp['rw_ln_b'].reshape(1, BR),
      _block_diag_ones())


def _hy_pre_kernel(p_ref, pv_ref, nx_ref, w_ref, b_ref, x0_o, z_o):
    w = w_ref[...]
    u = _conv3_tile(p_ref[...], pv_ref[...], nx_ref[...], w[0:1, :], w[1:2, :], w[2:3, :]) + b_ref[...]
    x0_o[...] = u[:, 0:BR]
    z_o[...] = u[:, 2 * BR:3 * BR] * u[:, BR:2 * BR]


def _hyena_pre(p_hy, lp):
    B, T, C = p_hy.shape
    tm = _tile(T, 512)
    main, prev, nxt = _halo_specs(tm, T, C)
    ospec = pl.BlockSpec((None, tm, BR), lambda b, i: (b, i, 0))
    return pl.pallas_call(
        _hy_pre_kernel,
        grid=(B, T // tm),
        in_specs=[main, prev, nxt, pl.BlockSpec((3, C), lambda b, i: (0, 0)), _row_spec(C)],
        out_specs=[ospec, ospec],
        out_shape=[jax.ShapeDtypeStruct((B, T, BR), F32)] * 2,
        compiler_params=_cp("parallel", "parallel"),
        name="hyena_pre",
    )(p_hy, p_hy, p_hy, lp['hy_conv_w'], lp['hy_conv_b'].reshape(1, C))


def _hy_filter_kernel(z_ref, t_ref, mf_ref, mb_ref, w1, b1, f1, w2, b2, f2, w3, dl, k_o, ss_o):
    h = jnp.sin(f1[...] * (_dot3s(z_ref[...], w1[...]) + b1[...]))
    h = jnp.sin(f2[...] * (_dot3s(h, w2[...]) + b2[...]))
    h = _dot1(h, w3[...]) * jnp.exp(-t_ref[...] * dl[...])
    kern = mf_ref[...] * h[:, :BR] + mb_ref[...] * h[:, BR:]
    k_o[...] = kern

    @pl.when(pl.program_id(0) == 0)
    def _():
        ss_o[...] = jnp.zeros_like(ss_o)

    ss_o[...] += jnp.sum(kern * kern, axis=0, keepdims=True)


def _hyena_filter(n, lp):
    t = jnp.linspace(0.0, 1.0, n, dtype=F32)
    bands = (HY_EMB - 1) // 2
    wpos = 2.0 * math.pi * jnp.arange(n, dtype=F32) / n
    fr = jnp.linspace(1e-4, bands - 1, bands, dtype=F32)
    t_ext = jnp.concatenate([t, t[::-1]])[:, None]
    fr_lane = jnp.concatenate([jnp.zeros((1,), F32), fr, fr, jnp.zeros((LANES - HY_EMB,), F32)])[None]
    ang = jnp.concatenate([wpos, wpos[::-1]])[:, None] * fr_lane
    lane = lax.broadcasted_iota(jnp.int32, ang.shape, 1)
    z_ext = jnp.where(lane == 0, t_ext, jnp.where(lane <= bands, jnp.cos(ang),
                                                  jnp.where(lane < HY_EMB, -jnp.sin(ang), 0.0)))
    rho = np.arange(2 * n)
    mf = jnp.asarray((rho < n).astype(np.float32))[:, None]
    mb = jnp.asarray((rho > n).astype(np.float32))[:, None]
    max_decay = math.log(HY_TARGET) / HY_FAST_DECAY
    min_decay = math.log(HY_TARGET) / HY_SLOW_DECAY
    deltas = jnp.abs(jnp.linspace(min_decay, max_decay, BR, dtype=F32))
    dl = jnp.tile(deltas, 2)[None]
    w1 = jnp.pad(lp['hy_w1'], ((0, LANES - HY_EMB), (0, 0)))
    params = [w1, lp['hy_b1'][None], lp['hy_f1'][None], lp['hy_w2'], lp['hy_b2'][None], lp['hy_f2'][None],
              lp['hy_w3'], dl]
    tm = _tile(2 * n, 512)
    return pl.pallas_call(
        _hy_filter_kernel,
        grid=(2 * n // tm,),
        in_specs=[pl.BlockSpec((tm, LANES), lambda i: (i, 0))] + [pl.BlockSpec((tm, 1), lambda i: (i, 0))] * 3
        + [pl.BlockSpec(a.shape, lambda i: (0, 0)) for a in params],
        out_specs=[pl.BlockSpec((tm, BR), lambda i: (i, 0)), pl.BlockSpec((1, BR), lambda i: (0, 0))],
        out_shape=[jax.ShapeDtypeStruct((2 * n, BR), F32), jax.ShapeDtypeStruct((1, BR), F32)],
        compiler_params=_cp("arbitrary"),
        name="hyena_filter",
    )(z_ext, t_ext, mf, mb, *params)


def _dft_tables(N1, N2):
    N = N1 * N2

    def mat(n):
        a = -2.0 * np.pi * np.outer(np.arange(n), np.arange(n)) / n
        return np.cos(a), np.sin(a)

    f1r, f1i = mat(N1)
    f2r, f2i = mat(N2)
    a = -2.0 * np.pi * np.outer(np.arange(N1), np.arange(N2)) / N
    return dict(f1r=f1r, f1i=f1i, f2r=f2r, f2i=f2i,
                twr=jnp.asarray(np.cos(a)[:, :, None], F32), twi=jnp.asarray(np.sin(a)[:, :, None], F32))


def _fft_in_kernel(passes, x_ref, frh, frl, fih, fil, ar_o, ai_o):
    x = x_ref[...]
    ar_o[...] = _dotc(frh[...], frl[...], x, passes).astype(ar_o.dtype)
    ai_o[...] = _dotc(fih[...], fil[...], x, passes).astype(ai_o.dtype)


def _fft_in(x2d, tabs, N1, passes):
    B, n1u, cols = x2d.shape
    consts = list(_np_split(tabs['f1r'][:, :n1u])) + list(_np_split(tabs['f1i'][:, :n1u]))
    tc = _tile(cols, 8192, LANES)
    ospec = pl.BlockSpec((None, N1, tc), lambda b, j: (b, 0, j))
    return pl.pallas_call(
        functools.partial(_fft_in_kernel, passes),
        grid=(B, cols // tc),
        in_specs=[pl.BlockSpec((None, n1u, tc), lambda b, j: (b, 0, j))]
        + [pl.BlockSpec((N1, n1u), lambda b, j: (0, 0))] * 4,
        out_specs=[ospec, ospec],
        out_shape=[jax.ShapeDtypeStruct((B, N1, cols), F32 if passes == 3 else BF16)] * 2,
        compiler_params=_cp("parallel", "parallel"),
        name="fft_first_axis",
    )(x2d, *consts)


def _cplx_dft(frh, frl, fih, fil, xr, xi, conj, passes):
    rr = _dotc(frh, frl, xr, passes)
    ii = _dotc(fih, fil, xi, passes)
    ri = _dotc(frh, frl, xi, passes)
    ir = _dotc(fih, fil, xr, passes)
    if conj:
        return rr + ii, ri - ir
    return rr - ii, ri + ir


def _fft_spec_kernel(ar, ai, twr, twi, frh, frl, fih, fil, sc, kr_o, ki_o):
    f = (frh[...], frl[...], fih[...], fil[...])
    for j in range(ar.shape[0]):
        tr, ti = twr[j], twi[j]
        xr = ar[j] * tr - ai[j] * ti
        xi = ar[j] * ti + ai[j] * tr
        br, bi = _cplx_dft(*f, xr, xi, False, FFT_FILTER_PASSES)
        kr_o[j] = br * sc[...]
        ki_o[j] = bi * sc[...]


def _fft_conv_kernel(ar, ai, twr, twi, frh, frl, fih, fil, kr, ki, dr_o, di_o):
    f = (frh[...], frl[...], fih[...], fil[...])
    rows = range(ar.shape[0])
    tw = [(twr[j], twi[j]) for j in rows]
    a = [(ar[j].astype(F32), ai[j].astype(F32)) for j in rows]
    x = [(a[j][0] * tw[j][0] - a[j][1] * tw[j][1], a[j][0] * tw[j][1] + a[j][1] * tw[j][0]) for j in rows]
    bf = [_cplx_dft(*f, x[j][0], x[j][1], False, FFT_DATA_PASSES) for j in rows]
    c = [(bf[j][0] * kr[j] - bf[j][1] * ki[j], bf[j][0] * ki[j] + bf[j][1] * kr[j]) for j in rows]
    d = [_cplx_dft(*f, c[j][0], c[j][1], True, FFT_DATA_PASSES) for j in rows]
    for j in rows:
        (dr, di), (tr, ti) = d[j], tw[j]
        dr_o[j] = (dr * tr + di * ti).astype(dr_o.dtype)
        di_o[j] = (di * tr - dr * ti).astype(di_o.dtype)


def _fft_mid(a_re, a_im, tabs, kf=None, scale=None):
    B, N1, N2, C = a_re.shape
    consts = list(_np_split(tabs['f2r'])) + list(_np_split(tabs['f2i']))
    kb = _tile(N1, FFT_ROWS_PER_STEP, 1)
    aspec = pl.BlockSpec((None, kb, N2, C), lambda b, k: (b, k, 0, 0))
    tspec = pl.BlockSpec((kb, N2, 1), lambda b, k: (k, 0, 0))
    fspec = pl.BlockSpec((N2, N2), lambda b, k: (0, 0))
    if kf is None:
        extra, especs, kern, nm = [scale], [_row_spec(C)], _fft_spec_kernel, "fft_filter_spectrum"
    else:
        kspec = pl.BlockSpec((None, kb, N2, C), lambda b, k: (0, k, 0, 0))
        extra, especs, kern, nm = list(kf), [kspec, kspec], _fft_conv_kernel, "fft_second_axis_conv"
    return pl.pallas_call(
        kern,
        grid=(B, N1 // kb),
        in_specs=[aspec, aspec, tspec, tspec] + [fspec] * 4 + especs,
        out_specs=[aspec, aspec],
        out_shape=[jax.ShapeDtypeStruct((B, N1, N2, C), a_re.dtype)] * 2,
        compiler_params=_cp("parallel", "parallel"),
        name=nm,
    )(a_re, a_im, tabs['twr'], tabs['twi'], *consts, *extra)


def _fft_out_kernel(dr, di, frh, frl, fih, fil, x0, z, bias, o):
    zc = (_dotc(frh[...], frl[...], dr[...].astype(F32), FFT_DATA_PASSES)
          + _dotc(fih[...], fil[...], di[...].astype(F32), FFT_DATA_PASSES))
    zz = z[...]
    o[...] = (x0[...] * (zc + zz * bias[...])).astype(o.dtype)


def _fft_out(d_re, d_im, tabs, x0_2d, z_2d, bias_row):
    B, N1, cols = d_re.shape
    n1u = x0_2d.shape[1]
    consts = list(_np_split(tabs['f1r'][:n1u, :])) + list(_np_split(tabs['f1i'][:n1u, :]))
    tc = _tile(cols, 8192, LANES)
    dspec = pl.BlockSpec((None, N1, tc), lambda b, j: (b, 0, j))
    xspec = pl.BlockSpec((None, n1u, tc), lambda b, j: (b, 0, j))
    return pl.pallas_call(
        _fft_out_kernel,
        grid=(B, cols // tc),
        in_specs=[dspec, dspec] + [pl.BlockSpec((n1u, N1), lambda b, j: (0, 0))] * 4
        + [xspec, xspec, pl.BlockSpec((1, tc), lambda b, j: (0, j))],
        out_specs=xspec,
        out_shape=jax.ShapeDtypeStruct((B, n1u, cols), MIX_OUT),
        compiler_params=_cp("parallel", "parallel"),
        name="fft_last_axis_gate",
    )(d_re, d_im, *consts, x0_2d, z_2d, bias_row)


def _hy_gate_kernel(zc, x0, z, bias, o):
    o[...] = (x0[...] * (zc[...] + z[...] * bias[...])).astype(o.dtype)


def _hyena(p_hy, lp):
    B, n, _ = p_hy.shape
    x0, z = _hyena_pre(p_hy, lp)
    kern, ss = _hyena_filter(n, lp)
    N = 2 * n
    N2 = min(256, n)
    N1 = N // N2
    tabs = _dft_tables(N1, N2)
    scale = lax.rsqrt(ss) * (1.0 / N)
    bias = lp['hy_bias'].reshape(1, BR)
    if N1 >= 4:
        n1u = N1 // 2
        k_re, k_im = _fft_in(kern.reshape(1, N1, N2 * BR), tabs, N1, 3)
        kf = _fft_mid(k_re.reshape(1, N1, N2, BR), k_im.reshape(1, N1, N2, BR), tabs, scale=scale)
        a_re, a_im = _fft_in(z.reshape(B, n1u, N2 * BR), tabs, N1, FFT_DATA_PASSES)
        d_re, d_im = _fft_mid(a_re.reshape(B, N1, N2, BR), a_im.reshape(B, N1, N2, BR), tabs, kf=kf)
        o = _fft_out(d_re.reshape(B, N1, N2 * BR), d_im.reshape(B, N1, N2 * BR), tabs,
                     x0.reshape(B, n1u, N2 * BR), z.reshape(B, n1u, N2 * BR), jnp.tile(bias, (1, N2)))
        return o.reshape(B, n, BR)
    tabs = _dft_tables(1, N)
    kf = _fft_mid(kern.reshape(1, 1, N, BR), jnp.zeros((1, 1, N, BR), F32), tabs, scale=scale)
    zp = jnp.pad(z, ((0, 0), (0, n), (0, 0))).reshape(B, 1, N, BR)
    d_re, _ = _fft_mid(zp, jnp.zeros_like(zp), tabs, kf=kf)
    zc = d_re.reshape(B, N, BR)[:, :n]
    tm = _tile(n, 512)
    spec = pl.BlockSpec((None, tm, BR), lambda b, i: (b, i, 0))
    return pl.pallas_call(
        _hy_gate_kernel,
        grid=(B, n // tm),
        in_specs=[spec, spec, spec, _row_spec(BR)],
        out_specs=spec,
        out_shape=jax.ShapeDtypeStruct((B, n, BR), MIX_OUT),
        compiler_params=_cp("parallel", "parallel"),
        name="hyena_gate",
    )(zc, x0, z, bias)


def _rope_tables(n_lat):
    t = lax.iota(jnp.int32, n_lat)
    row = (t // GRID_W).astype(F32)[:, None]
    col = (t % GRID_W).astype(F32)[:, None]
    half = MLA_ROPE // 2
    inv = ROPE_BASE ** (-jnp.arange(0, half, 2, dtype=F32) / half)
    j = np.arange(HP) % MLA_ROPE
    inv_lane = jnp.take(inv, jnp.asarray(j % (half // 2)))[None]
    ang = jnp.where(jnp.asarray(j < half)[None], row, col) * inv_lane
    return jnp.cos(ang), jnp.sin(ang)


def _rot_index():
    j = np.arange(32)
    first = (j % 16) < 8
    src = np.where(first, j + 8, j - 8)
    sign = np.where(first, -1.0, 1.0)
    return src, sign


def _take_cols(w, idx, sign=None):
    idx = np.asarray(idx)
    sg = np.where(idx >= 0, 1.0 if sign is None else np.asarray(sign, np.float64), 0.0)
    pieces, i, n = [], 0, len(idx)
    while i < n:
        j = i + 1
        if sg[i] == 0.0:
            while j < n and sg[j] == 0.0:
                j += 1
            pieces.append(jnp.zeros((w.shape[0], j - i), w.dtype))
        else:
            while j < n and sg[j] == sg[i] and idx[j] == idx[j - 1] + 1:
                j += 1
            piece = w[:, int(idx[i]):int(idx[i]) + (j - i)]
            pieces.append(piece if sg[i] == 1.0 else piece * sg[i])
        i = j
    return jnp.concatenate(pieces, axis=1)


def _kv_buffers(B, T, tm, bufs, first_row):
    qshape = jax.ShapeDtypeStruct((B, N_HEADS, T, HP), BF16)
    qspec = pl.BlockSpec((None, N_HEADS, tm, HP), lambda b, i: (b, 0, i, 0))
    off = first_row // tm
    kvspec = pl.BlockSpec((None, N_HEADS, tm, HP), lambda b, i: (b, 0, off + i, 0))
    kvshapes = [jax.ShapeDtypeStruct(a.shape, BF16) for a in bufs]
    return [pl.BlockSpec(memory_space=pl.ANY)] * len(bufs), [qspec] + [kvspec] * len(bufs), [qshape] + kvshapes


def _mla_prep_kernel(p_ref, cq_t, sq_t, ck_t, qn, kvn, wqc, wqs, wk, wv, epe, *rest):
    q_o, k_o, v_o = rest[-3:]
    p = p_ref[...].astype(F32)
    cq = p[:, 0:256]
    cqn = (cq * lax.rsqrt(jnp.sum(cq * cq, axis=-1, keepdims=True) * (1.0 / MLA_Q_RANK) + EPS) * qn[...]).astype(BF16)
    ckv = p[:, 256:384]
    ckvn = _rms(ckv, kvn[...]).astype(BF16)
    cq_all = jnp.concatenate([cq_t[...]] * N_HEADS, axis=1)
    sq_all = jnp.concatenate([sq_t[...]] * N_HEADS, axis=1)
    q = _dot(cqn, wqc[...]) * cq_all + _dot(cqn, wqs[...]) * sq_all
    pe = (p[:, 384:512] * ck_t[...]).astype(BF16)
    k = _dot(ckvn, wk[...]) + _dot(pe, epe[...])
    v = _dot(ckvn, wv[...])
    lane = lax.broadcasted_iota(jnp.int32, (p.shape[0], HP), 1)
    for h in range(N_HEADS):
        q_o[h] = q[:, h * HP:(h + 1) * HP].astype(BF16)
        k_o[h] = k[:, h * HP:(h + 1) * HP].astype(BF16)
        v_o[h] = jnp.where(lane == ONES_LANE, 1.0, v[:, h * HP:(h + 1) * HP]).astype(BF16)


def _mla_weights(lp):
    s = LOG2E / math.sqrt(MLA_NOPE + MLA_ROPE)
    src, sign = _rot_index()
    hd = MLA_NOPE + MLA_ROPE
    qc_idx, qs_idx, qs_sign, k_idx, v_idx = [], [], [], [], []
    for h in range(N_HEADS):
        qc_idx += list(h * hd + np.arange(hd)) + [-1] * (HP - hd)
        qs_idx += [-1] * MLA_NOPE + list(h * hd + MLA_NOPE + src) + [-1] * (HP - hd)
        qs_sign += [0.0] * MLA_NOPE + list(sign) + [0.0] * (HP - hd)
        k_idx += list(h * 128 + np.arange(MLA_NOPE)) + [-1] * (HP - MLA_NOPE)
        v_idx += list(h * 128 + MLA_NOPE + np.arange(64)) + [-1] * (HP - 64)
    wq = jnp.pad(lp['mla_wq_b'] * s, ((0, 256 - MLA_Q_RANK), (0, 0)))
    wqc = _take_cols(wq, qc_idx).astype(BF16)
    wqs = _take_cols(wq, qs_idx, qs_sign).astype(BF16)
    wk = _take_cols(lp['mla_wkv_b'], k_idx).astype(BF16)
    wv = _take_cols(lp['mla_wkv_b'], v_idx).astype(BF16)
    epe = np.zeros((128, N_HEADS * HP), np.float32)
    for h in range(N_HEADS):
        for j in range(MLA_ROPE):
            epe[j, h * HP + MLA_NOPE + j] = 1.0
            epe[MLA_ROPE + j, h * HP + MLA_NOPE + j] = 1.0
    qn = jnp.pad(lp['mla_q_norm'], (0, 256 - MLA_Q_RANK)).reshape(1, 256)
    return qn, lp['mla_kv_norm'].reshape(1, MLA_KV_RANK), wqc, wqs, wk, wv, jnp.asarray(epe, BF16)


def _mla_tables(cos, sin):
    lane = lax.broadcasted_iota(jnp.int32, cos.shape, 1)
    pe = (lane >= MLA_NOPE) & (lane < MLA_NOPE + MLA_ROPE)
    cq = jnp.where(lane < MLA_NOPE, 1.0, jnp.where(pe, cos, 0.0))
    sq = jnp.where(pe, sin, 0.0)
    ck = jnp.where(lane < MLA_ROPE, cos, jnp.where(lane < 2 * MLA_ROPE, sin, 0.0))
    return cq, sq, ck


def _mla_prep(p_mla, tables, weights, bufs, first_row):
    B, T, C = p_mla.shape
    tm = _tile(T, 256)
    tspecs = [pl.BlockSpec((tm, HP), lambda b, i: (i, 0))] * 3
    wspecs = [pl.BlockSpec(w.shape, lambda b, i: (0, 0)) for w in weights]
    aspecs, ospecs, oshapes = _kv_buffers(B, T, tm, bufs, first_row)
    n_in = 1 + len(tables) + len(weights)
    return pl.pallas_call(
        _mla_prep_kernel,
        grid=(B, T // tm),
        in_specs=[pl.BlockSpec((None, tm, C), lambda b, i: (b, i, 0))] + tspecs + wspecs + aspecs,
        out_specs=ospecs,
        out_shape=oshapes,
        input_output_aliases={n_in + j: 1 + j for j in range(len(bufs))},
        compiler_params=_cp("parallel", "parallel"),
        name="mla_prepare",
    )(p_mla, *tables, *weights, *bufs)


def _df_prep_kernel(p_ref, c_t, s_t, *rest):
    q_o, k1_o, k2_o, v_o = rest[-4:]
    p = p_ref[...].astype(F32)
    c = jnp.concatenate([c_t[...]] * 2, axis=1)
    s = jnp.concatenate([s_t[...]] * 2, axis=1)
    q = (p[:, 0:BR] * c + p[:, BR:2 * BR] * s) * (LOG2E / math.sqrt(DF_QK))
    k = p[:, 2 * BR:3 * BR] * c + p[:, 3 * BR:4 * BR] * s
    v = p[:, 4 * BR:5 * BR]
    lane = lax.broadcasted_iota(jnp.int32, (p.shape[0], HP), 1)
    for h in range(N_HEADS):
        pair = slice((h // 2) * HP, (h // 2 + 1) * HP)

        def head(x):
            return x[:, pair] if h % 2 == 0 else pltpu.roll(x[:, pair], HEAD_DIM, 1)

        kh = head(k)
        q_o[h] = jnp.where(lane < 2 * DF_QK, head(q), 0.0).astype(BF16)
        k1_o[h] = jnp.where(lane < DF_QK, kh, 0.0).astype(BF16)
        k2_o[h] = jnp.where((lane >= DF_QK) & (lane < 2 * DF_QK), kh, 0.0).astype(BF16)
        v_o[h] = jnp.where(lane == ONES_LANE, 1.0, jnp.where(lane < DF_V, head(v), 0.0)).astype(BF16)


def _df_tables(cos, sin):
    return cos, sin


def _df_prep(p_df, tables, bufs, first_row):
    B, T, C = p_df.shape
    tm = _tile(T, 256)
    tspec = pl.BlockSpec((tm, HP), lambda b, i: (i, 0))
    aspecs, ospecs, oshapes = _kv_buffers(B, T, tm, bufs, first_row)
    return pl.pallas_call(
        _df_prep_kernel,
        grid=(B, T // tm),
        in_specs=[pl.BlockSpec((None, tm, C), lambda b, i: (b, i, 0)), tspec, tspec] + aspecs,
        out_specs=ospecs,
        out_shape=oshapes,
        input_output_aliases={3 + j: 1 + j for j in range(len(bufs))},
        compiler_params=_cp("parallel", "parallel"),
        name="diff_prepare",
    )(p_df, *tables, *bufs)


def _softmax_pv(q, k_ref, v_ref):
    s = _dot(q, k_ref[...], NT)
    p = jnp.exp2((s - jnp.max(s, axis=-1, keepdims=True)).astype(BF16))
    acc = _dot(p, v_ref[...])
    return acc / acc[:, ONES_LANE:ONES_LANE + 1]


def _mla_attn_kernel(q_ref, k_ref, v_ref, o_ref):
    o = _softmax_pv(q_ref[...], k_ref, v_ref)
    lane = lax.broadcasted_iota(jnp.int32, o.shape, 1)
    o_ref[...] = jnp.where(lane < HEAD_DIM, o, 0.0).astype(o_ref.dtype)


def _attn_specs(q, key_rows):
    tq = _tile(q.shape[2], ATTN_TQ)
    first, n = key_rows
    assert first % n == 0
    qspec = pl.BlockSpec((None, None, tq, HP), lambda b, h, i: (b, h, i, 0))
    kspec = pl.BlockSpec((None, None, n, HP), lambda b, h, i: (b, h, first // n, 0))
    return tq, qspec, kspec


def _mla_attention(q, k, v, key_rows):
    B, H, Tq, _ = q.shape
    tq, qspec, kspec = _attn_specs(q, key_rows)
    return pl.pallas_call(
        _mla_attn_kernel,
        grid=(B, H, Tq // tq),
        in_specs=[qspec, kspec, kspec],
        out_specs=qspec,
        out_shape=jax.ShapeDtypeStruct((B, H, Tq, HP), MIX_OUT),
        compiler_params=_cp("parallel", "parallel", "parallel"),
        name="mla_attention",
    )(q, k, v)


def _df_attn_kernel(lam_init, q_ref, k1_ref, k2_ref, v_ref, lq1, lk1, lq2, lk2, sub, o_ref):
    q = q_ref[...]
    lam = (jnp.exp(jnp.sum(lq1[...] * lk1[...], axis=-1, keepdims=True))
           - jnp.exp(jnp.sum(lq2[...] * lk2[...], axis=-1, keepdims=True)) + lam_init)
    o = _softmax_pv(q, k1_ref, v_ref) - lam * _softmax_pv(q, k2_ref, v_ref)
    lane = lax.broadcasted_iota(jnp.int32, o.shape, 1)
    o = jnp.where(lane < DF_V, o, 0.0)
    ms = jnp.sum(o * o, axis=-1, keepdims=True) * (1.0 / DF_V)
    o_ref[...] = (o * lax.rsqrt(ms + DF_SUBLN_EPS) * sub[...] * (1.0 - lam_init)).astype(o_ref.dtype)


def _df_attention(q, k1, k2, v, key_rows, lp, lam_init):
    B, H, Tq, _ = q.shape
    tq, qspec, kspec = _attn_specs(q, key_rows)
    lspec = pl.BlockSpec((1, DF_QK), lambda b, h, i: (0, 0))
    sub = jnp.pad(lp['df_subln'], (0, HP - DF_V)).reshape(1, HP)
    return pl.pallas_call(
        functools.partial(_df_attn_kernel, lam_init),
        grid=(B, H, Tq // tq),
        in_specs=[qspec, kspec, kspec, kspec, lspec, lspec, lspec, lspec,
                  pl.BlockSpec((1, HP), lambda b, h, i: (0, 0))],
        out_specs=qspec,
        out_shape=jax.ShapeDtypeStruct((B, H, Tq, HP), MIX_OUT),
        compiler_params=_cp("parallel", "parallel", "parallel"),
        name="diff_attention",
    )(q, k1, k2, v, lp['df_lq1'].reshape(1, DF_QK), lp['df_lk1'].reshape(1, DF_QK),
      lp['df_lq2'].reshape(1, DF_QK), lp['df_lk2'].reshape(1, DF_QK), sub)


def _merge_kernel(x_ref, sh_ref, sc_ref, gt_ref, gpre_ref, gpost_ref, orw, ohy, omla, odf,
                  wg, bg, urw, uhy, umla, udf, wo, o_ref):
    rows = x_ref.shape[0] // MERGE_SPLIT
    grp = [pl.ds(i * rows, rows) for i in range(MERGE_SPLIT)]
    xs = [x_ref[g, :] for g in grp]
    hs = [(_rms(x, gpre_ref[...]) * (1.0 + sc_ref[...]) + sh_ref[...]).astype(BF16) for x in xs]
    accs = [None] * MERGE_SPLIT
    for n in range(4):
        for i, g in enumerate(grp):
            if n == 0:
                up = _dot1(orw[g, :], urw[...])
            elif n == 1:
                up = _dot1(ohy[g, :], uhy[...])
            else:
                o_ref_n, u_ref_n = (omla, umla) if n == 2 else (odf, udf)
                pairs = [o_ref_n[hh, g, :] + pltpu.roll(o_ref_n[hh + 1, g, :].astype(F32), HEAD_DIM, 1).astype(BF16)
                         for hh in range(0, N_HEADS, 2)]
                up = _dot(jnp.concatenate(pairs, axis=1), u_ref_n[...])
            t = _sigmoid(_dot(hs[i], wg[n]) + bg[n]) * up
            accs[i] = t if accs[i] is None else accs[i] + t
    ys = [_dot(acc.astype(BF16), wo[...]) for acc in accs]
    for i, g in enumerate(grp):
        o_ref[g, :] = xs[i] + gt_ref[...] * _rms(ys[i], gpost_ref[...])


def _merge(x, mod3, g_pre, g_post, outs, mw):
    B, T, D = x.shape
    o_rw, o_hy, o_mla, o_df = outs
    tm = _tile(T, 512)
    xspec = pl.BlockSpec((None, tm, D), lambda b, i: (b, i, 0))
    bspec = pl.BlockSpec((None, tm, BR), lambda b, i: (b, i, 0))
    hspec = pl.BlockSpec((None, N_HEADS, tm, HP), lambda b, i: (b, 0, i, 0))

    def full(a):
        nd = a.ndim
        return pl.BlockSpec(a.shape, lambda b, i: (0,) * nd)

    ws = [mw['wg'], mw['bg'], mw['urw'], mw['uhy'], mw['umla'], mw['udf'], mw['wo']]
    return pl.pallas_call(
        _merge_kernel,
        grid=(B, T // tm),
        in_specs=[xspec, _mod_spec(mod3[0], 2), _mod_spec(mod3[1], 2), _mod_spec(mod3[2], 2),
                  _row_spec(D), _row_spec(D), bspec, bspec, hspec, hspec] + [full(a) for a in ws],
        out_specs=xspec,
        out_shape=jax.ShapeDtypeStruct((B, T, D), F32),
        compiler_params=_cp("parallel", "parallel"),
        name="gated_merge",
    )(x, mod3[0], mod3[1], mod3[2], g_pre.reshape(1, D), g_post.reshape(1, D), o_rw, o_hy, o_mla, o_df, *ws)


def _mix_weights(w_in):
    src, sign = _rot_index()
    w_rw = w_in[:, :RW_COLS]
    w_hy = w_in[:, OFF_HY:OFF_HY + HY_COLS]
    kpe0 = OFF_MLA + MLA_Q_RANK + MLA_KV_RANK
    mla_idx = (list(OFF_MLA + np.arange(MLA_Q_RANK)) + [-1] * (256 - MLA_Q_RANK)
               + list(OFF_MLA + MLA_Q_RANK + np.arange(MLA_KV_RANK))
               + list(kpe0 + np.arange(MLA_ROPE)) + list(kpe0 + src) + [-1] * 64)
    mla_sign = [1.0] * 256 + [1.0] * MLA_KV_RANK + [1.0] * MLA_ROPE + list(sign) + [0.0] * 64
    w_mla = _take_cols(w_in, mla_idx, mla_sign)
    idx, sg = [], []
    for base, rot in ((OFF_DF, False), (OFF_DF, True), (OFF_DF + 256, False), (OFF_DF + 256, True),
                      (OFF_DF + 512, False)):
        for g in range(BR // DF_QK):
            idx += list(base + g * DF_QK + (src if rot else np.arange(DF_QK)))
            sg += list(sign) if rot else [1.0] * DF_QK
    w_df = _take_cols(w_in, idx, sg)
    return [w.astype(BF16) for w in (w_rw, w_hy, w_mla, w_df)]


def _merge_weights(lp):
    return dict(wg=lp['w_gate'].astype(BF16), bg=lp['b_gate'].reshape(4, 1, D_MODEL),
                urw=lp['w_up'][0].astype(BF16), uhy=lp['w_up'][1].astype(BF16),
                umla=lp['w_up'][2].astype(BF16), udf=lp['w_up'][3].astype(BF16),
                wo=lp['w_o'].astype(BF16))


def _token_mixing(x, xc, mod, modc, g_pre, lam_init, with_ctx_out, lp, rope):
    B, T, _ = x.shape
    ws = _mix_weights(lp['w_in'])
    p_rw, p_hy, p_mla, p_df = _project(x, mod[0], mod[1], g_pre, ws)
    c_rw, c_hy, c_mla, c_df = _project(xc, modc[0], modc[1], g_pre, ws)

    prep = _rwkv_prepare(p_rw, lp)
    prep_c = _rwkv_prepare(c_rw, lp)
    s0 = jnp.zeros((B, 2, N_HEADS, HEAD_DIM, HEAD_DIM), F32)
    yf_c, yb_c, s_c = _rwkv_scan(prep_c, s0)
    yf, yb, _ = _rwkv_scan(prep, s_c)
    o_rw = _rwkv_output(yf, yb, prep, lp)

    o_hy = _hyena(p_hy, lp)

    cos, sin = rope
    Tc = xc.shape[1]
    ones, zeros = jnp.ones((Tc, HP), F32), jnp.zeros((Tc, HP), F32)
    mla_w = _mla_weights(lp)
    empty = jnp.zeros((B, N_HEADS, T + Tc, HP), BF16)
    q, k, v = _mla_prep(p_mla, _mla_tables(cos, sin), mla_w, (empty, empty), 0)
    qc, k, v = _mla_prep(c_mla, _mla_tables(ones, zeros), mla_w, (k, v), T)
    o_mla = _mla_attention(q, k, v, (0, T + Tc))

    dq, dk1, dk2, dv = _df_prep(p_df, _df_tables(cos, sin), (empty, empty, empty), 0)
    dqc, dk1, dk2, dv = _df_prep(c_df, _df_tables(ones, zeros), (dk1, dk2, dv), T)
    o_df = _df_attention(dq, dk1, dk2, dv, (0, T + Tc), lp, lam_init)

    outs = (o_rw, o_hy, o_mla, o_df)
    if not with_ctx_out:
        return outs, None
    oc_rw = _rwkv_output(yf_c, yb_c, prep_c, lp)
    oc_hy = _hyena(c_hy, lp)
    oc_mla = _mla_attention(qc, k, v, (T, Tc))
    oc_df = _df_attention(dqc, dk1, dk2, dv, (T, Tc), lp, lam_init)
    return outs, (oc_rw, oc_hy, oc_mla, oc_df)


def kernel(x, c, ctx, c_ctx, w_mod, b_mod, norm_pre, norm_post, ffn_w_in, ffn_w_out, w_in, rw_mu, rw_w0, rw_w2, rw_a0, rw_a2, rw_g2, rw_kk, rw_ka, rw_rk, rw_ln_w, rw_ln_b, hy_conv_w, hy_conv_b, hy_w1, hy_b1, hy_f1, hy_w2, hy_b2, hy_f2, hy_w3, hy_bias, mla_q_norm, mla_wq_b, mla_kv_norm, mla_wkv_b, df_lq1, df_lk1, df_lq2, df_lk2, df_subln, w_up, w_gate, b_gate, w_o):
    params = dict(w_in=w_in, rw_mu=rw_mu, rw_w0=rw_w0, rw_w2=rw_w2, rw_a0=rw_a0, rw_a2=rw_a2, rw_g2=rw_g2,
                  rw_kk=rw_kk, rw_ka=rw_ka, rw_rk=rw_rk, rw_ln_w=rw_ln_w, rw_ln_b=rw_ln_b, hy_conv_w=hy_conv_w,
                  hy_conv_b=hy_conv_b, hy_w1=hy_w1, hy_b1=hy_b1, hy_f1=hy_f1, hy_w2=hy_w2, hy_b2=hy_b2,
                  hy_f2=hy_f2, hy_w3=hy_w3, hy_bias=hy_bias, mla_q_norm=mla_q_norm, mla_wq_b=mla_wq_b,
                  mla_kv_norm=mla_kv_norm, mla_wkv_b=mla_wkv_b, df_lq1=df_lq1, df_lk1=df_lk1, df_lq2=df_lq2,
                  df_lk2=df_lk2, df_subln=df_subln, w_up=w_up, w_gate=w_gate, b_gate=b_gate, w_o=w_o)
    B, T, D = x.shape
    depth = w_mod.shape[0]
    assert B <= 7 and D == D_MODEL and T % 128 == 0 and ctx.shape[1] % CHUNK == 0 and T % ctx.shape[1] == 0
    rope = _rope_tables(T)

    s_rows = jnp.concatenate([c, c_ctx[None], jnp.zeros((8 - B - 1, D), F32)], 0)
    mod_all = _modulation(s_rows, w_mod, b_mod).reshape(depth, 8, 3, 3, D)
    ffn_in = ffn_w_in.astype(BF16)
    ffn_out = ffn_w_out.astype(BF16)

    xc = ctx
    for l in range(depth):
        last = l == depth - 1
        lp = {k_: v_[l] for k_, v_ in params.items()}
        mod = [[mod_all[l, :B, s, m][:, None, :] for m in range(3)] for s in range(3)]
        modc = [[mod_all[l, B:B + 1, s, m][:, None, :] for m in range(3)] for s in range(3)]
        ffn_a = (norm_pre[l, 0], norm_post[l, 0], ffn_in[l, 0], ffn_out[l, 0])
        ffn_b = (norm_pre[l, 2], norm_post[l, 2], ffn_in[l, 1], ffn_out[l, 1])

        x = _ffn(x, mod[0], *ffn_a)
        xc = _ffn(xc, modc[0], *ffn_a)

        lam_init = 0.8 - 0.6 * math.exp(-0.3 * l)
        outs, outs_c = _token_mixing(x, xc, mod[1], modc[1], norm_pre[l, 1], lam_init, not last, lp, rope)
        mw = _merge_weights(lp)
        x = _merge(x, mod[1], norm_pre[l, 1], norm_post[l, 1], outs, mw)
        x = _ffn(x, mod[2], *ffn_b)
        if not last:
            xc = _merge(xc, modc[1], norm_pre[l, 1], norm_post[l, 1], outs_c, mw)
            xc = _ffn(xc, modc[2], *ffn_b)
    return x
```
